```python
import jax, jax.numpy as jnp
from jax import lax
import numpy as np

D_MODEL = 2048
BATCH = 8
SEQ = 8192
DEPTH = 2

CHUNK = 64
N_A = DEPTH // 2
N_B = DEPTH - N_A
POOL_WINDOWS = (2, 4, 8, 16)
N_GROUPS = len(POOL_WINDOWS)
GROUP_DIM = D_MODEL // N_GROUPS
HEAD_DIM = 128
N_HEADS = D_MODEL // HEAD_DIM
D_FF = 4 * D_MODEL
Q_BLOCK = 128
EPS = 1e-6
NEG_INF = -1e30

kernel_name = "yoco_pool_fox_hybrid"


def rmsnorm(x, g):
    xf = x.astype(jnp.float32)
    y = xf * lax.rsqrt(jnp.mean(xf * xf, axis=-1, keepdims=True) + EPS)
    return (y * g.astype(jnp.float32)).astype(x.dtype)


def multiscale_pool(h, w_pool, scale):
    B, S, D = h.shape
    hg = h.astype(jnp.float32).reshape(B, S, N_GROUPS, GROUP_DIM)
    cs = jnp.pad(jnp.cumsum(hg, axis=1), ((0, 0), (1, 0), (0, 0), (0, 0)))
    pos = jnp.arange(S)
    outs = []
    for g, w in enumerate(POOL_WINDOWS):
        cg = cs[:, :, g]
        upper = cg[:, 1:]
        lower = jnp.pad(cg[:, :S - w + 1], ((0, 0), (w - 1, 0), (0, 0)))
        count = jnp.minimum(pos + 1, w).astype(jnp.float32)[None, :, None]
        outs.append((upper - lower) / count)
    mean = jnp.stack(outs, axis=2)
    mixed = jnp.einsum('bsgc,gcd->bsgd', (mean - hg).astype(h.dtype), w_pool)
    return mixed.reshape(B, S, D) * scale


def squared_relu_mlp(h, w_up, w_down):
    u = jax.nn.relu(h @ w_up)
    return (u * u) @ w_down


def forgetting_attention(q, k, v, c):
    B, H, S, dh = q.shape
    nblk = S // Q_BLOCK
    scale = dh ** -0.5
    qb = q.reshape(B, H, nblk, Q_BLOCK, dh).transpose(2, 0, 1, 3, 4)
    cb = c.reshape(B, H, nblk, Q_BLOCK).transpose(2, 0, 1, 3)
    kpos = jnp.arange(S)

    def block(args):
        qi, ci, i = args
        s = jnp.einsum('bhqd,bhkd->bhqk', qi, k) * scale + (ci[..., :, None] - c[..., None, :])
        qpos = i * Q_BLOCK + jnp.arange(Q_BLOCK)
        s = jnp.where(kpos[None, :] <= qpos[:, None], s, NEG_INF)
        p = jax.nn.softmax(s, axis=-1)
        return jnp.einsum('bhqk,bhkd->bhqd', p, v)

    o = lax.map(block, (qb, cb, jnp.arange(nblk)))
    return o.transpose(1, 0, 3, 2, 4).reshape(B, S, H * dh)


def _fwd_setup_inputs(seed: int = 0) -> dict:
    key = jax.random.key(seed)
    ks = jax.random.split(key, 16)
    D = D_MODEL
    f32 = jnp.float32
    x = jax.random.normal(ks[0], (BATCH, SEQ, D), f32)
    norm_mix = 1.0 + 0.02 * jax.random.normal(ks[1], (DEPTH, D), f32)
    norm_mlp = 1.0 + 0.02 * jax.random.normal(ks[2], (DEPTH, D), f32)
    pool_w = jax.random.normal(ks[3], (N_A, N_GROUPS, GROUP_DIM, GROUP_DIM), f32) * GROUP_DIM ** -0.5
    pool_scale = 1.0 + 0.02 * jax.random.normal(ks[4], (N_A, D), f32)
    norm_kv = 1.0 + 0.02 * jax.random.normal(ks[5], (D,), f32)
    w_kvf = jax.random.normal(ks[6], (D, 2 * D + N_HEADS), f32) * D ** -0.5
    w_kvf = w_kvf.at[:, 2 * D:].multiply(0.1)
    b_f = jax.random.uniform(ks[7], (N_HEADS,), f32, 1.0, 6.0)
    w_q = jax.random.normal(ks[8], (N_B, D, D), f32) * D ** -0.5
    w_o = jax.random.normal(ks[9], (N_B, D, D), f32) * D ** -0.5
    w_up = jax.random.normal(ks[10], (DEPTH, D, D_FF), f32) * D ** -0.5
    w_down = jax.random.normal(ks[11], (DEPTH, D_FF, D), f32) * D_FF ** -0.5
    norm_out = 1.0 + 0.02 * jax.random.normal(ks[12], (D,), f32)
    return {"x": x, "norm_mix": norm_mix, "norm_mlp": norm_mlp, "pool_w": pool_w,
            "pool_scale": pool_scale, "norm_kv": norm_kv, "w_kvf": w_kvf, "b_f": b_f,
            "w_q": w_q, "w_o": w_o, "w_up": w_up, "w_down": w_down, "norm_out": norm_out}


def _fwd_reference(x, norm_mix, norm_mlp, pool_w, pool_scale, norm_kv, w_kvf, b_f,
              w_q, w_o, w_up, w_down, norm_out):
    B, S, D = x.shape
    H, dh = N_HEADS, HEAD_DIM
    h = x
    k = v = c = None
    for layer in range(DEPTH):
        if layer < N_A:
            hn = rmsnorm(h, norm_mix[layer])
            h = h + multiscale_pool(hn, pool_w[layer], pool_scale[layer])
        else:
            if layer == N_A:
                hkv = rmsnorm(h, norm_kv)
                kvf = hkv @ w_kvf
                k = kvf[..., :D].reshape(B, S, H, dh).transpose(0, 2, 1, 3).astype(jnp.float32)
                v = kvf[..., D:2 * D].reshape(B, S, H, dh).transpose(0, 2, 1, 3).astype(jnp.float32)
                logf = jax.nn.log_sigmoid(kvf[..., 2 * D:].astype(jnp.float32) + b_f.astype(jnp.float32))
                c = jnp.cumsum(logf, axis=1).transpose(0, 2, 1)
            j = layer - N_A
            hn = rmsnorm(h, norm_mix[layer])
            q = (hn @ w_q[j]).reshape(B, S, H, dh).transpose(0, 2, 1, 3).astype(jnp.float32)
            o = forgetting_attention(q, k, v, c).astype(h.dtype)
            h = h + o @ w_o[j]
        h = h + squared_relu_mlp(rmsnorm(h, norm_mlp[layer]), w_up[layer], w_down[layer])
    return rmsnorm(h, norm_out)


import jax as _jax
import jax.numpy as _jnp

TWIN_FORMAT = 'train_step'
FWD_PARAMS = ['x', 'norm_mix', 'norm_mlp', 'pool_w', 'pool_scale', 'norm_kv', 'w_kvf', 'b_f', 'w_q', 'w_o', 'w_up', 'w_down', 'norm_out']
TWIN_WEIGHTS = ['norm_mix', 'norm_mlp', 'pool_w', 'pool_scale', 'norm_kv', 'w_kvf', 'b_f', 'w_q', 'w_o', 'w_up', 'w_down', 'norm_out']
TWIN_DIFF_INPUT = 'x'
TWIN_INPUTS = ['x', 'norm_mix', 'norm_mlp', 'pool_w', 'pool_scale', 'norm_kv', 'w_kvf', 'b_f', 'w_q', 'w_o', 'w_up', 'w_down', 'norm_out', 'loss_target', 'm_norm_mix', 'm_norm_mlp', 'm_pool_w', 'm_pool_scale', 'm_norm_kv', 'm_w_kvf', 'm_b_f', 'm_w_q', 'm_w_o', 'm_w_up', 'm_w_down', 'm_norm_out', 'v_norm_mix', 'v_norm_mlp', 'v_pool_w', 'v_pool_scale', 'v_norm_kv', 'v_w_kvf', 'v_b_f', 'v_w_q', 'v_w_o', 'v_w_up', 'v_w_down', 'v_norm_out']
TWIN_OUTPUTS = ['loss', 'grad_x', 'grad_norm_mix', 'grad_norm_mlp', 'grad_pool_w', 'grad_pool_scale', 'grad_norm_kv', 'grad_w_kvf', 'grad_b_f', 'grad_w_q', 'grad_w_o', 'grad_w_up', 'grad_w_down', 'grad_norm_out', 'delta_norm_mix', 'delta_norm_mlp', 'delta_pool_w', 'delta_pool_scale', 'delta_norm_kv', 'delta_w_kvf', 'delta_b_f', 'delta_w_q', 'delta_w_o', 'delta_w_up', 'delta_w_down', 'delta_norm_out', 'new_m_norm_mix', 'new_m_norm_mlp', 'new_m_pool_w', 'new_m_pool_scale', 'new_m_norm_kv', 'new_m_w_kvf', 'new_m_b_f', 'new_m_w_q', 'new_m_w_o', 'new_m_w_up', 'new_m_w_down', 'new_m_norm_out', 'new_v_norm_mix', 'new_v_norm_mlp', 'new_v_pool_w', 'new_v_pool_scale', 'new_v_norm_kv', 'new_v_w_kvf', 'new_v_b_f', 'new_v_w_q', 'new_v_w_o', 'new_v_w_up', 'new_v_w_down', 'new_v_norm_out']
TWIN_LEAF_KINDS = {'loss': 'loss', 'grad_x': 'grad_x', 'grad_norm_mix': 'grad_w', 'grad_norm_mlp': 'grad_w', 'grad_pool_w': 'grad_w', 'grad_pool_scale': 'grad_w', 'grad_norm_kv': 'grad_w', 'grad_w_kvf': 'grad_w', 'grad_b_f': 'grad_w', 'grad_w_q': 'grad_w', 'grad_w_o': 'grad_w', 'grad_w_up': 'grad_w', 'grad_w_down': 'grad_w', 'grad_norm_out': 'grad_w', 'delta_norm_mix': 'delta_w', 'delta_norm_mlp': 'delta_w', 'delta_pool_w': 'delta_w', 'delta_pool_scale': 'delta_w', 'delta_norm_kv': 'delta_w', 'delta_w_kvf': 'delta_w', 'delta_b_f': 'delta_w', 'delta_w_q': 'delta_w', 'delta_w_o': 'delta_w', 'delta_w_up': 'delta_w', 'delta_w_down': 'delta_w', 'delta_norm_out': 'delta_w', 'new_m_norm_mix': 'new_m', 'new_m_norm_mlp': 'new_m', 'new_m_pool_w': 'new_m', 'new_m_pool_scale': 'new_m', 'new_m_norm_kv': 'new_m', 'new_m_w_kvf': 'new_m', 'new_m_b_f': 'new_m', 'new_m_w_q': 'new_m', 'new_m_w_o': 'new_m', 'new_m_w_up': 'new_m', 'new_m_w_down': 'new_m', 'new_m_norm_out': 'new_m', 'new_v_norm_mix': 'new_v', 'new_v_norm_mlp': 'new_v', 'new_v_pool_w': 'new_v', 'new_v_pool_scale': 'new_v', 'new_v_norm_kv': 'new_v', 'new_v_w_kvf': 'new_v', 'new_v_b_f': 'new_v', 'new_v_w_q': 'new_v', 'new_v_w_o': 'new_v', 'new_v_w_up': 'new_v', 'new_v_w_down': 'new_v', 'new_v_norm_out': 'new_v'}


def _forward(args):
    return _fwd_reference(*[args[k] for k in FWD_PARAMS])


def _output_shape():
    def fwd():
        inp = _fwd_setup_inputs(0)
        return _fwd_reference(*[inp[k] for k in FWD_PARAMS])
    out = _jax.eval_shape(fwd)
    return out.shape, out.dtype

N_MICROBATCH = 1
ADAM_LR = 0.001
ADAM_B1 = 0.9
ADAM_B2 = 0.999
ADAM_EPS = 1e-08
ADAM_WD = 0.01
ADAM_STEP = 10
PER_EXAMPLE_BATCH_AXIS = {'x': 0, 'loss_target': 0}
SHARED_INPUTS = []
_WEIGHT_DTYPES = {'norm_mix': _jnp.float32, 'norm_mlp': _jnp.float32, 'pool_w': _jnp.float32, 'pool_scale': _jnp.float32, 'norm_kv': _jnp.float32, 'w_kvf': _jnp.float32, 'b_f': _jnp.float32, 'w_q': _jnp.float32, 'w_o': _jnp.float32, 'w_up': _jnp.float32, 'w_down': _jnp.float32, 'norm_out': _jnp.float32}
MOMENT_SCALE = {'norm_mix': 6.850617e-02, 'norm_mlp': 9.254997e-02, 'pool_w': 9.359266e-02, 'pool_scale': 1.514439e-01, 'norm_kv': 4.526237e-02, 'w_kvf': 3.138706e-02, 'b_f': 1.348568e-01, 'w_q': 1.626593e-02, 'w_o': 4.280071e-02, 'w_up': 4.727867e-02, 'w_down': 1.065826e-01, 'norm_out': 3.262043e+01}


def _to_microbatches(a, axis):
    t = _jnp.moveaxis(a, axis, 0)
    t = t.reshape((N_MICROBATCH, t.shape[0] // N_MICROBATCH) + t.shape[1:])
    return _jnp.moveaxis(t, 1, axis + 1)


def setup_inputs(seed: int = 0) -> dict:
    inp = _fwd_setup_inputs(seed)
    key = _jax.random.fold_in(_jax.random.key(seed), 7919)
    shape, _ = _output_shape()
    out = dict(inp)
    out["loss_target"] = _jax.random.normal(_jax.random.fold_in(key, 0), shape, _jnp.float32)
    for i, name in enumerate(TWIN_WEIGHTS):
        w = inp[name].astype(_jnp.float32)
        if MOMENT_SCALE is None:
            s = _jnp.sqrt(_jnp.mean(_jnp.square(w)) + 1e-30)
        else:
            s = MOMENT_SCALE[name]
        km, kv = _jax.random.split(_jax.random.fold_in(key, i + 1))
        out[name] = w
        out["m_" + name] = s * _jax.random.normal(km, w.shape, _jnp.float32)
        out["v_" + name] = (s * s) * _jax.random.uniform(kv, w.shape, _jnp.float32, 0.5, 1.5)
    if N_MICROBATCH > 1:
        for name, axis in PER_EXAMPLE_BATCH_AXIS.items():
            out[name] = _to_microbatches(out[name], axis)
    return {'x': out['x'], 'norm_mix': out['norm_mix'], 'norm_mlp': out['norm_mlp'], 'pool_w': out['pool_w'], 'pool_scale': out['pool_scale'], 'norm_kv': out['norm_kv'], 'w_kvf': out['w_kvf'], 'b_f': out['b_f'], 'w_q': out['w_q'], 'w_o': out['w_o'], 'w_up': out['w_up'], 'w_down': out['w_down'], 'norm_out': out['norm_out'], 'loss_target': out['loss_target'], 'm_norm_mix': out['m_norm_mix'], 'm_norm_mlp': out['m_norm_mlp'], 'm_pool_w': out['m_pool_w'], 'm_pool_scale': out['m_pool_scale'], 'm_norm_kv': out['m_norm_kv'], 'm_w_kvf': out['m_w_kvf'], 'm_b_f': out['m_b_f'], 'm_w_q': out['m_w_q'], 'm_w_o': out['m_w_o'], 'm_w_up': out['m_w_up'], 'm_w_down': out['m_w_down'], 'm_norm_out': out['m_norm_out'], 'v_norm_mix': out['v_norm_mix'], 'v_norm_mlp': out['v_norm_mlp'], 'v_pool_w': out['v_pool_w'], 'v_pool_scale': out['v_pool_scale'], 'v_norm_kv': out['v_norm_kv'], 'v_w_kvf': out['v_w_kvf'], 'v_b_f': out['v_b_f'], 'v_w_q': out['v_w_q'], 'v_w_o': out['v_w_o'], 'v_w_up': out['v_w_up'], 'v_w_down': out['v_w_down'], 'v_norm_out': out['v_norm_out']}


def _loss(weights, diff, rest, loss_target):
    with _jax.named_scope("forward"):
        args = {**rest, TWIN_DIFF_INPUT: diff, **{k: w.astype(_WEIGHT_DTYPES[k]) for k, w in weights.items()}}
        y = _forward(args)
    with _jax.named_scope("loss_head"):
        err = _jnp.square(y.astype(_jnp.float32) - loss_target)
        return 0.5 * _jnp.sum(_jnp.mean(err, axis=-1)) if err.ndim else 0.5 * err


def _adamw(w, g, m, v):
    m = ADAM_B1 * m + (1.0 - ADAM_B1) * g
    v = ADAM_B2 * v + (1.0 - ADAM_B2) * _jnp.square(g)
    m_hat = m / (1.0 - ADAM_B1 ** ADAM_STEP)
    v_hat = v / (1.0 - ADAM_B2 ** ADAM_STEP)
    delta = -ADAM_LR * (m_hat / (_jnp.sqrt(v_hat) + ADAM_EPS) + ADAM_WD * w)
    return delta, m, v


def reference(x, norm_mix, norm_mlp, pool_w, pool_scale, norm_kv, w_kvf, b_f, w_q, w_o, w_up, w_down, norm_out, loss_target, m_norm_mix, m_norm_mlp, m_pool_w, m_pool_scale, m_norm_kv, m_w_kvf, m_b_f, m_w_q, m_w_o, m_w_up, m_w_down, m_norm_out, v_norm_mix, v_norm_mlp, v_pool_w, v_pool_scale, v_norm_kv, v_w_kvf, v_b_f, v_w_q, v_w_o, v_w_up, v_w_down, v_norm_out):
    given = dict(x=x, norm_mix=norm_mix, norm_mlp=norm_mlp, pool_w=pool_w, pool_scale=pool_scale, norm_kv=norm_kv, w_kvf=w_kvf, b_f=b_f, w_q=w_q, w_o=w_o, w_up=w_up, w_down=w_down, norm_out=norm_out, loss_target=loss_target, m_norm_mix=m_norm_mix, m_norm_mlp=m_norm_mlp, m_pool_w=m_pool_w, m_pool_scale=m_pool_scale, m_norm_kv=m_norm_kv, m_w_kvf=m_w_kvf, m_b_f=m_b_f, m_w_q=m_w_q, m_w_o=m_w_o, m_w_up=m_w_up, m_w_down=m_w_down, m_norm_out=m_norm_out, v_norm_mix=v_norm_mix, v_norm_mlp=v_norm_mlp, v_pool_w=v_pool_w, v_pool_scale=v_pool_scale, v_norm_kv=v_norm_kv, v_w_kvf=v_w_kvf, v_b_f=v_b_f, v_w_q=v_w_q, v_w_o=v_w_o, v_w_up=v_w_up, v_w_down=v_w_down, v_norm_out=v_norm_out)
    weights = {n: given[n] for n in TWIN_WEIGHTS}
    shared = {n: given[n] for n in SHARED_INPUTS}
    per_example = {n: given[n] for n in ['x']}
    grad_fn = _jax.value_and_grad(_loss, argnums=(0, 1))

    def one_microbatch(ex, loss_target):
        ex = dict(ex)
        diff = ex.pop(TWIN_DIFF_INPUT)
        return grad_fn(weights, diff, {**shared, **ex}, loss_target)

    if N_MICROBATCH == 1:
        loss, (grad_w, grad_x) = one_microbatch(per_example, given["loss_target"])
    else:
        def body(carry, xs):
            loss_sum, grad_sum = carry
            l_k, (gw_k, gx_k) = one_microbatch(xs[0], xs[1])
            with _jax.named_scope("update"):
                return (loss_sum + l_k, _jax.tree.map(_jnp.add, grad_sum, gw_k)), gx_k

        init = (_jnp.zeros((), _jnp.float32), _jax.tree.map(_jnp.zeros_like, weights))
        (loss, grad_w), grad_x = _jax.lax.scan(body, init, (per_example, given["loss_target"]))
    with _jax.named_scope("update"):
        delta_w, new_m, new_v = {}, {}, {}
        for n in TWIN_WEIGHTS:
            delta_w[n], new_m[n], new_v[n] = _adamw(weights[n], grad_w[n], given["m_" + n], given["v_" + n])
    return (loss, grad_x, *[grad_w[n] for n in TWIN_WEIGHTS], *[delta_w[n] for n in TWIN_WEIGHTS],
            *[new_m[n] for n in TWIN_WEIGHTS], *[new_v[n] for n in TWIN_WEIGHTS])
```

```python
import functools

import jax
import jax.numpy as jnp
from jax import lax
from jax.experimental import pallas as pl
from jax.experimental.pallas import tpu as pltpu

F32 = jnp.float32
BF16 = jnp.bfloat16
MESH = pl.DeviceIdType.MESH

N_DEV = 8
N_CHIP = 4
EPS = 1e-6
NEG_INF = -1e30
POOL_WINDOWS = (2, 4, 8, 16)
N_GROUPS = len(POOL_WINDOWS)
POOL_HALO = 16
HEAD_DIM = 128
LANES = 128

ADAM_LR = 0.001
ADAM_B1 = 0.9
ADAM_B2 = 0.999
ADAM_EPS = 1e-08
ADAM_WD = 0.01
ADAM_STEP = 10

VMEM_LIMIT = 56 * 1024 * 1024


def _tile(n, want):
    t = min(n, want)
    assert n % t == 0, (n, want)
    return t


def _params(sem, vmem=VMEM_LIMIT):
    return pltpu.CompilerParams(dimension_semantics=sem, vmem_limit_bytes=vmem)


_DIMS = {
    "nn": (((1,), (0,)), ((), ())),
    "nt": (((1,), (1,)), ((), ())),
    "tn": (((0,), (0,)), ((), ())),
}


def _matmul(name, a, b, *, mode, grid, tm, tn, a_spec, b_spec, outs, extras=(), extra_specs=(), epilogue=None):
    nk = grid[2]
    n_extra = len(extras)
    n_out = len(outs)
    dn = _DIMS[mode]

    def body(a_ref, b_ref, *rest):
        extra_refs = rest[:n_extra]
        out_refs = rest[n_extra:n_extra + n_out]
        acc_ref = rest[n_extra + n_out]
        k = pl.program_id(2)

        @pl.when(k == 0)
        def _():
            acc_ref[...] = jnp.zeros_like(acc_ref)

        acc_ref[...] += lax.dot_general(a_ref[...].astype(BF16), b_ref[...].astype(BF16), dn, preferred_element_type=F32)

        @pl.when(k == nk - 1)
        def _():
            acc = acc_ref[...]
            vals = (acc,) if epilogue is None else epilogue(acc, *[r[...] for r in extra_refs])
            for o_ref, val in zip(out_refs, vals):
                o_ref[...] = val.astype(o_ref.dtype)

    res = pl.pallas_call(
        body,
        grid=grid,
        in_specs=[a_spec, b_spec, *extra_specs],
        out_specs=[o[2] for o in outs],
        out_shape=[jax.ShapeDtypeStruct(o[0], o[1]) for o in outs],
        scratch_shapes=[pltpu.VMEM((tm, tn), F32)],
        compiler_params=_params(("parallel", "parallel", "arbitrary")),
        name=name,
    )(a, b, *extras)
    return res


def _ij(tm, tn):
    return pl.BlockSpec((tm, tn), lambda i, j, k: (i, j))


def _mm_nn(name, a, b, *, out_dtype, residual=None, tm=1024, tn=1024, tk=2048):
    M, K = a.shape
    N = b.shape[1]
    tm, tn, tk = _tile(M, tm), _tile(N, tn), _tile(K, tk)
    extras, especs, epi = (), (), None
    if residual is not None:
        extras, especs = (residual,), (_ij(tm, tn),)
        epi = lambda acc, r: (acc + r,)
    return _matmul(
        name, a, b, mode="nn", grid=(M // tm, N // tn, K // tk), tm=tm, tn=tn,
        a_spec=pl.BlockSpec((tm, tk), lambda i, j, k: (i, k)),
        b_spec=pl.BlockSpec((tk, tn), lambda i, j, k: (k, j)),
        outs=[((M, N), out_dtype, _ij(tm, tn))], extras=extras, extra_specs=especs, epilogue=epi)[0]


def _mm_nt(name, a, b, *, out_dtype, b_cols=None, residual=None, tm=1024, tn=1024, tk=2048):
    M, K = a.shape
    N = b.shape[0]
    c0 = 0 if b_cols is None else b_cols[0]
    tm, tn, tk = _tile(M, tm), _tile(N, tn), _tile(K, tk)
    assert c0 % tk == 0
    kb0 = c0 // tk
    extras, especs, epi = (), (), None
    if residual is not None:
        extras, especs = (residual,), (_ij(tm, tn),)
        epi = lambda acc, r: (acc + r,)
    return _matmul(
        name, a, b, mode="nt", grid=(M // tm, N // tn, K // tk), tm=tm, tn=tn,
        a_spec=pl.BlockSpec((tm, tk), lambda i, j, k: (i, k)),
        b_spec=pl.BlockSpec((tn, tk), lambda i, j, k: (j, kb0 + k)),
        outs=[((M, N), out_dtype, _ij(tm, tn))], extras=extras, extra_specs=especs, epilogue=epi)[0]


def _mm_tn(name, a, b, *, tm=1024, tn=1024, tk=1024):
    K, M = a.shape
    N = b.shape[1]
    tm, tn, tk = _tile(M, tm), _tile(N, tn), _tile(K, tk)
    return _matmul(
        name, a, b, mode="tn", grid=(M // tm, N // tn, K // tk), tm=tm, tn=tn,
        a_spec=pl.BlockSpec((tk, tm), lambda i, j, k: (k, i)),
        b_spec=pl.BlockSpec((tk, tn), lambda i, j, k: (k, j)),
        outs=[((M, N), F32, _ij(tm, tn))])[0]


def _mlp_fwd(tag, hn, h, w_up_g, w_down_g, layer, *, tm=1024, tk=2048):
    S, D = hn.shape
    fb = w_up_g.shape[3]
    F = N_DEV * fb
    tm, tku = _tile(S, tm), _tile(D, tk)

    def up_epi(acc):
        u = jnp.maximum(acc, 0.0)
        return u, u * u

    u, uu = _matmul(
        f"mlp_up_{tag}", hn, w_up_g, mode="nn", grid=(S // tm, N_DEV, D // tku), tm=tm, tn=fb,
        a_spec=pl.BlockSpec((tm, tku), lambda i, j, k: (i, k)),
        b_spec=pl.BlockSpec((None, None, tku, fb), lambda i, j, k: (j, layer, k, 0)),
        outs=[((S, F), BF16, _ij(tm, fb)), ((S, F), BF16, _ij(tm, fb))], epilogue=up_epi)

    tn = _tile(D, 1024)
    tkd = _tile(fb, tk)
    r = fb // tkd
    h_out = _matmul(
        f"mlp_down_{tag}", uu, w_down_g, mode="nn", grid=(S // tm, D // tn, F // tkd), tm=tm, tn=tn,
        a_spec=pl.BlockSpec((tm, tkd), lambda i, j, k: (i, k)),
        b_spec=pl.BlockSpec((None, None, tkd, tn), lambda i, j, k: (k // r, layer, k % r, j)),
        outs=[((S, D), F32, _ij(tm, tn))], extras=(h,), extra_specs=(_ij(tm, tn),),
        epilogue=lambda acc, res: (acc + res,))[0]
    return u, uu, h_out


def _mlp_bwd(tag, dh, hn, u, uu, w_up_g, w_down_g, layer, *, tm=1024, tk=2048):
    S, D = dh.shape
    fb = w_up_g.shape[3]
    F = N_DEV * fb
    tm, tkd = _tile(S, tm), _tile(D, tk)

    d_pre = _matmul(
        f"mlp_dpre_{tag}", dh, w_down_g, mode="nt", grid=(S // tm, N_DEV, D // tkd), tm=tm, tn=fb,
        a_spec=pl.BlockSpec((tm, tkd), lambda i, j, k: (i, k)),
        b_spec=pl.BlockSpec((None, None, fb, tkd), lambda i, j, k: (j, layer, 0, k)),
        outs=[((S, F), BF16, _ij(tm, fb))], extras=(u,), extra_specs=(_ij(tm, fb),),
        epilogue=lambda acc, uv: (acc * (2.0 * uv.astype(F32)),))[0]

    dw_down = _mm_tn(f"mlp_dwdown_{tag}", uu, dh)

    tmu = _tile(D, 1024)
    tks = _tile(S, 1024)
    dw_up = _matmul(
        f"mlp_dwup_{tag}", hn, d_pre, mode="tn", grid=(D // tmu, N_DEV, S // tks), tm=tmu, tn=fb,
        a_spec=pl.BlockSpec((tks, tmu), lambda i, j, k: (k, i)),
        b_spec=pl.BlockSpec((tks, fb), lambda i, j, k: (k, j)),
        outs=[((N_DEV, D, fb), F32, pl.BlockSpec((None, tmu, fb), lambda i, j, k: (j, i, 0)))])[0]

    tn = _tile(D, 1024)
    tkf = _tile(fb, tk)
    r = fb // tkf
    d_hn = _matmul(
        f"mlp_dhn_{tag}", d_pre, w_up_g, mode="nt", grid=(S // tm, D // tn, F // tkf), tm=tm, tn=tn,
        a_spec=pl.BlockSpec((tm, tkf), lambda i, j, k: (i, k)),
        b_spec=pl.BlockSpec((None, None, tn, tkf), lambda i, j, k: (k // r, layer, j, k % r)),
        outs=[((S, D), F32, _ij(tm, tn))])[0]
    return d_hn, dw_up, dw_down


def _row_spec(ts, D):
    return pl.BlockSpec((ts, D), lambda i: (i, 0))


def _vec_spec(n, D):
    return pl.BlockSpec((n, D), lambda i: (0, 0))


def _rms_fwd(name, x, gains, *, ts=512):
    S, D = x.shape
    n = gains.shape[0]
    ts = _tile(S, ts)

    def body(x_ref, g_ref, *o_refs):
        xv = x_ref[...]
        y = xv * lax.rsqrt(jnp.mean(xv * xv, axis=-1, keepdims=True) + EPS)
        for i, o_ref in enumerate(o_refs):
            o_ref[...] = (y * g_ref[i:i + 1, :]).astype(o_ref.dtype)

    return pl.pallas_call(
        body, grid=(S // ts,), in_specs=[_row_spec(ts, D), _vec_spec(n, D)],
        out_specs=[_row_spec(ts, D)] * n, out_shape=[jax.ShapeDtypeStruct((S, D), BF16)] * n,
        compiler_params=_params(("parallel",)), name=name)(x, gains)


def _rms_bwd(name, x, gains, dys, res, *, ts=512):
    S, D = x.shape
    n = gains.shape[0]
    ts = _tile(S, ts)

    def body(x_ref, g_ref, *rest):
        dy_refs = rest[:n]
        res_ref, dx_ref, dg_ref = rest[n:]
        i = pl.program_id(0)
        xv = x_ref[...]
        r = lax.rsqrt(jnp.mean(xv * xv, axis=-1, keepdims=True) + EPS)
        xhat = xv * r
        dxhat = None
        dgs = []
        for k in range(n):
            dy = dy_refs[k][...].astype(F32)
            dgs.append(jnp.sum(dy * xhat, axis=0, keepdims=True))
            term = dy * g_ref[k:k + 1, :]
            dxhat = term if dxhat is None else dxhat + term
        dx_ref[...] = res_ref[...] + r * (dxhat - xhat * jnp.mean(dxhat * xhat, axis=-1, keepdims=True))
        dg = jnp.concatenate(dgs, axis=0) if n > 1 else dgs[0]

        @pl.when(i == 0)
        def _():
            dg_ref[...] = dg

        @pl.when(i > 0)
        def _():
            dg_ref[...] += dg

    return pl.pallas_call(
        body, grid=(S // ts,),
        in_specs=[_row_spec(ts, D), _vec_spec(n, D)] + [_row_spec(ts, D)] * (n + 1),
        out_specs=[_row_spec(ts, D), _vec_spec(n, D)],
        out_shape=[jax.ShapeDtypeStruct((S, D), F32), jax.ShapeDtypeStruct((n, D), F32)],
        compiler_params=_params(("arbitrary",)), name=name)(x, gains, *dys, res)


def _loss_head(h, gain, target, *, ts=512):
    S, D = h.shape
    ts = _tile(S, ts)

    def body(x_ref, g_ref, t_ref, dx_ref, dg_ref, loss_ref):
        i = pl.program_id(0)
        xv = x_ref[...]
        g = g_ref[...]
        r = lax.rsqrt(jnp.mean(xv * xv, axis=-1, keepdims=True) + EPS)
        xhat = xv * r
        err = xhat * g - t_ref[...]
        part = 0.5 * jnp.sum(jnp.mean(err * err, axis=-1, keepdims=True), axis=0, keepdims=True)
        dy = err * (1.0 / D)
        dg = jnp.sum(dy * xhat, axis=0, keepdims=True)
        dxhat = dy * g
        dx_ref[...] = r * (dxhat - xhat * jnp.mean(dxhat * xhat, axis=-1, keepdims=True))

        @pl.when(i == 0)
        def _():
            dg_ref[...] = dg
            loss_ref[...] = jnp.broadcast_to(part, loss_ref.shape)

        @pl.when(i > 0)
        def _():
            dg_ref[...] += dg
            loss_ref[...] += jnp.broadcast_to(part, loss_ref.shape)

    return pl.pallas_call(
        body, grid=(S // ts,),
        in_specs=[_row_spec(ts, D), _vec_spec(1, D), _row_spec(ts, D)],
        out_specs=[_row_spec(ts, D), _vec_spec(1, D), pl.BlockSpec((8, LANES), lambda i: (0, 0))],
        out_shape=[jax.ShapeDtypeStruct((S, D), F32), jax.ShapeDtypeStruct((1, D), F32),
                   jax.ShapeDtypeStruct((8, LANES), F32)],
        compiler_params=_params(("arbitrary",)), name="loss_head")(h, gain, target)


def _window_counts(t, w):
    return jnp.minimum(t + 1, w).astype(F32)


def _pool_fwd(x, gain, pool_w, scale, *, ts=512):
    S, D = x.shape
    dg = D // N_GROUPS
    ts = _tile(S, ts)
    per = ts // POOL_HALO

    def body(x_ref, xh_ref, g_ref, w_ref, sc_ref, h_ref, diff_ref):
        i = pl.program_id(0)
        g = g_ref[...]

        def norm(v):
            return v * lax.rsqrt(jnp.mean(v * v, axis=-1, keepdims=True) + EPS) * g

        xc = x_ref[...]
        hn_c = norm(xc)
        hn_h = norm(xh_ref[...]) * (i > 0).astype(F32)
        ext = jnp.concatenate([hn_h, hn_c], axis=0)
        t = i * ts + lax.broadcasted_iota(jnp.int32, (ts, 1), 0)
        for gi, w in enumerate(POOL_WINDOWS):
            cols = slice(gi * dg, (gi + 1) * dg)
            s = ext[:, cols]
            step = 1
            while step < w:
                s = s + pltpu.roll(s, step, 0)
                step *= 2
            mean = s[POOL_HALO:] * (1.0 / _window_counts(t, w))
            diff = (mean - hn_c[:, cols]).astype(BF16)
            diff_ref[:, cols] = diff
            mixed = jnp.dot(diff, w_ref[gi], preferred_element_type=F32)
            h_ref[:, cols] = xc[:, cols] + mixed * sc_ref[:, cols]

    return pl.pallas_call(
        body, grid=(S // ts,),
        in_specs=[_row_spec(ts, D),
                  pl.BlockSpec((POOL_HALO, D), lambda i: (jnp.maximum(i * per - 1, 0), 0)),
                  _vec_spec(1, D), pl.BlockSpec((N_GROUPS, dg, dg), lambda i: (0, 0, 0)), _vec_spec(1, D)],
        out_specs=[_row_spec(ts, D), _row_spec(ts, D)],
        out_shape=[jax.ShapeDtypeStruct((S, D), F32), jax.ShapeDtypeStruct((S, D), BF16)],
        compiler_params=_params(("parallel",)), name="pool_fwd")(x, x, gain, pool_w, scale)


def _pool_bwd(x, dh, diff, gain, pool_w, scale, *, ts=256):
    S, D = x.shape
    dg = D // N_GROUPS
    ts = _tile(S, ts)
    per = ts // POOL_HALO
    n_tiles = S // ts
    n_halo = S // POOL_HALO
    ext_rows = ts + POOL_HALO

    def body(x_ref, dh_ref, dhn_ref, diff_ref, g_ref, w_ref, sc_ref, dx_ref, dw_ref, dsc_ref, dgain_ref):
        i = pl.program_id(0)
        xc = x_ref[...]
        g = g_ref[...]
        r = lax.rsqrt(jnp.mean(xc * xc, axis=-1, keepdims=True) + EPS)
        xhat = xc * r
        dh_c = dh_ref[...]
        dh_n = dhn_ref[...] * (i < n_tiles - 1).astype(F32)
        dh_ext = jnp.concatenate([dh_c, dh_n], axis=0)
        t_ext = i * ts + lax.broadcasted_iota(jnp.int32, (ext_rows, 1), 0)
        d_hn_parts, dsc_parts = [], []
        for gi, w in enumerate(POOL_WINDOWS):
            cols = slice(gi * dg, (gi + 1) * dg)
            wg = w_ref[gi]
            dmix = (dh_ext[:, cols] * sc_ref[:, cols]).astype(BF16)
            d_diff = lax.dot_general(dmix, wg, _DIMS["nt"], preferred_element_type=F32)
            diff_c = diff_ref[:, cols]
            dwg = lax.dot_general(diff_c, dmix[:ts], _DIMS["tn"], preferred_element_type=F32)
            mixed = jnp.dot(diff_c, wg, preferred_element_type=F32)
            dsc_parts.append(jnp.sum(dh_c[:, cols] * mixed, axis=0, keepdims=True))
            e = d_diff * (1.0 / _window_counts(t_ext, w))
            step = 1
            while step < w:
                e = e + pltpu.roll(e, ext_rows - step, 0)
                step *= 2
            d_hn_parts.append(e[:ts] - d_diff[:ts])

            @pl.when(i == 0)
            def _():
                dw_ref[gi] = dwg

            @pl.when(i > 0)
            def _():
                dw_ref[gi] += dwg

        d_hn = jnp.concatenate(d_hn_parts, axis=1)
        dsc = jnp.concatenate(dsc_parts, axis=1)
        dgain = jnp.sum(d_hn * xhat, axis=0, keepdims=True)
        dxhat = d_hn * g
        dx_ref[...] = dh_c + r * (dxhat - xhat * jnp.mean(dxhat * xhat, axis=-1, keepdims=True))

        @pl.when(i == 0)
        def _():
            dsc_ref[...] = dsc
            dgain_ref[...] = dgain

        @pl.when(i > 0)
        def _():
            dsc_ref[...] += dsc
            dgain_ref[...] += dgain

    return pl.pallas_call(
        body, grid=(n_tiles,),
        in_specs=[_row_spec(ts, D), _row_spec(ts, D),
                  pl.BlockSpec((POOL_HALO, D), lambda i: (jnp.minimum((i + 1) * per, n_halo - 1), 0)),
                  _row_spec(ts, D), _vec_spec(1, D),
                  pl.BlockSpec((N_GROUPS, dg, dg), lambda i: (0, 0, 0)), _vec_spec(1, D)],
        out_specs=[_row_spec(ts, D), pl.BlockSpec((N_GROUPS, dg, dg), lambda i: (0, 0, 0)),
                   _vec_spec(1, D), _vec_spec(1, D)],
        out_shape=[jax.ShapeDtypeStruct((S, D), F32), jax.ShapeDtypeStruct((N_GROUPS, dg, dg), F32),
                   jax.ShapeDtypeStruct((1, D), F32), jax.ShapeDtypeStruct((1, D), F32)],
        compiler_params=_params(("arbitrary",)), name="pool_bwd")(x, dh, dh, diff, gain, pool_w, scale)


def _gate_fwd(f_raw, b_pad, *, ts=512):
    S = f_raw.shape[0]
    ts = _tile(S, ts)

    def body(f_ref, b_ref, c_ref, carry_ref):
        i = pl.program_id(0)

        @pl.when(i == 0)
        def _():
            carry_ref[...] = jnp.zeros_like(carry_ref)

        z = f_ref[...] + b_ref[...]
        v = jnp.minimum(z, 0.0) - jnp.log1p(jnp.exp(-jnp.abs(z)))
        row = lax.broadcasted_iota(jnp.int32, (ts, 1), 0)
        step = 1
        while step < ts:
            v = v + jnp.where(row >= step, pltpu.roll(v, step, 0), 0.0)
            step *= 2
        out = v + carry_ref[0:1, :]
        c_ref[...] = out
        carry_ref[...] = jnp.broadcast_to(out[ts - 1:ts, :], carry_ref.shape)

    return pl.pallas_call(
        body, grid=(S // ts,),
        in_specs=[pl.BlockSpec((ts, LANES), lambda i: (i, 0)), pl.BlockSpec((1, LANES), lambda i: (0, 0))],
        out_specs=pl.BlockSpec((ts, LANES), lambda i: (i, 0)),
        out_shape=jax.ShapeDtypeStruct((S, LANES), F32),
        scratch_shapes=[pltpu.VMEM((8, LANES), F32)],
        compiler_params=_params(("arbitrary",)), name="gate_fwd")(f_raw, b_pad)


def _gate_bwd(dc, f_raw, b_pad, *, ts=512):
    S = f_raw.shape[0]
    ts = _tile(S, ts)
    n = S // ts

    def body(dc_ref, f_ref, b_ref, df_ref, db_ref, carry_ref):
        i = pl.program_id(0)

        @pl.when(i == 0)
        def _():
            carry_ref[...] = jnp.zeros_like(carry_ref)

        v = dc_ref[...]
        row = lax.broadcasted_iota(jnp.int32, (ts, 1), 0)
        step = 1
        while step < ts:
            v = v + jnp.where(row < ts - step, pltpu.roll(v, ts - step, 0), 0.0)
            step *= 2
        d_logf = v + carry_ref[0:1, :]
        carry_ref[...] = jnp.broadcast_to(d_logf[0:1, :], carry_ref.shape)
        z = f_ref[...] + b_ref[...]
        df = d_logf / (1.0 + jnp.exp(z))
        df_ref[...] = df
        db = jnp.sum(df, axis=0, keepdims=True)

        @pl.when(i == 0)
        def _():
            db_ref[...] = db

        @pl.when(i > 0)
        def _():
            db_ref[...] += db

    rev = lambda i: (n - 1 - i, 0)
    return pl.pallas_call(
        body, grid=(n,),
        in_specs=[pl.BlockSpec((ts, LANES), rev), pl.BlockSpec((ts, LANES), rev),
                  pl.BlockSpec((1, LANES), lambda i: (0, 0))],
        out_specs=[pl.BlockSpec((ts, LANES), rev), pl.BlockSpec((1, LANES), lambda i: (0, 0))],
        out_shape=[jax.ShapeDtypeStruct((S, LANES), F32), jax.ShapeDtypeStruct((1, LANES), F32)],
        scratch_shapes=[pltpu.VMEM((8, LANES), F32)],
        compiler_params=_params(("arbitrary",)), name="gate_bwd")(dc, f_raw, b_pad)


def _col_layout(a_sh, n_heads):
    S = a_sh.shape[0]
    return jnp.broadcast_to(a_sh[:, :n_heads].T[:, :, None], (n_heads, S, LANES))


def _row_layout(a_hs, t):
    n_heads, S = a_hs.shape
    return a_hs.reshape(n_heads, S // t, 1, t)


def _attn_fwd(q, kv, c_col, c_row, *, t=512):
    S, D = q.shape
    H = D // HEAD_DIM
    t = _tile(S, t)
    nb = S // t
    scale = HEAD_DIM ** -0.5

    def body(q_ref, k_ref, v_ref, cc_ref, cr_ref, o_ref, lse_ref):
        i = pl.program_id(1)
        qv = q_ref[...]
        cq = cc_ref[:, 0:1]

        def step(j, carry, masked):
            m, l, acc = carry
            r0 = pl.multiple_of(j * t, t)
            kb = k_ref[pl.ds(r0, t), :]
            vb = v_ref[pl.ds(r0, t), :]
            s = lax.dot_general(qv, kb, _DIMS["nt"], preferred_element_type=F32) * scale + (cq - cr_ref[j])
            if masked:
                rr = lax.broadcasted_iota(jnp.int32, (t, t), 0)
                cc = lax.broadcasted_iota(jnp.int32, (t, t), 1)
                s = jnp.where(cc <= rr, s, NEG_INF)
            m_new = jnp.maximum(m, jnp.max(s, axis=-1, keepdims=True))
            alpha = jnp.exp(m - m_new)
            p = jnp.exp(s - m_new)
            l = alpha * l + jnp.sum(p, axis=-1, keepdims=True)
            acc = alpha * acc + jnp.dot(p.astype(BF16), vb, preferred_element_type=F32)
            return m_new, l, acc

        init = (jnp.full((t, 1), NEG_INF, F32), jnp.zeros((t, 1), F32), jnp.zeros((t, HEAD_DIM), F32))
        carry = lax.fori_loop(0, i, lambda j, c: step(j, c, False), init)
        m, l, acc = step(i, carry, True)
        o_ref[...] = (acc / l).astype(o_ref.dtype)
        lse_ref[...] = jnp.broadcast_to(m + jnp.log(l), lse_ref.shape)

    return pl.pallas_call(
        body, grid=(H, nb),
        in_specs=[pl.BlockSpec((t, HEAD_DIM), lambda h, i: (i, h)),
                  pl.BlockSpec((S, HEAD_DIM), lambda h, i: (0, h)),
                  pl.BlockSpec((S, HEAD_DIM), lambda h, i: (0, H + h)),
                  pl.BlockSpec((None, t, LANES), lambda h, i: (h, i, 0)),
                  pl.BlockSpec((None, nb, 1, t), lambda h, i: (h, 0, 0, 0))],
        out_specs=[pl.BlockSpec((t, HEAD_DIM), lambda h, i: (i, h)),
                   pl.BlockSpec((None, t, LANES), lambda h, i: (h, i, 0))],
        out_shape=[jax.ShapeDtypeStruct((S, D), BF16), jax.ShapeDtypeStruct((H, S, LANES), F32)],
        compiler_params=_params(("parallel", "arbitrary")), name="attn_fwd")(q, kv, kv, c_col, c_row)


def _attn_delta(o, do, *, t=512):
    S, D = o.shape
    H = D // HEAD_DIM
    t = _tile(S, t)

    def body(o_ref, do_ref, d_ref):
        d = jnp.sum(o_ref[...].astype(F32) * do_ref[...].astype(F32), axis=-1, keepdims=True)
        d_ref[...] = jnp.broadcast_to(d, d_ref.shape)

    spec = pl.BlockSpec((t, HEAD_DIM), lambda h, i: (i, h))
    return pl.pallas_call(
        body, grid=(H, S // t), in_specs=[spec, spec],
        out_specs=pl.BlockSpec((None, t, LANES), lambda h, i: (h, i, 0)),
        out_shape=jax.ShapeDtypeStruct((H, S, LANES), F32),
        compiler_params=_params(("parallel", "parallel")), name="attn_delta")(o, do)


def _attn_dq(q, kv, do, c_col, c_row, lse_col, delta_col, *, t=512):
    S, D = q.shape
    H = D // HEAD_DIM
    t = _tile(S, t)
    nb = S // t
    scale = HEAD_DIM ** -0.5

    def body(q_ref, k_ref, v_ref, do_ref, cc_ref, cr_ref, lse_ref, dl_ref, dq_ref, dcq_ref):
        i = pl.program_id(1)
        qv = q_ref[...]
        dov = do_ref[...]
        cq = cc_ref[:, 0:1]
        lse = lse_ref[:, 0:1]
        delta = dl_ref[:, 0:1]

        def step(j, carry, masked):
            acc, rowsum = carry
            r0 = pl.multiple_of(j * t, t)
            kb = k_ref[pl.ds(r0, t), :]
            vb = v_ref[pl.ds(r0, t), :]
            s = lax.dot_general(qv, kb, _DIMS["nt"], preferred_element_type=F32) * scale + (cq - cr_ref[j])
            p = jnp.exp(s - lse)
            if masked:
                rr = lax.broadcasted_iota(jnp.int32, (t, t), 0)
                cc = lax.broadcasted_iota(jnp.int32, (t, t), 1)
                p = jnp.where(cc <= rr, p, 0.0)
            dp = lax.dot_general(dov, vb, _DIMS["nt"], preferred_element_type=F32)
            ds = p * (dp - delta)
            rowsum = rowsum + jnp.sum(ds, axis=-1, keepdims=True)
            return acc + jnp.dot(ds.astype(BF16), kb, preferred_element_type=F32), rowsum

        init = (jnp.zeros((t, HEAD_DIM), F32), jnp.zeros((t, 1), F32))
        carry = lax.fori_loop(0, i, lambda j, a: step(j, a, False), init)
        acc, rowsum = step(i, carry, True)
        dq_ref[...] = (acc * scale).astype(dq_ref.dtype)
        dcq_ref[...] = jnp.broadcast_to(rowsum, dcq_ref.shape)

    qspec = pl.BlockSpec((t, HEAD_DIM), lambda h, i: (i, h))
    colspec = pl.BlockSpec((None, t, LANES), lambda h, i: (h, i, 0))
    return pl.pallas_call(
        body, grid=(H, nb),
        in_specs=[qspec,
                  pl.BlockSpec((S, HEAD_DIM), lambda h, i: (0, h)),
                  pl.BlockSpec((S, HEAD_DIM), lambda h, i: (0, H + h)),
                  qspec, colspec,
                  pl.BlockSpec((None, nb, 1, t), lambda h, i: (h, 0, 0, 0)),
                  colspec, colspec],
        out_specs=[qspec, colspec],
        out_shape=[jax.ShapeDtypeStruct((S, D), BF16), jax.ShapeDtypeStruct((H, S, LANES), F32)],
        compiler_params=_params(("parallel", "arbitrary")), name="attn_dq")(
            q, kv, kv, do, c_col, c_row, lse_col, delta_col)


def _attn_dkv(q, kv, do, c_col, c_row, lse_row, delta_row, *, t=512):
    S, D = q.shape
    H = D // HEAD_DIM
    t = _tile(S, t)
    nb = S // t
    scale = HEAD_DIM ** -0.5

    def body(q_ref, do_ref, k_ref, v_ref, cc_ref, cr_ref, lse_ref, dl_ref, dk_ref, dv_ref, dc_ref):
        j = pl.program_id(1)
        kb = k_ref[...]
        vb = v_ref[...]
        ck = cc_ref[:, 0:1]

        def step(i, carry, masked):
            dk, dv, dc = carry
            r0 = pl.multiple_of(i * t, t)
            qb = q_ref[pl.ds(r0, t), :]
            dob = do_ref[pl.ds(r0, t), :]
            s = lax.dot_general(kb, qb, _DIMS["nt"], preferred_element_type=F32) * scale + (cr_ref[i] - ck)
            p = jnp.exp(s - lse_ref[i])
            if masked:
                rr = lax.broadcasted_iota(jnp.int32, (t, t), 0)
                cc = lax.broadcasted_iota(jnp.int32, (t, t), 1)
                p = jnp.where(rr <= cc, p, 0.0)
            dv = dv + jnp.dot(p.astype(BF16), dob, preferred_element_type=F32)
            dp = lax.dot_general(vb, dob, _DIMS["nt"], preferred_element_type=F32)
            ds = p * (dp - dl_ref[i])
            dc = dc - jnp.sum(ds, axis=-1, keepdims=True)
            dk = dk + jnp.dot(ds.astype(BF16), qb, preferred_element_type=F32)
            return dk, dv, dc

        init = (jnp.zeros((t, HEAD_DIM), F32), jnp.zeros((t, HEAD_DIM), F32), jnp.zeros((t, 1), F32))
        carry = step(j, init, True)
        dk, dv, dc = lax.fori_loop(j + 1, nb, lambda i, c: step(i, c, False), carry)
        dk_ref[...] = (dk * scale).astype(dk_ref.dtype)
        dv_ref[...] = dv.astype(dv_ref.dtype)
        dc_ref[...] = jnp.broadcast_to(dc, dc_ref.shape)

    kspec = pl.BlockSpec((t, HEAD_DIM), lambda h, j: (j, h))
    rowspec = pl.BlockSpec((None, nb, 1, t), lambda h, j: (h, 0, 0, 0))
    return pl.pallas_call(
        body, grid=(H, nb),
        in_specs=[pl.BlockSpec((S, HEAD_DIM), lambda h, j: (0, h)),
                  pl.BlockSpec((S, HEAD_DIM), lambda h, j: (0, h)),
                  kspec,
                  pl.BlockSpec((t, HEAD_DIM), lambda h, j: (j, H + h)),
                  pl.BlockSpec((None, t, LANES), lambda h, j: (h, j, 0)),
                  rowspec, rowspec, rowspec],
        out_specs=[kspec, kspec, pl.BlockSpec((None, t, LANES), lambda h, j: (h, j, 0))],
        out_shape=[jax.ShapeDtypeStruct((S, D), BF16), jax.ShapeDtypeStruct((S, D), BF16),
                   jax.ShapeDtypeStruct((H, S, LANES), F32)],
        compiler_params=_params(("parallel", "arbitrary")), name="attn_dkv")(
            q, do, kv, kv, c_col, c_row, lse_row, delta_row)


def _adamw(name, parts, w, m, v, layer=None, *, tr=128):
    P, R, C = parts.shape
    tr = _tile(R, tr)

    def body(p_ref, w_ref, m_ref, v_ref, g_ref, d_ref, nm_ref, nv_ref):
        g = p_ref[0]
        for k in range(1, P):
            g = g + p_ref[k]
        wv = w_ref[...]
        nm = ADAM_B1 * m_ref[...] + (1.0 - ADAM_B1) * g
        nv = ADAM_B2 * v_ref[...] + (1.0 - ADAM_B2) * (g * g)
        m_hat = nm / (1.0 - ADAM_B1 ** ADAM_STEP)
        v_hat = nv / (1.0 - ADAM_B2 ** ADAM_STEP)
        g_ref[...] = g
        d_ref[...] = -ADAM_LR * (m_hat / (jnp.sqrt(v_hat) + ADAM_EPS) + ADAM_WD * wv)
        nm_ref[...] = nm
        nv_ref[...] = nv

    if layer is None:
        wspec = pl.BlockSpec((tr, C), lambda i: (i, 0))
    else:
        wspec = pl.BlockSpec((None, tr, C), lambda i: (layer, i, 0))
    ospec = pl.BlockSpec((tr, C), lambda i: (i, 0))
    return pl.pallas_call(
        body, grid=(R // tr,),
        in_specs=[pl.BlockSpec((P, tr, C), lambda i: (0, i, 0)), wspec, wspec, wspec],
        out_specs=[ospec] * 4, out_shape=[jax.ShapeDtypeStruct((R, C), F32)] * 4,
        compiler_params=_params(("parallel",)), name=name)(parts, w, m, v)


_ANY = pl.BlockSpec(memory_space=pl.ANY)


def _position():
    return lax.axis_index("x"), lax.axis_index("y"), lax.axis_index("c")


def _all_gather(shards):
    n = len(shards)

    def body(*refs):
        ins, outs = refs[:n], refs[n:2 * n]
        send_sems, recv_sems, local_sems = refs[2 * n:]
        x, y, c = _position()
        me, sibling = (x, y, c), (x, y, 1 - c)
        chips = [(1 - x, y), (x, 1 - y), (1 - x, 1 - y)]

        def slot(a, px, py, pc):
            return outs[a].at[4 * px + 2 * py + pc]

        def copy(a, k, block, to, src=None):
            return pltpu.make_async_remote_copy(
                src_ref=slot(a, *block) if src is None else src, dst_ref=slot(a, *block),
                send_sem=send_sems.at[a, k], recv_sem=recv_sems.at[a, k], device_id=to, device_id_type=MESH)

        mine = [pltpu.make_async_copy(ins[a], slot(a, *me), local_sems.at[a]) for a in range(n)]
        first = []
        for a in range(n):
            mine[a].start()
            first.append(copy(a, 0, me, sibling, src=ins[a]))
            first += [copy(a, 1 + j, me, (*chip, c), src=ins[a]) for j, chip in enumerate(chips)]
        for cp in first:
            cp.start()
        passed = []
        for a in range(n):
            for j, chip in enumerate(chips):
                copy(a, 1 + j, (*chip, c), me).wait_recv()
                fwd = copy(a, 4 + j, (*chip, c), sibling)
                fwd.start()
                passed.append(fwd)
        for a in range(n):
            copy(a, 0, sibling, me).wait_recv()
            for j, chip in enumerate(chips):
                copy(a, 4 + j, (*chip, 1 - c), me).wait_recv()
        for cp in first + passed:
            cp.wait_send()
        for cp in mine:
            cp.wait()

    return pl.pallas_call(
        body, in_specs=[_ANY] * n, out_specs=[_ANY] * n,
        out_shape=[jax.ShapeDtypeStruct((N_DEV, *s.shape), s.dtype) for s in shards],
        scratch_shapes=[pltpu.SemaphoreType.DMA((n, 7)), pltpu.SemaphoreType.DMA((n, 7)),
                        pltpu.SemaphoreType.DMA((n,))],
        name="all_gather")(*shards)


def _exchange_sibling(grads):
    n = len(grads)

    def body(*refs):
        ins, outs = refs[:n], refs[n:2 * n]
        send_sems, recv_sems = refs[2 * n:]
        x, y, c = _position()
        copies = []
        for a in range(n):
            for q in range(N_CHIP):
                copies.append(pltpu.make_async_remote_copy(
                    src_ref=ins[a].at[2 * q + (1 - c)], dst_ref=outs[a].at[q],
                    send_sem=send_sems.at[a, q], recv_sem=recv_sems.at[a, q],
                    device_id=(x, y, 1 - c), device_id_type=MESH))
        for cp in copies:
            cp.start()
        for cp in copies:
            cp.wait()

    return pl.pallas_call(
        body, in_specs=[_ANY] * n, out_specs=[_ANY] * n,
        out_shape=[jax.ShapeDtypeStruct((N_CHIP, *g.shape[1:]), g.dtype) for g in grads],
        scratch_shapes=[pltpu.SemaphoreType.DMA((n, N_CHIP)), pltpu.SemaphoreType.DMA((n, N_CHIP))],
        name="rs_exchange_sibling")(*grads)


def _add_sibling(name, grad, got, core, *, tr=256):
    _, R, C = grad.shape
    tr = _tile(R, tr)
    g4 = grad.reshape(N_CHIP, 2, R, C)

    def body(core_ref, a_ref, b_ref, o_ref):
        o_ref[...] = a_ref[...] + b_ref[...]

    spec = pl.BlockSpec((None, tr, C), lambda q, i, core_ref: (q, i, 0))
    return pl.pallas_call(
        body,
        grid_spec=pltpu.PrefetchScalarGridSpec(
            num_scalar_prefetch=1, grid=(N_CHIP, R // tr),
            in_specs=[pl.BlockSpec((None, None, tr, C), lambda q, i, core_ref: (q, core_ref[0], i, 0)), spec],
            out_specs=spec),
        out_shape=jax.ShapeDtypeStruct((N_CHIP, R, C), grad.dtype),
        compiler_params=_params(("parallel", "parallel")), name=name)(core, g4, got)


def _exchange_chips(parts):
    n = len(parts)

    def body(*refs):
        ins, outs = refs[:n], refs[n:2 * n]
        send_sems, recv_sems, local_sems = refs[2 * n:]
        x, y, c = _position()
        my_chip = 2 * x + y
        chips = [(1 - x, y), (x, 1 - y), (1 - x, 1 - y)]
        local, remote = [], []
        for a in range(n):
            local.append(pltpu.make_async_copy(ins[a].at[my_chip], outs[a].at[my_chip], local_sems.at[a]))
            for k, (qx, qy) in enumerate(chips):
                remote.append(pltpu.make_async_remote_copy(
                    src_ref=ins[a].at[2 * qx + qy], dst_ref=outs[a].at[my_chip],
                    send_sem=send_sems.at[a, k], recv_sem=recv_sems.at[a, k],
                    device_id=(qx, qy, c), device_id_type=MESH))
        for cp in local + remote:
            cp.start()
        for a in range(n):
            for k, (qx, qy) in enumerate(chips):
                pltpu.make_async_remote_copy(
                    src_ref=ins[a].at[my_chip], dst_ref=outs[a].at[2 * qx + qy],
                    send_sem=send_sems.at[a, k], recv_sem=recv_sems.at[a, k],
                    device_id=(qx, qy, c), device_id_type=MESH).wait_recv()
        for cp in remote:
            cp.wait_send()
        for cp in local:
            cp.wait()

    return pl.pallas_call(
        body, in_specs=[_ANY] * n, out_specs=[_ANY] * n,
        out_shape=[jax.ShapeDtypeStruct(p.shape, p.dtype) for p in parts],
        scratch_shapes=[pltpu.SemaphoreType.DMA((n, 3)), pltpu.SemaphoreType.DMA((n, 3)),
                        pltpu.SemaphoreType.DMA((n,))],
        name="rs_exchange_chips")(*parts)


def _all_reduce_small(vec):
    R = vec.shape[0]

    def body(v_ref, o_ref, buf_ref, send_sems, recv_sems):
        x, y, c = _position()
        me = 4 * x + 2 * y + c
        buf_ref[me] = v_ref[...]
        copies = []
        for k in range(1, N_DEV):
            peer = (x ^ (k >> 2), y ^ ((k >> 1) & 1), c ^ (k & 1))
            copies.append(pltpu.make_async_remote_copy(
                src_ref=buf_ref.at[me], dst_ref=buf_ref.at[me], send_sem=send_sems.at[k], recv_sem=recv_sems.at[k],
                device_id=peer, device_id_type=MESH))
        for cp in copies:
            cp.start()
        for cp in copies:
            cp.wait()
        total = buf_ref[0]
        for d in range(1, N_DEV):
            total = total + buf_ref[d]
        o_ref[...] = total

    vm = pl.BlockSpec(memory_space=pltpu.VMEM)
    return pl.pallas_call(
        body, in_specs=[vm], out_specs=vm, out_shape=jax.ShapeDtypeStruct((R, LANES), F32),
        scratch_shapes=[pltpu.VMEM((N_DEV, R, LANES), F32), pltpu.SemaphoreType.DMA((N_DEV,)),
                        pltpu.SemaphoreType.DMA((N_DEV,))],
        name="all_reduce_small")(vec)


def _local_grads(x2, tgt, norm_mix, norm_mlp, norm_kv, norm_out, b_pad, pw, scale_full, w_kv, w_f, wq, wo, g_up, g_down):
    S, D = x2.shape
    H = D // HEAD_DIM

    h1, diff = _pool_fwd(x2, norm_mix[0:1], pw, scale_full)
    (hn_m0,) = _rms_fwd("rms_mlp0", h1, norm_mlp[0:1])
    u0, uu0, h2 = _mlp_fwd("l0", hn_m0, h1, g_up, g_down, 0)

    gains_kv_q = jnp.stack([norm_kv, norm_mix[1]])
    hkv, hn_q = _rms_fwd("rms_kv_q", h2, gains_kv_q)
    kv = _mm_nn("kv_proj", hkv, w_kv, out_dtype=BF16)
    f_raw = _mm_nn("f_proj", hkv, w_f, out_dtype=F32)
    c_sh = _gate_fwd(f_raw, b_pad)
    t_attn = _tile(S, 512)
    c_col = _col_layout(c_sh, H)
    c_row = _row_layout(c_sh[:, :H].T, t_attn)
    q = _mm_nn("q_proj", hn_q, wq, out_dtype=BF16)
    o, lse_col = _attn_fwd(q, kv, c_col, c_row, t=t_attn)
    h3 = _mm_nn("o_proj", o, wo, out_dtype=F32, residual=h2)
    (hn_m1,) = _rms_fwd("rms_mlp1", h3, norm_mlp[1:2])
    u1, uu1, h4 = _mlp_fwd("l1", hn_m1, h3, g_up, g_down, 1)
    dh4, d_norm_out, loss_part = _loss_head(h4, norm_out.reshape(1, D), tgt)

    d_hn, dw_up1, dw_down1 = _mlp_bwd("l1", dh4, hn_m1, u1, uu1, g_up, g_down, 1)
    dh3, d_norm_mlp1 = _rms_bwd("rms_mlp1_bwd", h3, norm_mlp[1:2], [d_hn], dh4)

    do = _mm_nt("o_proj_dx", dh3, wo, out_dtype=BF16)
    dw_o = _mm_tn("o_proj_dw", o, dh3)
    delta_col = _attn_delta(o, do, t=t_attn)
    lse_row = _row_layout(lse_col[:, :, 0], t_attn)
    delta_row = _row_layout(delta_col[:, :, 0], t_attn)
    dq, dcq_col = _attn_dq(q, kv, do, c_col, c_row, lse_col, delta_col, t=t_attn)
    dk, dv, dc_col = _attn_dkv(q, kv, do, c_col, c_row, lse_row, delta_row, t=t_attn)
    dw_q = _mm_tn("q_proj_dw", hn_q, dq)
    d_hn_q = _mm_nt("q_proj_dx", dq, wq, out_dtype=F32)

    dc_sh = jnp.pad((dc_col[:, :, 0] + dcq_col[:, :, 0]).T, ((0, 0), (0, LANES - H)))
    d_f, d_b = _gate_bwd(dc_sh, f_raw, b_pad)
    dw_k = _mm_tn("k_proj_dw", hkv, dk)
    dw_v = _mm_tn("v_proj_dw", hkv, dv)
    dw_f = _mm_tn("f_proj_dw", hkv, d_f)
    d_hkv = _mm_nt("k_proj_dx", dk, w_kv, b_cols=(0, D), out_dtype=F32)
    d_hkv = _mm_nt("v_proj_dx", dv, w_kv, b_cols=(D, D), out_dtype=F32, residual=d_hkv)
    d_hkv = _mm_nt("f_proj_dx", d_f, w_f, out_dtype=F32, residual=d_hkv)
    dh2, d_norm_kv_q = _rms_bwd("rms_kv_q_bwd", h2, gains_kv_q, [d_hkv, d_hn_q], dh3)

    d_hn, dw_up0, dw_down0 = _mlp_bwd("l0", dh2, hn_m0, u0, uu0, g_up, g_down, 0)
    dh1, d_norm_mlp0 = _rms_bwd("rms_mlp0_bwd", h1, norm_mlp[0:1], [d_hn], dh2)
    grad_x, dw_pool, d_scale, d_norm_mix0 = _pool_bwd(x2, dh1, diff, norm_mix[0:1], pw, scale_full)
    return (loss_part, grad_x, d_norm_mix0, d_norm_kv_q, d_norm_mlp0, d_norm_mlp1, d_norm_out, d_scale, d_b, dw_pool,
            dw_k, dw_v, dw_f, dw_q, dw_o, dw_up0, dw_up1, dw_down0, dw_down1)


def _rows(a):
    flat = a.reshape(-1)
    pad = (-flat.shape[0]) % LANES
    if pad:
        flat = jnp.pad(flat, (0, pad))
    return flat.reshape(-1, LANES)


def kernel(x, norm_mix, norm_mlp, pool_w, pool_scale, norm_kv, w_kvf, b_f, w_q, w_o, w_up, w_down, norm_out, loss_target, m_norm_mix, m_norm_mlp, m_pool_w, m_pool_scale, m_norm_kv, m_w_kvf, m_b_f, m_w_q, m_w_o, m_w_up, m_w_down, m_norm_out, v_norm_mix, v_norm_mlp, v_pool_w, v_pool_scale, v_norm_kv, v_w_kvf, v_b_f, v_w_q, v_w_o, v_w_up, v_w_down, v_norm_out):
    _, S, D = x.shape
    H = D // HEAD_DIM
    dg = D // N_GROUPS
    n_kvf = 2 * D + H
    kvf_b = w_kvf.shape[1]
    fb = w_up.shape[2]
    F = N_DEV * fb
    ps_b = pool_scale.shape[1]
    xi, yi, ci = _position()
    my_block = 4 * xi + 2 * yi + ci

    g_pool, g_kvf, g_q, g_o, g_up, g_down, g_scale = _all_gather(
        [pool_w.astype(BF16), w_kvf.astype(BF16), w_q.astype(BF16), w_o.astype(BF16), w_up.astype(BF16),
         w_down.astype(BF16), pool_scale])
    pw = g_pool[:, 0].transpose(1, 0, 2, 3).reshape(N_GROUPS, dg, dg)
    wkvf = g_kvf.transpose(1, 0, 2).reshape(D, n_kvf)
    w_kv = wkvf[:, :2 * D]
    w_f = jnp.pad(wkvf[:, 2 * D:], ((0, 0), (0, LANES - H)))
    wq = g_q.reshape(D, D)
    wo = g_o.reshape(D, D)
    scale_full = g_scale.reshape(1, D)
    b_pad = jnp.pad(b_f, (0, LANES - H)).reshape(1, LANES)

    (loss_part, grad_x, d_norm_mix0, d_norm_kv_q, d_norm_mlp0, d_norm_mlp1, d_norm_out, d_scale, d_b, dw_pool, dw_k,
     dw_v, dw_f, dw_q, dw_o, dw_up0, dw_up1, dw_down0, dw_down1) = _local_grads(
        x.reshape(S, D), loss_target.reshape(S, D), norm_mix, norm_mlp, norm_kv, norm_out, b_pad, pw, scale_full,
        w_kv, w_f, wq, wo, g_up, g_down)

    dw_kvf = jnp.concatenate([dw_k, dw_v, dw_f[:, :H]], axis=1)
    full = [
        dw_pool.reshape(N_GROUPS, N_DEV, dg // N_DEV, dg).transpose(1, 0, 2, 3).reshape(N_DEV, N_GROUPS * dg // N_DEV, dg),
        dw_kvf.reshape(D, N_DEV, kvf_b).transpose(1, 0, 2),
        dw_q.reshape(N_DEV, D // N_DEV, D),
        dw_o.reshape(N_DEV, D // N_DEV, D),
        dw_up0, dw_up1,
        dw_down0.reshape(N_DEV, fb, D), dw_down1.reshape(N_DEV, fb, D),
    ]
    got = _exchange_sibling(full)
    core = ci.astype(jnp.int32).reshape(1)
    names = ["pool", "kvf", "q", "o", "up0", "up1", "down0", "down1"]
    chip_parts = [_add_sibling(f"rs_add_{nm}", g, r, core) for nm, g, r in zip(names, full, got)]
    parts = _exchange_chips(chip_parts)

    small = jnp.concatenate([
        _rows(jnp.concatenate([d_norm_mix0, d_norm_kv_q[1:2]], axis=0)),
        _rows(jnp.concatenate([d_norm_mlp0, d_norm_mlp1], axis=0)),
        _rows(d_norm_kv_q[0:1]),
        _rows(d_norm_out),
        _rows(d_scale),
        d_b,
        jnp.pad(loss_part[0:1, 0:1], ((0, 0), (0, LANES - 1))),
    ], axis=0)
    n_small = small.shape[0]
    small = jnp.pad(small, ((0, (-n_small) % 8), (0, 0)))
    total = _all_reduce_small(small)
    rd = D // LANES
    loss = total[7 * rd + 1, 0]
    g_scale_mine = lax.dynamic_slice(total[6 * rd:7 * rd].reshape(D), (my_block * ps_b,), (ps_b,))

    def pack(nm_, nl_, kv_, out_, bf_, ps_):
        return jnp.concatenate([_rows(nm_), _rows(nl_), _rows(kv_), _rows(out_), _rows(bf_), _rows(ps_)], axis=0)

    g_small = jnp.concatenate([total[:6 * rd], total[7 * rd:7 * rd + 1], _rows(g_scale_mine)], axis=0)
    w_small = pack(norm_mix, norm_mlp, norm_kv, norm_out, b_f, pool_scale)
    m_small = pack(m_norm_mix, m_norm_mlp, m_norm_kv, m_norm_out, m_b_f, m_pool_scale)
    v_small = pack(v_norm_mix, v_norm_mlp, v_norm_kv, v_norm_out, v_b_f, v_pool_scale)
    rs = g_small.shape[0]
    padr = (-rs) % 8
    pad8 = lambda a: jnp.pad(a, ((0, padr), (0, 0)))
    small_out = _adamw("adamw_small", pad8(g_small)[None], pad8(w_small), pad8(m_small),
                       jnp.pad(v_small, ((0, padr), (0, 0)), constant_values=1.0), tr=rs + padr)

    def unpack(a):
        o0 = 0
        res = []
        for shape in [(2, D), (2, D), (D,), (D,)]:
            nr = (2 * rd) if len(shape) == 2 else rd
            res.append(a[o0:o0 + nr].reshape(shape))
            o0 += nr
        res.append(a[o0, :H])
        res.append(a[o0 + 1:o0 + 1 + ps_b // LANES].reshape(1, ps_b))
        return res

    small_res = [unpack(a) for a in small_out]

    p_pool, p_kvf, p_q, p_o, p_up0, p_up1, p_down0, p_down1 = parts
    r_pool = _adamw("adamw_pool", p_pool, pool_w.reshape(-1, dg), m_pool_w.reshape(-1, dg), v_pool_w.reshape(-1, dg))
    r_kvf = _adamw("adamw_kvf", p_kvf, w_kvf, m_w_kvf, v_w_kvf)
    r_q = _adamw("adamw_q", p_q, w_q[0], m_w_q[0], v_w_q[0])
    r_o = _adamw("adamw_o", p_o, w_o[0], m_w_o[0], v_w_o[0])
    r_up = [_adamw(f"adamw_up{l}", p, w_up, m_w_up, v_w_up, layer=l) for l, p in enumerate([p_up0, p_up1])]
    r_down = [_adamw(f"adamw_down{l}", p, w_down, m_w_down, v_w_down, layer=l) for l, p in enumerate([p_down0, p_down1])]

    def big(kind):
        sm = small_res[kind]
        return [
            sm[0], sm[1],
            r_pool[kind].reshape(pool_w.shape),
            sm[5],
            sm[2],
            r_kvf[kind],
            sm[4],
            r_q[kind][None], r_o[kind][None],
            jnp.stack([r_up[0][kind], r_up[1][kind]]),
            jnp.stack([r_down[0][kind], r_down[1][kind]]),
            sm[3],
        ]

    return (loss, grad_x.reshape(x.shape), *big(0), *big(1), *big(2), *big(3))
```

```python
import functools

import jax
import jax.numpy as jnp
from jax import lax
from jax.experimental import pallas as pl
from jax.experimental.pallas import tpu as pltpu

F32 = jnp.float32
BF16 = jnp.bfloat16
MESH = pl.DeviceIdType.MESH

N_DEV = 8
N_CHIP = 4
EPS = 1e-6
NEG_INF = -1e30
POOL_WINDOWS = (2, 4, 8, 16)
N_GROUPS = len(POOL_WINDOWS)
POOL_HALO = 16
HEAD_DIM = 128
LANES = 128

ADAM_LR = 0.001
ADAM_B1 = 0.9
ADAM_B2 = 0.999
ADAM_EPS = 1e-08
ADAM_WD = 0.01
ADAM_STEP = 10

VMEM_LIMIT = 56 * 1024 * 1024


def _tile(n, want):
    t = min(n, want)
    assert n % t == 0, (n, want)
    return t


def _params(sem, vmem=VMEM_LIMIT):
    return pltpu.CompilerParams(dimension_semantics=sem, vmem_limit_bytes=vmem)


_DIMS = {
    "nn": (((1,), (0,)), ((), ())),
    "nt": (((1,), (1,)), ((), ())),
    "tn": (((0,), (0,)), ((), ())),
}


def _matmul(name, a, b, *, mode, grid, tm, tn, a_spec, b_spec, outs, extras=(), extra_specs=(), epilogue=None):
    nk = grid[2]
    n_extra = len(extras)
    n_out = len(outs)
    dn = _DIMS[mode]

    def body(a_ref, b_ref, *rest):
        extra_refs = rest[:n_extra]
        out_refs = rest[n_extra:n_extra + n_out]
        acc_ref = rest[n_extra + n_out]
        k = pl.program_id(2)

        @pl.when(k == 0)
        def _():
            acc_ref[...] = jnp.zeros_like(acc_ref)

        acc_ref[...] += lax.dot_general(a_ref[...].astype(BF16), b_ref[...].astype(BF16), dn, preferred_element_type=F32)

        @pl.when(k == nk - 1)
        def _():
            acc = acc_ref[...]
            vals = (acc,) if epilogue is None else epilogue(acc, *[r[...] for r in extra_refs])
            for o_ref, val in zip(out_refs, vals):
                o_ref[...] = val.astype(o_ref.dtype)

    res = pl.pallas_call(
        body,
        grid=grid,
        in_specs=[a_spec, b_spec, *extra_specs],
        out_specs=[o[2] for o in outs],
        out_shape=[jax.ShapeDtypeStruct(o[0], o[1]) for o in outs],
        scratch_shapes=[pltpu.VMEM((tm, tn), F32)],
        compiler_params=_params(("parallel", "parallel", "arbitrary")),
        name=name,
    )(a, b, *extras)
    return res


def _ij(tm, tn):
    return pl.BlockSpec((tm, tn), lambda i, j, k: (i, j))


def _mm_nn(name, a, b, *, out_dtype, residual=None, out_scale=None, tm=1024, tn=1024, tk=2048):
    M, K = a.shape
    N = b.shape[1]
    tm, tn, tk = _tile(M, tm), _tile(N, tn), _tile(K, tk)
    extras, especs, epi = (), (), None
    if residual is not None:
        extras, especs = (residual,), (_ij(tm, tn),)
        epi = lambda acc, r: (acc + r,)
    elif out_scale is not None:
        epi = lambda acc: (acc * out_scale,)
    return _matmul(
        name, a, b, mode="nn", grid=(M // tm, N // tn, K // tk), tm=tm, tn=tn,
        a_spec=pl.BlockSpec((tm, tk), lambda i, j, k: (i, k)),
        b_spec=pl.BlockSpec((tk, tn), lambda i, j, k: (k, j)),
        outs=[((M, N), out_dtype, _ij(tm, tn))], extras=extras, extra_specs=especs, epilogue=epi)[0]


def _mm_nt(name, a, b, *, out_dtype, b_cols=None, residual=None, tm=1024, tn=1024, tk=2048):
    M, K = a.shape
    N = b.shape[0]
    c0 = 0 if b_cols is None else b_cols[0]
    tm, tn, tk = _tile(M, tm), _tile(N, tn), _tile(K, tk)
    assert c0 % tk == 0
    kb0 = c0 // tk
    extras, especs, epi = (), (), None
    if residual is not None:
        extras, especs = (residual,), (_ij(tm, tn),)
        epi = lambda acc, r: (acc + r,)
    return _matmul(
        name, a, b, mode="nt", grid=(M // tm, N // tn, K // tk), tm=tm, tn=tn,
        a_spec=pl.BlockSpec((tm, tk), lambda i, j, k: (i, k)),
        b_spec=pl.BlockSpec((tn, tk), lambda i, j, k: (j, kb0 + k)),
        outs=[((M, N), out_dtype, _ij(tm, tn))], extras=extras, extra_specs=especs, epilogue=epi)[0]


def _mm_tn(name, a, b, *, tm=1024, tn=1024, tk=1024):
    K, M = a.shape
    N = b.shape[1]
    tm, tn, tk = _tile(M, tm), _tile(N, tn), _tile(K, tk)
    return _matmul(
        name, a, b, mode="tn", grid=(M // tm, N // tn, K // tk), tm=tm, tn=tn,
        a_spec=pl.BlockSpec((tk, tm), lambda i, j, k: (k, i)),
        b_spec=pl.BlockSpec((tk, tn), lambda i, j, k: (k, j)),
        outs=[((M, N), BF16, _ij(tm, tn))])[0]


def _mlp_fwd(tag, hn, h, w_up_g, w_down_g, layer, *, tm=1024, tk=2048):
    S, D = hn.shape
    fb = w_up_g.shape[3]
    F = N_DEV * fb
    tm, tku = _tile(S, tm), _tile(D, tk)

    def up_epi(acc):
        u = jnp.maximum(acc, 0.0)
        return u, u * u

    u, uu = _matmul(
        f"mlp_up_{tag}", hn, w_up_g, mode="nn", grid=(S // tm, N_DEV, D // tku), tm=tm, tn=fb,
        a_spec=pl.BlockSpec((tm, tku), lambda i, j, k: (i, k)),
        b_spec=pl.BlockSpec((None, None, tku, fb), lambda i, j, k: (j, layer, k, 0)),
        outs=[((S, F), BF16, _ij(tm, fb)), ((S, F), BF16, _ij(tm, fb))], epilogue=up_epi)

    tn = _tile(D, 1024)
    tkd = _tile(fb, tk)
    r = fb // tkd
    h_out = _matmul(
        f"mlp_down_{tag}", uu, w_down_g, mode="nn", grid=(S // tm, D // tn, F // tkd), tm=tm, tn=tn,
        a_spec=pl.BlockSpec((tm, tkd), lambda i, j, k: (i, k)),
        b_spec=pl.BlockSpec((None, None, tkd, tn), lambda i, j, k: (k // r, layer, k % r, j)),
        outs=[((S, D), F32, _ij(tm, tn))], extras=(h,), extra_specs=(_ij(tm, tn),),
        epilogue=lambda acc, res: (acc + res,))[0]
    return u, uu, h_out


def _mlp_bwd(tag, dh, hn, u, uu, w_up_g, w_down_g, layer, *, tm=1024, tk=2048):
    S, D = dh.shape
    fb = w_up_g.shape[3]
    F = N_DEV * fb
    tm, tkd = _tile(S, tm), _tile(D, tk)

    d_pre = _matmul(
        f"mlp_dpre_{tag}", dh, w_down_g, mode="nt", grid=(S // tm, N_DEV, D // tkd), tm=tm, tn=fb,
        a_spec=pl.BlockSpec((tm, tkd), lambda i, j, k: (i, k)),
        b_spec=pl.BlockSpec((None, None, fb, tkd), lambda i, j, k: (j, layer, 0, k)),
        outs=[((S, F), BF16, _ij(tm, fb))], extras=(u,), extra_specs=(_ij(tm, fb),),
        epilogue=lambda acc, uv: (acc * (2.0 * uv.astype(F32)),))[0]

    dw_down = _mm_tn(f"mlp_dwdown_{tag}", uu, dh)

    tmu = _tile(D, 1024)
    tks = _tile(S, 1024)
    dw_up = _matmul(
        f"mlp_dwup_{tag}", hn, d_pre, mode="tn", grid=(D // tmu, N_DEV, S // tks), tm=tmu, tn=fb,
        a_spec=pl.BlockSpec((tks, tmu), lambda i, j, k: (k, i)),
        b_spec=pl.BlockSpec((tks, fb), lambda i, j, k: (k, j)),
        outs=[((N_DEV, D, fb), BF16, pl.BlockSpec((None, tmu, fb), lambda i, j, k: (j, i, 0)))])[0]

    tn = _tile(D, 1024)
    tkf = _tile(fb, tk)
    r = fb // tkf
    d_hn = _matmul(
        f"mlp_dhn_{tag}", d_pre, w_up_g, mode="nt", grid=(S // tm, D // tn, F // tkf), tm=tm, tn=tn,
        a_spec=pl.BlockSpec((tm, tkf), lambda i, j, k: (i, k)),
        b_spec=pl.BlockSpec((None, None, tn, tkf), lambda i, j, k: (k // r, layer, j, k % r)),
        outs=[((S, D), F32, _ij(tm, tn))])[0]
    return d_hn, dw_up, dw_down


def _row_spec(ts, D):
    return pl.BlockSpec((ts, D), lambda i: (i, 0))


def _vec_spec(n, D):
    return pl.BlockSpec((n, D), lambda i: (0, 0))


def _rms_fwd(name, x, gains, *, ts=512):
    S, D = x.shape
    n = gains.shape[0]
    ts = _tile(S, ts)

    def body(x_ref, g_ref, *o_refs):
        xv = x_ref[...]
        y = xv * lax.rsqrt(jnp.mean(xv * xv, axis=-1, keepdims=True) + EPS)
        for i, o_ref in enumerate(o_refs):
            o_ref[...] = (y * g_ref[i:i + 1, :]).astype(o_ref.dtype)

    return pl.pallas_call(
        body, grid=(S // ts,), in_specs=[_row_spec(ts, D), _vec_spec(n, D)],
        out_specs=[_row_spec(ts, D)] * n, out_shape=[jax.ShapeDtypeStruct((S, D), BF16)] * n,
        compiler_params=_params(("parallel",)), name=name)(x, gains)


def _rms_bwd(name, x, gains, dys, res, *, ts=512):
    S, D = x.shape
    n = gains.shape[0]
    ts = _tile(S, ts)

    def body(x_ref, g_ref, *rest):
        dy_refs = rest[:n]
        res_ref, dx_ref, dg_ref = rest[n:]
        i = pl.program_id(0)
        xv = x_ref[...]
        r = lax.rsqrt(jnp.mean(xv * xv, axis=-1, keepdims=True) + EPS)
        xhat = xv * r
        dxhat = None
        dgs = []
        for k in range(n):
            dy = dy_refs[k][...].astype(F32)
            dgs.append(jnp.sum(dy * xhat, axis=0, keepdims=True))
            term = dy * g_ref[k:k + 1, :]
            dxhat = term if dxhat is None else dxhat + term
        dx_ref[...] = res_ref[...] + r * (dxhat - xhat * jnp.mean(dxhat * xhat, axis=-1, keepdims=True))
        dg = jnp.concatenate(dgs, axis=0) if n > 1 else dgs[0]

        @pl.when(i == 0)
        def _():
            dg_ref[...] = dg

        @pl.when(i > 0)
        def _():
            dg_ref[...] += dg

    return pl.pallas_call(
        body, grid=(S // ts,),
        in_specs=[_row_spec(ts, D), _vec_spec(n, D)] + [_row_spec(ts, D)] * (n + 1),
        out_specs=[_row_spec(ts, D), _vec_spec(n, D)],
        out_shape=[jax.ShapeDtypeStruct((S, D), F32), jax.ShapeDtypeStruct((n, D), F32)],
        compiler_params=_params(("arbitrary",)), name=name)(x, gains, *dys, res)


def _loss_head(h, gain, target, *, ts=512):
    S, D = h.shape
    ts = _tile(S, ts)

    def body(x_ref, g_ref, t_ref, dx_ref, dg_ref, loss_ref):
        i = pl.program_id(0)
        xv = x_ref[...]
        g = g_ref[...]
        r = lax.rsqrt(jnp.mean(xv * xv, axis=-1, keepdims=True) + EPS)
        xhat = xv * r
        err = xhat * g - t_ref[...]
        part = 0.5 * jnp.sum(jnp.mean(err * err, axis=-1, keepdims=True), axis=0, keepdims=True)
        dy = err * (1.0 / D)
        dg = jnp.sum(dy * xhat, axis=0, keepdims=True)
        dxhat = dy * g
        dx_ref[...] = r * (dxhat - xhat * jnp.mean(dxhat * xhat, axis=-1, keepdims=True))

        @pl.when(i == 0)
        def _():
            dg_ref[...] = dg
            loss_ref[...] = jnp.broadcast_to(part, loss_ref.shape)

        @pl.when(i > 0)
        def _():
            dg_ref[...] += dg
            loss_ref[...] += jnp.broadcast_to(part, loss_ref.shape)

    return pl.pallas_call(
        body, grid=(S // ts,),
        in_specs=[_row_spec(ts, D), _vec_spec(1, D), _row_spec(ts, D)],
        out_specs=[_row_spec(ts, D), _vec_spec(1, D), pl.BlockSpec((8, LANES), lambda i: (0, 0))],
        out_shape=[jax.ShapeDtypeStruct((S, D), F32), jax.ShapeDtypeStruct((1, D), F32),
                   jax.ShapeDtypeStruct((8, LANES), F32)],
        compiler_params=_params(("arbitrary",)), name="loss_head")(h, gain, target)


def _window_counts(t, w):
    return jnp.minimum(t + 1, w).astype(F32)


def _pool_fwd(x, gain, pool_w, scale, *, ts=512):
    S, D = x.shape
    dg = D // N_GROUPS
    ts = _tile(S, ts)
    per = ts // POOL_HALO

    def body(x_ref, xh_ref, g_ref, w_ref, sc_ref, h_ref, diff_ref):
        i = pl.program_id(0)
        g = g_ref[...]

        def norm(v):
            return v * lax.rsqrt(jnp.mean(v * v, axis=-1, keepdims=True) + EPS) * g

        xc = x_ref[...]
        hn_c = norm(xc)
        hn_h = norm(xh_ref[...]) * (i > 0).astype(F32)
        ext = jnp.concatenate([hn_h, hn_c], axis=0)
        t = i * ts + lax.broadcasted_iota(jnp.int32, (ts, 1), 0)
        for gi, w in enumerate(POOL_WINDOWS):
            cols = slice(gi * dg, (gi + 1) * dg)
            s = ext[:, cols]
            step = 1
            while step < w:
                s = s + pltpu.roll(s, step, 0)
                step *= 2
            mean = s[POOL_HALO:] * (1.0 / _window_counts(t, w))
            diff = (mean - hn_c[:, cols]).astype(BF16)
            diff_ref[:, cols] = diff
            mixed = jnp.dot(diff, w_ref[gi], preferred_element_type=F32)
            h_ref[:, cols] = xc[:, cols] + mixed * sc_ref[:, cols]

    return pl.pallas_call(
        body, grid=(S // ts,),
        in_specs=[_row_spec(ts, D),
                  pl.BlockSpec((POOL_HALO, D), lambda i: (jnp.maximum(i * per - 1, 0), 0)),
                  _vec_spec(1, D), pl.BlockSpec((N_GROUPS, dg, dg), lambda i: (0, 0, 0)), _vec_spec(1, D)],
        out_specs=[_row_spec(ts, D), _row_spec(ts, D)],
        out_shape=[jax.ShapeDtypeStruct((S, D), F32), jax.ShapeDtypeStruct((S, D), BF16)],
        compiler_params=_params(("parallel",)), name="pool_fwd")(x, x, gain, pool_w, scale)


def _pool_bwd(x, dh, diff, gain, pool_w, scale, *, ts=256):
    S, D = x.shape
    dg = D // N_GROUPS
    ts = _tile(S, ts)
    per = ts // POOL_HALO
    n_tiles = S // ts
    n_halo = S // POOL_HALO
    ext_rows = ts + POOL_HALO

    def body(x_ref, dh_ref, dhn_ref, diff_ref, g_ref, w_ref, sc_ref, dx_ref, dw_ref, dsc_ref, dgain_ref):
        i = pl.program_id(0)
        xc = x_ref[...]
        g = g_ref[...]
        r = lax.rsqrt(jnp.mean(xc * xc, axis=-1, keepdims=True) + EPS)
        xhat = xc * r
        dh_c = dh_ref[...]
        dh_n = dhn_ref[...] * (i < n_tiles - 1).astype(F32)
        dh_ext = jnp.concatenate([dh_c, dh_n], axis=0)
        t_ext = i * ts + lax.broadcasted_iota(jnp.int32, (ext_rows, 1), 0)
        d_hn_parts, dsc_parts = [], []
        for gi, w in enumerate(POOL_WINDOWS):
            cols = slice(gi * dg, (gi + 1) * dg)
            wg = w_ref[gi]
            dmix = (dh_ext[:, cols] * sc_ref[:, cols]).astype(BF16)
            d_diff = lax.dot_general(dmix, wg, _DIMS["nt"], preferred_element_type=F32)
            diff_c = diff_ref[:, cols]
            dwg = lax.dot_general(diff_c, dmix[:ts], _DIMS["tn"], preferred_element_type=F32)
            mixed = jnp.dot(diff_c, wg, preferred_element_type=F32)
            dsc_parts.append(jnp.sum(dh_c[:, cols] * mixed, axis=0, keepdims=True))
            e = d_diff * (1.0 / _window_counts(t_ext, w))
            step = 1
            while step < w:
                e = e + pltpu.roll(e, ext_rows - step, 0)
                step *= 2
            d_hn_parts.append(e[:ts] - d_diff[:ts])

            @pl.when(i == 0)
            def _():
                dw_ref[gi] = dwg

            @pl.when(i > 0)
            def _():
                dw_ref[gi] += dwg

        d_hn = jnp.concatenate(d_hn_parts, axis=1)
        dsc = jnp.concatenate(dsc_parts, axis=1)
        dgain = jnp.sum(d_hn * xhat, axis=0, keepdims=True)
        dxhat = d_hn * g
        dx_ref[...] = dh_c + r * (dxhat - xhat * jnp.mean(dxhat * xhat, axis=-1, keepdims=True))

        @pl.when(i == 0)
        def _():
            dsc_ref[...] = dsc
            dgain_ref[...] = dgain

        @pl.when(i > 0)
        def _():
            dsc_ref[...] += dsc
            dgain_ref[...] += dgain

    return pl.pallas_call(
        body, grid=(n_tiles,),
        in_specs=[_row_spec(ts, D), _row_spec(ts, D),
                  pl.BlockSpec((POOL_HALO, D), lambda i: (jnp.minimum((i + 1) * per, n_halo - 1), 0)),
                  _row_spec(ts, D), _vec_spec(1, D),
                  pl.BlockSpec((N_GROUPS, dg, dg), lambda i: (0, 0, 0)), _vec_spec(1, D)],
        out_specs=[_row_spec(ts, D), pl.BlockSpec((N_GROUPS, dg, dg), lambda i: (0, 0, 0)),
                   _vec_spec(1, D), _vec_spec(1, D)],
        out_shape=[jax.ShapeDtypeStruct((S, D), F32), jax.ShapeDtypeStruct((N_GROUPS, dg, dg), F32),
                   jax.ShapeDtypeStruct((1, D), F32), jax.ShapeDtypeStruct((1, D), F32)],
        compiler_params=_params(("arbitrary",)), name="pool_bwd")(x, dh, dh, diff, gain, pool_w, scale)


def _gate_fwd(f_raw, b_pad, *, ts=512):
    S = f_raw.shape[0]
    ts = _tile(S, ts)

    def body(f_ref, b_ref, c_ref, carry_ref):
        i = pl.program_id(0)

        @pl.when(i == 0)
        def _():
            carry_ref[...] = jnp.zeros_like(carry_ref)

        z = f_ref[...] + b_ref[...]
        v = jnp.minimum(z, 0.0) - jnp.log1p(jnp.exp(-jnp.abs(z)))
        row = lax.broadcasted_iota(jnp.int32, (ts, 1), 0)
        step = 1
        while step < ts:
            v = v + jnp.where(row >= step, pltpu.roll(v, step, 0), 0.0)
            step *= 2
        out = v + carry_ref[0:1, :]
        c_ref[...] = out
        carry_ref[...] = jnp.broadcast_to(out[ts - 1:ts, :], carry_ref.shape)

    return pl.pallas_call(
        body, grid=(S // ts,),
        in_specs=[pl.BlockSpec((ts, LANES), lambda i: (i, 0)), pl.BlockSpec((1, LANES), lambda i: (0, 0))],
        out_specs=pl.BlockSpec((ts, LANES), lambda i: (i, 0)),
        out_shape=jax.ShapeDtypeStruct((S, LANES), F32),
        scratch_shapes=[pltpu.VMEM((8, LANES), F32)],
        compiler_params=_params(("arbitrary",)), name="gate_fwd")(f_raw, b_pad)


def _gate_bwd(dc, f_raw, b_pad, *, ts=512):
    S = f_raw.shape[0]
    ts = _tile(S, ts)
    n = S // ts

    def body(dc_ref, f_ref, b_ref, df_ref, db_ref, carry_ref):
        i = pl.program_id(0)

        @pl.when(i == 0)
        def _():
            carry_ref[...] = jnp.zeros_like(carry_ref)

        v = dc_ref[...]
        row = lax.broadcasted_iota(jnp.int32, (ts, 1), 0)
        step = 1
        while step < ts:
            v = v + jnp.where(row < ts - step, pltpu.roll(v, ts - step, 0), 0.0)
            step *= 2
        d_logf = v + carry_ref[0:1, :]
        carry_ref[...] = jnp.broadcast_to(d_logf[0:1, :], carry_ref.shape)
        z = f_ref[...] + b_ref[...]
        df = d_logf / (1.0 + jnp.exp(z))
        df_ref[...] = df
        db = jnp.sum(df, axis=0, keepdims=True)

        @pl.when(i == 0)
        def _():
            db_ref[...] = db

        @pl.when(i > 0)
        def _():
            db_ref[...] += db

    rev = lambda i: (n - 1 - i, 0)
    return pl.pallas_call(
        body, grid=(n,),
        in_specs=[pl.BlockSpec((ts, LANES), rev), pl.BlockSpec((ts, LANES), rev),
                  pl.BlockSpec((1, LANES), lambda i: (0, 0))],
        out_specs=[pl.BlockSpec((ts, LANES), rev), pl.BlockSpec((1, LANES), lambda i: (0, 0))],
        out_shape=[jax.ShapeDtypeStruct((S, LANES), F32), jax.ShapeDtypeStruct((1, LANES), F32)],
        scratch_shapes=[pltpu.VMEM((8, LANES), F32)],
        compiler_params=_params(("arbitrary",)), name="gate_bwd")(dc, f_raw, b_pad)


def _col_layout(a_sh, n_heads):
    S = a_sh.shape[0]
    return jnp.broadcast_to(a_sh[:, :n_heads].T[:, :, None], (n_heads, S, LANES))


def _row_layout(a_hs, t):
    n_heads, S = a_hs.shape
    return a_hs.reshape(n_heads, S // t, 1, t)


def _attn_fwd(q, kv, c_col, c_row, *, t=512):
    S, D = q.shape
    H = D // HEAD_DIM
    t = _tile(S, t)
    nb = S // t
    scale = HEAD_DIM ** -0.5

    def body(q_ref, k_ref, v_ref, cc_ref, cr_ref, o_ref, lse_ref):
        i = pl.program_id(1)
        qv = q_ref[...]
        cq = cc_ref[:, 0:1]

        def step(j, carry, masked):
            m, l, acc = carry
            r0 = pl.multiple_of(j * t, t)
            kb = k_ref[pl.ds(r0, t), :]
            vb = v_ref[pl.ds(r0, t), :]
            s = lax.dot_general(qv, kb, _DIMS["nt"], preferred_element_type=F32) * scale + (cq - cr_ref[j])
            if masked:
                rr = lax.broadcasted_iota(jnp.int32, (t, t), 0)
                cc = lax.broadcasted_iota(jnp.int32, (t, t), 1)
                s = jnp.where(cc <= rr, s, NEG_INF)
            m_new = jnp.maximum(m, jnp.max(s, axis=-1, keepdims=True))
            alpha = jnp.exp(m - m_new)
            p = jnp.exp(s - m_new)
            l = alpha * l + jnp.sum(p, axis=-1, keepdims=True)
            acc = alpha * acc + jnp.dot(p.astype(BF16), vb, preferred_element_type=F32)
            return m_new, l, acc

        init = (jnp.full((t, 1), NEG_INF, F32), jnp.zeros((t, 1), F32), jnp.zeros((t, HEAD_DIM), F32))
        carry = lax.fori_loop(0, i, lambda j, c: step(j, c, False), init)
        m, l, acc = step(i, carry, True)
        o_ref[...] = (acc / l).astype(o_ref.dtype)
        lse_ref[...] = jnp.broadcast_to(m + jnp.log(l), lse_ref.shape)

    return pl.pallas_call(
        body, grid=(H, nb),
        in_specs=[pl.BlockSpec((t, HEAD_DIM), lambda h, i: (i, h)),
                  pl.BlockSpec((S, HEAD_DIM), lambda h, i: (0, h)),
                  pl.BlockSpec((S, HEAD_DIM), lambda h, i: (0, H + h)),
                  pl.BlockSpec((None, t, LANES), lambda h, i: (h, i, 0)),
                  pl.BlockSpec((None, nb, 1, t), lambda h, i: (h, 0, 0, 0))],
        out_specs=[pl.BlockSpec((t, HEAD_DIM), lambda h, i: (i, h)),
                   pl.BlockSpec((None, t, LANES), lambda h, i: (h, i, 0))],
        out_shape=[jax.ShapeDtypeStruct((S, D), BF16), jax.ShapeDtypeStruct((H, S, LANES), F32)],
        compiler_params=_params(("parallel", "arbitrary")), name="attn_fwd")(q, kv, kv, c_col, c_row)


def _attn_delta(o, do, *, t=512):
    S, D = o.shape
    H = D // HEAD_DIM
    t = _tile(S, t)

    def body(o_ref, do_ref, d_ref):
        d = jnp.sum(o_ref[...].astype(F32) * do_ref[...].astype(F32), axis=-1, keepdims=True)
        d_ref[...] = jnp.broadcast_to(d, d_ref.shape)

    spec = pl.BlockSpec((t, HEAD_DIM), lambda h, i: (i, h))
    return pl.pallas_call(
        body, grid=(H, S // t), in_specs=[spec, spec],
        out_specs=pl.BlockSpec((None, t, LANES), lambda h, i: (h, i, 0)),
        out_shape=jax.ShapeDtypeStruct((H, S, LANES), F32),
        compiler_params=_params(("parallel", "parallel")), name="attn_delta")(o, do)


def _attn_dq(q, kv, do, c_col, c_row, lse_col, delta_col, *, t=512):
    S, D = q.shape
    H = D // HEAD_DIM
    t = _tile(S, t)
    nb = S // t
    scale = HEAD_DIM ** -0.5

    def body(q_ref, k_ref, v_ref, do_ref, cc_ref, cr_ref, lse_ref, dl_ref, dq_ref, dcq_ref):
        i = pl.program_id(1)
        qv = q_ref[...]
        dov = do_ref[...]
        cq = cc_ref[:, 0:1]
        lse = lse_ref[:, 0:1]
        delta = dl_ref[:, 0:1]

        def step(j, carry, masked):
            acc, rowsum = carry
            r0 = pl.multiple_of(j * t, t)
            kb = k_ref[pl.ds(r0, t), :]
            vb = v_ref[pl.ds(r0, t), :]
            s = lax.dot_general(qv, kb, _DIMS["nt"], preferred_element_type=F32) * scale + (cq - cr_ref[j])
            p = jnp.exp(s - lse)
            if masked:
                rr = lax.broadcasted_iota(jnp.int32, (t, t), 0)
                cc = lax.broadcasted_iota(jnp.int32, (t, t), 1)
                p = jnp.where(cc <= rr, p, 0.0)
            dp = lax.dot_general(dov, vb, _DIMS["nt"], preferred_element_type=F32)
            ds = p * (dp - delta)
            rowsum = rowsum + jnp.sum(ds, axis=-1, keepdims=True)
            return acc + jnp.dot(ds.astype(BF16), kb, preferred_element_type=F32), rowsum

        init = (jnp.zeros((t, HEAD_DIM), F32), jnp.zeros((t, 1), F32))
        carry = lax.fori_loop(0, i, lambda j, a: step(j, a, False), init)
        acc, rowsum = step(i, carry, True)
        dq_ref[...] = (acc * scale).astype(dq_ref.dtype)
        dcq_ref[...] = jnp.broadcast_to(rowsum, dcq_ref.shape)

    qspec = pl.BlockSpec((t, HEAD_DIM), lambda h, i: (i, h))
    colspec = pl.BlockSpec((None, t, LANES), lambda h, i: (h, i, 0))
    return pl.pallas_call(
        body, grid=(H, nb),
        in_specs=[qspec,
                  pl.BlockSpec((S, HEAD_DIM), lambda h, i: (0, h)),
                  pl.BlockSpec((S, HEAD_DIM), lambda h, i: (0, H + h)),
                  qspec, colspec,
                  pl.BlockSpec((None, nb, 1, t), lambda h, i: (h, 0, 0, 0)),
                  colspec, colspec],
        out_specs=[qspec, colspec],
        out_shape=[jax.ShapeDtypeStruct((S, D), BF16), jax.ShapeDtypeStruct((H, S, LANES), F32)],
        compiler_params=_params(("parallel", "arbitrary")), name="attn_dq")(
            q, kv, kv, do, c_col, c_row, lse_col, delta_col)


def _attn_dkv(q, kv, do, c_col, c_row, lse_row, delta_row, *, t=512):
    S, D = q.shape
    H = D // HEAD_DIM
    t = _tile(S, t)
    nb = S // t
    scale = HEAD_DIM ** -0.5

    def body(q_ref, do_ref, k_ref, v_ref, cc_ref, cr_ref, lse_ref, dl_ref, dk_ref, dv_ref, dc_ref):
        j = pl.program_id(1)
        kb = k_ref[...]
        vb = v_ref[...]
        ck = cc_ref[:, 0:1]

        def step(i, carry, masked):
            dk, dv, dc = carry
            r0 = pl.multiple_of(i * t, t)
            qb = q_ref[pl.ds(r0, t), :]
            dob = do_ref[pl.ds(r0, t), :]
            s = lax.dot_general(kb, qb, _DIMS["nt"], preferred_element_type=F32) * scale + (cr_ref[i] - ck)
            p = jnp.exp(s - lse_ref[i])
            if masked:
                rr = lax.broadcasted_iota(jnp.int32, (t, t), 0)
                cc = lax.broadcasted_iota(jnp.int32, (t, t), 1)
                p = jnp.where(rr <= cc, p, 0.0)
            dv = dv + jnp.dot(p.astype(BF16), dob, preferred_element_type=F32)
            dp = lax.dot_general(vb, dob, _DIMS["nt"], preferred_element_type=F32)
            ds = p * (dp - dl_ref[i])
            dc = dc - jnp.sum(ds, axis=-1, keepdims=True)
            dk = dk + jnp.dot(ds.astype(BF16), qb, preferred_element_type=F32)
            return dk, dv, dc

        init = (jnp.zeros((t, HEAD_DIM), F32), jnp.zeros((t, HEAD_DIM), F32), jnp.zeros((t, 1), F32))
        carry = step(j, init, True)
        dk, dv, dc = lax.fori_loop(j + 1, nb, lambda i, c: step(i, c, False), carry)
        dk_ref[...] = (dk * scale).astype(dk_ref.dtype)
        dv_ref[...] = dv.astype(dv_ref.dtype)
        dc_ref[...] = jnp.broadcast_to(dc, dc_ref.shape)

    kspec = pl.BlockSpec((t, HEAD_DIM), lambda h, j: (j, h))
    rowspec = pl.BlockSpec((None, nb, 1, t), lambda h, j: (h, 0, 0, 0))
    return pl.pallas_call(
        body, grid=(H, nb),
        in_specs=[pl.BlockSpec((S, HEAD_DIM), lambda h, j: (0, h)),
                  pl.BlockSpec((S, HEAD_DIM), lambda h, j: (0, h)),
                  kspec,
                  pl.BlockSpec((t, HEAD_DIM), lambda h, j: (j, H + h)),
                  pl.BlockSpec((None, t, LANES), lambda h, j: (h, j, 0)),
                  rowspec, rowspec, rowspec],
        out_specs=[kspec, kspec, pl.BlockSpec((None, t, LANES), lambda h, j: (h, j, 0))],
        out_shape=[jax.ShapeDtypeStruct((S, D), BF16), jax.ShapeDtypeStruct((S, D), BF16),
                   jax.ShapeDtypeStruct((H, S, LANES), F32)],
        compiler_params=_params(("parallel", "arbitrary")), name="attn_dkv")(
            q, do, kv, kv, c_col, c_row, lse_row, delta_row)


LOG2E = 1.4426950408889634
LN2 = 0.6931471805599453
SUB = 512
HEADS_PER_STEP = 2


def _lane_pick(blk, h):
    lane = lax.broadcasted_iota(jnp.int32, blk.shape, 1)
    return jnp.sum(jnp.where(lane == h, blk, 0.0), axis=-1, keepdims=True)


def _lane_put(ref, h, col, first=True):
    lane = lax.broadcasted_iota(jnp.int32, ref.shape, 1)
    if not first:
        ref[...] = jnp.where(lane == h, col, ref[...])
        return

    @pl.when(h == 0)
    def _():
        ref[...] = jnp.where(lane == 0, col, 0.0)

    @pl.when(h > 0)
    def _():
        ref[...] = jnp.where(lane == h, col, ref[...])


def _causal(s, row0, masked, fill, rows_are_queries=True):
    if not masked:
        return s
    rr = row0 + lax.broadcasted_iota(jnp.int32, s.shape, 0)
    cc = lax.broadcasted_iota(jnp.int32, s.shape, 1)
    keep = (cc <= rr) if rows_are_queries else (rr <= cc)
    return jnp.where(keep, s, fill)


def _fox_fwd(q2, kv, c_row, *, t=512):
    S, D = q2.shape
    H = D // HEAD_DIM
    t = _tile(S, t)
    nb = S // t
    hp = HEADS_PER_STEP
    wide = hp * HEAD_DIM

    def body(q_ref, k_ref, v_ref, cr_ref, o_ref, l2_ref):
        i = pl.program_id(0)
        g = pl.program_id(1)
        cols = [slice(a * HEAD_DIM, (a + 1) * HEAD_DIM) for a in range(hp)]
        refs_i = [cr_ref[a, i][:, 0:1] for a in range(hp)]
        qs = [q_ref[:, cols[a]] for a in range(hp)]

        def step(j, carry, masked):
            r0 = pl.multiple_of(j * t, t)
            out = []
            for a in range(hp):
                m, l, acc = carry[3 * a:3 * a + 3]
                kb = k_ref[pl.ds(r0, t), cols[a]]
                vb = v_ref[pl.ds(r0, t), cols[a]]
                ck = cr_ref[a, j] - refs_i[a]
                s = lax.dot_general(qs[a], kb, _DIMS["nt"], preferred_element_type=F32) - ck
                s = _causal(s, 0, masked, NEG_INF)
                m_new = jnp.maximum(m, jnp.max(s, axis=-1, keepdims=True))
                alpha = jnp.exp2(m - m_new)
                p = jnp.exp2(s - m_new)
                l = alpha * l + jnp.sum(p, axis=-1, keepdims=True)
                acc = alpha * acc + jnp.dot(p.astype(BF16), vb, preferred_element_type=F32)
                out += [m_new, l, acc]
            return tuple(out)

        init = (jnp.full((t, 1), NEG_INF, F32), jnp.zeros((t, 1), F32), jnp.zeros((t, HEAD_DIM), F32)) * hp
        carry = lax.fori_loop(0, i, lambda j, c: step(j, c, False), init)
        carry = step(i, carry, True)
        for a in range(hp):
            m, l, acc = carry[3 * a:3 * a + 3]
            o_ref[:, cols[a]] = (acc / l).astype(o_ref.dtype)
            _lane_put(l2_ref, g * hp + a, m + jnp.log2(l), first=(a == 0))

    return pl.pallas_call(
        body, grid=(nb, H // hp),
        in_specs=[pl.BlockSpec((t, wide), lambda i, g: (i, g)),
                  pl.BlockSpec((S, wide), lambda i, g: (0, g)),
                  pl.BlockSpec((S, wide), lambda i, g: (0, H // hp + g)),
                  pl.BlockSpec((hp, nb, 1, t), lambda i, g: (g, 0, 0, 0))],
        out_specs=[pl.BlockSpec((t, wide), lambda i, g: (i, g)),
                   pl.BlockSpec((t, LANES), lambda i, g: (i, 0))],
        out_shape=[jax.ShapeDtypeStruct((S, D), BF16), jax.ShapeDtypeStruct((S, LANES), F32)],
        compiler_params=_params(("parallel", "arbitrary")), name="fox_fwd")(q2, kv, kv, c_row)


def _fox_dq(q2, kv, o, do, c_row, l2_sh, *, t=512):
    S, D = q2.shape
    H = D // HEAD_DIM
    t = _tile(S, t)
    nb = S // t
    sub = _tile(t, SUB)
    ns = t // sub
    scale = HEAD_DIM ** -0.5

    def body(q_ref, k_ref, v_ref, o_ref, do_ref, cr_ref, l2_ref, dq_ref, dl_ref, dcq_ref):
        i = pl.program_id(0)
        h = pl.program_id(1)
        ref_i = cr_ref[i][:, 0:1]
        l2 = _lane_pick(l2_ref[...], h)
        dov = do_ref[...]
        delta = jnp.sum(o_ref[...].astype(F32) * dov.astype(F32), axis=-1, keepdims=True)
        qs = [q_ref[a * sub:(a + 1) * sub, :] for a in range(ns)]
        dos = [dov[a * sub:(a + 1) * sub, :] for a in range(ns)]
        l2s = [l2[a * sub:(a + 1) * sub, :] for a in range(ns)]
        dls = [delta[a * sub:(a + 1) * sub, :] for a in range(ns)]

        def step(j, carry, masked):
            r0 = pl.multiple_of(j * t, t)
            kb = k_ref[pl.ds(r0, t), :]
            vb = v_ref[pl.ds(r0, t), :]
            ck = cr_ref[j] - ref_i
            out = []
            for a in range(ns):
                acc, rowsum = carry[2 * a:2 * a + 2]
                s = lax.dot_general(qs[a], kb, _DIMS["nt"], preferred_element_type=F32) - ck
                p = _causal(jnp.exp2(s - l2s[a]), a * sub, masked, 0.0)
                dp = lax.dot_general(dos[a], vb, _DIMS["nt"], preferred_element_type=F32)
                ds = p * (dp - dls[a])
                rowsum = rowsum + jnp.sum(ds, axis=-1, keepdims=True)
                acc = acc + jnp.dot(ds.astype(BF16), kb, preferred_element_type=F32)
                out += [acc, rowsum]
            return tuple(out)

        init = (jnp.zeros((sub, HEAD_DIM), F32), jnp.zeros((sub, 1), F32)) * ns
        carry = lax.fori_loop(0, i, lambda j, c: step(j, c, False), init)
        carry = step(i, carry, True)
        for a in range(ns):
            dq_ref[a * sub:(a + 1) * sub, :] = (carry[2 * a] * scale).astype(dq_ref.dtype)
        _lane_put(dl_ref, h, delta)
        _lane_put(dcq_ref, h, jnp.concatenate([carry[2 * a + 1] for a in range(ns)], axis=0))

    qspec = pl.BlockSpec((t, HEAD_DIM), lambda i, h: (i, h))
    shspec = pl.BlockSpec((t, LANES), lambda i, h: (i, 0))
    return pl.pallas_call(
        body, grid=(nb, H),
        in_specs=[qspec,
                  pl.BlockSpec((S, HEAD_DIM), lambda i, h: (0, h)),
                  pl.BlockSpec((S, HEAD_DIM), lambda i, h: (0, H + h)),
                  qspec, qspec,
                  pl.BlockSpec((None, nb, 1, t), lambda i, h: (h, 0, 0, 0)),
                  shspec],
        out_specs=[qspec, shspec, shspec],
        out_shape=[jax.ShapeDtypeStruct((S, D), BF16), jax.ShapeDtypeStruct((S, LANES), F32),
                   jax.ShapeDtypeStruct((S, LANES), F32)],
        compiler_params=_params(("parallel", "arbitrary")), name="fox_dq")(q2, kv, kv, o, do, c_row, l2_sh)


def _fox_dkv(q2, kv, do, c_sh, c_row, l2_row, delta_row, *, t=512):
    S, D = q2.shape
    H = D // HEAD_DIM
    t = _tile(S, t)
    nb = S // t
    sub = _tile(t, SUB)
    ns = t // sub

    def body(q_ref, do_ref, k_ref, v_ref, csh_ref, cr_ref, l2_ref, dl_ref, dk_ref, dv_ref, dck_ref):
        j = pl.program_id(0)
        h = pl.program_id(1)
        ck = _lane_pick(csh_ref[...], h)
        ks = [k_ref[a * sub:(a + 1) * sub, :] for a in range(ns)]
        vs = [v_ref[a * sub:(a + 1) * sub, :] for a in range(ns)]
        cks = [ck[a * sub:(a + 1) * sub, :] for a in range(ns)]

        def step(i, carry, masked):
            r0 = pl.multiple_of(i * t, t)
            qb = q_ref[pl.ds(r0, t), :]
            dob = do_ref[pl.ds(r0, t), :]
            ref_i = cr_ref[i][:, 0:1]
            l2 = l2_ref[i]
            dl = dl_ref[i]
            out = []
            for a in range(ns):
                dk, dv, dc = carry[3 * a:3 * a + 3]
                s = lax.dot_general(ks[a], qb, _DIMS["nt"], preferred_element_type=F32) - (cks[a] - ref_i)
                p = _causal(jnp.exp2(s - l2), a * sub, masked, 0.0, rows_are_queries=False)
                dv = dv + jnp.dot(p.astype(BF16), dob, preferred_element_type=F32)
                dp = lax.dot_general(vs[a], dob, _DIMS["nt"], preferred_element_type=F32)
                ds = p * (dp - dl)
                dc = dc - jnp.sum(ds, axis=-1, keepdims=True)
                dk = dk + jnp.dot(ds.astype(BF16), qb, preferred_element_type=F32)
                out += [dk, dv, dc]
            return tuple(out)

        init = (jnp.zeros((sub, HEAD_DIM), F32), jnp.zeros((sub, HEAD_DIM), F32), jnp.zeros((sub, 1), F32)) * ns
        carry = step(j, init, True)
        carry = lax.fori_loop(j + 1, nb, lambda i, c: step(i, c, False), carry)
        for a in range(ns):
            dk_ref[a * sub:(a + 1) * sub, :] = (carry[3 * a] * LN2).astype(dk_ref.dtype)
            dv_ref[a * sub:(a + 1) * sub, :] = carry[3 * a + 1].astype(dv_ref.dtype)
        _lane_put(dck_ref, h, jnp.concatenate([carry[3 * a + 2] for a in range(ns)], axis=0))

    kspec = pl.BlockSpec((t, HEAD_DIM), lambda j, h: (j, h))
    rowspec = pl.BlockSpec((None, nb, 1, t), lambda j, h: (h, 0, 0, 0))
    shspec = pl.BlockSpec((t, LANES), lambda j, h: (j, 0))
    return pl.pallas_call(
        body, grid=(nb, H),
        in_specs=[pl.BlockSpec((S, HEAD_DIM), lambda j, h: (0, h)),
                  pl.BlockSpec((S, HEAD_DIM), lambda j, h: (0, h)),
                  kspec,
                  pl.BlockSpec((t, HEAD_DIM), lambda j, h: (j, H + h)),
                  shspec, rowspec, rowspec, rowspec],
        out_specs=[kspec, kspec, shspec],
        out_shape=[jax.ShapeDtypeStruct((S, D), BF16), jax.ShapeDtypeStruct((S, D), BF16),
                   jax.ShapeDtypeStruct((S, LANES), F32)],
        compiler_params=_params(("parallel", "arbitrary")), name="fox_dkv")(
            q2, do, kv, kv, c_sh, c_row, l2_row, delta_row)


def _adamw(name, parts, w, m, v, layer=None, *, tr=128):
    P, R, C = parts.shape
    tr = _tile(R, tr)

    def body(p_ref, w_ref, m_ref, v_ref, g_ref, d_ref, nm_ref, nv_ref):
        g = p_ref[0].astype(F32)
        for k in range(1, P):
            g = g + p_ref[k].astype(F32)
        wv = w_ref[...]
        nm = ADAM_B1 * m_ref[...] + (1.0 - ADAM_B1) * g
        nv = ADAM_B2 * v_ref[...] + (1.0 - ADAM_B2) * (g * g)
        m_hat = nm / (1.0 - ADAM_B1 ** ADAM_STEP)
        v_hat = nv / (1.0 - ADAM_B2 ** ADAM_STEP)
        g_ref[...] = g
        d_ref[...] = -ADAM_LR * (m_hat / (jnp.sqrt(v_hat) + ADAM_EPS) + ADAM_WD * wv)
        nm_ref[...] = nm
        nv_ref[...] = nv

    if layer is None:
        wspec = pl.BlockSpec((tr, C), lambda i: (i, 0))
    else:
        wspec = pl.BlockSpec((None, tr, C), lambda i: (layer, i, 0))
    ospec = pl.BlockSpec((tr, C), lambda i: (i, 0))
    return pl.pallas_call(
        body, grid=(R // tr,),
        in_specs=[pl.BlockSpec((P, tr, C), lambda i: (0, i, 0)), wspec, wspec, wspec],
        out_specs=[ospec] * 4, out_shape=[jax.ShapeDtypeStruct((R, C), F32)] * 4,
        compiler_params=_params(("parallel",)), name=name)(parts, w, m, v)


_ANY = pl.BlockSpec(memory_space=pl.ANY)


def _position():
    return lax.axis_index("x"), lax.axis_index("y"), lax.axis_index("c")


def _all_gather(shards):
    n = len(shards)

    def body(*refs):
        ins, outs = refs[:n], refs[n:2 * n]
        send_sems, recv_sems, local_sems = refs[2 * n:]
        x, y, c = _position()
        me, sibling = (x, y, c), (x, y, 1 - c)
        chips = [(1 - x, y), (x, 1 - y), (1 - x, 1 - y)]

        def slot(a, px, py, pc):
            return outs[a].at[4 * px + 2 * py + pc]

        def copy(a, k, block, to, src=None):
            return pltpu.make_async_remote_copy(
                src_ref=slot(a, *block) if src is None else src, dst_ref=slot(a, *block),
                send_sem=send_sems.at[a, k], recv_sem=recv_sems.at[a, k], device_id=to, device_id_type=MESH)

        mine = [pltpu.make_async_copy(ins[a], slot(a, *me), local_sems.at[a]) for a in range(n)]
        first = []
        for a in range(n):
            mine[a].start()
            first.append(copy(a, 0, me, sibling, src=ins[a]))
            first += [copy(a, 1 + j, me, (*chip, c), src=ins[a]) for j, chip in enumerate(chips)]
        for cp in first:
            cp.start()
        passed = []
        for a in range(n):
            for j, chip in enumerate(chips):
                copy(a, 1 + j, (*chip, c), me).wait_recv()
                fwd = copy(a, 4 + j, (*chip, c), sibling)
                fwd.start()
                passed.append(fwd)
        for a in range(n):
            copy(a, 0, sibling, me).wait_recv()
            for j, chip in enumerate(chips):
                copy(a, 4 + j, (*chip, 1 - c), me).wait_recv()
        for cp in first + passed:
            cp.wait_send()
        for cp in mine:
            cp.wait()

    return pl.pallas_call(
        body, in_specs=[_ANY] * n, out_specs=[_ANY] * n,
        out_shape=[jax.ShapeDtypeStruct((N_DEV, *s.shape), s.dtype) for s in shards],
        scratch_shapes=[pltpu.SemaphoreType.DMA((n, 7)), pltpu.SemaphoreType.DMA((n, 7)),
                        pltpu.SemaphoreType.DMA((n,))],
        name="all_gather")(*shards)


def _exchange_sibling(grads):
    n = len(grads)

    def body(*refs):
        ins, outs = refs[:n], refs[n:2 * n]
        send_sems, recv_sems = refs[2 * n:]
        x, y, c = _position()
        copies = []
        for a in range(n):
            for q in range(N_CHIP):
                copies.append(pltpu.make_async_remote_copy(
                    src_ref=ins[a].at[2 * q + (1 - c)], dst_ref=outs[a].at[q],
                    send_sem=send_sems.at[a, q], recv_sem=recv_sems.at[a, q],
                    device_id=(x, y, 1 - c), device_id_type=MESH))
        for cp in copies:
            cp.start()
        for cp in copies:
            cp.wait()

    return pl.pallas_call(
        body, in_specs=[_ANY] * n, out_specs=[_ANY] * n,
        out_shape=[jax.ShapeDtypeStruct((N_CHIP, *g.shape[1:]), g.dtype) for g in grads],
        scratch_shapes=[pltpu.SemaphoreType.DMA((n, N_CHIP)), pltpu.SemaphoreType.DMA((n, N_CHIP))],
        name="rs_exchange_sibling")(*grads)


def _add_sibling(name, grad, got, core, *, tr=256):
    _, R, C = grad.shape
    tr = _tile(R, tr)
    g4 = grad.reshape(N_CHIP, 2, R, C)

    def body(core_ref, a_ref, b_ref, o_ref):
        o_ref[...] = (a_ref[...].astype(F32) + b_ref[...].astype(F32)).astype(o_ref.dtype)

    spec = pl.BlockSpec((None, tr, C), lambda q, i, core_ref: (q, i, 0))
    return pl.pallas_call(
        body,
        grid_spec=pltpu.PrefetchScalarGridSpec(
            num_scalar_prefetch=1, grid=(N_CHIP, R // tr),
            in_specs=[pl.BlockSpec((None, None, tr, C), lambda q, i, core_ref: (q, core_ref[0], i, 0)), spec],
            out_specs=spec),
        out_shape=jax.ShapeDtypeStruct((N_CHIP, R, C), grad.dtype),
        compiler_params=_params(("parallel", "parallel")), name=name)(core, g4, got)


def _exchange_chips(parts):
    n = len(parts)

    def body(*refs):
        ins, outs = refs[:n], refs[n:2 * n]
        send_sems, recv_sems, local_sems = refs[2 * n:]
        x, y, c = _position()
        my_chip = 2 * x + y
        chips = [(1 - x, y), (x, 1 - y), (1 - x, 1 - y)]
        local, remote = [], []
        for a in range(n):
            local.append(pltpu.make_async_copy(ins[a].at[my_chip], outs[a].at[my_chip], local_sems.at[a]))
            for k, (qx, qy) in enumerate(chips):
                remote.append(pltpu.make_async_remote_copy(
                    src_ref=ins[a].at[2 * qx + qy], dst_ref=outs[a].at[my_chip],
                    send_sem=send_sems.at[a, k], recv_sem=recv_sems.at[a, k],
                    device_id=(qx, qy, c), device_id_type=MESH))
        for cp in local + remote:
            cp.start()
        for a in range(n):
            for k, (qx, qy) in enumerate(chips):
                pltpu.make_async_remote_copy(
                    src_ref=ins[a].at[my_chip], dst_ref=outs[a].at[2 * qx + qy],
                    send_sem=send_sems.at[a, k], recv_sem=recv_sems.at[a, k],
                    device_id=(qx, qy, c), device_id_type=MESH).wait_recv()
        for cp in remote:
            cp.wait_send()
        for cp in local:
            cp.wait()

    return pl.pallas_call(
        body, in_specs=[_ANY] * n, out_specs=[_ANY] * n,
        out_shape=[jax.ShapeDtypeStruct(p.shape, p.dtype) for p in parts],
        scratch_shapes=[pltpu.SemaphoreType.DMA((n, 3)), pltpu.SemaphoreType.DMA((n, 3)),
                        pltpu.SemaphoreType.DMA((n,))],
        name="rs_exchange_chips")(*parts)


def _all_reduce_small(vec):
    R = vec.shape[0]

    def body(v_ref, o_ref, buf_ref, send_sems, recv_sems):
        x, y, c = _position()
        me = 4 * x + 2 * y + c
        buf_ref[me] = v_ref[...]
        copies = []
        for k in range(1, N_DEV):
            peer = (x ^ (k >> 2), y ^ ((k >> 1) & 1), c ^ (k & 1))
            copies.append(pltpu.make_async_remote_copy(
                src_ref=buf_ref.at[me], dst_ref=buf_ref.at[me], send_sem=send_sems.at[k], recv_sem=recv_sems.at[k],
                device_id=peer, device_id_type=MESH))
        for cp in copies:
            cp.start()
        for cp in copies:
            cp.wait()
        total = buf_ref[0]
        for d in range(1, N_DEV):
            total = total + buf_ref[d]
        o_ref[...] = total

    vm = pl.BlockSpec(memory_space=pltpu.VMEM)
    return pl.pallas_call(
        body, in_specs=[vm], out_specs=vm, out_shape=jax.ShapeDtypeStruct((R, LANES), F32),
        scratch_shapes=[pltpu.VMEM((N_DEV, R, LANES), F32), pltpu.SemaphoreType.DMA((N_DEV,)),
                        pltpu.SemaphoreType.DMA((N_DEV,))],
        name="all_reduce_small")(vec)


def _local_grads(x2, tgt, norm_mix, norm_mlp, norm_kv, norm_out, b_pad, pw, scale_full, w_kv, w_f, wq, wo, g_up, g_down):
    S, D = x2.shape
    H = D // HEAD_DIM

    h1, diff = _pool_fwd(x2, norm_mix[0:1], pw, scale_full)
    (hn_m0,) = _rms_fwd("rms_mlp0", h1, norm_mlp[0:1])
    u0, uu0, h2 = _mlp_fwd("l0", hn_m0, h1, g_up, g_down, 0)

    gains_kv_q = jnp.stack([norm_kv, norm_mix[1]])
    hkv, hn_q = _rms_fwd("rms_kv_q", h2, gains_kv_q)
    kv = _mm_nn("kv_proj", hkv, w_kv, out_dtype=BF16)
    f_raw = _mm_nn("f_proj", hkv, w_f, out_dtype=F32)
    c_sh = _gate_fwd(f_raw, b_pad)
    t_attn = _tile(S, 512)
    c2_sh = c_sh * LOG2E
    c2_row = _row_layout(c2_sh[:, :H].T, t_attn)
    q2 = _mm_nn("q_proj", hn_q, wq, out_dtype=BF16, out_scale=HEAD_DIM ** -0.5 * LOG2E)
    o, l2_sh = _fox_fwd(q2, kv, c2_row, t=t_attn)
    h3 = _mm_nn("o_proj", o, wo, out_dtype=F32, residual=h2)
    (hn_m1,) = _rms_fwd("rms_mlp1", h3, norm_mlp[1:2])
    u1, uu1, h4 = _mlp_fwd("l1", hn_m1, h3, g_up, g_down, 1)
    dh4, d_norm_out, loss_part = _loss_head(h4, norm_out.reshape(1, D), tgt)

    d_hn, dw_up1, dw_down1 = _mlp_bwd("l1", dh4, hn_m1, u1, uu1, g_up, g_down, 1)
    dh3, d_norm_mlp1 = _rms_bwd("rms_mlp1_bwd", h3, norm_mlp[1:2], [d_hn], dh4)

    do = _mm_nt("o_proj_dx", dh3, wo, out_dtype=BF16)
    dw_o = _mm_tn("o_proj_dw", o, dh3)
    dq, delta_sh, dcq_sh = _fox_dq(q2, kv, o, do, c2_row, l2_sh, t=t_attn)
    l2_row = _row_layout(l2_sh[:, :H].T, t_attn)
    delta_row = _row_layout(delta_sh[:, :H].T, t_attn)
    dk, dv, dck_sh = _fox_dkv(q2, kv, do, c2_sh, c2_row, l2_row, delta_row, t=t_attn)
    dw_q = _mm_tn("q_proj_dw", hn_q, dq)
    d_hn_q = _mm_nt("q_proj_dx", dq, wq, out_dtype=F32)

    d_f, d_b = _gate_bwd(dck_sh + dcq_sh, f_raw, b_pad)
    dw_k = _mm_tn("k_proj_dw", hkv, dk)
    dw_v = _mm_tn("v_proj_dw", hkv, dv)
    dw_f = _mm_tn("f_proj_dw", hkv, d_f)
    d_hkv = _mm_nt("k_proj_dx", dk, w_kv, b_cols=(0, D), out_dtype=F32)
    d_hkv = _mm_nt("v_proj_dx", dv, w_kv, b_cols=(D, D), out_dtype=F32, residual=d_hkv)
    d_hkv = _mm_nt("f_proj_dx", d_f, w_f, out_dtype=F32, residual=d_hkv)
    dh2, d_norm_kv_q = _rms_bwd("rms_kv_q_bwd", h2, gains_kv_q, [d_hkv, d_hn_q], dh3)

    d_hn, dw_up0, dw_down0 = _mlp_bwd("l0", dh2, hn_m0, u0, uu0, g_up, g_down, 0)
    dh1, d_norm_mlp0 = _rms_bwd("rms_mlp0_bwd", h1, norm_mlp[0:1], [d_hn], dh2)
    grad_x, dw_pool, d_scale, d_norm_mix0 = _pool_bwd(x2, dh1, diff, norm_mix[0:1], pw, scale_full)
    return (loss_part, grad_x, d_norm_mix0, d_norm_kv_q, d_norm_mlp0, d_norm_mlp1, d_norm_out, d_scale, d_b, dw_pool,
            dw_k, dw_v, dw_f, dw_q, dw_o, dw_up0, dw_up1, dw_down0, dw_down1)


def _rows(a):
    flat = a.reshape(-1)
    pad = (-flat.shape[0]) % LANES
    if pad:
        flat = jnp.pad(flat, (0, pad))
    return flat.reshape(-1, LANES)


def kernel(x, norm_mix, norm_mlp, pool_w, pool_scale, norm_kv, w_kvf, b_f, w_q, w_o, w_up, w_down, norm_out, loss_target, m_norm_mix, m_norm_mlp, m_pool_w, m_pool_scale, m_norm_kv, m_w_kvf, m_b_f, m_w_q, m_w_o, m_w_up, m_w_down, m_norm_out, v_norm_mix, v_norm_mlp, v_pool_w, v_pool_scale, v_norm_kv, v_w_kvf, v_b_f, v_w_q, v_w_o, v_w_up, v_w_down, v_norm_out):
    _, S, D = x.shape
    H = D // HEAD_DIM
    dg = D // N_GROUPS
    n_kvf = 2 * D + H
    kvf_b = w_kvf.shape[1]
    fb = w_up.shape[2]
    F = N_DEV * fb
    ps_b = pool_scale.shape[1]
    xi, yi, ci = _position()
    my_block = 4 * xi + 2 * yi + ci

    g_pool, g_kvf, g_q, g_o, g_up, g_down, g_scale = _all_gather(
        [pool_w.astype(BF16), w_kvf.astype(BF16), w_q.astype(BF16), w_o.astype(BF16), w_up.astype(BF16),
         w_down.astype(BF16), pool_scale])
    pw = g_pool[:, 0].transpose(1, 0, 2, 3).reshape(N_GROUPS, dg, dg)
    wkvf = g_kvf.transpose(1, 0, 2).reshape(D, n_kvf)
    w_kv = wkvf[:, :2 * D]
    w_f = jnp.pad(wkvf[:, 2 * D:], ((0, 0), (0, LANES - H)))
    wq = g_q.reshape(D, D)
    wo = g_o.reshape(D, D)
    scale_full = g_scale.reshape(1, D)
    b_pad = jnp.pad(b_f, (0, LANES - H)).reshape(1, LANES)

    (loss_part, grad_x, d_norm_mix0, d_norm_kv_q, d_norm_mlp0, d_norm_mlp1, d_norm_out, d_scale, d_b, dw_pool, dw_k,
     dw_v, dw_f, dw_q, dw_o, dw_up0, dw_up1, dw_down0, dw_down1) = _local_grads(
        x.reshape(S, D), loss_target.reshape(S, D), norm_mix, norm_mlp, norm_kv, norm_out, b_pad, pw, scale_full,
        w_kv, w_f, wq, wo, g_up, g_down)

    dw_kvf = jnp.concatenate([dw_k, dw_v, dw_f[:, :H]], axis=1)
    full = [
        dw_pool.astype(BF16).reshape(N_GROUPS, N_DEV, dg // N_DEV, dg).transpose(1, 0, 2, 3).reshape(
            N_DEV, N_GROUPS * dg // N_DEV, dg),
        dw_kvf.reshape(D, N_DEV, kvf_b).transpose(1, 0, 2),
        dw_q.reshape(N_DEV, D // N_DEV, D),
        dw_o.reshape(N_DEV, D // N_DEV, D),
        dw_up0, dw_up1,
        dw_down0.reshape(N_DEV, fb, D), dw_down1.reshape(N_DEV, fb, D),
    ]
    got = _exchange_sibling(full)
    core = ci.astype(jnp.int32).reshape(1)
    names = ["pool", "kvf", "q", "o", "up0", "up1", "down0", "down1"]
    chip_parts = [_add_sibling(f"rs_add_{nm}", g, r, core) for nm, g, r in zip(names, full, got)]
    parts = _exchange_chips(chip_parts)

    small = jnp.concatenate([
        _rows(jnp.concatenate([d_norm_mix0, d_norm_kv_q[1:2]], axis=0)),
        _rows(jnp.concatenate([d_norm_mlp0, d_norm_mlp1], axis=0)),
        _rows(d_norm_kv_q[0:1]),
        _rows(d_norm_out),
        _rows(d_scale),
        d_b,
        jnp.pad(loss_part[0:1, 0:1], ((0, 0), (0, LANES - 1))),
    ], axis=0)
    n_small = small.shape[0]
    small = jnp.pad(small, ((0, (-n_small) % 8), (0, 0)))
    total = _all_reduce_small(small)
    rd = D // LANES
    loss = total[7 * rd + 1, 0]
    g_scale_mine = lax.dynamic_slice(total[6 * rd:7 * rd].reshape(D), (my_block * ps_b,), (ps_b,))

    def pack(nm_, nl_, kv_, out_, bf_, ps_):
        return jnp.concatenate([_rows(nm_), _rows(nl_), _rows(kv_), _rows(out_), _rows(bf_), _rows(ps_)], axis=0)

    g_small = jnp.concatenate([total[:6 * rd], total[7 * rd:7 * rd + 1], _rows(g_scale_mine)], axis=0)
    w_small = pack(norm_mix, norm_mlp, norm_kv, norm_out, b_f, pool_scale)
    m_small = pack(m_norm_mix, m_norm_mlp, m_norm_kv, m_norm_out, m_b_f, m_pool_scale)
    v_small = pack(v_norm_mix, v_norm_mlp, v_norm_kv, v_norm_out, v_b_f, v_pool_scale)
    rs = g_small.shape[0]
    padr = (-rs) % 8
    pad8 = lambda a: jnp.pad(a, ((0, padr), (0, 0)))
    small_out = _adamw("adamw_small", pad8(g_small)[None], pad8(w_small), pad8(m_small),
                       jnp.pad(v_small, ((0, padr), (0, 0)), constant_values=1.0), tr=rs + padr)

    def unpack(a):
        o0 = 0
        res = []
        for shape in [(2, D), (2, D), (D,), (D,)]:
            nr = (2 * rd) if len(shape) == 2 else rd
            res.append(a[o0:o0 + nr].reshape(shape))
            o0 += nr
        res.append(a[o0, :H])
        res.append(a[o0 + 1:o0 + 1 + ps_b // LANES].reshape(1, ps_b))
        return res

    small_res = [unpack(a) for a in small_out]

    p_pool, p_kvf, p_q, p_o, p_up0, p_up1, p_down0, p_down1 = parts
    r_pool = _adamw("adamw_pool", p_pool, pool_w.reshape(-1, dg), m_pool_w.reshape(-1, dg), v_pool_w.reshape(-1, dg))
    r_kvf = _adamw("adamw_kvf", p_kvf, w_kvf, m_w_kvf, v_w_kvf)
    r_q = _adamw("adamw_q", p_q, w_q[0], m_w_q[0], v_w_q[0])
    r_o = _adamw("adamw_o", p_o, w_o[0], m_w_o[0], v_w_o[0])
    r_up = [_adamw(f"adamw_up{l}", p, w_up, m_w_up, v_w_up, layer=l) for l, p in enumerate([p_up0, p_up1])]
    r_down = [_adamw(f"adamw_down{l}", p, w_down, m_w_down, v_w_down, layer=l) for l, p in enumerate([p_down0, p_down1])]

    def big(kind):
        sm = small_res[kind]
        return [
            sm[0], sm[1],
            r_pool[kind].reshape(pool_w.shape),
            sm[5],
            sm[2],
            r_kvf[kind],
            sm[4],
            r_q[kind][None], r_o[kind][None],
            jnp.stack([r_up[0][kind], r_up[1][kind]]),
            jnp.stack([r_down[0][kind], r_down[1][kind]]),
            sm[3],
        ]

    return (loss, grad_x.reshape(x.shape), *big(0), *big(1), *big(2), *big(3))
```

```python
import jax
import jax.numpy as jnp
from jax import lax
from jax.experimental import pallas as pl
from jax.experimental.pallas import tpu as pltpu

F32 = jnp.float32
BF16 = jnp.bfloat16
MESH = pl.DeviceIdType.MESH

N_DEV = 8
EPS = 1e-6
NEG_INF = -1e30
POOL_WINDOWS = (2, 4, 8, 16)
N_GROUPS = len(POOL_WINDOWS)
POOL_HALO = 16
HEAD_DIM = 128
LANES = 128
HEADS_PER_STEP = 2
LOG2E = 1.4426950408889634
LN2 = 0.6931471805599453

ADAM_LR = 0.001
ADAM_B1 = 0.9
ADAM_B2 = 0.999
ADAM_EPS = 1e-08
ADAM_WD = 0.01
ADAM_STEP = 10

VMEM_LIMIT = 56 * 1024 * 1024

_ANY = pl.BlockSpec(memory_space=pl.ANY)


def _tile(n, want):
    t = min(n, want)
    assert n % t == 0, (n, want)
    return t


def _params(sem, vmem=VMEM_LIMIT):
    return pltpu.CompilerParams(dimension_semantics=sem, vmem_limit_bytes=vmem)


def _position():
    return lax.axis_index("x"), lax.axis_index("y"), lax.axis_index("c")


class _Exchange:
    def __init__(self, kind, arrays):
        assert kind in ("gather", "scatter")
        self.kind = kind
        self.arrays = list(arrays)
        self.n = len(self.arrays)
        shapes = [(N_DEV, *a.shape) if kind == "gather" else a.shape for a in self.arrays]
        self.out_shape = [jax.ShapeDtypeStruct(s, a.dtype) for s, a in zip(shapes, self.arrays)]
        self.scratch = [pltpu.SemaphoreType.DMA((self.n, N_DEV)), pltpu.SemaphoreType.DMA((self.n, N_DEV))]

    def _copies(self, ins, outs, send_sems, recv_sems, with_recv=True):
        x, y, c = _position()
        me = 4 * x + 2 * y + c
        gather = self.kind == "gather"
        local, sends, recvs = [], [], []
        for a in range(self.n):
            local.append(pltpu.make_async_copy(ins[a] if gather else ins[a].at[me], outs[a].at[me], send_sems.at[a, 0]))
            for k in range(1, N_DEV):
                px, py, pc = x ^ (k >> 2), y ^ ((k >> 1) & 1), c ^ (k & 1)
                peer = 4 * px + 2 * py + pc
                src = ins[a] if gather else ins[a].at[peer]
                common = dict(send_sem=send_sems.at[a, k], recv_sem=recv_sems.at[a, k], device_id=(px, py, pc),
                              device_id_type=MESH)
                sends.append(pltpu.make_async_remote_copy(src_ref=src, dst_ref=outs[a].at[me], **common))
                if with_recv:
                    recvs.append(pltpu.make_async_remote_copy(src_ref=src, dst_ref=outs[a].at[peer], **common))
        return local, sends, recvs

    def start(self, ins, outs, send_sems, recv_sems):
        local, sends, _ = self._copies(ins, outs, send_sems, recv_sems, with_recv=False)
        for cp in local + sends:
            cp.start()

    def wait(self, ins, outs, send_sems, recv_sems):
        local, sends, recvs = self._copies(ins, outs, send_sems, recv_sems)
        for send, recv in zip(sends, recvs):
            send.wait_send()
            recv.wait_recv()
        for cp in local:
            cp.wait()


def _call(body, *, name, grid, in_specs, out_specs, out_shape, args, scratch_shapes=(), sem=None, exchange=None):
    if exchange is None:
        outs = pl.pallas_call(
            body, grid=grid, in_specs=in_specs, out_specs=out_specs, out_shape=out_shape,
            scratch_shapes=list(scratch_shapes), compiler_params=_params(sem), name=name)(*args)
        return list(outs), []

    n_in, n_out, n_scr, n = len(in_specs), len(out_specs), len(scratch_shapes), exchange.n

    def hosted(*refs):
        ins, refs = refs[:n_in], refs[n_in:]
        ex_in, refs = refs[:n], refs[n:]
        outs, refs = refs[:n_out], refs[n_out:]
        ex_out, refs = refs[:n], refs[n:]
        scratch, sems = refs[:n_scr], refs[n_scr:]
        first = _all_true([pl.program_id(d) == 0 for d in range(len(grid))])
        last = _all_true([pl.program_id(d) == grid[d] - 1 for d in range(len(grid))])

        @pl.when(first)
        def _():
            exchange.start(ex_in, ex_out, *sems)

        body(*ins, *outs, *scratch)

        @pl.when(last)
        def _():
            exchange.wait(ex_in, ex_out, *sems)

    res = pl.pallas_call(
        hosted, grid=grid, in_specs=[*in_specs, *[_ANY] * n], out_specs=[*out_specs, *[_ANY] * n],
        out_shape=[*out_shape, *exchange.out_shape], scratch_shapes=[*scratch_shapes, *exchange.scratch],
        compiler_params=_params(("arbitrary",) * len(grid)), name=name)(*args, *exchange.arrays)
    return list(res[:n_out]), list(res[n_out:])


def _all_true(preds):
    out = preds[0]
    for p in preds[1:]:
        out = jnp.logical_and(out, p)
    return out


def _exchange_now(name, exchange):
    def body(*refs):
        n = exchange.n
        exchange.start(refs[:n], refs[n:2 * n], *refs[2 * n:])
        exchange.wait(refs[:n], refs[n:2 * n], *refs[2 * n:])

    return pl.pallas_call(
        body, in_specs=[_ANY] * exchange.n, out_specs=[_ANY] * exchange.n, out_shape=exchange.out_shape,
        scratch_shapes=exchange.scratch, name=name)(*exchange.arrays)


def _all_gather_two_level(shards):
    n = len(shards)

    def body(*refs):
        ins, outs = refs[:n], refs[n:2 * n]
        send_sems, recv_sems, local_sems = refs[2 * n:]
        x, y, c = _position()
        me, sibling = (x, y, c), (x, y, 1 - c)
        chips = [(1 - x, y), (x, 1 - y), (1 - x, 1 - y)]

        def slot(a, px, py, pc):
            return outs[a].at[4 * px + 2 * py + pc]

        def copy(a, k, block, to, src=None):
            return pltpu.make_async_remote_copy(
                src_ref=slot(a, *block) if src is None else src, dst_ref=slot(a, *block),
                send_sem=send_sems.at[a, k], recv_sem=recv_sems.at[a, k], device_id=to, device_id_type=MESH)

        mine = [pltpu.make_async_copy(ins[a], slot(a, *me), local_sems.at[a]) for a in range(n)]
        first = []
        for a in range(n):
            mine[a].start()
            first.append(copy(a, 0, me, sibling, src=ins[a]))
            first += [copy(a, 1 + j, me, (*chip, c), src=ins[a]) for j, chip in enumerate(chips)]
        for cp in first:
            cp.start()
        passed = []
        for a in range(n):
            for j, chip in enumerate(chips):
                copy(a, 1 + j, (*chip, c), me).wait_recv()
                fwd = copy(a, 4 + j, (*chip, c), sibling)
                fwd.start()
                passed.append(fwd)
        for a in range(n):
            copy(a, 0, sibling, me).wait_recv()
            for j, chip in enumerate(chips):
                copy(a, 4 + j, (*chip, 1 - c), me).wait_recv()
        for cp in first + passed:
            cp.wait_send()
        for cp in mine:
            cp.wait()

    return pl.pallas_call(
        body, in_specs=[_ANY] * n, out_specs=[_ANY] * n,
        out_shape=[jax.ShapeDtypeStruct((N_DEV, *s.shape), s.dtype) for s in shards],
        scratch_shapes=[pltpu.SemaphoreType.DMA((n, 7)), pltpu.SemaphoreType.DMA((n, 7)),
                        pltpu.SemaphoreType.DMA((n,))],
        name="all_gather_first")(*shards)


def _all_reduce_small(vec):
    R = vec.shape[0]

    def body(v_ref, o_ref, buf_ref, send_sems, recv_sems):
        x, y, c = _position()
        me = 4 * x + 2 * y + c
        buf_ref[me] = v_ref[...]
        copies = []
        for k in range(1, N_DEV):
            peer = (x ^ (k >> 2), y ^ ((k >> 1) & 1), c ^ (k & 1))
            copies.append(pltpu.make_async_remote_copy(
                src_ref=buf_ref.at[me], dst_ref=buf_ref.at[me], send_sem=send_sems.at[k], recv_sem=recv_sems.at[k],
                device_id=peer, device_id_type=MESH))
        for cp in copies:
            cp.start()
        for cp in copies:
            cp.wait()
        total = buf_ref[0]
        for d in range(1, N_DEV):
            total = total + buf_ref[d]
        o_ref[...] = total

    vm = pl.BlockSpec(memory_space=pltpu.VMEM)
    return pl.pallas_call(
        body, in_specs=[vm], out_specs=vm, out_shape=jax.ShapeDtypeStruct((R, LANES), F32),
        scratch_shapes=[pltpu.VMEM((N_DEV, R, LANES), F32), pltpu.SemaphoreType.DMA((N_DEV,)),
                        pltpu.SemaphoreType.DMA((N_DEV,))],
        name="all_reduce_small")(vec)


_DIMS = {
    "nn": (((1,), (0,)), ((), ())),
    "nt": (((1,), (1,)), ((), ())),
    "tn": (((0,), (0,)), ((), ())),
}


def _matmul(name, a, b, *, mode, grid, tm, tn, a_spec, b_spec, outs, extras=(), extra_specs=(), epilogue=None,
            exchange=None):
    nk = grid[2]
    n_extra = len(extras)
    dn = _DIMS[mode]

    def body(a_ref, b_ref, *rest):
        extra_refs = rest[:n_extra]
        out_refs = rest[n_extra:-1]
        acc_ref = rest[-1]
        k = pl.program_id(2)

        @pl.when(k == 0)
        def _():
            acc_ref[...] = jnp.zeros_like(acc_ref)

        acc_ref[...] += lax.dot_general(a_ref[...].astype(BF16), b_ref[...].astype(BF16), dn, preferred_element_type=F32)

        @pl.when(k == nk - 1)
        def _():
            acc = acc_ref[...]
            vals = (acc,) if epilogue is None else epilogue(acc, *[r[...] for r in extra_refs])
            for o_ref, val in zip(out_refs, vals):
                o_ref[...] = val.astype(o_ref.dtype)

    return _call(
        body, name=name, grid=grid, in_specs=[a_spec, b_spec, *extra_specs], out_specs=[o[2] for o in outs],
        out_shape=[jax.ShapeDtypeStruct(o[0], o[1]) for o in outs], scratch_shapes=[pltpu.VMEM((tm, tn), F32)],
        sem=("parallel", "parallel", "arbitrary"), args=(a, b, *extras), exchange=exchange)


def _ij(tm, tn):
    return pl.BlockSpec((tm, tn), lambda i, j, k: (i, j))


def _mm_nn(name, a, b, *, out_dtype, residual=None, out_scale=None, tm=1024, tn=1024, tk=2048):
    M, K = a.shape
    N = b.shape[1]
    tm, tn, tk = _tile(M, tm), _tile(N, tn), _tile(K, tk)
    extras, especs, epi = (), (), None
    if residual is not None:
        extras, especs = (residual,), (_ij(tm, tn),)
        epi = lambda acc, r: (acc + r,)
    elif out_scale is not None:
        epi = lambda acc: (acc * out_scale,)
    return _matmul(
        name, a, b, mode="nn", grid=(M // tm, N // tn, K // tk), tm=tm, tn=tn,
        a_spec=pl.BlockSpec((tm, tk), lambda i, j, k: (i, k)),
        b_spec=pl.BlockSpec((tk, tn), lambda i, j, k: (k, j)),
        outs=[((M, N), out_dtype, _ij(tm, tn))], extras=extras, extra_specs=especs, epilogue=epi)[0][0]


def _mm_nt(name, a, b, *, out_dtype, b_cols=None, residual=None, tm=1024, tn=1024, tk=2048):
    M, K = a.shape
    N = b.shape[0]
    c0 = 0 if b_cols is None else b_cols[0]
    tm, tn, tk = _tile(M, tm), _tile(N, tn), _tile(K, tk)
    assert c0 % tk == 0
    kb0 = c0 // tk
    extras, especs, epi = (), (), None
    if residual is not None:
        extras, especs = (residual,), (_ij(tm, tn),)
        epi = lambda acc, r: (acc + r,)
    return _matmul(
        name, a, b, mode="nt", grid=(M // tm, N // tn, K // tk), tm=tm, tn=tn,
        a_spec=pl.BlockSpec((tm, tk), lambda i, j, k: (i, k)),
        b_spec=pl.BlockSpec((tn, tk), lambda i, j, k: (j, kb0 + k)),
        outs=[((M, N), out_dtype, _ij(tm, tn))], extras=extras, extra_specs=especs, epilogue=epi)[0][0]


def _mm_tn(name, a, b, *, exchange=None, tm=1024, tn=1024, tk=1024):
    K, M = a.shape
    N = b.shape[1]
    tm, tn, tk = _tile(M, tm), _tile(N, tn), _tile(K, tk)
    outs, got = _matmul(
        name, a, b, mode="tn", grid=(M // tm, N // tn, K // tk), tm=tm, tn=tn,
        a_spec=pl.BlockSpec((tk, tm), lambda i, j, k: (k, i)),
        b_spec=pl.BlockSpec((tk, tn), lambda i, j, k: (k, j)),
        outs=[((M, N), BF16, _ij(tm, tn))], exchange=exchange)
    return outs[0], got


def _mlp_fwd(tag, hn, h, w_up_g, w_down_g, *, ex_up=None, ex_down=None, tm=1024, tk=2048):
    S, D = hn.shape
    fb = w_up_g.shape[2]
    F = N_DEV * fb
    tm, tku = _tile(S, tm), _tile(D, tk)

    def up_epi(acc):
        u = jnp.maximum(acc, 0.0)
        return u, u * u

    (u, uu), got_up = _matmul(
        f"mlp_up_{tag}", hn, w_up_g, mode="nn", grid=(S // tm, N_DEV, D // tku), tm=tm, tn=fb,
        a_spec=pl.BlockSpec((tm, tku), lambda i, j, k: (i, k)),
        b_spec=pl.BlockSpec((None, tku, fb), lambda i, j, k: (j, k, 0)),
        outs=[((S, F), BF16, _ij(tm, fb)), ((S, F), BF16, _ij(tm, fb))], epilogue=up_epi, exchange=ex_up)

    tn = _tile(D, 1024)
    tkd = _tile(fb, tk)
    r = fb // tkd
    (h_out,), got_down = _matmul(
        f"mlp_down_{tag}", uu, w_down_g, mode="nn", grid=(S // tm, D // tn, F // tkd), tm=tm, tn=tn,
        a_spec=pl.BlockSpec((tm, tkd), lambda i, j, k: (i, k)),
        b_spec=pl.BlockSpec((None, tkd, tn), lambda i, j, k: (k // r, k % r, j)),
        outs=[((S, D), F32, _ij(tm, tn))], extras=(h,), extra_specs=(_ij(tm, tn),),
        epilogue=lambda acc, res: (acc + res,), exchange=ex_down)
    return u, uu, h_out, got_up, got_down


def _mlp_dpre(tag, dh, u, w_down_g, *, tm=1024, tk=2048):
    S, D = dh.shape
    fb = w_down_g.shape[1]
    tm, tkd = _tile(S, tm), _tile(D, tk)
    return _matmul(
        f"mlp_dpre_{tag}", dh, w_down_g, mode="nt", grid=(S // tm, N_DEV, D // tkd), tm=tm, tn=fb,
        a_spec=pl.BlockSpec((tm, tkd), lambda i, j, k: (i, k)),
        b_spec=pl.BlockSpec((None, fb, tkd), lambda i, j, k: (j, 0, k)),
        outs=[((S, N_DEV * fb), BF16, _ij(tm, fb))], extras=(u,), extra_specs=(_ij(tm, fb),),
        epilogue=lambda acc, uv: (acc * (2.0 * uv.astype(F32)),))[0][0]


def _mlp_dwup(tag, hn, d_pre, fb, *, exchange=None):
    S, D = hn.shape
    tmu = _tile(D, 1024)
    tks = _tile(S, 1024)
    outs, got = _matmul(
        f"mlp_dwup_{tag}", hn, d_pre, mode="tn", grid=(D // tmu, N_DEV, S // tks), tm=tmu, tn=fb,
        a_spec=pl.BlockSpec((tks, tmu), lambda i, j, k: (k, i)),
        b_spec=pl.BlockSpec((tks, fb), lambda i, j, k: (k, j)),
        outs=[((N_DEV, D, fb), BF16, pl.BlockSpec((None, tmu, fb), lambda i, j, k: (j, i, 0)))], exchange=exchange)
    return outs[0], got


def _mlp_dhn(tag, d_pre, w_up_g, *, exchange=None, tm=1024, tk=2048):
    S, F = d_pre.shape
    D, fb = w_up_g.shape[1], w_up_g.shape[2]
    tm, tn = _tile(S, tm), _tile(D, 1024)
    tkf = _tile(fb, tk)
    r = fb // tkf
    outs, got = _matmul(
        f"mlp_dhn_{tag}", d_pre, w_up_g, mode="nt", grid=(S // tm, D // tn, F // tkf), tm=tm, tn=tn,
        a_spec=pl.BlockSpec((tm, tkf), lambda i, j, k: (i, k)),
        b_spec=pl.BlockSpec((None, tn, tkf), lambda i, j, k: (k // r, j, k % r)),
        outs=[((S, D), F32, _ij(tm, tn))], exchange=exchange)
    return outs[0], got


def _row_spec(ts, D):
    return pl.BlockSpec((ts, D), lambda i: (i, 0))


def _vec_spec(n, D):
    return pl.BlockSpec((n, D), lambda i: (0, 0))


def _rms_fwd(name, x, gains, *, ts=512):
    S, D = x.shape
    n = gains.shape[0]
    ts = _tile(S, ts)

    def body(x_ref, g_ref, *o_refs):
        xv = x_ref[...]
        y = xv * lax.rsqrt(jnp.mean(xv * xv, axis=-1, keepdims=True) + EPS)
        for i, o_ref in enumerate(o_refs):
            o_ref[...] = (y * g_ref[i:i + 1, :]).astype(o_ref.dtype)

    return pl.pallas_call(
        body, grid=(S // ts,), in_specs=[_row_spec(ts, D), _vec_spec(n, D)],
        out_specs=[_row_spec(ts, D)] * n, out_shape=[jax.ShapeDtypeStruct((S, D), BF16)] * n,
        compiler_params=_params(("parallel",)), name=name)(x, gains)


def _rms_bwd(name, x, gains, dys, res, *, ts=512):
    S, D = x.shape
    n = gains.shape[0]
    ts = _tile(S, ts)

    def body(x_ref, g_ref, *rest):
        dy_refs = rest[:n]
        res_ref, dx_ref, dg_ref = rest[n:]
        i = pl.program_id(0)
        xv = x_ref[...]
        r = lax.rsqrt(jnp.mean(xv * xv, axis=-1, keepdims=True) + EPS)
        xhat = xv * r
        dxhat = None
        dgs = []
        for k in range(n):
            dy = dy_refs[k][...].astype(F32)
            dgs.append(jnp.sum(dy * xhat, axis=0, keepdims=True))
            term = dy * g_ref[k:k + 1, :]
            dxhat = term if dxhat is None else dxhat + term
        dx_ref[...] = res_ref[...] + r * (dxhat - xhat * jnp.mean(dxhat * xhat, axis=-1, keepdims=True))
        dg = jnp.concatenate(dgs, axis=0) if n > 1 else dgs[0]

        @pl.when(i == 0)
        def _():
            dg_ref[...] = dg

        @pl.when(i > 0)
        def _():
            dg_ref[...] += dg

    return pl.pallas_call(
        body, grid=(S // ts,),
        in_specs=[_row_spec(ts, D), _vec_spec(n, D)] + [_row_spec(ts, D)] * (n + 1),
        out_specs=[_row_spec(ts, D), _vec_spec(n, D)],
        out_shape=[jax.ShapeDtypeStruct((S, D), F32), jax.ShapeDtypeStruct((n, D), F32)],
        compiler_params=_params(("arbitrary",)), name=name)(x, gains, *dys, res)


def _loss_head(h, gain, target, *, ts=512):
    S, D = h.shape
    ts = _tile(S, ts)

    def body(x_ref, g_ref, t_ref, dx_ref, dg_ref, loss_ref):
        i = pl.program_id(0)
        xv = x_ref[...]
        g = g_ref[...]
        r = lax.rsqrt(jnp.mean(xv * xv, axis=-1, keepdims=True) + EPS)
        xhat = xv * r
        err = xhat * g - t_ref[...]
        part = 0.5 * jnp.sum(jnp.mean(err * err, axis=-1, keepdims=True), axis=0, keepdims=True)
        dy = err * (1.0 / D)
        dg = jnp.sum(dy * xhat, axis=0, keepdims=True)
        dxhat = dy * g
        dx_ref[...] = r * (dxhat - xhat * jnp.mean(dxhat * xhat, axis=-1, keepdims=True))

        @pl.when(i == 0)
        def _():
            dg_ref[...] = dg
            loss_ref[...] = jnp.broadcast_to(part, loss_ref.shape)

        @pl.when(i > 0)
        def _():
            dg_ref[...] += dg
            loss_ref[...] += jnp.broadcast_to(part, loss_ref.shape)

    return pl.pallas_call(
        body, grid=(S // ts,),
        in_specs=[_row_spec(ts, D), _vec_spec(1, D), _row_spec(ts, D)],
        out_specs=[_row_spec(ts, D), _vec_spec(1, D), pl.BlockSpec((8, LANES), lambda i: (0, 0))],
        out_shape=[jax.ShapeDtypeStruct((S, D), F32), jax.ShapeDtypeStruct((1, D), F32),
                   jax.ShapeDtypeStruct((8, LANES), F32)],
        compiler_params=_params(("arbitrary",)), name="loss_head")(h, gain, target)


def _window_counts(t, w):
    return jnp.minimum(t + 1, w).astype(F32)


def _pool_fwd(x, gain, pool_w, scale, *, ts=512):
    S, D = x.shape
    dg = D // N_GROUPS
    ts = _tile(S, ts)
    per = ts // POOL_HALO

    def body(x_ref, xh_ref, g_ref, w_ref, sc_ref, h_ref, diff_ref):
        i = pl.program_id(0)
        g = g_ref[...]

        def norm(v):
            return v * lax.rsqrt(jnp.mean(v * v, axis=-1, keepdims=True) + EPS) * g

        xc = x_ref[...]
        hn_c = norm(xc)
        hn_h = norm(xh_ref[...]) * (i > 0).astype(F32)
        ext = jnp.concatenate([hn_h, hn_c], axis=0)
        t = i * ts + lax.broadcasted_iota(jnp.int32, (ts, 1), 0)
        for gi, w in enumerate(POOL_WINDOWS):
            cols = slice(gi * dg, (gi + 1) * dg)
            s = ext[:, cols]
            step = 1
            while step < w:
                s = s + pltpu.roll(s, step, 0)
                step *= 2
            mean = s[POOL_HALO:] * (1.0 / _window_counts(t, w))
            diff = (mean - hn_c[:, cols]).astype(BF16)
            diff_ref[:, cols] = diff
            mixed = jnp.dot(diff, w_ref[gi], preferred_element_type=F32)
            h_ref[:, cols] = xc[:, cols] + mixed * sc_ref[:, cols]

    return pl.pallas_call(
        body, grid=(S // ts,),
        in_specs=[_row_spec(ts, D),
                  pl.BlockSpec((POOL_HALO, D), lambda i: (jnp.maximum(i * per - 1, 0), 0)),
                  _vec_spec(1, D), pl.BlockSpec((N_GROUPS, dg, dg), lambda i: (0, 0, 0)), _vec_spec(1, D)],
        out_specs=[_row_spec(ts, D), _row_spec(ts, D)],
        out_shape=[jax.ShapeDtypeStruct((S, D), F32), jax.ShapeDtypeStruct((S, D), BF16)],
        compiler_params=_params(("parallel",)), name="pool_fwd")(x, x, gain, pool_w, scale)


def _pool_bwd(x, dh, diff, gain, pool_w, scale, *, ts=256):
    S, D = x.shape
    dg = D // N_GROUPS
    ts = _tile(S, ts)
    per = ts // POOL_HALO
    n_tiles = S // ts
    n_halo = S // POOL_HALO
    ext_rows = ts + POOL_HALO

    def body(x_ref, dh_ref, dhn_ref, diff_ref, g_ref, w_ref, sc_ref, dx_ref, dw_ref, dsc_ref, dgain_ref):
        i = pl.program_id(0)
        xc = x_ref[...]
        g = g_ref[...]
        r = lax.rsqrt(jnp.mean(xc * xc, axis=-1, keepdims=True) + EPS)
        xhat = xc * r
        dh_c = dh_ref[...]
        dh_n = dhn_ref[...] * (i < n_tiles - 1).astype(F32)
        dh_ext = jnp.concatenate([dh_c, dh_n], axis=0)
        t_ext = i * ts + lax.broadcasted_iota(jnp.int32, (ext_rows, 1), 0)
        d_hn_parts, dsc_parts = [], []
        for gi, w in enumerate(POOL_WINDOWS):
            cols = slice(gi * dg, (gi + 1) * dg)
            wg = w_ref[gi]
            dmix = (dh_ext[:, cols] * sc_ref[:, cols]).astype(BF16)
            d_diff = lax.dot_general(dmix, wg, _DIMS["nt"], preferred_element_type=F32)
            diff_c = diff_ref[:, cols]
            dwg = lax.dot_general(diff_c, dmix[:ts], _DIMS["tn"], preferred_element_type=F32)
            mixed = jnp.dot(diff_c, wg, preferred_element_type=F32)
            dsc_parts.append(jnp.sum(dh_c[:, cols] * mixed, axis=0, keepdims=True))
            e = d_diff * (1.0 / _window_counts(t_ext, w))
            step = 1
            while step < w:
                e = e + pltpu.roll(e, ext_rows - step, 0)
                step *= 2
            d_hn_parts.append(e[:ts] - d_diff[:ts])

            @pl.when(i == 0)
            def _():
                dw_ref[gi] = dwg

            @pl.when(i > 0)
            def _():
                dw_ref[gi] += dwg

        d_hn = jnp.concatenate(d_hn_parts, axis=1)
        dsc = jnp.concatenate(dsc_parts, axis=1)
        dgain = jnp.sum(d_hn * xhat, axis=0, keepdims=True)
        dxhat = d_hn * g
        dx_ref[...] = dh_c + r * (dxhat - xhat * jnp.mean(dxhat * xhat, axis=-1, keepdims=True))

        @pl.when(i == 0)
        def _():
            dsc_ref[...] = dsc
            dgain_ref[...] = dgain

        @pl.when(i > 0)
        def _():
            dsc_ref[...] += dsc
            dgain_ref[...] += dgain

    return pl.pallas_call(
        body, grid=(n_tiles,),
        in_specs=[_row_spec(ts, D), _row_spec(ts, D),
                  pl.BlockSpec((POOL_HALO, D), lambda i: (jnp.minimum((i + 1) * per, n_halo - 1), 0)),
                  _row_spec(ts, D), _vec_spec(1, D),
                  pl.BlockSpec((N_GROUPS, dg, dg), lambda i: (0, 0, 0)), _vec_spec(1, D)],
        out_specs=[_row_spec(ts, D), pl.BlockSpec((N_GROUPS, dg, dg), lambda i: (0, 0, 0)),
                   _vec_spec(1, D), _vec_spec(1, D)],
        out_shape=[jax.ShapeDtypeStruct((S, D), F32), jax.ShapeDtypeStruct((N_GROUPS, dg, dg), F32),
                   jax.ShapeDtypeStruct((1, D), F32), jax.ShapeDtypeStruct((1, D), F32)],
        compiler_params=_params(("arbitrary",)), name="pool_bwd")(x, dh, dh, diff, gain, pool_w, scale)


def _gate_fwd(f_raw, b_pad, *, ts=512):
    S = f_raw.shape[0]
    ts = _tile(S, ts)

    def body(f_ref, b_ref, c_ref, carry_ref):
        i = pl.program_id(0)

        @pl.when(i == 0)
        def _():
            carry_ref[...] = jnp.zeros_like(carry_ref)

        z = f_ref[...] + b_ref[...]
        v = jnp.minimum(z, 0.0) - jnp.log1p(jnp.exp(-jnp.abs(z)))
        row = lax.broadcasted_iota(jnp.int32, (ts, 1), 0)
        step = 1
        while step < ts:
            v = v + jnp.where(row >= step, pltpu.roll(v, step, 0), 0.0)
            step *= 2
        out = v + carry_ref[0:1, :]
        c_ref[...] = out
        carry_ref[...] = jnp.broadcast_to(out[ts - 1:ts, :], carry_ref.shape)

    return pl.pallas_call(
        body, grid=(S // ts,),
        in_specs=[pl.BlockSpec((ts, LANES), lambda i: (i, 0)), pl.BlockSpec((1, LANES), lambda i: (0, 0))],
        out_specs=pl.BlockSpec((ts, LANES), lambda i: (i, 0)),
        out_shape=jax.ShapeDtypeStruct((S, LANES), F32),
        scratch_shapes=[pltpu.VMEM((8, LANES), F32)],
        compiler_params=_params(("arbitrary",)), name="gate_fwd")(f_raw, b_pad)


def _gate_bwd(dc, f_raw, b_pad, *, ts=512):
    S = f_raw.shape[0]
    ts = _tile(S, ts)
    n = S // ts

    def body(dc_ref, f_ref, b_ref, df_ref, db_ref, carry_ref):
        i = pl.program_id(0)

        @pl.when(i == 0)
        def _():
            carry_ref[...] = jnp.zeros_like(carry_ref)

        v = dc_ref[...]
        row = lax.broadcasted_iota(jnp.int32, (ts, 1), 0)
        step = 1
        while step < ts:
            v = v + jnp.where(row < ts - step, pltpu.roll(v, ts - step, 0), 0.0)
            step *= 2
        d_logf = v + carry_ref[0:1, :]
        carry_ref[...] = jnp.broadcast_to(d_logf[0:1, :], carry_ref.shape)
        z = f_ref[...] + b_ref[...]
        df = d_logf / (1.0 + jnp.exp(z))
        df_ref[...] = df
        db = jnp.sum(df, axis=0, keepdims=True)

        @pl.when(i == 0)
        def _():
            db_ref[...] = db

        @pl.when(i > 0)
        def _():
            db_ref[...] += db

    rev = lambda i: (n - 1 - i, 0)
    return pl.pallas_call(
        body, grid=(n,),
        in_specs=[pl.BlockSpec((ts, LANES), rev), pl.BlockSpec((ts, LANES), rev),
                  pl.BlockSpec((1, LANES), lambda i: (0, 0))],
        out_specs=[pl.BlockSpec((ts, LANES), rev), pl.BlockSpec((1, LANES), lambda i: (0, 0))],
        out_shape=[jax.ShapeDtypeStruct((S, LANES), F32), jax.ShapeDtypeStruct((1, LANES), F32)],
        scratch_shapes=[pltpu.VMEM((8, LANES), F32)],
        compiler_params=_params(("arbitrary",)), name="gate_bwd")(dc, f_raw, b_pad)


def _row_layout(a_hs, t):
    n_heads, S = a_hs.shape
    return a_hs.reshape(n_heads, S // t, 1, t)


def _lane_pick(blk, h):
    lane = lax.broadcasted_iota(jnp.int32, blk.shape, 1)
    return jnp.sum(jnp.where(lane == h, blk, 0.0), axis=-1, keepdims=True)


def _lane_put(ref, h, col, first=True):
    lane = lax.broadcasted_iota(jnp.int32, ref.shape, 1)
    if not first:
        ref[...] = jnp.where(lane == h, col, ref[...])
        return

    @pl.when(h == 0)
    def _():
        ref[...] = jnp.where(lane == 0, col, 0.0)

    @pl.when(h > 0)
    def _():
        ref[...] = jnp.where(lane == h, col, ref[...])


def _causal(s, masked, fill, rows_are_queries=True):
    if not masked:
        return s
    rr = lax.broadcasted_iota(jnp.int32, s.shape, 0)
    cc = lax.broadcasted_iota(jnp.int32, s.shape, 1)
    keep = (cc <= rr) if rows_are_queries else (rr <= cc)
    return jnp.where(keep, s, fill)


def _fox_fwd(q2, kv, c_row, *, t=512, exchange=None):
    S, D = q2.shape
    H = D // HEAD_DIM
    t = _tile(S, t)
    nb = S // t
    hp = HEADS_PER_STEP
    wide = hp * HEAD_DIM

    def body(q_ref, k_ref, v_ref, cr_ref, o_ref, l2_ref):
        i = pl.program_id(0)
        g = pl.program_id(1)
        cols = [slice(a * HEAD_DIM, (a + 1) * HEAD_DIM) for a in range(hp)]
        refs_i = [cr_ref[a, i][:, 0:1] for a in range(hp)]
        qs = [q_ref[:, cols[a]] for a in range(hp)]

        def step(j, carry, masked):
            r0 = pl.multiple_of(j * t, t)
            out = []
            for a in range(hp):
                m, l, acc = carry[3 * a:3 * a + 3]
                kb = k_ref[pl.ds(r0, t), cols[a]]
                vb = v_ref[pl.ds(r0, t), cols[a]]
                ck = cr_ref[a, j] - refs_i[a]
                s = lax.dot_general(qs[a], kb, _DIMS["nt"], preferred_element_type=F32) - ck
                s = _causal(s, masked, NEG_INF)
                m_new = jnp.maximum(m, jnp.max(s, axis=-1, keepdims=True))
                alpha = jnp.exp2(m - m_new)
                p = jnp.exp2(s - m_new)
                l = alpha * l + jnp.sum(p, axis=-1, keepdims=True)
                acc = alpha * acc + jnp.dot(p.astype(BF16), vb, preferred_element_type=F32)
                out += [m_new, l, acc]
            return tuple(out)

        init = (jnp.full((t, 1), NEG_INF, F32), jnp.zeros((t, 1), F32), jnp.zeros((t, HEAD_DIM), F32)) * hp
        carry = lax.fori_loop(0, i, lambda j, c: step(j, c, False), init)
        carry = step(i, carry, True)
        for a in range(hp):
            m, l, acc = carry[3 * a:3 * a + 3]
            o_ref[:, cols[a]] = (acc / l).astype(o_ref.dtype)
            _lane_put(l2_ref, g * hp + a, m + jnp.log2(l), first=(a == 0))

    (o, l2_sh), got = _call(
        body, name="fox_fwd", grid=(nb, H // hp),
        in_specs=[pl.BlockSpec((t, wide), lambda i, g: (i, g)),
                  pl.BlockSpec((S, wide), lambda i, g: (0, g)),
                  pl.BlockSpec((S, wide), lambda i, g: (0, H // hp + g)),
                  pl.BlockSpec((hp, nb, 1, t), lambda i, g: (g, 0, 0, 0))],
        out_specs=[pl.BlockSpec((t, wide), lambda i, g: (i, g)),
                   pl.BlockSpec((t, LANES), lambda i, g: (i, 0))],
        out_shape=[jax.ShapeDtypeStruct((S, D), BF16), jax.ShapeDtypeStruct((S, LANES), F32)],
        sem=("parallel", "arbitrary"), args=(q2, kv, kv, c_row), exchange=exchange)
    return o, l2_sh, got


def _fox_dq(q2, kv, o, do, c_row, l2_sh, *, t=512, exchange=None):
    S, D = q2.shape
    H = D // HEAD_DIM
    t = _tile(S, t)
    nb = S // t
    scale = HEAD_DIM ** -0.5

    def body(q_ref, k_ref, v_ref, o_ref, do_ref, cr_ref, l2_ref, dq_ref, dl_ref, dcq_ref):
        i = pl.program_id(0)
        h = pl.program_id(1)
        ref_i = cr_ref[i][:, 0:1]
        l2 = _lane_pick(l2_ref[...], h)
        qv = q_ref[...]
        dov = do_ref[...]
        delta = jnp.sum(o_ref[...].astype(F32) * dov.astype(F32), axis=-1, keepdims=True)

        def step(j, carry, masked):
            acc, rowsum = carry
            r0 = pl.multiple_of(j * t, t)
            kb = k_ref[pl.ds(r0, t), :]
            vb = v_ref[pl.ds(r0, t), :]
            s = lax.dot_general(qv, kb, _DIMS["nt"], preferred_element_type=F32) - (cr_ref[j] - ref_i)
            p = _causal(jnp.exp2(s - l2), masked, 0.0)
            dp = lax.dot_general(dov, vb, _DIMS["nt"], preferred_element_type=F32)
            ds = p * (dp - delta)
            rowsum = rowsum + jnp.sum(ds, axis=-1, keepdims=True)
            return acc + jnp.dot(ds.astype(BF16), kb, preferred_element_type=F32), rowsum

        init = (jnp.zeros((t, HEAD_DIM), F32), jnp.zeros((t, 1), F32))
        carry = lax.fori_loop(0, i, lambda j, c: step(j, c, False), init)
        acc, rowsum = step(i, carry, True)
        dq_ref[...] = (acc * scale).astype(dq_ref.dtype)
        _lane_put(dl_ref, h, delta)
        _lane_put(dcq_ref, h, rowsum)

    qspec = pl.BlockSpec((t, HEAD_DIM), lambda i, h: (i, h))
    shspec = pl.BlockSpec((t, LANES), lambda i, h: (i, 0))
    (dq, delta_sh, dcq_sh), got = _call(
        body, name="fox_dq", grid=(nb, H),
        in_specs=[qspec,
                  pl.BlockSpec((S, HEAD_DIM), lambda i, h: (0, h)),
                  pl.BlockSpec((S, HEAD_DIM), lambda i, h: (0, H + h)),
                  qspec, qspec,
                  pl.BlockSpec((None, nb, 1, t), lambda i, h: (h, 0, 0, 0)),
                  shspec],
        out_specs=[qspec, shspec, shspec],
        out_shape=[jax.ShapeDtypeStruct((S, D), BF16), jax.ShapeDtypeStruct((S, LANES), F32),
                   jax.ShapeDtypeStruct((S, LANES), F32)],
        sem=("parallel", "arbitrary"), args=(q2, kv, kv, o, do, c_row, l2_sh), exchange=exchange)
    return dq, delta_sh, dcq_sh, got


def _fox_dkv(q2, kv, do, c_sh, c_row, l2_row, delta_row, *, t=512, exchange=None):
    S, D = q2.shape
    H = D // HEAD_DIM
    t = _tile(S, t)
    nb = S // t

    def body(q_ref, do_ref, k_ref, v_ref, csh_ref, cr_ref, l2_ref, dl_ref, dk_ref, dv_ref, dck_ref):
        j = pl.program_id(0)
        h = pl.program_id(1)
        ck = _lane_pick(csh_ref[...], h)
        kb = k_ref[...]
        vb = v_ref[...]

        def step(i, carry, masked):
            dk, dv, dc = carry
            r0 = pl.multiple_of(i * t, t)
            qb = q_ref[pl.ds(r0, t), :]
            dob = do_ref[pl.ds(r0, t), :]
            ref_i = cr_ref[i][:, 0:1]
            s = lax.dot_general(kb, qb, _DIMS["nt"], preferred_element_type=F32) - (ck - ref_i)
            p = _causal(jnp.exp2(s - l2_ref[i]), masked, 0.0, rows_are_queries=False)
            dv = dv + jnp.dot(p.astype(BF16), dob, preferred_element_type=F32)
            dp = lax.dot_general(vb, dob, _DIMS["nt"], preferred_element_type=F32)
            ds = p * (dp - dl_ref[i])
            dc = dc - jnp.sum(ds, axis=-1, keepdims=True)
            dk = dk + jnp.dot(ds.astype(BF16), qb, preferred_element_type=F32)
            return dk, dv, dc

        init = (jnp.zeros((t, HEAD_DIM), F32), jnp.zeros((t, HEAD_DIM), F32), jnp.zeros((t, 1), F32))
        carry = step(j, init, True)
        dk, dv, dc = lax.fori_loop(j + 1, nb, lambda i, c: step(i, c, False), carry)
        dk_ref[...] = (dk * LN2).astype(dk_ref.dtype)
        dv_ref[...] = dv.astype(dv_ref.dtype)
        _lane_put(dck_ref, h, dc)

    kspec = pl.BlockSpec((t, HEAD_DIM), lambda j, h: (j, h))
    rowspec = pl.BlockSpec((None, nb, 1, t), lambda j, h: (h, 0, 0, 0))
    shspec = pl.BlockSpec((t, LANES), lambda j, h: (j, 0))
    (dk, dv, dck_sh), got = _call(
        body, name="fox_dkv", grid=(nb, H),
        in_specs=[pl.BlockSpec((S, HEAD_DIM), lambda j, h: (0, h)),
                  pl.BlockSpec((S, HEAD_DIM), lambda j, h: (0, h)),
                  kspec,
                  pl.BlockSpec((t, HEAD_DIM), lambda j, h: (j, H + h)),
                  shspec, rowspec, rowspec, rowspec],
        out_specs=[kspec, kspec, shspec],
        out_shape=[jax.ShapeDtypeStruct((S, D), BF16), jax.ShapeDtypeStruct((S, D), BF16),
                   jax.ShapeDtypeStruct((S, LANES), F32)],
        sem=("parallel", "arbitrary"), args=(q2, do, kv, kv, c_sh, c_row, l2_row, delta_row), exchange=exchange)
    return dk, dv, dck_sh, got


def _adamw(name, parts, w, m, v, layer=None, *, tr=128):
    P, R, C = parts.shape
    tr = _tile(R, tr)

    def body(p_ref, w_ref, m_ref, v_ref, g_ref, d_ref, nm_ref, nv_ref):
        g = p_ref[0].astype(F32)
        for k in range(1, P):
            g = g + p_ref[k].astype(F32)
        wv = w_ref[...]
        nm = ADAM_B1 * m_ref[...] + (1.0 - ADAM_B1) * g
        nv = ADAM_B2 * v_ref[...] + (1.0 - ADAM_B2) * (g * g)
        m_hat = nm / (1.0 - ADAM_B1 ** ADAM_STEP)
        v_hat = nv / (1.0 - ADAM_B2 ** ADAM_STEP)
        g_ref[...] = g
        d_ref[...] = -ADAM_LR * (m_hat / (jnp.sqrt(v_hat) + ADAM_EPS) + ADAM_WD * wv)
        nm_ref[...] = nm
        nv_ref[...] = nv

    if layer is None:
        wspec = pl.BlockSpec((tr, C), lambda i: (i, 0))
    else:
        wspec = pl.BlockSpec((None, tr, C), lambda i: (layer, i, 0))
    ospec = pl.BlockSpec((tr, C), lambda i: (i, 0))
    return pl.pallas_call(
        body, grid=(R // tr,),
        in_specs=[pl.BlockSpec((P, tr, C), lambda i: (0, i, 0)), wspec, wspec, wspec],
        out_specs=[ospec] * 4, out_shape=[jax.ShapeDtypeStruct((R, C), F32)] * 4,
        compiler_params=_params(("parallel",)), name=name)(parts, w, m, v)


def _rows(a):
    flat = a.reshape(-1)
    pad = (-flat.shape[0]) % LANES
    if pad:
        flat = jnp.pad(flat, (0, pad))
    return flat.reshape(-1, LANES)


def kernel(x, norm_mix, norm_mlp, pool_w, pool_scale, norm_kv, w_kvf, b_f, w_q, w_o, w_up, w_down, norm_out, loss_target, m_norm_mix, m_norm_mlp, m_pool_w, m_pool_scale, m_norm_kv, m_w_kvf, m_b_f, m_w_q, m_w_o, m_w_up, m_w_down, m_norm_out, v_norm_mix, v_norm_mlp, v_pool_w, v_pool_scale, v_norm_kv, v_w_kvf, v_b_f, v_w_q, v_w_o, v_w_up, v_w_down, v_norm_out):
    _, S, D = x.shape
    H = D // HEAD_DIM
    dg = D // N_GROUPS
    n_kvf = 2 * D + H
    kvf_b = w_kvf.shape[1]
    fb = w_up.shape[2]
    ps_b = pool_scale.shape[1]
    xi, yi, ci = _position()
    my_block = 4 * xi + 2 * yi + ci
    x2 = x.reshape(S, D)
    tgt = loss_target.reshape(S, D)
    b_pad = jnp.pad(b_f, (0, LANES - H)).reshape(1, LANES)

    g_pool, g_scale, g_up0, g_down0 = _all_gather_two_level(
        [pool_w.astype(BF16), pool_scale, w_up[0].astype(BF16), w_down[0].astype(BF16)])
    pw = g_pool[:, 0].transpose(1, 0, 2, 3).reshape(N_GROUPS, dg, dg)
    scale_full = g_scale.reshape(1, D)
    ex_attn_w = _Exchange("gather", [w_kvf.astype(BF16), w_q[0].astype(BF16), w_o[0].astype(BF16)])
    ex_up1 = _Exchange("gather", [w_up[1].astype(BF16)])
    ex_down1 = _Exchange("gather", [w_down[1].astype(BF16)])

    h1, diff = _pool_fwd(x2, norm_mix[0:1], pw, scale_full)
    (hn_m0,) = _rms_fwd("rms_mlp0", h1, norm_mlp[0:1])
    u0, uu0, h2, (g_kvf, g_q, g_o), (g_up1,) = _mlp_fwd("l0", hn_m0, h1, g_up0, g_down0, ex_up=ex_attn_w, ex_down=ex_up1)
    wkvf = g_kvf.transpose(1, 0, 2).reshape(D, n_kvf)
    w_kv = wkvf[:, :2 * D]
    w_f = jnp.pad(wkvf[:, 2 * D:], ((0, 0), (0, LANES - H)))
    wq = g_q.reshape(D, D)
    wo = g_o.reshape(D, D)

    gains_kv_q = jnp.stack([norm_kv, norm_mix[1]])
    hkv, hn_q = _rms_fwd("rms_kv_q", h2, gains_kv_q)
    kv = _mm_nn("kv_proj", hkv, w_kv, out_dtype=BF16)
    f_raw = _mm_nn("f_proj", hkv, w_f, out_dtype=F32)
    c_sh = _gate_fwd(f_raw, b_pad)
    t_attn = _tile(S, 512)
    c2_sh = c_sh * LOG2E
    c2_row = _row_layout(c2_sh[:, :H].T, t_attn)
    q2 = _mm_nn("q_proj", hn_q, wq, out_dtype=BF16, out_scale=HEAD_DIM ** -0.5 * LOG2E)
    o, l2_sh, (g_down1,) = _fox_fwd(q2, kv, c2_row, t=t_attn, exchange=ex_down1)
    h3 = _mm_nn("o_proj", o, wo, out_dtype=F32, residual=h2)
    (hn_m1,) = _rms_fwd("rms_mlp1", h3, norm_mlp[1:2])
    u1, uu1, h4, _, _ = _mlp_fwd("l1", hn_m1, h3, g_up1, g_down1)
    dh4, d_norm_out, loss_part = _loss_head(h4, norm_out.reshape(1, D), tgt)

    d_pre = _mlp_dpre("l1", dh4, u1, g_down1)
    dw_down1, _ = _mm_tn("mlp_dwdown_l1", uu1, dh4)
    dw_up1, _ = _mlp_dwup("l1", hn_m1, d_pre, fb)
    d_hn, _ = _mlp_dhn("l1", d_pre, g_up1)
    dh3, d_norm_mlp1 = _rms_bwd("rms_mlp1_bwd", h3, norm_mlp[1:2], [d_hn], dh4)

    do = _mm_nt("o_proj_dx", dh3, wo, out_dtype=BF16)
    dw_o, _ = _mm_tn("o_proj_dw", o, dh3)
    dq, delta_sh, dcq_sh, (p_up1, p_down1) = _fox_dq(
        q2, kv, o, do, c2_row, l2_sh, t=t_attn,
        exchange=_Exchange("scatter", [dw_up1, dw_down1.reshape(N_DEV, fb, D)]))
    l2_row = _row_layout(l2_sh[:, :H].T, t_attn)
    delta_row = _row_layout(delta_sh[:, :H].T, t_attn)
    dk, dv, dck_sh, (p_o,) = _fox_dkv(
        q2, kv, do, c2_sh, c2_row, l2_row, delta_row, t=t_attn,
        exchange=_Exchange("scatter", [dw_o.reshape(N_DEV, D // N_DEV, D)]))
    dw_q, _ = _mm_tn("q_proj_dw", hn_q, dq)
    d_hn_q = _mm_nt("q_proj_dx", dq, wq, out_dtype=F32)

    d_f, d_b = _gate_bwd(dck_sh + dcq_sh, f_raw, b_pad)
    dw_k, _ = _mm_tn("k_proj_dw", hkv, dk)
    dw_v, _ = _mm_tn("v_proj_dw", hkv, dv)
    dw_f, _ = _mm_tn("f_proj_dw", hkv, d_f)
    d_hkv = _mm_nt("k_proj_dx", dk, w_kv, b_cols=(0, D), out_dtype=F32)
    d_hkv = _mm_nt("v_proj_dx", dv, w_kv, b_cols=(D, D), out_dtype=F32, residual=d_hkv)
    d_hkv = _mm_nt("f_proj_dx", d_f, w_f, out_dtype=F32, residual=d_hkv)
    dh2, d_norm_kv_q = _rms_bwd("rms_kv_q_bwd", h2, gains_kv_q, [d_hkv, d_hn_q], dh3)
    dw_kvf = jnp.concatenate([dw_k, dw_v, dw_f[:, :H]], axis=1).reshape(D, N_DEV, kvf_b).transpose(1, 0, 2)

    d_pre = _mlp_dpre("l0", dh2, u0, g_down0)
    dw_down0, (p_kvf, p_q) = _mm_tn(
        "mlp_dwdown_l0", uu0, dh2, exchange=_Exchange("scatter", [dw_kvf, dw_q.reshape(N_DEV, D // N_DEV, D)]))
    dw_up0, (p_down0,) = _mlp_dwup(
        "l0", hn_m0, d_pre, fb, exchange=_Exchange("scatter", [dw_down0.reshape(N_DEV, fb, D)]))
    d_hn, (p_up0,) = _mlp_dhn("l0", d_pre, g_up0, exchange=_Exchange("scatter", [dw_up0]))
    dh1, d_norm_mlp0 = _rms_bwd("rms_mlp0_bwd", h1, norm_mlp[0:1], [d_hn], dh2)
    grad_x, dw_pool, d_scale, d_norm_mix0 = _pool_bwd(x2, dh1, diff, norm_mix[0:1], pw, scale_full)
    (p_pool,) = _exchange_now("scatter_pool", _Exchange("scatter", [
        dw_pool.astype(BF16).reshape(N_GROUPS, N_DEV, dg // N_DEV, dg).transpose(1, 0, 2, 3).reshape(
            N_DEV, N_GROUPS * dg // N_DEV, dg)]))

    small = jnp.concatenate([
        _rows(jnp.concatenate([d_norm_mix0, d_norm_kv_q[1:2]], axis=0)),
        _rows(jnp.concatenate([d_norm_mlp0, d_norm_mlp1], axis=0)),
        _rows(d_norm_kv_q[0:1]),
        _rows(d_norm_out),
        _rows(d_scale),
        d_b,
        jnp.pad(loss_part[0:1, 0:1], ((0, 0), (0, LANES - 1))),
    ], axis=0)
    n_small = small.shape[0]
    small = jnp.pad(small, ((0, (-n_small) % 8), (0, 0)))
    total = _all_reduce_small(small)
    rd = D // LANES
    loss = total[7 * rd + 1, 0]
    g_scale_mine = lax.dynamic_slice(total[6 * rd:7 * rd].reshape(D), (my_block * ps_b,), (ps_b,))

    def pack(nm_, nl_, kv_, out_, bf_, ps_):
        return jnp.concatenate([_rows(nm_), _rows(nl_), _rows(kv_), _rows(out_), _rows(bf_), _rows(ps_)], axis=0)

    g_small = jnp.concatenate([total[:6 * rd], total[7 * rd:7 * rd + 1], _rows(g_scale_mine)], axis=0)
    w_small = pack(norm_mix, norm_mlp, norm_kv, norm_out, b_f, pool_scale)
    m_small = pack(m_norm_mix, m_norm_mlp, m_norm_kv, m_norm_out, m_b_f, m_pool_scale)
    v_small = pack(v_norm_mix, v_norm_mlp, v_norm_kv, v_norm_out, v_b_f, v_pool_scale)
    rs = g_small.shape[0]
    padr = (-rs) % 8
    pad8 = lambda a: jnp.pad(a, ((0, padr), (0, 0)))
    small_out = _adamw("adamw_small", pad8(g_small)[None], pad8(w_small), pad8(m_small), pad8(v_small), tr=rs + padr)

    def unpack(a):
        o0 = 0
        res = []
        for shape in [(2, D), (2, D), (D,), (D,)]:
            nr = (2 * rd) if len(shape) == 2 else rd
            res.append(a[o0:o0 + nr].reshape(shape))
            o0 += nr
        res.append(a[o0, :H])
        res.append(a[o0 + 1:o0 + 1 + ps_b // LANES].reshape(1, ps_b))
        return res

    small_res = [unpack(a) for a in small_out]

    r_pool = _adamw("adamw_pool", p_pool, pool_w.reshape(-1, dg), m_pool_w.reshape(-1, dg), v_pool_w.reshape(-1, dg))
    r_kvf = _adamw("adamw_kvf", p_kvf, w_kvf, m_w_kvf, v_w_kvf)
    r_q = _adamw("adamw_q", p_q, w_q[0], m_w_q[0], v_w_q[0])
    r_o = _adamw("adamw_o", p_o, w_o[0], m_w_o[0], v_w_o[0])
    r_up = [_adamw(f"adamw_up{l}", p, w_up, m_w_up, v_w_up, layer=l) for l, p in enumerate([p_up0, p_up1])]
    r_down = [_adamw(f"adamw_down{l}", p, w_down, m_w_down, v_w_down, layer=l) for l, p in enumerate([p_down0, p_down1])]

    def leaves(kind):
        sm = small_res[kind]
        return [
            sm[0], sm[1],
            r_pool[kind].reshape(pool_w.shape),
            sm[5],
            sm[2],
            r_kvf[kind],
            sm[4],
            r_q[kind][None], r_o[kind][None],
            jnp.stack([r_up[0][kind], r_up[1][kind]]),
            jnp.stack([r_down[0][kind], r_down[1][kind]]),
            sm[3],
        ]

    return (loss, grad_x.reshape(x.shape), *leaves(0), *leaves(1), *leaves(2), *leaves(3))
```

```python
import jax
import jax.numpy as jnp
from jax import lax
from jax.experimental import pallas as pl
from jax.experimental.pallas import tpu as pltpu

F32 = jnp.float32
BF16 = jnp.bfloat16
MESH = pl.DeviceIdType.MESH

N_DEV = 8
EPS = 1e-6
NEG_INF = -1e30
POOL_WINDOWS = (2, 4, 8, 16)
N_GROUPS = len(POOL_WINDOWS)
POOL_HALO = 16
HEAD_DIM = 128
LANES = 128
HEADS_PER_STEP = 2
LOG2E = 1.4426950408889634
LN2 = 0.6931471805599453

ADAM_LR = 0.001
ADAM_B1 = 0.9
ADAM_B2 = 0.999
ADAM_EPS = 1e-08
ADAM_WD = 0.01
ADAM_STEP = 10

VMEM_LIMIT = 56 * 1024 * 1024

_ANY = pl.BlockSpec(memory_space=pl.ANY)


def _tile(n, want):
    t = min(n, want)
    assert n % t == 0, (n, want)
    return t


def _params(sem, vmem=VMEM_LIMIT):
    return pltpu.CompilerParams(dimension_semantics=sem, vmem_limit_bytes=vmem)


def _position():
    return lax.axis_index("x"), lax.axis_index("y"), lax.axis_index("c")


class _Exchange:
    def __init__(self, kind, arrays):
        assert kind in ("gather", "scatter")
        self.kind = kind
        self.arrays = list(arrays)
        self.n = len(self.arrays)
        shapes = [(N_DEV, *a.shape) if kind == "gather" else a.shape for a in self.arrays]
        self.out_shape = [jax.ShapeDtypeStruct(s, a.dtype) for s, a in zip(shapes, self.arrays)]
        self.scratch = [pltpu.SemaphoreType.DMA((self.n, N_DEV)), pltpu.SemaphoreType.DMA((self.n, N_DEV))]

    def _copies(self, ins, outs, send_sems, recv_sems, with_recv=True):
        x, y, c = _position()
        me = 4 * x + 2 * y + c
        gather = self.kind == "gather"
        local, sends, recvs = [], [], []
        for a in range(self.n):
            local.append(pltpu.make_async_copy(ins[a] if gather else ins[a].at[me], outs[a].at[me], send_sems.at[a, 0]))
            for k in range(1, N_DEV):
                px, py, pc = x ^ (k >> 2), y ^ ((k >> 1) & 1), c ^ (k & 1)
                peer = 4 * px + 2 * py + pc
                src = ins[a] if gather else ins[a].at[peer]
                common = dict(send_sem=send_sems.at[a, k], recv_sem=recv_sems.at[a, k], device_id=(px, py, pc),
                              device_id_type=MESH)
                sends.append(pltpu.make_async_remote_copy(src_ref=src, dst_ref=outs[a].at[me], **common))
                if with_recv:
                    recvs.append(pltpu.make_async_remote_copy(src_ref=src, dst_ref=outs[a].at[peer], **common))
        return local, sends, recvs

    def start(self, ins, outs, send_sems, recv_sems):
        local, sends, _ = self._copies(ins, outs, send_sems, recv_sems, with_recv=False)
        for cp in local + sends:
            cp.start()

    def wait(self, ins, outs, send_sems, recv_sems):
        local, sends, recvs = self._copies(ins, outs, send_sems, recv_sems)
        for send, recv in zip(sends, recvs):
            send.wait_send()
            recv.wait_recv()
        for cp in local:
            cp.wait()


def _call(body, *, name, grid, in_specs, out_specs, out_shape, args, scratch_shapes=(), sem=None, exchange=None):
    if exchange is None:
        outs = pl.pallas_call(
            body, grid=grid, in_specs=in_specs, out_specs=out_specs, out_shape=out_shape,
            scratch_shapes=list(scratch_shapes), compiler_params=_params(sem), name=name)(*args)
        return list(outs), []

    n_in, n_out, n_scr, n = len(in_specs), len(out_specs), len(scratch_shapes), exchange.n

    def hosted(*refs):
        ins, refs = refs[:n_in], refs[n_in:]
        ex_in, refs = refs[:n], refs[n:]
        outs, refs = refs[:n_out], refs[n_out:]
        ex_out, refs = refs[:n], refs[n:]
        scratch, sems = refs[:n_scr], refs[n_scr:]
        first = _all_true([pl.program_id(d) == 0 for d in range(len(grid))])
        last = _all_true([pl.program_id(d) == grid[d] - 1 for d in range(len(grid))])

        @pl.when(first)
        def _():
            exchange.start(ex_in, ex_out, *sems)

        body(*ins, *outs, *scratch)

        @pl.when(last)
        def _():
            exchange.wait(ex_in, ex_out, *sems)

    res = pl.pallas_call(
        hosted, grid=grid, in_specs=[*in_specs, *[_ANY] * n], out_specs=[*out_specs, *[_ANY] * n],
        out_shape=[*out_shape, *exchange.out_shape], scratch_shapes=[*scratch_shapes, *exchange.scratch],
        compiler_params=_params(("arbitrary",) * len(grid)), name=name)(*args, *exchange.arrays)
    return list(res[:n_out]), list(res[n_out:])


def _all_true(preds):
    out = preds[0]
    for p in preds[1:]:
        out = jnp.logical_and(out, p)
    return out


def _exchange_now(name, exchange):
    def body(*refs):
        n = exchange.n
        exchange.start(refs[:n], refs[n:2 * n], *refs[2 * n:])
        exchange.wait(refs[:n], refs[n:2 * n], *refs[2 * n:])

    return pl.pallas_call(
        body, in_specs=[_ANY] * exchange.n, out_specs=[_ANY] * exchange.n, out_shape=exchange.out_shape,
        scratch_shapes=exchange.scratch, name=name)(*exchange.arrays)


def _all_gather_two_level(shards):
    n = len(shards)

    def body(*refs):
        ins, outs = refs[:n], refs[n:2 * n]
        send_sems, recv_sems, local_sems = refs[2 * n:]
        x, y, c = _position()
        me, sibling = (x, y, c), (x, y, 1 - c)
        chips = [(1 - x, y), (x, 1 - y), (1 - x, 1 - y)]

        def slot(a, px, py, pc):
            return outs[a].at[4 * px + 2 * py + pc]

        def copy(a, k, block, to, src=None):
            return pltpu.make_async_remote_copy(
                src_ref=slot(a, *block) if src is None else src, dst_ref=slot(a, *block),
                send_sem=send_sems.at[a, k], recv_sem=recv_sems.at[a, k], device_id=to, device_id_type=MESH)

        mine = [pltpu.make_async_copy(ins[a], slot(a, *me), local_sems.at[a]) for a in range(n)]
        first = []
        for a in range(n):
            mine[a].start()
            first.append(copy(a, 0, me, sibling, src=ins[a]))
            first += [copy(a, 1 + j, me, (*chip, c), src=ins[a]) for j, chip in enumerate(chips)]
        for cp in first:
            cp.start()
        passed = []
        for a in range(n):
            for j, chip in enumerate(chips):
                copy(a, 1 + j, (*chip, c), me).wait_recv()
                fwd = copy(a, 4 + j, (*chip, c), sibling)
                fwd.start()
                passed.append(fwd)
        for a in range(n):
            copy(a, 0, sibling, me).wait_recv()
            for j, chip in enumerate(chips):
                copy(a, 4 + j, (*chip, 1 - c), me).wait_recv()
        for cp in first + passed:
            cp.wait_send()
        for cp in mine:
            cp.wait()

    return pl.pallas_call(
        body, in_specs=[_ANY] * n, out_specs=[_ANY] * n,
        out_shape=[jax.ShapeDtypeStruct((N_DEV, *s.shape), s.dtype) for s in shards],
        scratch_shapes=[pltpu.SemaphoreType.DMA((n, 7)), pltpu.SemaphoreType.DMA((n, 7)),
                        pltpu.SemaphoreType.DMA((n,))],
        name="all_gather_first")(*shards)


def _all_reduce_small(vec):
    R = vec.shape[0]

    def body(v_ref, o_ref, buf_ref, send_sems, recv_sems):
        x, y, c = _position()
        me = 4 * x + 2 * y + c
        buf_ref[me] = v_ref[...]
        copies = []
        for k in range(1, N_DEV):
            peer = (x ^ (k >> 2), y ^ ((k >> 1) & 1), c ^ (k & 1))
            copies.append(pltpu.make_async_remote_copy(
                src_ref=buf_ref.at[me], dst_ref=buf_ref.at[me], send_sem=send_sems.at[k], recv_sem=recv_sems.at[k],
                device_id=peer, device_id_type=MESH))
        for cp in copies:
            cp.start()
        for cp in copies:
            cp.wait()
        total = buf_ref[0]
        for d in range(1, N_DEV):
            total = total + buf_ref[d]
        o_ref[...] = total

    vm = pl.BlockSpec(memory_space=pltpu.VMEM)
    return pl.pallas_call(
        body, in_specs=[vm], out_specs=vm, out_shape=jax.ShapeDtypeStruct((R, LANES), F32),
        scratch_shapes=[pltpu.VMEM((N_DEV, R, LANES), F32), pltpu.SemaphoreType.DMA((N_DEV,)),
                        pltpu.SemaphoreType.DMA((N_DEV,))],
        name="all_reduce_small")(vec)


_DIMS = {
    "nn": (((1,), (0,)), ((), ())),
    "nt": (((1,), (1,)), ((), ())),
    "tn": (((0,), (0,)), ((), ())),
}


def _matmul(name, a, b, *, mode, grid, tm, tn, a_spec, b_spec, outs, extras=(), extra_specs=(), epilogue=None,
            exchange=None):
    nk = grid[2]
    n_extra = len(extras)
    dn = _DIMS[mode]

    def body(a_ref, b_ref, *rest):
        extra_refs = rest[:n_extra]
        out_refs = rest[n_extra:-1]
        acc_ref = rest[-1]
        k = pl.program_id(2)

        @pl.when(k == 0)
        def _():
            acc_ref[...] = jnp.zeros_like(acc_ref)

        acc_ref[...] += lax.dot_general(a_ref[...].astype(BF16), b_ref[...].astype(BF16), dn, preferred_element_type=F32)

        @pl.when(k == nk - 1)
        def _():
            acc = acc_ref[...]
            vals = (acc,) if epilogue is None else epilogue(acc, *[r[...] for r in extra_refs])
            for o_ref, val in zip(out_refs, vals):
                o_ref[...] = val.astype(o_ref.dtype)

    return _call(
        body, name=name, grid=grid, in_specs=[a_spec, b_spec, *extra_specs], out_specs=[o[2] for o in outs],
        out_shape=[jax.ShapeDtypeStruct(o[0], o[1]) for o in outs], scratch_shapes=[pltpu.VMEM((tm, tn), F32)],
        sem=("parallel", "parallel", "arbitrary"), args=(a, b, *extras), exchange=exchange)


def _ij(tm, tn):
    return pl.BlockSpec((tm, tn), lambda i, j, k: (i, j))


def _mm_nn(name, a, b, *, out_dtype, residual=None, out_scale=None, tm=1024, tn=1024, tk=2048):
    M, K = a.shape
    N = b.shape[1]
    tm, tn, tk = _tile(M, tm), _tile(N, tn), _tile(K, tk)
    extras, especs, epi = (), (), None
    if residual is not None:
        extras, especs = (residual,), (_ij(tm, tn),)
        epi = lambda acc, r: (acc + r,)
    elif out_scale is not None:
        epi = lambda acc: (acc * out_scale,)
    return _matmul(
        name, a, b, mode="nn", grid=(M // tm, N // tn, K // tk), tm=tm, tn=tn,
        a_spec=pl.BlockSpec((tm, tk), lambda i, j, k: (i, k)),
        b_spec=pl.BlockSpec((tk, tn), lambda i, j, k: (k, j)),
        outs=[((M, N), out_dtype, _ij(tm, tn))], extras=extras, extra_specs=especs, epilogue=epi)[0][0]


def _mm_nt(name, a, b, *, out_dtype, b_cols=None, residual=None, tm=1024, tn=1024, tk=2048):
    M, K = a.shape
    N = b.shape[0]
    c0 = 0 if b_cols is None else b_cols[0]
    tm, tn, tk = _tile(M, tm), _tile(N, tn), _tile(K, tk)
    assert c0 % tk == 0
    kb0 = c0 // tk
    extras, especs, epi = (), (), None
    if residual is not None:
        extras, especs = (residual,), (_ij(tm, tn),)
        epi = lambda acc, r: (acc + r,)
    return _matmul(
        name, a, b, mode="nt", grid=(M // tm, N // tn, K // tk), tm=tm, tn=tn,
        a_spec=pl.BlockSpec((tm, tk), lambda i, j, k: (i, k)),
        b_spec=pl.BlockSpec((tn, tk), lambda i, j, k: (j, kb0 + k)),
        outs=[((M, N), out_dtype, _ij(tm, tn))], extras=extras, extra_specs=especs, epilogue=epi)[0][0]


def _mm_tn(name, a, b, *, exchange=None, tm=1024, tn=1024, tk=1024):
    K, M = a.shape
    N = b.shape[1]
    tm, tn, tk = _tile(M, tm), _tile(N, tn), _tile(K, tk)
    outs, got = _matmul(
        name, a, b, mode="tn", grid=(M // tm, N // tn, K // tk), tm=tm, tn=tn,
        a_spec=pl.BlockSpec((tk, tm), lambda i, j, k: (k, i)),
        b_spec=pl.BlockSpec((tk, tn), lambda i, j, k: (k, j)),
        outs=[((M, N), BF16, _ij(tm, tn))], exchange=exchange)
    return outs[0], got


def _mlp_fwd(tag, hn, h, w_up_g, w_down_g, *, ex_up=None, ex_down=None, tm=1024, tk=2048):
    S, D = hn.shape
    fb = w_up_g.shape[2]
    F = N_DEV * fb
    tm, tku = _tile(S, tm), _tile(D, tk)

    def up_epi(acc):
        u = jnp.maximum(acc, 0.0)
        return u, u * u

    (u, uu), got_up = _matmul(
        f"mlp_up_{tag}", hn, w_up_g, mode="nn", grid=(S // tm, N_DEV, D // tku), tm=tm, tn=fb,
        a_spec=pl.BlockSpec((tm, tku), lambda i, j, k: (i, k)),
        b_spec=pl.BlockSpec((None, tku, fb), lambda i, j, k: (j, k, 0)),
        outs=[((S, F), BF16, _ij(tm, fb)), ((S, F), BF16, _ij(tm, fb))], epilogue=up_epi, exchange=ex_up)
    if w_down_g is None:
        w_down_g = got_up[0]

    tn = _tile(D, 1024)
    tkd = _tile(fb, tk)
    r = fb // tkd
    (h_out,), got_down = _matmul(
        f"mlp_down_{tag}", uu, w_down_g, mode="nn", grid=(S // tm, D // tn, F // tkd), tm=tm, tn=tn,
        a_spec=pl.BlockSpec((tm, tkd), lambda i, j, k: (i, k)),
        b_spec=pl.BlockSpec((None, tkd, tn), lambda i, j, k: (k // r, k % r, j)),
        outs=[((S, D), F32, _ij(tm, tn))], extras=(h,), extra_specs=(_ij(tm, tn),),
        epilogue=lambda acc, res: (acc + res,), exchange=ex_down)
    return u, uu, h_out, got_up, got_down


def _mlp_dpre(tag, dh, u, w_down_g, *, tm=1024, tk=2048):
    S, D = dh.shape
    fb = w_down_g.shape[1]
    tm, tkd = _tile(S, tm), _tile(D, tk)
    return _matmul(
        f"mlp_dpre_{tag}", dh, w_down_g, mode="nt", grid=(S // tm, N_DEV, D // tkd), tm=tm, tn=fb,
        a_spec=pl.BlockSpec((tm, tkd), lambda i, j, k: (i, k)),
        b_spec=pl.BlockSpec((None, fb, tkd), lambda i, j, k: (j, 0, k)),
        outs=[((S, N_DEV * fb), BF16, _ij(tm, fb))], extras=(u,), extra_specs=(_ij(tm, fb),),
        epilogue=lambda acc, uv: (acc * (2.0 * uv.astype(F32)),))[0][0]


def _mlp_dwup(tag, hn, d_pre, fb, *, exchange=None):
    S, D = hn.shape
    tmu = _tile(D, 1024)
    tks = _tile(S, 1024)
    outs, got = _matmul(
        f"mlp_dwup_{tag}", hn, d_pre, mode="tn", grid=(D // tmu, N_DEV, S // tks), tm=tmu, tn=fb,
        a_spec=pl.BlockSpec((tks, tmu), lambda i, j, k: (k, i)),
        b_spec=pl.BlockSpec((tks, fb), lambda i, j, k: (k, j)),
        outs=[((N_DEV, D, fb), BF16, pl.BlockSpec((None, tmu, fb), lambda i, j, k: (j, i, 0)))], exchange=exchange)
    return outs[0], got


def _mlp_dhn(tag, d_pre, w_up_g, *, exchange=None, tm=1024, tk=2048):
    S, F = d_pre.shape
    D, fb = w_up_g.shape[1], w_up_g.shape[2]
    tm, tn = _tile(S, tm), _tile(D, 1024)
    tkf = _tile(fb, tk)
    r = fb // tkf
    outs, got = _matmul(
        f"mlp_dhn_{tag}", d_pre, w_up_g, mode="nt", grid=(S // tm, D // tn, F // tkf), tm=tm, tn=tn,
        a_spec=pl.BlockSpec((tm, tkf), lambda i, j, k: (i, k)),
        b_spec=pl.BlockSpec((None, tn, tkf), lambda i, j, k: (k // r, j, k % r)),
        outs=[((S, D), F32, _ij(tm, tn))], exchange=exchange)
    return outs[0], got


def _row_spec(ts, D):
    return pl.BlockSpec((ts, D), lambda i: (i, 0))


def _vec_spec(n, D):
    return pl.BlockSpec((n, D), lambda i: (0, 0))


def _rms_fwd(name, x, gains, *, ts=512):
    S, D = x.shape
    n = gains.shape[0]
    ts = _tile(S, ts)

    def body(x_ref, g_ref, *o_refs):
        xv = x_ref[...]
        y = xv * lax.rsqrt(jnp.mean(xv * xv, axis=-1, keepdims=True) + EPS)
        for i, o_ref in enumerate(o_refs):
            o_ref[...] = (y * g_ref[i:i + 1, :]).astype(o_ref.dtype)

    return pl.pallas_call(
        body, grid=(S // ts,), in_specs=[_row_spec(ts, D), _vec_spec(n, D)],
        out_specs=[_row_spec(ts, D)] * n, out_shape=[jax.ShapeDtypeStruct((S, D), BF16)] * n,
        compiler_params=_params(("parallel",)), name=name)(x, gains)


def _rms_bwd(name, x, gains, dys, res, *, ts=512):
    S, D = x.shape
    n = gains.shape[0]
    ts = _tile(S, ts)

    def body(x_ref, g_ref, *rest):
        dy_refs = rest[:n]
        res_ref, dx_ref, dg_ref = rest[n:]
        i = pl.program_id(0)
        xv = x_ref[...]
        r = lax.rsqrt(jnp.mean(xv * xv, axis=-1, keepdims=True) + EPS)
        xhat = xv * r
        dxhat = None
        dgs = []
        for k in range(n):
            dy = dy_refs[k][...].astype(F32)
            dgs.append(jnp.sum(dy * xhat, axis=0, keepdims=True))
            term = dy * g_ref[k:k + 1, :]
            dxhat = term if dxhat is None else dxhat + term
        dx_ref[...] = res_ref[...] + r * (dxhat - xhat * jnp.mean(dxhat * xhat, axis=-1, keepdims=True))
        dg = jnp.concatenate(dgs, axis=0) if n > 1 else dgs[0]

        @pl.when(i == 0)
        def _():
            dg_ref[...] = dg

        @pl.when(i > 0)
        def _():
            dg_ref[...] += dg

    return pl.pallas_call(
        body, grid=(S // ts,),
        in_specs=[_row_spec(ts, D), _vec_spec(n, D)] + [_row_spec(ts, D)] * (n + 1),
        out_specs=[_row_spec(ts, D), _vec_spec(n, D)],
        out_shape=[jax.ShapeDtypeStruct((S, D), F32), jax.ShapeDtypeStruct((n, D), F32)],
        compiler_params=_params(("arbitrary",)), name=name)(x, gains, *dys, res)


def _loss_head(h, gain, target, *, ts=512):
    S, D = h.shape
    ts = _tile(S, ts)

    def body(x_ref, g_ref, t_ref, dx_ref, dg_ref, loss_ref):
        i = pl.program_id(0)
        xv = x_ref[...]
        g = g_ref[...]
        r = lax.rsqrt(jnp.mean(xv * xv, axis=-1, keepdims=True) + EPS)
        xhat = xv * r
        err = xhat * g - t_ref[...]
        part = 0.5 * jnp.sum(jnp.mean(err * err, axis=-1, keepdims=True), axis=0, keepdims=True)
        dy = err * (1.0 / D)
        dg = jnp.sum(dy * xhat, axis=0, keepdims=True)
        dxhat = dy * g
        dx_ref[...] = r * (dxhat - xhat * jnp.mean(dxhat * xhat, axis=-1, keepdims=True))

        @pl.when(i == 0)
        def _():
            dg_ref[...] = dg
            loss_ref[...] = jnp.broadcast_to(part, loss_ref.shape)

        @pl.when(i > 0)
        def _():
            dg_ref[...] += dg
            loss_ref[...] += jnp.broadcast_to(part, loss_ref.shape)

    return pl.pallas_call(
        body, grid=(S // ts,),
        in_specs=[_row_spec(ts, D), _vec_spec(1, D), _row_spec(ts, D)],
        out_specs=[_row_spec(ts, D), _vec_spec(1, D), pl.BlockSpec((8, LANES), lambda i: (0, 0))],
        out_shape=[jax.ShapeDtypeStruct((S, D), F32), jax.ShapeDtypeStruct((1, D), F32),
                   jax.ShapeDtypeStruct((8, LANES), F32)],
        compiler_params=_params(("arbitrary",)), name="loss_head")(h, gain, target)


def _window_counts(t, w):
    return jnp.minimum(t + 1, w).astype(F32)


def _pool_fwd(x, gain, pool_w, scale, *, ts=512):
    S, D = x.shape
    dg = D // N_GROUPS
    ts = _tile(S, ts)
    per = ts // POOL_HALO

    def body(x_ref, xh_ref, g_ref, w_ref, sc_ref, h_ref, diff_ref):
        i = pl.program_id(0)
        g = g_ref[...]

        def norm(v):
            return v * lax.rsqrt(jnp.mean(v * v, axis=-1, keepdims=True) + EPS) * g

        xc = x_ref[...]
        hn_c = norm(xc)
        hn_h = norm(xh_ref[...]) * (i > 0).astype(F32)
        ext = jnp.concatenate([hn_h, hn_c], axis=0)
        t = i * ts + lax.broadcasted_iota(jnp.int32, (ts, 1), 0)
        for gi, w in enumerate(POOL_WINDOWS):
            cols = slice(gi * dg, (gi + 1) * dg)
            s = ext[:, cols]
            step = 1
            while step < w:
                s = s + pltpu.roll(s, step, 0)
                step *= 2
            mean = s[POOL_HALO:] * (1.0 / _window_counts(t, w))
            diff = (mean - hn_c[:, cols]).astype(BF16)
            diff_ref[:, cols] = diff
            mixed = jnp.dot(diff, w_ref[gi], preferred_element_type=F32)
            h_ref[:, cols] = xc[:, cols] + mixed * sc_ref[:, cols]

    return pl.pallas_call(
        body, grid=(S // ts,),
        in_specs=[_row_spec(ts, D),
                  pl.BlockSpec((POOL_HALO, D), lambda i: (jnp.maximum(i * per - 1, 0), 0)),
                  _vec_spec(1, D), pl.BlockSpec((N_GROUPS, dg, dg), lambda i: (0, 0, 0)), _vec_spec(1, D)],
        out_specs=[_row_spec(ts, D), _row_spec(ts, D)],
        out_shape=[jax.ShapeDtypeStruct((S, D), F32), jax.ShapeDtypeStruct((S, D), BF16)],
        compiler_params=_params(("parallel",)), name="pool_fwd")(x, x, gain, pool_w, scale)


def _pool_bwd(x, dh, diff, gain, pool_w, scale, *, ts=256):
    S, D = x.shape
    dg = D // N_GROUPS
    ts = _tile(S, ts)
    per = ts // POOL_HALO
    n_tiles = S // ts
    n_halo = S // POOL_HALO
    ext_rows = ts + POOL_HALO

    def body(x_ref, dh_ref, dhn_ref, diff_ref, g_ref, w_ref, sc_ref, dx_ref, dw_ref, dsc_ref, dgain_ref):
        i = pl.program_id(0)
        xc = x_ref[...]
        g = g_ref[...]
        r = lax.rsqrt(jnp.mean(xc * xc, axis=-1, keepdims=True) + EPS)
        xhat = xc * r
        dh_c = dh_ref[...]
        dh_n = dhn_ref[...] * (i < n_tiles - 1).astype(F32)
        dh_ext = jnp.concatenate([dh_c, dh_n], axis=0)
        t_ext = i * ts + lax.broadcasted_iota(jnp.int32, (ext_rows, 1), 0)
        d_hn_parts, dsc_parts = [], []
        for gi, w in enumerate(POOL_WINDOWS):
            cols = slice(gi * dg, (gi + 1) * dg)
            wg = w_ref[gi]
            dmix = (dh_ext[:, cols] * sc_ref[:, cols]).astype(BF16)
            d_diff = lax.dot_general(dmix, wg, _DIMS["nt"], preferred_element_type=F32)
            diff_c = diff_ref[:, cols]
            dwg = lax.dot_general(diff_c, dmix[:ts], _DIMS["tn"], preferred_element_type=F32)
            mixed = jnp.dot(diff_c, wg, preferred_element_type=F32)
            dsc_parts.append(jnp.sum(dh_c[:, cols] * mixed, axis=0, keepdims=True))
            e = d_diff * (1.0 / _window_counts(t_ext, w))
            step = 1
            while step < w:
                e = e + pltpu.roll(e, ext_rows - step, 0)
                step *= 2
            d_hn_parts.append(e[:ts] - d_diff[:ts])

            @pl.when(i == 0)
            def _():
                dw_ref[gi] = dwg

            @pl.when(i > 0)
            def _():
                dw_ref[gi] += dwg

        d_hn = jnp.concatenate(d_hn_parts, axis=1)
        dsc = jnp.concatenate(dsc_parts, axis=1)
        dgain = jnp.sum(d_hn * xhat, axis=0, keepdims=True)
        dxhat = d_hn * g
        dx_ref[...] = dh_c + r * (dxhat - xhat * jnp.mean(dxhat * xhat, axis=-1, keepdims=True))

        @pl.when(i == 0)
        def _():
            dsc_ref[...] = dsc
            dgain_ref[...] = dgain

        @pl.when(i > 0)
        def _():
            dsc_ref[...] += dsc
            dgain_ref[...] += dgain

    return pl.pallas_call(
        body, grid=(n_tiles,),
        in_specs=[_row_spec(ts, D), _row_spec(ts, D),
                  pl.BlockSpec((POOL_HALO, D), lambda i: (jnp.minimum((i + 1) * per, n_halo - 1), 0)),
                  _row_spec(ts, D), _vec_spec(1, D),
                  pl.BlockSpec((N_GROUPS, dg, dg), lambda i: (0, 0, 0)), _vec_spec(1, D)],
        out_specs=[_row_spec(ts, D), pl.BlockSpec((N_GROUPS, dg, dg), lambda i: (0, 0, 0)),
                   _vec_spec(1, D), _vec_spec(1, D)],
        out_shape=[jax.ShapeDtypeStruct((S, D), F32), jax.ShapeDtypeStruct((N_GROUPS, dg, dg), F32),
                   jax.ShapeDtypeStruct((1, D), F32), jax.ShapeDtypeStruct((1, D), F32)],
        compiler_params=_params(("arbitrary",)), name="pool_bwd")(x, dh, dh, diff, gain, pool_w, scale)


def _gate_fwd(f_raw, b_pad, *, ts=512):
    S = f_raw.shape[0]
    ts = _tile(S, ts)

    def body(f_ref, b_ref, c_ref, carry_ref):
        i = pl.program_id(0)

        @pl.when(i == 0)
        def _():
            carry_ref[...] = jnp.zeros_like(carry_ref)

        z = f_ref[...] + b_ref[...]
        v = jnp.minimum(z, 0.0) - jnp.log1p(jnp.exp(-jnp.abs(z)))
        row = lax.broadcasted_iota(jnp.int32, (ts, 1), 0)
        step = 1
        while step < ts:
            v = v + jnp.where(row >= step, pltpu.roll(v, step, 0), 0.0)
            step *= 2
        out = v + carry_ref[0:1, :]
        c_ref[...] = out
        carry_ref[...] = jnp.broadcast_to(out[ts - 1:ts, :], carry_ref.shape)

    return pl.pallas_call(
        body, grid=(S // ts,),
        in_specs=[pl.BlockSpec((ts, LANES), lambda i: (i, 0)), pl.BlockSpec((1, LANES), lambda i: (0, 0))],
        out_specs=pl.BlockSpec((ts, LANES), lambda i: (i, 0)),
        out_shape=jax.ShapeDtypeStruct((S, LANES), F32),
        scratch_shapes=[pltpu.VMEM((8, LANES), F32)],
        compiler_params=_params(("arbitrary",)), name="gate_fwd")(f_raw, b_pad)


def _gate_bwd(dc, f_raw, b_pad, *, ts=512):
    S = f_raw.shape[0]
    ts = _tile(S, ts)
    n = S // ts

    def body(dc_ref, f_ref, b_ref, df_ref, db_ref, carry_ref):
        i = pl.program_id(0)

        @pl.when(i == 0)
        def _():
            carry_ref[...] = jnp.zeros_like(carry_ref)

        v = dc_ref[...]
        row = lax.broadcasted_iota(jnp.int32, (ts, 1), 0)
        step = 1
        while step < ts:
            v = v + jnp.where(row < ts - step, pltpu.roll(v, ts - step, 0), 0.0)
            step *= 2
        d_logf = v + carry_ref[0:1, :]
        carry_ref[...] = jnp.broadcast_to(d_logf[0:1, :], carry_ref.shape)
        z = f_ref[...] + b_ref[...]
        df = d_logf / (1.0 + jnp.exp(z))
        df_ref[...] = df
        db = jnp.sum(df, axis=0, keepdims=True)

        @pl.when(i == 0)
        def _():
            db_ref[...] = db

        @pl.when(i > 0)
        def _():
            db_ref[...] += db

    rev = lambda i: (n - 1 - i, 0)
    return pl.pallas_call(
        body, grid=(n,),
        in_specs=[pl.BlockSpec((ts, LANES), rev), pl.BlockSpec((ts, LANES), rev),
                  pl.BlockSpec((1, LANES), lambda i: (0, 0))],
        out_specs=[pl.BlockSpec((ts, LANES), rev), pl.BlockSpec((1, LANES), lambda i: (0, 0))],
        out_shape=[jax.ShapeDtypeStruct((S, LANES), F32), jax.ShapeDtypeStruct((1, LANES), F32)],
        scratch_shapes=[pltpu.VMEM((8, LANES), F32)],
        compiler_params=_params(("arbitrary",)), name="gate_bwd")(dc, f_raw, b_pad)


def _row_layout(a_hs, t):
    n_heads, S = a_hs.shape
    return a_hs.reshape(n_heads, S // t, 1, t)


def _lane_pick(blk, h):
    lane = lax.broadcasted_iota(jnp.int32, blk.shape, 1)
    return jnp.sum(jnp.where(lane == h, blk, 0.0), axis=-1, keepdims=True)


def _lane_put(ref, h, col, first=True):
    lane = lax.broadcasted_iota(jnp.int32, ref.shape, 1)
    if not first:
        ref[...] = jnp.where(lane == h, col, ref[...])
        return

    @pl.when(h == 0)
    def _():
        ref[...] = jnp.where(lane == 0, col, 0.0)

    @pl.when(h > 0)
    def _():
        ref[...] = jnp.where(lane == h, col, ref[...])


def _causal(s, masked, fill, rows_are_queries=True):
    if not masked:
        return s
    rr = lax.broadcasted_iota(jnp.int32, s.shape, 0)
    cc = lax.broadcasted_iota(jnp.int32, s.shape, 1)
    keep = (cc <= rr) if rows_are_queries else (rr <= cc)
    return jnp.where(keep, s, fill)


def _fox_fwd(q2, kv, c_row, *, t=512, exchange=None):
    S, D = q2.shape
    H = D // HEAD_DIM
    t = _tile(S, t)
    nb = S // t
    hp = HEADS_PER_STEP
    wide = hp * HEAD_DIM

    def body(q_ref, k_ref, v_ref, cr_ref, o_ref, l2_ref):
        i = pl.program_id(0)
        g = pl.program_id(1)
        cols = [slice(a * HEAD_DIM, (a + 1) * HEAD_DIM) for a in range(hp)]
        refs_i = [cr_ref[a, i][:, 0:1] for a in range(hp)]
        qs = [q_ref[:, cols[a]] for a in range(hp)]

        def step(j, carry, masked):
            r0 = pl.multiple_of(j * t, t)
            out = []
            for a in range(hp):
                m, l, acc = carry[3 * a:3 * a + 3]
                kb = k_ref[pl.ds(r0, t), cols[a]]
                vb = v_ref[pl.ds(r0, t), cols[a]]
                ck = cr_ref[a, j] - refs_i[a]
                s = lax.dot_general(qs[a], kb, _DIMS["nt"], preferred_element_type=F32) - ck
                s = _causal(s, masked, NEG_INF)
                m_new = jnp.maximum(m, jnp.max(s, axis=-1, keepdims=True))
                alpha = jnp.exp2(m - m_new)
                p = jnp.exp2(s - m_new)
                l = alpha * l + jnp.sum(p, axis=-1, keepdims=True)
                acc = alpha * acc + jnp.dot(p.astype(BF16), vb, preferred_element_type=F32)
                out += [m_new, l, acc]
            return tuple(out)

        init = (jnp.full((t, 1), NEG_INF, F32), jnp.zeros((t, 1), F32), jnp.zeros((t, HEAD_DIM), F32)) * hp
        carry = lax.fori_loop(0, i, lambda j, c: step(j, c, False), init)
        carry = step(i, carry, True)
        for a in range(hp):
            m, l, acc = carry[3 * a:3 * a + 3]
            o_ref[:, cols[a]] = (acc / l).astype(o_ref.dtype)
            _lane_put(l2_ref, g * hp + a, m + jnp.log2(l), first=(a == 0))

    (o, l2_sh), got = _call(
        body, name="fox_fwd", grid=(nb, H // hp),
        in_specs=[pl.BlockSpec((t, wide), lambda i, g: (i, g)),
                  pl.BlockSpec((S, wide), lambda i, g: (0, g)),
                  pl.BlockSpec((S, wide), lambda i, g: (0, H // hp + g)),
                  pl.BlockSpec((hp, nb, 1, t), lambda i, g: (g, 0, 0, 0))],
        out_specs=[pl.BlockSpec((t, wide), lambda i, g: (i, g)),
                   pl.BlockSpec((t, LANES), lambda i, g: (i, 0))],
        out_shape=[jax.ShapeDtypeStruct((S, D), BF16), jax.ShapeDtypeStruct((S, LANES), F32)],
        sem=("parallel", "arbitrary"), args=(q2, kv, kv, c_row), exchange=exchange)
    return o, l2_sh, got


def _fox_dq(q2, kv, o, do, c_row, l2_sh, *, t=512, exchange=None):
    S, D = q2.shape
    H = D // HEAD_DIM
    t = _tile(S, t)
    nb = S // t
    scale = HEAD_DIM ** -0.5

    def body(q_ref, k_ref, v_ref, o_ref, do_ref, cr_ref, l2_ref, dq_ref, dl_ref, dcq_ref):
        i = pl.program_id(0)
        h = pl.program_id(1)
        ref_i = cr_ref[i][:, 0:1]
        l2 = _lane_pick(l2_ref[...], h)
        qv = q_ref[...]
        dov = do_ref[...]
        delta = jnp.sum(o_ref[...].astype(F32) * dov.astype(F32), axis=-1, keepdims=True)

        def step(j, carry, masked):
            acc, rowsum = carry
            r0 = pl.multiple_of(j * t, t)
            kb = k_ref[pl.ds(r0, t), :]
            vb = v_ref[pl.ds(r0, t), :]
            s = lax.dot_general(qv, kb, _DIMS["nt"], preferred_element_type=F32) - (cr_ref[j] - ref_i)
            p = _causal(jnp.exp2(s - l2), masked, 0.0)
            dp = lax.dot_general(dov, vb, _DIMS["nt"], preferred_element_type=F32)
            ds = p * (dp - delta)
            rowsum = rowsum + jnp.sum(ds, axis=-1, keepdims=True)
            return acc + jnp.dot(ds.astype(BF16), kb, preferred_element_type=F32), rowsum

        init = (jnp.zeros((t, HEAD_DIM), F32), jnp.zeros((t, 1), F32))
        carry = lax.fori_loop(0, i, lambda j, c: step(j, c, False), init)
        acc, rowsum = step(i, carry, True)
        dq_ref[...] = (acc * scale).astype(dq_ref.dtype)
        _lane_put(dl_ref, h, delta)
        _lane_put(dcq_ref, h, rowsum)

    qspec = pl.BlockSpec((t, HEAD_DIM), lambda i, h: (i, h))
    shspec = pl.BlockSpec((t, LANES), lambda i, h: (i, 0))
    (dq, delta_sh, dcq_sh), got = _call(
        body, name="fox_dq", grid=(nb, H),
        in_specs=[qspec,
                  pl.BlockSpec((S, HEAD_DIM), lambda i, h: (0, h)),
                  pl.BlockSpec((S, HEAD_DIM), lambda i, h: (0, H + h)),
                  qspec, qspec,
                  pl.BlockSpec((None, nb, 1, t), lambda i, h: (h, 0, 0, 0)),
                  shspec],
        out_specs=[qspec, shspec, shspec],
        out_shape=[jax.ShapeDtypeStruct((S, D), BF16), jax.ShapeDtypeStruct((S, LANES), F32),
                   jax.ShapeDtypeStruct((S, LANES), F32)],
        sem=("parallel", "arbitrary"), args=(q2, kv, kv, o, do, c_row, l2_sh), exchange=exchange)
    return dq, delta_sh, dcq_sh, got


def _fox_dkv(q2, kv, do, c_sh, c_row, l2_row, delta_row, *, t=512, exchange=None):
    S, D = q2.shape
    H = D // HEAD_DIM
    t = _tile(S, t)
    nb = S // t

    def body(q_ref, do_ref, k_ref, v_ref, csh_ref, cr_ref, l2_ref, dl_ref, dk_ref, dv_ref, dck_ref):
        j = pl.program_id(0)
        h = pl.program_id(1)
        ck = _lane_pick(csh_ref[...], h)
        kb = k_ref[...]
        vb = v_ref[...]

        def step(i, carry, masked):
            dk, dv, dc = carry
            r0 = pl.multiple_of(i * t, t)
            qb = q_ref[pl.ds(r0, t), :]
            dob = do_ref[pl.ds(r0, t), :]
            ref_i = cr_ref[i][:, 0:1]
            s = lax.dot_general(kb, qb, _DIMS["nt"], preferred_element_type=F32) - (ck - ref_i)
            p = _causal(jnp.exp2(s - l2_ref[i]), masked, 0.0, rows_are_queries=False)
            dv = dv + jnp.dot(p.astype(BF16), dob, preferred_element_type=F32)
            dp = lax.dot_general(vb, dob, _DIMS["nt"], preferred_element_type=F32)
            ds = p * (dp - dl_ref[i])
            dc = dc - jnp.sum(ds, axis=-1, keepdims=True)
            dk = dk + jnp.dot(ds.astype(BF16), qb, preferred_element_type=F32)
            return dk, dv, dc

        init = (jnp.zeros((t, HEAD_DIM), F32), jnp.zeros((t, HEAD_DIM), F32), jnp.zeros((t, 1), F32))
        carry = step(j, init, True)
        dk, dv, dc = lax.fori_loop(j + 1, nb, lambda i, c: step(i, c, False), carry)
        dk_ref[...] = (dk * LN2).astype(dk_ref.dtype)
        dv_ref[...] = dv.astype(dv_ref.dtype)
        _lane_put(dck_ref, h, dc)

    kspec = pl.BlockSpec((t, HEAD_DIM), lambda j, h: (j, h))
    rowspec = pl.BlockSpec((None, nb, 1, t), lambda j, h: (h, 0, 0, 0))
    shspec = pl.BlockSpec((t, LANES), lambda j, h: (j, 0))
    (dk, dv, dck_sh), got = _call(
        body, name="fox_dkv", grid=(nb, H),
        in_specs=[pl.BlockSpec((S, HEAD_DIM), lambda j, h: (0, h)),
                  pl.BlockSpec((S, HEAD_DIM), lambda j, h: (0, h)),
                  kspec,
                  pl.BlockSpec((t, HEAD_DIM), lambda j, h: (j, H + h)),
                  shspec, rowspec, rowspec, rowspec],
        out_specs=[kspec, kspec, shspec],
        out_shape=[jax.ShapeDtypeStruct((S, D), BF16), jax.ShapeDtypeStruct((S, D), BF16),
                   jax.ShapeDtypeStruct((S, LANES), F32)],
        sem=("parallel", "arbitrary"), args=(q2, do, kv, kv, c_sh, c_row, l2_row, delta_row), exchange=exchange)
    return dk, dv, dck_sh, got


def _fox_delta(o, do, *, t=512):
    S, D = o.shape
    H = D // HEAD_DIM
    t = _tile(S, t)

    def body(o_ref, do_ref, dl_ref):
        h = pl.program_id(1)
        _lane_put(dl_ref, h, jnp.sum(o_ref[...].astype(F32) * do_ref[...].astype(F32), axis=-1, keepdims=True))

    spec = pl.BlockSpec((t, HEAD_DIM), lambda i, h: (i, h))
    return pl.pallas_call(
        body, grid=(S // t, H), in_specs=[spec, spec], out_specs=pl.BlockSpec((t, LANES), lambda i, h: (i, 0)),
        out_shape=jax.ShapeDtypeStruct((S, LANES), F32),
        compiler_params=_params(("parallel", "arbitrary")), name="fox_delta")(o, do)


def _fox_bwd(q2, kv, do, c_sh, c_row, l2_row, delta_row, *, t=512, exchange=None):
    S, D = q2.shape
    H = D // HEAD_DIM
    t = _tile(S, t)
    nb = S // t
    scale = HEAD_DIM ** -0.5

    def body(q_ref, do_ref, k_ref, v_ref, csh_ref, cr_ref, l2_ref, dl_ref, dq_ref, dk_ref, dv_ref, dck_ref, dcq_ref,
             dq_acc):
        h = pl.program_id(0)
        j = pl.program_id(1)
        ck = _lane_pick(csh_ref[...], h)
        kb = k_ref[...]
        vb = v_ref[...]

        @pl.when(j == 0)
        def _():
            dq_acc[...] = jnp.zeros_like(dq_acc)
            dcq_ref[...] = jnp.zeros_like(dcq_ref)

        def step(i, carry, masked):
            dk, dv, dc = carry
            r0 = pl.multiple_of(i * t, t)
            qb = q_ref[pl.ds(r0, t), :]
            dob = do_ref[pl.ds(r0, t), :]
            s = lax.dot_general(kb, qb, _DIMS["nt"], preferred_element_type=F32) - (ck - cr_ref[i][:, 0:1])
            p = _causal(jnp.exp2(s - l2_ref[i]), masked, 0.0, rows_are_queries=False)
            dv = dv + jnp.dot(p.astype(BF16), dob, preferred_element_type=F32)
            dp = lax.dot_general(vb, dob, _DIMS["nt"], preferred_element_type=F32)
            ds = p * (dp - dl_ref[i])
            dc = dc - jnp.sum(ds, axis=-1, keepdims=True)
            dcq_ref[i] += jnp.sum(ds, axis=0, keepdims=True)
            dsb = ds.astype(BF16)
            dk = dk + jnp.dot(dsb, qb, preferred_element_type=F32)
            dq_acc[pl.ds(r0, t), :] += lax.dot_general(dsb, kb, _DIMS["tn"], preferred_element_type=F32)
            return dk, dv, dc

        init = (jnp.zeros((t, HEAD_DIM), F32), jnp.zeros((t, HEAD_DIM), F32), jnp.zeros((t, 1), F32))
        carry = step(j, init, True)
        dk, dv, dc = lax.fori_loop(j + 1, nb, lambda i, c: step(i, c, False), carry)
        dk_ref[...] = (dk * LN2).astype(dk_ref.dtype)
        dv_ref[...] = dv.astype(dv_ref.dtype)
        rows = pl.ds(pl.multiple_of(j * t, t), t)
        lane = lax.broadcasted_iota(jnp.int32, (t, LANES), 1)

        @pl.when(h == 0)
        def _():
            dck_ref[rows, :] = jnp.where(lane == 0, dc, 0.0)

        @pl.when(h > 0)
        def _():
            dck_ref[rows, :] = jnp.where(lane == h, dc, dck_ref[rows, :])

        @pl.when(j == nb - 1)
        def _():
            dq_ref[...] = (dq_acc[...] * scale).astype(dq_ref.dtype)

    kspec = pl.BlockSpec((t, HEAD_DIM), lambda h, j: (j, h))
    headspec = pl.BlockSpec((S, HEAD_DIM), lambda h, j: (0, h))
    rowspec = pl.BlockSpec((None, nb, 1, t), lambda h, j: (h, 0, 0, 0))
    (dq, dk, dv, dck_sh, dcq_row), got = _call(
        body, name="fox_bwd", grid=(H, nb),
        in_specs=[headspec, headspec, kspec,
                  pl.BlockSpec((t, HEAD_DIM), lambda h, j: (j, H + h)),
                  pl.BlockSpec((t, LANES), lambda h, j: (j, 0)),
                  rowspec, rowspec, rowspec],
        out_specs=[headspec, kspec, kspec, pl.BlockSpec((S, LANES), lambda h, j: (0, 0)), rowspec],
        out_shape=[jax.ShapeDtypeStruct((S, D), BF16), jax.ShapeDtypeStruct((S, D), BF16),
                   jax.ShapeDtypeStruct((S, D), BF16), jax.ShapeDtypeStruct((S, LANES), F32),
                   jax.ShapeDtypeStruct((H, nb, 1, t), F32)],
        scratch_shapes=[pltpu.VMEM((S, HEAD_DIM), F32)],
        sem=("arbitrary", "arbitrary"), args=(q2, do, kv, kv, c_sh, c_row, l2_row, delta_row), exchange=exchange)
    return dq, dk, dv, dck_sh, dcq_row, got


def _adamw(name, parts, w, m, v, layer=None, *, tr=128):
    P, R, C = parts.shape
    tr = _tile(R, tr)

    def body(p_ref, w_ref, m_ref, v_ref, g_ref, d_ref, nm_ref, nv_ref):
        g = p_ref[0].astype(F32)
        for k in range(1, P):
            g = g + p_ref[k].astype(F32)
        wv = w_ref[...]
        nm = ADAM_B1 * m_ref[...] + (1.0 - ADAM_B1) * g
        nv = ADAM_B2 * v_ref[...] + (1.0 - ADAM_B2) * (g * g)
        m_hat = nm / (1.0 - ADAM_B1 ** ADAM_STEP)
        v_hat = nv / (1.0 - ADAM_B2 ** ADAM_STEP)
        g_ref[...] = g
        d_ref[...] = -ADAM_LR * (m_hat / (jnp.sqrt(v_hat) + ADAM_EPS) + ADAM_WD * wv)
        nm_ref[...] = nm
        nv_ref[...] = nv

    if layer is None:
        wspec = pl.BlockSpec((tr, C), lambda i: (i, 0))
    else:
        wspec = pl.BlockSpec((None, tr, C), lambda i: (layer, i, 0))
    ospec = pl.BlockSpec((tr, C), lambda i: (i, 0))
    return pl.pallas_call(
        body, grid=(R // tr,),
        in_specs=[pl.BlockSpec((P, tr, C), lambda i: (0, i, 0)), wspec, wspec, wspec],
        out_specs=[ospec] * 4, out_shape=[jax.ShapeDtypeStruct((R, C), F32)] * 4,
        compiler_params=_params(("parallel",)), name=name)(parts, w, m, v)


def _rows(a):
    flat = a.reshape(-1)
    pad = (-flat.shape[0]) % LANES
    if pad:
        flat = jnp.pad(flat, (0, pad))
    return flat.reshape(-1, LANES)


def kernel(x, norm_mix, norm_mlp, pool_w, pool_scale, norm_kv, w_kvf, b_f, w_q, w_o, w_up, w_down, norm_out, loss_target, m_norm_mix, m_norm_mlp, m_pool_w, m_pool_scale, m_norm_kv, m_w_kvf, m_b_f, m_w_q, m_w_o, m_w_up, m_w_down, m_norm_out, v_norm_mix, v_norm_mlp, v_pool_w, v_pool_scale, v_norm_kv, v_w_kvf, v_b_f, v_w_q, v_w_o, v_w_up, v_w_down, v_norm_out):
    _, S, D = x.shape
    H = D // HEAD_DIM
    dg = D // N_GROUPS
    n_kvf = 2 * D + H
    kvf_b = w_kvf.shape[1]
    fb = w_up.shape[2]
    ps_b = pool_scale.shape[1]
    xi, yi, ci = _position()
    my_block = 4 * xi + 2 * yi + ci
    x2 = x.reshape(S, D)
    tgt = loss_target.reshape(S, D)
    b_pad = jnp.pad(b_f, (0, LANES - H)).reshape(1, LANES)

    g_pool, g_scale, g_up0 = _all_gather_two_level([pool_w.astype(BF16), pool_scale, w_up[0].astype(BF16)])
    pw = g_pool[:, 0].transpose(1, 0, 2, 3).reshape(N_GROUPS, dg, dg)
    scale_full = g_scale.reshape(1, D)
    ex_down0 = _Exchange("gather", [w_down[0].astype(BF16)])
    ex_attn_w = _Exchange("gather", [w_kvf.astype(BF16), w_q[0].astype(BF16), w_o[0].astype(BF16)])
    ex_layer1 = _Exchange("gather", [w_up[1].astype(BF16), w_down[1].astype(BF16)])

    h1, diff = _pool_fwd(x2, norm_mix[0:1], pw, scale_full)
    (hn_m0,) = _rms_fwd("rms_mlp0", h1, norm_mlp[0:1])
    u0, uu0, h2, (g_down0,), (g_kvf, g_q, g_o) = _mlp_fwd("l0", hn_m0, h1, g_up0, None, ex_up=ex_down0, ex_down=ex_attn_w)
    wkvf = g_kvf.transpose(1, 0, 2).reshape(D, n_kvf)
    w_kv = wkvf[:, :2 * D]
    w_f = jnp.pad(wkvf[:, 2 * D:], ((0, 0), (0, LANES - H)))
    wq = g_q.reshape(D, D)
    wo = g_o.reshape(D, D)

    gains_kv_q = jnp.stack([norm_kv, norm_mix[1]])
    hkv, hn_q = _rms_fwd("rms_kv_q", h2, gains_kv_q)
    kv = _mm_nn("kv_proj", hkv, w_kv, out_dtype=BF16)
    f_raw = _mm_nn("f_proj", hkv, w_f, out_dtype=F32)
    c_sh = _gate_fwd(f_raw, b_pad)
    t_attn = _tile(S, 1024)
    c2_sh = c_sh * LOG2E
    c2_row = _row_layout(c2_sh[:, :H].T, t_attn)
    q2 = _mm_nn("q_proj", hn_q, wq, out_dtype=BF16, out_scale=HEAD_DIM ** -0.5 * LOG2E)
    o, l2_sh, (g_up1, g_down1) = _fox_fwd(q2, kv, c2_row, t=t_attn, exchange=ex_layer1)
    h3 = _mm_nn("o_proj", o, wo, out_dtype=F32, residual=h2)
    (hn_m1,) = _rms_fwd("rms_mlp1", h3, norm_mlp[1:2])
    u1, uu1, h4, _, _ = _mlp_fwd("l1", hn_m1, h3, g_up1, g_down1)
    dh4, d_norm_out, loss_part = _loss_head(h4, norm_out.reshape(1, D), tgt)

    d_pre = _mlp_dpre("l1", dh4, u1, g_down1)
    dw_down1, _ = _mm_tn("mlp_dwdown_l1", uu1, dh4)
    dw_up1, _ = _mlp_dwup("l1", hn_m1, d_pre, fb)
    d_hn, _ = _mlp_dhn("l1", d_pre, g_up1)
    dh3, d_norm_mlp1 = _rms_bwd("rms_mlp1_bwd", h3, norm_mlp[1:2], [d_hn], dh4)

    do = _mm_nt("o_proj_dx", dh3, wo, out_dtype=BF16)
    dw_o, _ = _mm_tn("o_proj_dw", o, dh3)
    l2_row = _row_layout(l2_sh[:, :H].T, t_attn)
    delta_row = _row_layout(_fox_delta(o, do, t=t_attn)[:, :H].T, t_attn)
    dq, dk, dv, dck_sh, dcq_row, (p_up1, p_down1, p_o) = _fox_bwd(
        q2, kv, do, c2_sh, c2_row, l2_row, delta_row, t=t_attn,
        exchange=_Exchange("scatter", [dw_up1, dw_down1.reshape(N_DEV, fb, D), dw_o.reshape(N_DEV, D // N_DEV, D)]))
    dw_q, _ = _mm_tn("q_proj_dw", hn_q, dq)
    d_hn_q = _mm_nt("q_proj_dx", dq, wq, out_dtype=F32)

    dcq_sh = jnp.pad(dcq_row.reshape(H, S).T, ((0, 0), (0, LANES - H)))
    d_f, d_b = _gate_bwd(dck_sh + dcq_sh, f_raw, b_pad)
    dw_k, _ = _mm_tn("k_proj_dw", hkv, dk)
    dw_v, _ = _mm_tn("v_proj_dw", hkv, dv)
    dw_f, _ = _mm_tn("f_proj_dw", hkv, d_f)
    d_hkv = _mm_nt("k_proj_dx", dk, w_kv, b_cols=(0, D), out_dtype=F32)
    d_hkv = _mm_nt("v_proj_dx", dv, w_kv, b_cols=(D, D), out_dtype=F32, residual=d_hkv)
    d_hkv = _mm_nt("f_proj_dx", d_f, w_f, out_dtype=F32, residual=d_hkv)
    dh2, d_norm_kv_q = _rms_bwd("rms_kv_q_bwd", h2, gains_kv_q, [d_hkv, d_hn_q], dh3)
    dw_kvf = jnp.concatenate([dw_k, dw_v, dw_f[:, :H]], axis=1).reshape(D, N_DEV, kvf_b).transpose(1, 0, 2)

    d_pre = _mlp_dpre("l0", dh2, u0, g_down0)
    dw_down0, (p_kvf, p_q) = _mm_tn(
        "mlp_dwdown_l0", uu0, dh2, exchange=_Exchange("scatter", [dw_kvf, dw_q.reshape(N_DEV, D // N_DEV, D)]))
    dw_up0, (p_down0,) = _mlp_dwup(
        "l0", hn_m0, d_pre, fb, exchange=_Exchange("scatter", [dw_down0.reshape(N_DEV, fb, D)]))
    d_hn, (p_up0,) = _mlp_dhn("l0", d_pre, g_up0, exchange=_Exchange("scatter", [dw_up0]))
    dh1, d_norm_mlp0 = _rms_bwd("rms_mlp0_bwd", h1, norm_mlp[0:1], [d_hn], dh2)
    grad_x, dw_pool, d_scale, d_norm_mix0 = _pool_bwd(x2, dh1, diff, norm_mix[0:1], pw, scale_full)
    (p_pool,) = _exchange_now("scatter_pool", _Exchange("scatter", [
        dw_pool.astype(BF16).reshape(N_GROUPS, N_DEV, dg // N_DEV, dg).transpose(1, 0, 2, 3).reshape(
            N_DEV, N_GROUPS * dg // N_DEV, dg)]))

    small = jnp.concatenate([
        _rows(jnp.concatenate([d_norm_mix0, d_norm_kv_q[1:2]], axis=0)),
        _rows(jnp.concatenate([d_norm_mlp0, d_norm_mlp1], axis=0)),
        _rows(d_norm_kv_q[0:1]),
        _rows(d_norm_out),
        _rows(d_scale),
        d_b,
        jnp.pad(loss_part[0:1, 0:1], ((0, 0), (0, LANES - 1))),
    ], axis=0)
    n_small = small.shape[0]
    small = jnp.pad(small, ((0, (-n_small) % 8), (0, 0)))
    total = _all_reduce_small(small)
    rd = D // LANES
    loss = total[7 * rd + 1, 0]
    g_scale_mine = lax.dynamic_slice(total[6 * rd:7 * rd].reshape(D), (my_block * ps_b,), (ps_b,))

    def pack(nm_, nl_, kv_, out_, bf_, ps_):
        return jnp.concatenate([_rows(nm_), _rows(nl_), _rows(kv_), _rows(out_), _rows(bf_), _rows(ps_)], axis=0)

    g_small = jnp.concatenate([total[:6 * rd], total[7 * rd:7 * rd + 1], _rows(g_scale_mine)], axis=0)
    w_small = pack(norm_mix, norm_mlp, norm_kv, norm_out, b_f, pool_scale)
    m_small = pack(m_norm_mix, m_norm_mlp, m_norm_kv, m_norm_out, m_b_f, m_pool_scale)
    v_small = pack(v_norm_mix, v_norm_mlp, v_norm_kv, v_norm_out, v_b_f, v_pool_scale)
    rs = g_small.shape[0]
    padr = (-rs) % 8
    pad8 = lambda a: jnp.pad(a, ((0, padr), (0, 0)))
    small_out = _adamw("adamw_small", pad8(g_small)[None], pad8(w_small), pad8(m_small), pad8(v_small), tr=rs + padr)

    def unpack(a):
        o0 = 0
        res = []
        for shape in [(2, D), (2, D), (D,), (D,)]:
            nr = (2 * rd) if len(shape) == 2 else rd
            res.append(a[o0:o0 + nr].reshape(shape))
            o0 += nr
        res.append(a[o0, :H])
        res.append(a[o0 + 1:o0 + 1 + ps_b // LANES].reshape(1, ps_b))
        return res

    small_res = [unpack(a) for a in small_out]

    r_pool = _adamw("adamw_pool", p_pool, pool_w.reshape(-1, dg), m_pool_w.reshape(-1, dg), v_pool_w.reshape(-1, dg))
    r_kvf = _adamw("adamw_kvf", p_kvf, w_kvf, m_w_kvf, v_w_kvf)
    r_q = _adamw("adamw_q", p_q, w_q[0], m_w_q[0], v_w_q[0])
    r_o = _adamw("adamw_o", p_o, w_o[0], m_w_o[0], v_w_o[0])
    r_up = [_adamw(f"adamw_up{l}", p, w_up, m_w_up, v_w_up, layer=l) for l, p in enumerate([p_up0, p_up1])]
    r_down = [_adamw(f"adamw_down{l}", p, w_down, m_w_down, v_w_down, layer=l) for l, p in enumerate([p_down0, p_down1])]

    def leaves(kind):
        sm = small_res[kind]
        return [
            sm[0], sm[1],
            r_pool[kind].reshape(pool_w.shape),
            sm[5],
            sm[2],
            r_kvf[kind],
            sm[4],
            r_q[kind][None], r_o[kind][None],
            jnp.stack([r_up[0][kind], r_up[1][kind]]),
            jnp.stack([r_down[0][kind], r_down[1][kind]]),
            sm[3],
        ]

    return (loss, grad_x.reshape(x.shape), *leaves(0), *leaves(1), *leaves(2), *leaves(3))
```

```python
import jax
import jax.numpy as jnp
from jax import lax
from jax.experimental import pallas as pl
from jax.experimental.pallas import tpu as pltpu

F32 = jnp.float32
BF16 = jnp.bfloat16
MESH = pl.DeviceIdType.MESH

N_DEV = 8
EPS = 1e-6
NEG_INF = -1e30
POOL_WINDOWS = (2, 4, 8, 16)
N_GROUPS = len(POOL_WINDOWS)
POOL_HALO = 16
HEAD_DIM = 128
LANES = 128
HEADS_PER_STEP = 2
LOG2E = 1.4426950408889634
LN2 = 0.6931471805599453

ADAM_LR = 0.001
ADAM_B1 = 0.9
ADAM_B2 = 0.999
ADAM_EPS = 1e-08
ADAM_WD = 0.01
ADAM_STEP = 10

VMEM_LIMIT = 56 * 1024 * 1024

_ANY = pl.BlockSpec(memory_space=pl.ANY)


def _tile(n, want):
    t = min(n, want)
    assert n % t == 0, (n, want)
    return t


def _params(sem, vmem=VMEM_LIMIT):
    return pltpu.CompilerParams(dimension_semantics=sem, vmem_limit_bytes=vmem)


def _position():
    return lax.axis_index("x"), lax.axis_index("y"), lax.axis_index("c")


class _Exchange:
    def __init__(self, kind, arrays):
        assert kind in ("gather", "scatter")
        self.kind = kind
        self.arrays = list(arrays)
        self.n = len(self.arrays)
        shapes = [(N_DEV, *a.shape) if kind == "gather" else a.shape for a in self.arrays]
        self.out_shape = [jax.ShapeDtypeStruct(s, a.dtype) for s, a in zip(shapes, self.arrays)]
        self.scratch = [pltpu.SemaphoreType.DMA((self.n, N_DEV)), pltpu.SemaphoreType.DMA((self.n, N_DEV))]

    def _copies(self, ins, outs, send_sems, recv_sems, with_recv=True):
        x, y, c = _position()
        me = 4 * x + 2 * y + c
        gather = self.kind == "gather"
        local, sends, recvs = [], [], []
        for a in range(self.n):
            local.append(pltpu.make_async_copy(ins[a] if gather else ins[a].at[me], outs[a].at[me], send_sems.at[a, 0]))
            for k in range(1, N_DEV):
                px, py, pc = x ^ (k >> 2), y ^ ((k >> 1) & 1), c ^ (k & 1)
                peer = 4 * px + 2 * py + pc
                src = ins[a] if gather else ins[a].at[peer]
                common = dict(send_sem=send_sems.at[a, k], recv_sem=recv_sems.at[a, k], device_id=(px, py, pc),
                              device_id_type=MESH)
                sends.append(pltpu.make_async_remote_copy(src_ref=src, dst_ref=outs[a].at[me], **common))
                if with_recv:
                    recvs.append(pltpu.make_async_remote_copy(src_ref=src, dst_ref=outs[a].at[peer], **common))
        return local, sends, recvs

    def start(self, ins, outs, send_sems, recv_sems):
        local, sends, _ = self._copies(ins, outs, send_sems, recv_sems, with_recv=False)
        for cp in local + sends:
            cp.start()

    def wait(self, ins, outs, send_sems, recv_sems):
        local, sends, recvs = self._copies(ins, outs, send_sems, recv_sems)
        for send, recv in zip(sends, recvs):
            send.wait_send()
            recv.wait_recv()
        for cp in local:
            cp.wait()


def _call(body, *, name, grid, in_specs, out_specs, out_shape, args, scratch_shapes=(), sem=None, exchange=None):
    if exchange is None:
        outs = pl.pallas_call(
            body, grid=grid, in_specs=in_specs, out_specs=out_specs, out_shape=out_shape,
            scratch_shapes=list(scratch_shapes), compiler_params=_params(sem), name=name)(*args)
        return list(outs), []

    n_in, n_out, n_scr, n = len(in_specs), len(out_specs), len(scratch_shapes), exchange.n

    def hosted(*refs):
        ins, refs = refs[:n_in], refs[n_in:]
        ex_in, refs = refs[:n], refs[n:]
        outs, refs = refs[:n_out], refs[n_out:]
        ex_out, refs = refs[:n], refs[n:]
        scratch, sems = refs[:n_scr], refs[n_scr:]
        first = _all_true([pl.program_id(d) == 0 for d in range(len(grid))])
        last = _all_true([pl.program_id(d) == grid[d] - 1 for d in range(len(grid))])

        @pl.when(first)
        def _():
            exchange.start(ex_in, ex_out, *sems)

        body(*ins, *outs, *scratch)

        @pl.when(last)
        def _():
            exchange.wait(ex_in, ex_out, *sems)

    res = pl.pallas_call(
        hosted, grid=grid, in_specs=[*in_specs, *[_ANY] * n], out_specs=[*out_specs, *[_ANY] * n],
        out_shape=[*out_shape, *exchange.out_shape], scratch_shapes=[*scratch_shapes, *exchange.scratch],
        compiler_params=_params(("arbitrary",) * len(grid)), name=name)(*args, *exchange.arrays)
    return list(res[:n_out]), list(res[n_out:])


def _all_true(preds):
    out = preds[0]
    for p in preds[1:]:
        out = jnp.logical_and(out, p)
    return out


def _exchange_now(name, exchange):
    def body(*refs):
        n = exchange.n
        exchange.start(refs[:n], refs[n:2 * n], *refs[2 * n:])
        exchange.wait(refs[:n], refs[n:2 * n], *refs[2 * n:])

    return pl.pallas_call(
        body, in_specs=[_ANY] * exchange.n, out_specs=[_ANY] * exchange.n, out_shape=exchange.out_shape,
        scratch_shapes=exchange.scratch, name=name)(*exchange.arrays)


def _all_gather_two_level(shards):
    n = len(shards)

    def body(*refs):
        ins, outs = refs[:n], refs[n:2 * n]
        send_sems, recv_sems, local_sems = refs[2 * n:]
        x, y, c = _position()
        me, sibling = (x, y, c), (x, y, 1 - c)
        chips = [(1 - x, y), (x, 1 - y), (1 - x, 1 - y)]

        def slot(a, px, py, pc):
            return outs[a].at[4 * px + 2 * py + pc]

        def copy(a, k, block, to, src=None):
            return pltpu.make_async_remote_copy(
                src_ref=slot(a, *block) if src is None else src, dst_ref=slot(a, *block),
                send_sem=send_sems.at[a, k], recv_sem=recv_sems.at[a, k], device_id=to, device_id_type=MESH)

        mine = [pltpu.make_async_copy(ins[a], slot(a, *me), local_sems.at[a]) for a in range(n)]
        first = []
        for a in range(n):
            mine[a].start()
            first.append(copy(a, 0, me, sibling, src=ins[a]))
            first += [copy(a, 1 + j, me, (*chip, c), src=ins[a]) for j, chip in enumerate(chips)]
        for cp in first:
            cp.start()
        passed = []
        for a in range(n):
            for j, chip in enumerate(chips):
                copy(a, 1 + j, (*chip, c), me).wait_recv()
                fwd = copy(a, 4 + j, (*chip, c), sibling)
                fwd.start()
                passed.append(fwd)
        for a in range(n):
            copy(a, 0, sibling, me).wait_recv()
            for j, chip in enumerate(chips):
                copy(a, 4 + j, (*chip, 1 - c), me).wait_recv()
        for cp in first + passed:
            cp.wait_send()
        for cp in mine:
            cp.wait()

    return pl.pallas_call(
        body, in_specs=[_ANY] * n, out_specs=[_ANY] * n,
        out_shape=[jax.ShapeDtypeStruct((N_DEV, *s.shape), s.dtype) for s in shards],
        scratch_shapes=[pltpu.SemaphoreType.DMA((n, 7)), pltpu.SemaphoreType.DMA((n, 7)),
                        pltpu.SemaphoreType.DMA((n,))],
        name="all_gather_first")(*shards)


def _all_reduce_small(vec):
    R = vec.shape[0]

    def body(v_ref, o_ref, buf_ref, send_sems, recv_sems):
        x, y, c = _position()
        me = 4 * x + 2 * y + c
        buf_ref[me] = v_ref[...]
        copies = []
        for k in range(1, N_DEV):
            peer = (x ^ (k >> 2), y ^ ((k >> 1) & 1), c ^ (k & 1))
            copies.append(pltpu.make_async_remote_copy(
                src_ref=buf_ref.at[me], dst_ref=buf_ref.at[me], send_sem=send_sems.at[k], recv_sem=recv_sems.at[k],
                device_id=peer, device_id_type=MESH))
        for cp in copies:
            cp.start()
        for cp in copies:
            cp.wait()
        total = buf_ref[0]
        for d in range(1, N_DEV):
            total = total + buf_ref[d]
        o_ref[...] = total

    vm = pl.BlockSpec(memory_space=pltpu.VMEM)
    return pl.pallas_call(
        body, in_specs=[vm], out_specs=vm, out_shape=jax.ShapeDtypeStruct((R, LANES), F32),
        scratch_shapes=[pltpu.VMEM((N_DEV, R, LANES), F32), pltpu.SemaphoreType.DMA((N_DEV,)),
                        pltpu.SemaphoreType.DMA((N_DEV,))],
        name="all_reduce_small")(vec)


_DIMS = {
    "nn": (((1,), (0,)), ((), ())),
    "nt": (((1,), (1,)), ((), ())),
    "tn": (((0,), (0,)), ((), ())),
}


def _matmul(name, a, b, *, mode, grid, tm, tn, a_spec, b_spec, outs, extras=(), extra_specs=(), epilogue=None,
            acc_outs=(), exchange=None):
    nk = grid[2]
    n_extra = len(extras)
    dn = _DIMS[mode]

    def body(a_ref, b_ref, *rest):
        extra_refs = rest[:n_extra]
        out_refs = rest[n_extra:-1]
        acc_ref = rest[-1]
        j = pl.program_id(1)
        k = pl.program_id(2)

        def product():
            b = b_ref[...]
            if b.ndim == 3:
                b = b.reshape(-1, b.shape[-1]) if mode == "nn" else jnp.concatenate([b[0], b[1]], axis=1)
            return lax.dot_general(a_ref[...].astype(BF16), b.astype(BF16), dn, preferred_element_type=F32)

        def finish(acc):
            vals = (acc,) if epilogue is None else epilogue(acc, *[r[...] for r in extra_refs])
            for idx, (o_ref, val) in enumerate(zip(out_refs, vals)):
                if idx in acc_outs:
                    @pl.when(j == 0)
                    def _():
                        o_ref[...] = val.astype(o_ref.dtype)

                    @pl.when(j > 0)
                    def _():
                        o_ref[...] += val.astype(o_ref.dtype)
                else:
                    o_ref[...] = val.astype(o_ref.dtype)

        if nk == 1:
            finish(product())
            return

        @pl.when(k == 0)
        def _():
            acc_ref[...] = product()

        @pl.when(k > 0)
        def _():
            acc_ref[...] += product()

        @pl.when(k == nk - 1)
        def _():
            finish(acc_ref[...])

    return _call(
        body, name=name, grid=grid, in_specs=[a_spec, b_spec, *extra_specs], out_specs=[o[2] for o in outs],
        out_shape=[jax.ShapeDtypeStruct(o[0], o[1]) for o in outs], scratch_shapes=[pltpu.VMEM((tm, tn), F32)],
        sem=("parallel", "arbitrary" if acc_outs else "parallel", "arbitrary"), args=(a, b, *extras), exchange=exchange)


def _ij(tm, tn):
    return pl.BlockSpec((tm, tn), lambda i, j, k: (i, j))


def _mm_nn(name, a, b, *, out_dtype, residual=None, out_scale=None, exchange=None, tm=1024, tn=1024, tk=2048):
    M, K = a.shape
    N = b.shape[1]
    tm, tn, tk = _tile(M, tm), _tile(N, tn), _tile(K, tk)
    extras, especs, epi = (), (), None
    if residual is not None:
        extras, especs = (residual,), (_ij(tm, tn),)
        epi = lambda acc, r: (acc + r,)
    elif out_scale is not None:
        epi = lambda acc: (acc * out_scale,)
    outs, got = _matmul(
        name, a, b, mode="nn", grid=(M // tm, N // tn, K // tk), tm=tm, tn=tn,
        a_spec=pl.BlockSpec((tm, tk), lambda i, j, k: (i, k)),
        b_spec=pl.BlockSpec((tk, tn), lambda i, j, k: (k, j)),
        outs=[((M, N), out_dtype, _ij(tm, tn))], extras=extras, extra_specs=especs, epilogue=epi, exchange=exchange)
    return outs[0] if exchange is None else (outs[0], got)


def _mm_nt(name, a, b, *, out_dtype, b_cols=None, residual=None, tm=1024, tn=1024, tk=2048):
    M, K = a.shape
    N = b.shape[0]
    c0 = 0 if b_cols is None else b_cols[0]
    tm, tn, tk = _tile(M, tm), _tile(N, tn), _tile(K, tk)
    assert c0 % tk == 0
    kb0 = c0 // tk
    extras, especs, epi = (), (), None
    if residual is not None:
        extras, especs = (residual,), (_ij(tm, tn),)
        epi = lambda acc, r: (acc + r,)
    return _matmul(
        name, a, b, mode="nt", grid=(M // tm, N // tn, K // tk), tm=tm, tn=tn,
        a_spec=pl.BlockSpec((tm, tk), lambda i, j, k: (i, k)),
        b_spec=pl.BlockSpec((tn, tk), lambda i, j, k: (j, kb0 + k)),
        outs=[((M, N), out_dtype, _ij(tm, tn))], extras=extras, extra_specs=especs, epilogue=epi)[0][0]


def _mm_tn(name, a, b, *, exchange=None, tm=1024, tn=1024, tk=2048):
    K, M = a.shape
    N = b.shape[1]
    tm, tn, tk = _tile(M, tm), _tile(N, tn), _tile(K, tk)
    outs, got = _matmul(
        name, a, b, mode="tn", grid=(M // tm, N // tn, K // tk), tm=tm, tn=tn,
        a_spec=pl.BlockSpec((tk, tm), lambda i, j, k: (k, i)),
        b_spec=pl.BlockSpec((tk, tn), lambda i, j, k: (k, j)),
        outs=[((M, N), BF16, _ij(tm, tn))], exchange=exchange)
    return outs[0], got


def _mlp_fwd(tag, hn, h, w_up_g, w_down_g, *, ex_up=None, ex_down=None, tm=1024, tk=2048):
    S, D = hn.shape
    fb = w_up_g.shape[2]
    F = N_DEV * fb
    tm, tku = _tile(S, tm), _tile(D, tk)

    def up_epi(acc):
        u = jnp.maximum(acc, 0.0)
        return u, u * u

    (u, uu), got_up = _matmul(
        f"mlp_up_{tag}", hn, w_up_g, mode="nn", grid=(S // tm, N_DEV, D // tku), tm=tm, tn=fb,
        a_spec=pl.BlockSpec((tm, tku), lambda i, j, k: (i, k)),
        b_spec=pl.BlockSpec((None, tku, fb), lambda i, j, k: (j, k, 0)),
        outs=[((S, F), BF16, _ij(tm, fb)), ((S, F), BF16, _ij(tm, fb))], epilogue=up_epi, exchange=ex_up)
    if w_down_g is None:
        w_down_g = got_up[0]

    tn = _tile(D, 1024)
    (h_out,), got_down = _matmul(
        f"mlp_down_{tag}", uu, w_down_g, mode="nn", grid=(S // tm, D // tn, N_DEV // 2), tm=tm, tn=tn,
        a_spec=pl.BlockSpec((tm, 2 * fb), lambda i, j, k: (i, k)),
        b_spec=pl.BlockSpec((2, fb, tn), lambda i, j, k: (k, 0, j)),
        outs=[((S, D), F32, _ij(tm, tn))], extras=(h,), extra_specs=(_ij(tm, tn),),
        epilogue=lambda acc, res: (acc + res,), exchange=ex_down)
    return u, uu, h_out, got_up, got_down


def _mlp_dpre(tag, dh, u, w_down_g, *, tm=1024, tk=2048):
    S, D = dh.shape
    fb = w_down_g.shape[1]
    tm, tkd = _tile(S, tm), _tile(D, tk)
    return _matmul(
        f"mlp_dpre_{tag}", dh, w_down_g, mode="nt", grid=(S // tm, N_DEV, D // tkd), tm=tm, tn=fb,
        a_spec=pl.BlockSpec((tm, tkd), lambda i, j, k: (i, k)),
        b_spec=pl.BlockSpec((None, fb, tkd), lambda i, j, k: (j, 0, k)),
        outs=[((S, N_DEV * fb), BF16, _ij(tm, fb))], extras=(u,), extra_specs=(_ij(tm, fb),),
        epilogue=lambda acc, uv: (acc * (2.0 * uv.astype(F32)),))[0][0]


def _mlp_dwup(tag, hn, d_pre, fb, *, exchange=None):
    S, D = hn.shape
    tmu = _tile(D, 1024)
    tks = _tile(S, 2048)
    outs, got = _matmul(
        f"mlp_dwup_{tag}", hn, d_pre, mode="tn", grid=(D // tmu, N_DEV, S // tks), tm=tmu, tn=fb,
        a_spec=pl.BlockSpec((tks, tmu), lambda i, j, k: (k, i)),
        b_spec=pl.BlockSpec((tks, fb), lambda i, j, k: (k, j)),
        outs=[((N_DEV, D, fb), BF16, pl.BlockSpec((None, tmu, fb), lambda i, j, k: (j, i, 0)))], exchange=exchange)
    return outs[0], got


def _mlp_dhn(tag, d_pre, w_up_g, *, exchange=None, tm=1024):
    S, F = d_pre.shape
    D, fb = w_up_g.shape[1], w_up_g.shape[2]
    tm, tn = _tile(S, tm), _tile(D, 1024)
    outs, got = _matmul(
        f"mlp_dhn_{tag}", d_pre, w_up_g, mode="nt", grid=(S // tm, D // tn, N_DEV // 2), tm=tm, tn=tn,
        a_spec=pl.BlockSpec((tm, 2 * fb), lambda i, j, k: (i, k)),
        b_spec=pl.BlockSpec((2, tn, fb), lambda i, j, k: (k, j, 0)),
        outs=[((S, D), F32, _ij(tm, tn))], exchange=exchange)
    return outs[0], got


def _row_spec(ts, D):
    return pl.BlockSpec((ts, D), lambda i: (i, 0))


def _vec_spec(n, D):
    return pl.BlockSpec((n, D), lambda i: (0, 0))


def _rms_fwd(name, x, gains, *, ts=512):
    S, D = x.shape
    n = gains.shape[0]
    ts = _tile(S, ts)

    def body(x_ref, g_ref, *o_refs):
        xv = x_ref[...]
        y = xv * lax.rsqrt(jnp.mean(xv * xv, axis=-1, keepdims=True) + EPS)
        for i, o_ref in enumerate(o_refs):
            o_ref[...] = (y * g_ref[i:i + 1, :]).astype(o_ref.dtype)

    return pl.pallas_call(
        body, grid=(S // ts,), in_specs=[_row_spec(ts, D), _vec_spec(n, D)],
        out_specs=[_row_spec(ts, D)] * n, out_shape=[jax.ShapeDtypeStruct((S, D), BF16)] * n,
        compiler_params=_params(("parallel",)), name=name)(x, gains)


def _rms_bwd(name, x, gains, dys, res, *, ts=256):
    S, D = x.shape
    n = gains.shape[0]
    ts = _tile(S, ts)

    def body(x_ref, g_ref, *rest):
        dy_refs = rest[:n]
        res_ref, dx_ref, dxb_ref, dg_ref = rest[n:]
        i = pl.program_id(0)
        xv = x_ref[...]
        r = lax.rsqrt(jnp.mean(xv * xv, axis=-1, keepdims=True) + EPS)
        xhat = xv * r
        dxhat = None
        dgs = []
        for k in range(n):
            dy = dy_refs[k][...].astype(F32)
            dgs.append(jnp.sum(dy * xhat, axis=0, keepdims=True))
            term = dy * g_ref[k:k + 1, :]
            dxhat = term if dxhat is None else dxhat + term
        dx = res_ref[...] + r * (dxhat - xhat * jnp.mean(dxhat * xhat, axis=-1, keepdims=True))
        dx_ref[...] = dx
        dxb_ref[...] = dx.astype(BF16)
        dg = jnp.concatenate(dgs, axis=0) if n > 1 else dgs[0]

        @pl.when(i == 0)
        def _():
            dg_ref[...] = dg

        @pl.when(i > 0)
        def _():
            dg_ref[...] += dg

    return pl.pallas_call(
        body, grid=(S // ts,),
        in_specs=[_row_spec(ts, D), _vec_spec(n, D)] + [_row_spec(ts, D)] * (n + 1),
        out_specs=[_row_spec(ts, D), _row_spec(ts, D), _vec_spec(n, D)],
        out_shape=[jax.ShapeDtypeStruct((S, D), F32), jax.ShapeDtypeStruct((S, D), BF16),
                   jax.ShapeDtypeStruct((n, D), F32)],
        compiler_params=_params(("arbitrary",)), name=name)(x, gains, *dys, res)


def _loss_head(h, gain, target, *, ts=512):
    S, D = h.shape
    ts = _tile(S, ts)

    def body(x_ref, g_ref, t_ref, dx_ref, dxb_ref, dg_ref, loss_ref):
        i = pl.program_id(0)
        xv = x_ref[...]
        g = g_ref[...]
        r = lax.rsqrt(jnp.mean(xv * xv, axis=-1, keepdims=True) + EPS)
        xhat = xv * r
        err = xhat * g - t_ref[...]
        part = 0.5 * jnp.sum(jnp.mean(err * err, axis=-1, keepdims=True), axis=0, keepdims=True)
        dy = err * (1.0 / D)
        dg = jnp.sum(dy * xhat, axis=0, keepdims=True)
        dxhat = dy * g
        dx = r * (dxhat - xhat * jnp.mean(dxhat * xhat, axis=-1, keepdims=True))
        dx_ref[...] = dx
        dxb_ref[...] = dx.astype(BF16)

        @pl.when(i == 0)
        def _():
            dg_ref[...] = dg
            loss_ref[...] = jnp.broadcast_to(part, loss_ref.shape)

        @pl.when(i > 0)
        def _():
            dg_ref[...] += dg
            loss_ref[...] += jnp.broadcast_to(part, loss_ref.shape)

    return pl.pallas_call(
        body, grid=(S // ts,),
        in_specs=[_row_spec(ts, D), _vec_spec(1, D), _row_spec(ts, D)],
        out_specs=[_row_spec(ts, D), _row_spec(ts, D), _vec_spec(1, D), pl.BlockSpec((8, LANES), lambda i: (0, 0))],
        out_shape=[jax.ShapeDtypeStruct((S, D), F32), jax.ShapeDtypeStruct((S, D), BF16),
                   jax.ShapeDtypeStruct((1, D), F32), jax.ShapeDtypeStruct((8, LANES), F32)],
        compiler_params=_params(("arbitrary",)), name="loss_head")(h, gain, target)


def _window_counts(t, w):
    return jnp.minimum(t + 1, w).astype(F32)


def _pool_fwd(x, gain, pool_w, scale, *, ts=512):
    S, D = x.shape
    dg = D // N_GROUPS
    ts = _tile(S, ts)
    per = ts // POOL_HALO

    def body(x_ref, xh_ref, g_ref, w_ref, sc_ref, h_ref, diff_ref):
        i = pl.program_id(0)
        g = g_ref[...]

        def norm(v):
            return v * lax.rsqrt(jnp.mean(v * v, axis=-1, keepdims=True) + EPS) * g

        xc = x_ref[...]
        hn_c = norm(xc)
        hn_h = norm(xh_ref[...]) * (i > 0).astype(F32)
        ext = jnp.concatenate([hn_h, hn_c], axis=0)
        t = i * ts + lax.broadcasted_iota(jnp.int32, (ts, 1), 0)
        for gi, w in enumerate(POOL_WINDOWS):
            cols = slice(gi * dg, (gi + 1) * dg)
            s = ext[:, cols]
            step = 1
            while step < w:
                s = s + pltpu.roll(s, step, 0)
                step *= 2
            mean = s[POOL_HALO:] * (1.0 / _window_counts(t, w))
            diff = (mean - hn_c[:, cols]).astype(BF16)
            diff_ref[:, cols] = diff
            mixed = jnp.dot(diff, w_ref[gi], preferred_element_type=F32)
            h_ref[:, cols] = xc[:, cols] + mixed * sc_ref[:, cols]

    return pl.pallas_call(
        body, grid=(S // ts,),
        in_specs=[_row_spec(ts, D),
                  pl.BlockSpec((POOL_HALO, D), lambda i: (jnp.maximum(i * per - 1, 0), 0)),
                  _vec_spec(1, D), pl.BlockSpec((N_GROUPS, dg, dg), lambda i: (0, 0, 0)), _vec_spec(1, D)],
        out_specs=[_row_spec(ts, D), _row_spec(ts, D)],
        out_shape=[jax.ShapeDtypeStruct((S, D), F32), jax.ShapeDtypeStruct((S, D), BF16)],
        compiler_params=_params(("parallel",)), name="pool_fwd")(x, x, gain, pool_w, scale)


def _pool_bwd(x, dh, diff, gain, pool_w, scale, *, ts=256):
    S, D = x.shape
    dg = D // N_GROUPS
    ts = _tile(S, ts)
    per = ts // POOL_HALO
    n_tiles = S // ts
    n_halo = S // POOL_HALO
    ext_rows = ts + POOL_HALO

    def body(x_ref, dh_ref, dhn_ref, diff_ref, g_ref, w_ref, sc_ref, dx_ref, dw_ref, dsc_ref, dgain_ref):
        i = pl.program_id(0)
        xc = x_ref[...]
        g = g_ref[...]
        r = lax.rsqrt(jnp.mean(xc * xc, axis=-1, keepdims=True) + EPS)
        xhat = xc * r
        dh_c = dh_ref[...]
        dh_n = dhn_ref[...] * (i < n_tiles - 1).astype(F32)
        dh_ext = jnp.concatenate([dh_c, dh_n], axis=0)
        t_ext = i * ts + lax.broadcasted_iota(jnp.int32, (ext_rows, 1), 0)
        d_hn_parts, dsc_parts = [], []
        for gi, w in enumerate(POOL_WINDOWS):
            cols = slice(gi * dg, (gi + 1) * dg)
            wg = w_ref[gi]
            dmix = (dh_ext[:, cols] * sc_ref[:, cols]).astype(BF16)
            d_diff = lax.dot_general(dmix, wg, _DIMS["nt"], preferred_element_type=F32)
            diff_c = diff_ref[:, cols]
            dwg = lax.dot_general(diff_c, dmix[:ts], _DIMS["tn"], preferred_element_type=F32)
            mixed = jnp.dot(diff_c, wg, preferred_element_type=F32)
            dsc_parts.append(jnp.sum(dh_c[:, cols] * mixed, axis=0, keepdims=True))
            e = d_diff * (1.0 / _window_counts(t_ext, w))
            step = 1
            while step < w:
                e = e + pltpu.roll(e, ext_rows - step, 0)
                step *= 2
            d_hn_parts.append(e[:ts] - d_diff[:ts])

            @pl.when(i == 0)
            def _():
                dw_ref[gi] = dwg

            @pl.when(i > 0)
            def _():
                dw_ref[gi] += dwg

        d_hn = jnp.concatenate(d_hn_parts, axis=1)
        dsc = jnp.concatenate(dsc_parts, axis=1)
        dgain = jnp.sum(d_hn * xhat, axis=0, keepdims=True)
        dxhat = d_hn * g
        dx_ref[...] = dh_c + r * (dxhat - xhat * jnp.mean(dxhat * xhat, axis=-1, keepdims=True))

        @pl.when(i == 0)
        def _():
            dsc_ref[...] = dsc
            dgain_ref[...] = dgain

        @pl.when(i > 0)
        def _():
            dsc_ref[...] += dsc
            dgain_ref[...] += dgain

    return pl.pallas_call(
        body, grid=(n_tiles,),
        in_specs=[_row_spec(ts, D), _row_spec(ts, D),
                  pl.BlockSpec((POOL_HALO, D), lambda i: (jnp.minimum((i + 1) * per, n_halo - 1), 0)),
                  _row_spec(ts, D), _vec_spec(1, D),
                  pl.BlockSpec((N_GROUPS, dg, dg), lambda i: (0, 0, 0)), _vec_spec(1, D)],
        out_specs=[_row_spec(ts, D), pl.BlockSpec((N_GROUPS, dg, dg), lambda i: (0, 0, 0)),
                   _vec_spec(1, D), _vec_spec(1, D)],
        out_shape=[jax.ShapeDtypeStruct((S, D), F32), jax.ShapeDtypeStruct((N_GROUPS, dg, dg), F32),
                   jax.ShapeDtypeStruct((1, D), F32), jax.ShapeDtypeStruct((1, D), F32)],
        compiler_params=_params(("arbitrary",)), name="pool_bwd")(x, dh, dh, diff, gain, pool_w, scale)


def _gate_fwd(f_raw, b_pad, *, ts=512):
    S = f_raw.shape[0]
    ts = _tile(S, ts)

    def body(f_ref, b_ref, c_ref, carry_ref):
        i = pl.program_id(0)

        @pl.when(i == 0)
        def _():
            carry_ref[...] = jnp.zeros_like(carry_ref)

        z = f_ref[...] + b_ref[...]
        v = jnp.minimum(z, 0.0) - jnp.log1p(jnp.exp(-jnp.abs(z)))
        row = lax.broadcasted_iota(jnp.int32, (ts, 1), 0)
        step = 1
        while step < ts:
            v = v + jnp.where(row >= step, pltpu.roll(v, step, 0), 0.0)
            step *= 2
        out = v + carry_ref[0:1, :]
        c_ref[...] = out
        carry_ref[...] = jnp.broadcast_to(out[ts - 1:ts, :], carry_ref.shape)

    return pl.pallas_call(
        body, grid=(S // ts,),
        in_specs=[pl.BlockSpec((ts, LANES), lambda i: (i, 0)), pl.BlockSpec((1, LANES), lambda i: (0, 0))],
        out_specs=pl.BlockSpec((ts, LANES), lambda i: (i, 0)),
        out_shape=jax.ShapeDtypeStruct((S, LANES), F32),
        scratch_shapes=[pltpu.VMEM((8, LANES), F32)],
        compiler_params=_params(("arbitrary",)), name="gate_fwd")(f_raw, b_pad)


def _gate_bwd(dc, f_raw, b_pad, *, ts=512):
    S = f_raw.shape[0]
    ts = _tile(S, ts)
    n = S // ts

    def body(dc_ref, f_ref, b_ref, df_ref, db_ref, carry_ref):
        i = pl.program_id(0)

        @pl.when(i == 0)
        def _():
            carry_ref[...] = jnp.zeros_like(carry_ref)

        v = dc_ref[...]
        row = lax.broadcasted_iota(jnp.int32, (ts, 1), 0)
        step = 1
        while step < ts:
            v = v + jnp.where(row < ts - step, pltpu.roll(v, ts - step, 0), 0.0)
            step *= 2
        d_logf = v + carry_ref[0:1, :]
        carry_ref[...] = jnp.broadcast_to(d_logf[0:1, :], carry_ref.shape)
        z = f_ref[...] + b_ref[...]
        df = d_logf / (1.0 + jnp.exp(z))
        df_ref[...] = df
        db = jnp.sum(df, axis=0, keepdims=True)

        @pl.when(i == 0)
        def _():
            db_ref[...] = db

        @pl.when(i > 0)
        def _():
            db_ref[...] += db

    rev = lambda i: (n - 1 - i, 0)
    return pl.pallas_call(
        body, grid=(n,),
        in_specs=[pl.BlockSpec((ts, LANES), rev), pl.BlockSpec((ts, LANES), rev),
                  pl.BlockSpec((1, LANES), lambda i: (0, 0))],
        out_specs=[pl.BlockSpec((ts, LANES), rev), pl.BlockSpec((1, LANES), lambda i: (0, 0))],
        out_shape=[jax.ShapeDtypeStruct((S, LANES), F32), jax.ShapeDtypeStruct((1, LANES), F32)],
        scratch_shapes=[pltpu.VMEM((8, LANES), F32)],
        compiler_params=_params(("arbitrary",)), name="gate_bwd")(dc, f_raw, b_pad)


def _row_layout(a_hs, t):
    n_heads, S = a_hs.shape
    return a_hs.reshape(n_heads, S // t, 1, t)


def _lane_pick(blk, h):
    lane = lax.broadcasted_iota(jnp.int32, blk.shape, 1)
    return jnp.sum(jnp.where(lane == h, blk, 0.0), axis=-1, keepdims=True)


def _lane_put(ref, h, col, first=True):
    lane = lax.broadcasted_iota(jnp.int32, ref.shape, 1)
    if not first:
        ref[...] = jnp.where(lane == h, col, ref[...])
        return

    @pl.when(h == 0)
    def _():
        ref[...] = jnp.where(lane == 0, col, 0.0)

    @pl.when(h > 0)
    def _():
        ref[...] = jnp.where(lane == h, col, ref[...])


def _causal(s, masked, fill, rows_are_queries=True):
    if not masked:
        return s
    rr = lax.broadcasted_iota(jnp.int32, s.shape, 0)
    cc = lax.broadcasted_iota(jnp.int32, s.shape, 1)
    keep = (cc <= rr) if rows_are_queries else (rr <= cc)
    return jnp.where(keep, s, fill)


def _fox_fwd(q2, kv, c_row, *, t=512, exchange=None):
    S, D = q2.shape
    H = D // HEAD_DIM
    t = _tile(S, t)
    nb = S // t
    hp = HEADS_PER_STEP
    wide = hp * HEAD_DIM

    def body(q_ref, k_ref, v_ref, cr_ref, o_ref, l2_ref):
        i = pl.program_id(0)
        g = pl.program_id(1)
        cols = [slice(a * HEAD_DIM, (a + 1) * HEAD_DIM) for a in range(hp)]
        refs_i = [cr_ref[a, i][:, 0:1] for a in range(hp)]
        qs = [q_ref[:, cols[a]] for a in range(hp)]

        def step(j, carry, masked):
            r0 = pl.multiple_of(j * t, t)
            out = []
            for a in range(hp):
                m, l, acc = carry[3 * a:3 * a + 3]
                kb = k_ref[pl.ds(r0, t), cols[a]]
                vb = v_ref[pl.ds(r0, t), cols[a]]
                ck = cr_ref[a, j] - refs_i[a]
                s = lax.dot_general(qs[a], kb, _DIMS["nt"], preferred_element_type=F32) - ck
                s = _causal(s, masked, NEG_INF)
                m_new = jnp.maximum(m, jnp.max(s, axis=-1, keepdims=True))
                alpha = jnp.exp2(m - m_new)
                p = jnp.exp2(s - m_new)
                l = alpha * l + jnp.sum(p, axis=-1, keepdims=True)
                acc = alpha * acc + jnp.dot(p.astype(BF16), vb, preferred_element_type=F32)
                out += [m_new, l, acc]
            return tuple(out)

        init = (jnp.full((t, 1), NEG_INF, F32), jnp.zeros((t, 1), F32), jnp.zeros((t, HEAD_DIM), F32)) * hp
        carry = lax.fori_loop(0, i, lambda j, c: step(j, c, False), init)
        carry = step(i, carry, True)
        for a in range(hp):
            m, l, acc = carry[3 * a:3 * a + 3]
            o_ref[:, cols[a]] = (acc / l).astype(o_ref.dtype)
            _lane_put(l2_ref, g * hp + a, m + jnp.log2(l), first=(a == 0))

    (o, l2_sh), got = _call(
        body, name="fox_fwd", grid=(nb, H // hp),
        in_specs=[pl.BlockSpec((t, wide), lambda i, g: (i, g)),
                  pl.BlockSpec((S, wide), lambda i, g: (0, g)),
                  pl.BlockSpec((S, wide), lambda i, g: (0, H // hp + g)),
                  pl.BlockSpec((hp, nb, 1, t), lambda i, g: (g, 0, 0, 0))],
        out_specs=[pl.BlockSpec((t, wide), lambda i, g: (i, g)),
                   pl.BlockSpec((t, LANES), lambda i, g: (i, 0))],
        out_shape=[jax.ShapeDtypeStruct((S, D), BF16), jax.ShapeDtypeStruct((S, LANES), F32)],
        sem=("parallel", "arbitrary"), args=(q2, kv, kv, c_row), exchange=exchange)
    return o, l2_sh, got


def _o_proj_dx(dh, wo, o, *, tm=1024, tn=1024):
    S, D = dh.shape
    tm, tn = _tile(S, tm), _tile(D, tn)
    heads = tn // HEAD_DIM

    def epilogue(acc, ov):
        j = pl.program_id(1)
        lane = lax.broadcasted_iota(jnp.int32, (tm, LANES), 1)
        prod = acc * ov.astype(F32)
        blk = jnp.zeros((tm, LANES), F32)
        for hh in range(heads):
            col = jnp.sum(prod[:, hh * HEAD_DIM:(hh + 1) * HEAD_DIM], axis=-1, keepdims=True)
            blk = jnp.where(lane == j * heads + hh, col, blk)
        return acc, blk

    (do, delta_sh), _ = _matmul(
        "o_proj_dx", dh, wo, mode="nt", grid=(S // tm, D // tn, 1), tm=tm, tn=tn,
        a_spec=pl.BlockSpec((tm, D), lambda i, j, k: (i, 0)),
        b_spec=pl.BlockSpec((tn, D), lambda i, j, k: (j, 0)),
        outs=[((S, D), BF16, _ij(tm, tn)), ((S, LANES), F32, pl.BlockSpec((tm, LANES), lambda i, j, k: (i, 0)))],
        extras=(o,), extra_specs=(_ij(tm, tn),), epilogue=epilogue, acc_outs=(1,))
    return do, delta_sh


def _fox_bwd(q2, kv, do, c_sh, c_row, l2_row, delta_row, *, t=512, exchange=None):
    S, D = q2.shape
    H = D // HEAD_DIM
    t = _tile(S, t)
    nb = S // t
    scale = HEAD_DIM ** -0.5

    def body(q_ref, do_ref, k_ref, v_ref, csh_ref, cr_ref, l2_ref, dl_ref, dq_ref, dk_ref, dv_ref, dck_ref, dcq_ref,
             dq_acc):
        h = pl.program_id(0)
        j = pl.program_id(1)
        ck = _lane_pick(csh_ref[...], h)
        kb = k_ref[...]
        vb = v_ref[...]

        @pl.when(j == 0)
        def _():
            dq_acc[...] = jnp.zeros_like(dq_acc)
            dcq_ref[...] = jnp.zeros_like(dcq_ref)

        def step(i, carry, masked):
            dk, dv, dc = carry
            r0 = pl.multiple_of(i * t, t)
            qb = q_ref[pl.ds(r0, t), :]
            dob = do_ref[pl.ds(r0, t), :]
            s = lax.dot_general(kb, qb, _DIMS["nt"], preferred_element_type=F32) - (ck - cr_ref[i][:, 0:1])
            p = _causal(jnp.exp2(s - l2_ref[i]), masked, 0.0, rows_are_queries=False)
            dv = dv + jnp.dot(p.astype(BF16), dob, preferred_element_type=F32)
            dp = lax.dot_general(vb, dob, _DIMS["nt"], preferred_element_type=F32)
            ds = p * (dp - dl_ref[i])
            dc = dc - jnp.sum(ds, axis=-1, keepdims=True)
            dcq_ref[i] += jnp.sum(ds, axis=0, keepdims=True)
            dsb = ds.astype(BF16)
            dk = dk + jnp.dot(dsb, qb, preferred_element_type=F32)
            dq_acc[pl.ds(r0, t), :] += lax.dot_general(dsb, kb, _DIMS["tn"], preferred_element_type=F32)
            return dk, dv, dc

        init = (jnp.zeros((t, HEAD_DIM), F32), jnp.zeros((t, HEAD_DIM), F32), jnp.zeros((t, 1), F32))
        carry = step(j, init, True)
        dk, dv, dc = lax.fori_loop(j + 1, nb, lambda i, c: step(i, c, False), carry)
        dk_ref[...] = (dk * LN2).astype(dk_ref.dtype)
        dv_ref[...] = dv.astype(dv_ref.dtype)
        rows = pl.ds(pl.multiple_of(j * t, t), t)
        lane = lax.broadcasted_iota(jnp.int32, (t, LANES), 1)

        @pl.when(h == 0)
        def _():
            dck_ref[rows, :] = jnp.where(lane == 0, dc, 0.0)

        @pl.when(h > 0)
        def _():
            dck_ref[rows, :] = jnp.where(lane == h, dc, dck_ref[rows, :])

        @pl.when(j == nb - 1)
        def _():
            dq_ref[...] = (dq_acc[...] * scale).astype(dq_ref.dtype)

    kspec = pl.BlockSpec((t, HEAD_DIM), lambda h, j: (j, h))
    headspec = pl.BlockSpec((S, HEAD_DIM), lambda h, j: (0, h))
    rowspec = pl.BlockSpec((None, nb, 1, t), lambda h, j: (h, 0, 0, 0))
    (dq, dk, dv, dck_sh, dcq_row), got = _call(
        body, name="fox_bwd", grid=(H, nb),
        in_specs=[headspec, headspec, kspec,
                  pl.BlockSpec((t, HEAD_DIM), lambda h, j: (j, H + h)),
                  pl.BlockSpec((t, LANES), lambda h, j: (j, 0)),
                  rowspec, rowspec, rowspec],
        out_specs=[headspec, kspec, kspec, pl.BlockSpec((S, LANES), lambda h, j: (0, 0)), rowspec],
        out_shape=[jax.ShapeDtypeStruct((S, D), BF16), jax.ShapeDtypeStruct((S, D), BF16),
                   jax.ShapeDtypeStruct((S, D), BF16), jax.ShapeDtypeStruct((S, LANES), F32),
                   jax.ShapeDtypeStruct((H, nb, 1, t), F32)],
        scratch_shapes=[pltpu.VMEM((S, HEAD_DIM), F32)],
        sem=("arbitrary", "arbitrary"), args=(q2, do, kv, kv, c_sh, c_row, l2_row, delta_row), exchange=exchange)
    return dq, dk, dv, dck_sh, dcq_row, got


def _adamw(name, parts, w, m, v, layer=None, *, tr=128):
    P, R, C = parts.shape
    tr = _tile(R, tr)

    def body(p_ref, w_ref, m_ref, v_ref, g_ref, d_ref, nm_ref, nv_ref):
        g = p_ref[0].astype(F32)
        for k in range(1, P):
            g = g + p_ref[k].astype(F32)
        wv = w_ref[...]
        nm = ADAM_B1 * m_ref[...] + (1.0 - ADAM_B1) * g
        nv = ADAM_B2 * v_ref[...] + (1.0 - ADAM_B2) * (g * g)
        m_hat = nm / (1.0 - ADAM_B1 ** ADAM_STEP)
        v_hat = nv / (1.0 - ADAM_B2 ** ADAM_STEP)
        g_ref[...] = g
        d_ref[...] = -ADAM_LR * (m_hat / (jnp.sqrt(v_hat) + ADAM_EPS) + ADAM_WD * wv)
        nm_ref[...] = nm
        nv_ref[...] = nv

    if layer is None:
        wspec = pl.BlockSpec((tr, C), lambda i: (i, 0))
    else:
        wspec = pl.BlockSpec((None, tr, C), lambda i: (layer, i, 0))
    ospec = pl.BlockSpec((tr, C), lambda i: (i, 0))
    return pl.pallas_call(
        body, grid=(R // tr,),
        in_specs=[pl.BlockSpec((P, tr, C), lambda i: (0, i, 0)), wspec, wspec, wspec],
        out_specs=[ospec] * 4, out_shape=[jax.ShapeDtypeStruct((R, C), F32)] * 4,
        compiler_params=_params(("parallel",)), name=name)(parts, w, m, v)


def _rows(a):
    flat = a.reshape(-1)
    pad = (-flat.shape[0]) % LANES
    if pad:
        flat = jnp.pad(flat, (0, pad))
    return flat.reshape(-1, LANES)


def kernel(x, norm_mix, norm_mlp, pool_w, pool_scale, norm_kv, w_kvf, b_f, w_q, w_o, w_up, w_down, norm_out, loss_target, m_norm_mix, m_norm_mlp, m_pool_w, m_pool_scale, m_norm_kv, m_w_kvf, m_b_f, m_w_q, m_w_o, m_w_up, m_w_down, m_norm_out, v_norm_mix, v_norm_mlp, v_pool_w, v_pool_scale, v_norm_kv, v_w_kvf, v_b_f, v_w_q, v_w_o, v_w_up, v_w_down, v_norm_out):
    _, S, D = x.shape
    H = D // HEAD_DIM
    dg = D // N_GROUPS
    n_kvf = 2 * D + H
    kvf_b = w_kvf.shape[1]
    fb = w_up.shape[2]
    ps_b = pool_scale.shape[1]
    xi, yi, ci = _position()
    my_block = 4 * xi + 2 * yi + ci
    x2 = x.reshape(S, D)
    tgt = loss_target.reshape(S, D)
    b_pad = jnp.pad(b_f, (0, LANES - H)).reshape(1, LANES)

    g_pool, g_scale, g_up0 = _all_gather_two_level([pool_w.astype(BF16), pool_scale, w_up[0].astype(BF16)])
    pw = g_pool[:, 0].transpose(1, 0, 2, 3).reshape(N_GROUPS, dg, dg)
    scale_full = g_scale.reshape(1, D)
    ex_down0 = _Exchange("gather", [w_down[0].astype(BF16)])
    ex_kvf = _Exchange("gather", [w_kvf.astype(BF16)])
    ex_q = _Exchange("gather", [w_q[0].astype(BF16)])
    ex_late = _Exchange("gather", [w_o[0].astype(BF16), w_up[1].astype(BF16), w_down[1].astype(BF16)])

    h1, diff = _pool_fwd(x2, norm_mix[0:1], pw, scale_full)
    (hn_m0,) = _rms_fwd("rms_mlp0", h1, norm_mlp[0:1])
    u0, uu0, h2, (g_down0,), (g_kvf,) = _mlp_fwd("l0", hn_m0, h1, g_up0, None, ex_up=ex_down0, ex_down=ex_kvf)
    wkvf = g_kvf.transpose(1, 0, 2).reshape(D, n_kvf)
    w_kv = wkvf[:, :2 * D]
    w_f = jnp.pad(wkvf[:, 2 * D:], ((0, 0), (0, LANES - H)))

    gains_kv_q = jnp.stack([norm_kv, norm_mix[1]])
    hkv, hn_q = _rms_fwd("rms_kv_q", h2, gains_kv_q)
    kv, (g_q,) = _mm_nn("kv_proj", hkv, w_kv, out_dtype=BF16, exchange=ex_q)
    wq = g_q.reshape(D, D)
    f_raw = _mm_nn("f_proj", hkv, w_f, out_dtype=F32)
    c_sh = _gate_fwd(f_raw, b_pad)
    t_attn = _tile(S, 1024)
    c2_sh = c_sh * LOG2E
    c2_row = _row_layout(c2_sh[:, :H].T, t_attn)
    q2 = _mm_nn("q_proj", hn_q, wq, out_dtype=BF16, out_scale=HEAD_DIM ** -0.5 * LOG2E)
    o, l2_sh, (g_o, g_up1, g_down1) = _fox_fwd(q2, kv, c2_row, t=t_attn, exchange=ex_late)
    wo = g_o.reshape(D, D)
    h3 = _mm_nn("o_proj", o, wo, out_dtype=F32, residual=h2)
    (hn_m1,) = _rms_fwd("rms_mlp1", h3, norm_mlp[1:2])
    u1, uu1, h4, _, _ = _mlp_fwd("l1", hn_m1, h3, g_up1, g_down1)
    dh4, dh4_b, d_norm_out, loss_part = _loss_head(h4, norm_out.reshape(1, D), tgt)

    d_pre = _mlp_dpre("l1", dh4_b, u1, g_down1)
    dw_down1, _ = _mm_tn("mlp_dwdown_l1", uu1, dh4_b)
    dw_up1, _ = _mlp_dwup("l1", hn_m1, d_pre, fb)
    d_hn, _ = _mlp_dhn("l1", d_pre, g_up1)
    dh3, dh3_b, d_norm_mlp1 = _rms_bwd("rms_mlp1_bwd", h3, norm_mlp[1:2], [d_hn], dh4)

    do, delta_sh = _o_proj_dx(dh3_b, wo, o)
    dw_o, _ = _mm_tn("o_proj_dw", o, dh3_b)
    l2_row = _row_layout(l2_sh[:, :H].T, t_attn)
    delta_row = _row_layout(delta_sh[:, :H].T, t_attn)
    dq, dk, dv, dck_sh, dcq_row, (p_up1, p_down1, p_o) = _fox_bwd(
        q2, kv, do, c2_sh, c2_row, l2_row, delta_row, t=t_attn,
        exchange=_Exchange("scatter", [dw_up1, dw_down1.reshape(N_DEV, fb, D), dw_o.reshape(N_DEV, D // N_DEV, D)]))
    dw_q, _ = _mm_tn("q_proj_dw", hn_q, dq)
    d_hn_q = _mm_nt("q_proj_dx", dq, wq, out_dtype=F32)

    dcq_sh = jnp.pad(dcq_row.reshape(H, S).T, ((0, 0), (0, LANES - H)))
    d_f, d_b = _gate_bwd(dck_sh + dcq_sh, f_raw, b_pad)
    dw_k, _ = _mm_tn("k_proj_dw", hkv, dk)
    dw_v, _ = _mm_tn("v_proj_dw", hkv, dv)
    dw_f, _ = _mm_tn("f_proj_dw", hkv, d_f)
    d_hkv = _mm_nt("k_proj_dx", dk, w_kv, b_cols=(0, D), out_dtype=F32)
    d_hkv = _mm_nt("v_proj_dx", dv, w_kv, b_cols=(D, D), out_dtype=F32, residual=d_hkv)
    d_hkv = _mm_nt("f_proj_dx", d_f, w_f, out_dtype=F32, residual=d_hkv)
    dh2, dh2_b, d_norm_kv_q = _rms_bwd("rms_kv_q_bwd", h2, gains_kv_q, [d_hkv, d_hn_q], dh3)
    dw_kvf = jnp.concatenate([dw_k, dw_v, dw_f[:, :H]], axis=1).reshape(D, N_DEV, kvf_b).transpose(1, 0, 2)

    d_pre = _mlp_dpre("l0", dh2_b, u0, g_down0)
    dw_down0, (p_kvf, p_q) = _mm_tn(
        "mlp_dwdown_l0", uu0, dh2_b, exchange=_Exchange("scatter", [dw_kvf, dw_q.reshape(N_DEV, D // N_DEV, D)]))
    dw_up0, (p_down0,) = _mlp_dwup(
        "l0", hn_m0, d_pre, fb, exchange=_Exchange("scatter", [dw_down0.reshape(N_DEV, fb, D)]))
    d_hn, (p_up0,) = _mlp_dhn("l0", d_pre, g_up0, exchange=_Exchange("scatter", [dw_up0]))
    dh1, _, d_norm_mlp0 = _rms_bwd("rms_mlp0_bwd", h1, norm_mlp[0:1], [d_hn], dh2)
    grad_x, dw_pool, d_scale, d_norm_mix0 = _pool_bwd(x2, dh1, diff, norm_mix[0:1], pw, scale_full)
    (p_pool,) = _exchange_now("scatter_pool", _Exchange("scatter", [
        dw_pool.astype(BF16).reshape(N_GROUPS, N_DEV, dg // N_DEV, dg).transpose(1, 0, 2, 3).reshape(
            N_DEV, N_GROUPS * dg // N_DEV, dg)]))

    small = jnp.concatenate([
        _rows(jnp.concatenate([d_norm_mix0, d_norm_kv_q[1:2]], axis=0)),
        _rows(jnp.concatenate([d_norm_mlp0, d_norm_mlp1], axis=0)),
        _rows(d_norm_kv_q[0:1]),
        _rows(d_norm_out),
        _rows(d_scale),
        d_b,
        jnp.pad(loss_part[0:1, 0:1], ((0, 0), (0, LANES - 1))),
    ], axis=0)
    n_small = small.shape[0]
    small = jnp.pad(small, ((0, (-n_small) % 8), (0, 0)))
    total = _all_reduce_small(small)
    rd = D // LANES
    loss = total[7 * rd + 1, 0]
    g_scale_mine = lax.dynamic_slice(total[6 * rd:7 * rd].reshape(D), (my_block * ps_b,), (ps_b,))

    def pack(nm_, nl_, kv_, out_, bf_, ps_):
        return jnp.concatenate([_rows(nm_), _rows(nl_), _rows(kv_), _rows(out_), _rows(bf_), _rows(ps_)], axis=0)

    g_small = jnp.concatenate([total[:6 * rd], total[7 * rd:7 * rd + 1], _rows(g_scale_mine)], axis=0)
    w_small = pack(norm_mix, norm_mlp, norm_kv, norm_out, b_f, pool_scale)
    m_small = pack(m_norm_mix, m_norm_mlp, m_norm_kv, m_norm_out, m_b_f, m_pool_scale)
    v_small = pack(v_norm_mix, v_norm_mlp, v_norm_kv, v_norm_out, v_b_f, v_pool_scale)
    rs = g_small.shape[0]
    padr = (-rs) % 8
    pad8 = lambda a: jnp.pad(a, ((0, padr), (0, 0)))
    small_out = _adamw("adamw_small", pad8(g_small)[None], pad8(w_small), pad8(m_small), pad8(v_small), tr=rs + padr)

    def unpack(a):
        o0 = 0
        res = []
        for shape in [(2, D), (2, D), (D,), (D,)]:
            nr = (2 * rd) if len(shape) == 2 else rd
            res.append(a[o0:o0 + nr].reshape(shape))
            o0 += nr
        res.append(a[o0, :H])
        res.append(a[o0 + 1:o0 + 1 + ps_b // LANES].reshape(1, ps_b))
        return res

    small_res = [unpack(a) for a in small_out]

    r_pool = _adamw("adamw_pool", p_pool, pool_w.reshape(-1, dg), m_pool_w.reshape(-1, dg), v_pool_w.reshape(-1, dg))
    r_kvf = _adamw("adamw_kvf", p_kvf, w_kvf, m_w_kvf, v_w_kvf)
    r_q = _adamw("adamw_q", p_q, w_q[0], m_w_q[0], v_w_q[0])
    r_o = _adamw("adamw_o", p_o, w_o[0], m_w_o[0], v_w_o[0])
    r_up = [_adamw(f"adamw_up{l}", p, w_up, m_w_up, v_w_up, layer=l) for l, p in enumerate([p_up0, p_up1])]
    r_down = [_adamw(f"adamw_down{l}", p, w_down, m_w_down, v_w_down, layer=l) for l, p in enumerate([p_down0, p_down1])]

    def leaves(kind):
        sm = small_res[kind]
        return [
            sm[0], sm[1],
            r_pool[kind].reshape(pool_w.shape),
            sm[5],
            sm[2],
            r_kvf[kind],
            sm[4],
            r_q[kind][None], r_o[kind][None],
            jnp.stack([r_up[0][kind], r_up[1][kind]]),
            jnp.stack([r_down[0][kind], r_down[1][kind]]),
            sm[3],
        ]

    return (loss, grad_x.reshape(x.shape), *leaves(0), *leaves(1), *leaves(2), *leaves(3))
```

```python
import jax
import jax.numpy as jnp
from jax import lax
from jax.experimental import pallas as pl
from jax.experimental.pallas import tpu as pltpu

F32 = jnp.float32
BF16 = jnp.bfloat16
MESH = pl.DeviceIdType.MESH

N_DEV = 8
EPS = 1e-6
NEG_INF = -1e30
POOL_WINDOWS = (2, 4, 8, 16)
N_GROUPS = len(POOL_WINDOWS)
POOL_HALO = 16
HEAD_DIM = 128
LANES = 128
HEADS_PER_STEP = 2
LOG2E = 1.4426950408889634
LN2 = 0.6931471805599453

ADAM_LR = 0.001
ADAM_B1 = 0.9
ADAM_B2 = 0.999
ADAM_EPS = 1e-08
ADAM_WD = 0.01
ADAM_STEP = 10

VMEM_LIMIT = 56 * 1024 * 1024

_ANY = pl.BlockSpec(memory_space=pl.ANY)


def _tile(n, want):
    t = min(n, want)
    assert n % t == 0, (n, want)
    return t


def _params(sem, vmem=VMEM_LIMIT):
    return pltpu.CompilerParams(dimension_semantics=sem, vmem_limit_bytes=vmem)


def _position():
    return lax.axis_index("x"), lax.axis_index("y"), lax.axis_index("c")


class _Exchange:
    def __init__(self, kind, arrays):
        assert kind in ("gather", "scatter")
        self.kind = kind
        self.arrays = list(arrays)
        self.n = len(self.arrays)
        shapes = [(N_DEV, *a.shape) if kind == "gather" else a.shape for a in self.arrays]
        self.out_shape = [jax.ShapeDtypeStruct(s, a.dtype) for s, a in zip(shapes, self.arrays)]
        self.scratch = [pltpu.SemaphoreType.DMA((self.n, N_DEV)), pltpu.SemaphoreType.DMA((self.n, N_DEV))]

    def _copies(self, ins, outs, send_sems, recv_sems, with_recv=True):
        x, y, c = _position()
        me = 4 * x + 2 * y + c
        gather = self.kind == "gather"
        local, sends, recvs = [], [], []
        for a in range(self.n):
            local.append(pltpu.make_async_copy(ins[a] if gather else ins[a].at[me], outs[a].at[me], send_sems.at[a, 0]))
            for k in range(1, N_DEV):
                px, py, pc = x ^ (k >> 2), y ^ ((k >> 1) & 1), c ^ (k & 1)
                peer = 4 * px + 2 * py + pc
                src = ins[a] if gather else ins[a].at[peer]
                common = dict(send_sem=send_sems.at[a, k], recv_sem=recv_sems.at[a, k], device_id=(px, py, pc),
                              device_id_type=MESH)
                sends.append(pltpu.make_async_remote_copy(src_ref=src, dst_ref=outs[a].at[me], **common))
                if with_recv:
                    recvs.append(pltpu.make_async_remote_copy(src_ref=src, dst_ref=outs[a].at[peer], **common))
        return local, sends, recvs

    def start(self, ins, outs, send_sems, recv_sems):
        local, sends, _ = self._copies(ins, outs, send_sems, recv_sems, with_recv=False)
        for cp in local + sends:
            cp.start()

    def wait(self, ins, outs, send_sems, recv_sems):
        local, sends, recvs = self._copies(ins, outs, send_sems, recv_sems)
        for send, recv in zip(sends, recvs):
            send.wait_send()
            recv.wait_recv()
        for cp in local:
            cp.wait()


def _call(body, *, name, grid, in_specs, out_specs, out_shape, args, scratch_shapes=(), sem=None, exchange=None):
    if exchange is None:
        outs = pl.pallas_call(
            body, grid=grid, in_specs=in_specs, out_specs=out_specs, out_shape=out_shape,
            scratch_shapes=list(scratch_shapes), compiler_params=_params(sem), name=name)(*args)
        return list(outs), []

    n_in, n_out, n_scr, n = len(in_specs), len(out_specs), len(scratch_shapes), exchange.n

    def hosted(*refs):
        ins, refs = refs[:n_in], refs[n_in:]
        ex_in, refs = refs[:n], refs[n:]
        outs, refs = refs[:n_out], refs[n_out:]
        ex_out, refs = refs[:n], refs[n:]
        scratch, sems = refs[:n_scr], refs[n_scr:]
        first = _all_true([pl.program_id(d) == 0 for d in range(len(grid))])
        last = _all_true([pl.program_id(d) == grid[d] - 1 for d in range(len(grid))])

        @pl.when(first)
        def _():
            exchange.start(ex_in, ex_out, *sems)

        body(*ins, *outs, *scratch)

        @pl.when(last)
        def _():
            exchange.wait(ex_in, ex_out, *sems)

    res = pl.pallas_call(
        hosted, grid=grid, in_specs=[*in_specs, *[_ANY] * n], out_specs=[*out_specs, *[_ANY] * n],
        out_shape=[*out_shape, *exchange.out_shape], scratch_shapes=[*scratch_shapes, *exchange.scratch],
        compiler_params=_params(("arbitrary",) * len(grid)), name=name)(*args, *exchange.arrays)
    return list(res[:n_out]), list(res[n_out:])


def _all_true(preds):
    out = preds[0]
    for p in preds[1:]:
        out = jnp.logical_and(out, p)
    return out


def _exchange_now(name, exchange):
    def body(*refs):
        n = exchange.n
        exchange.start(refs[:n], refs[n:2 * n], *refs[2 * n:])
        exchange.wait(refs[:n], refs[n:2 * n], *refs[2 * n:])

    return pl.pallas_call(
        body, in_specs=[_ANY] * exchange.n, out_specs=[_ANY] * exchange.n, out_shape=exchange.out_shape,
        scratch_shapes=exchange.scratch, name=name)(*exchange.arrays)


def _all_gather_two_level(shards):
    n = len(shards)

    def body(*refs):
        ins, outs = refs[:n], refs[n:2 * n]
        send_sems, recv_sems, local_sems = refs[2 * n:]
        x, y, c = _position()
        me, sibling = (x, y, c), (x, y, 1 - c)
        chips = [(1 - x, y), (x, 1 - y), (1 - x, 1 - y)]

        def slot(a, px, py, pc):
            return outs[a].at[4 * px + 2 * py + pc]

        def copy(a, k, block, to, src=None):
            return pltpu.make_async_remote_copy(
                src_ref=slot(a, *block) if src is None else src, dst_ref=slot(a, *block),
                send_sem=send_sems.at[a, k], recv_sem=recv_sems.at[a, k], device_id=to, device_id_type=MESH)

        mine = [pltpu.make_async_copy(ins[a], slot(a, *me), local_sems.at[a]) for a in range(n)]
        first = []
        for a in range(n):
            mine[a].start()
            first.append(copy(a, 0, me, sibling, src=ins[a]))
            first += [copy(a, 1 + j, me, (*chip, c), src=ins[a]) for j, chip in enumerate(chips)]
        for cp in first:
            cp.start()
        passed = []
        for a in range(n):
            for j, chip in enumerate(chips):
                copy(a, 1 + j, (*chip, c), me).wait_recv()
                fwd = copy(a, 4 + j, (*chip, c), sibling)
                fwd.start()
                passed.append(fwd)
        for a in range(n):
            copy(a, 0, sibling, me).wait_recv()
            for j, chip in enumerate(chips):
                copy(a, 4 + j, (*chip, 1 - c), me).wait_recv()
        for cp in first + passed:
            cp.wait_send()
        for cp in mine:
            cp.wait()

    return pl.pallas_call(
        body, in_specs=[_ANY] * n, out_specs=[_ANY] * n,
        out_shape=[jax.ShapeDtypeStruct((N_DEV, *s.shape), s.dtype) for s in shards],
        scratch_shapes=[pltpu.SemaphoreType.DMA((n, 7)), pltpu.SemaphoreType.DMA((n, 7)),
                        pltpu.SemaphoreType.DMA((n,))],
        name="all_gather_first")(*shards)


def _all_reduce_small(vec):
    R = vec.shape[0]

    def body(v_ref, o_ref, buf_ref, send_sems, recv_sems):
        x, y, c = _position()
        me = 4 * x + 2 * y + c
        buf_ref[me] = v_ref[...]
        copies = []
        for k in range(1, N_DEV):
            peer = (x ^ (k >> 2), y ^ ((k >> 1) & 1), c ^ (k & 1))
            copies.append(pltpu.make_async_remote_copy(
                src_ref=buf_ref.at[me], dst_ref=buf_ref.at[me], send_sem=send_sems.at[k], recv_sem=recv_sems.at[k],
                device_id=peer, device_id_type=MESH))
        for cp in copies:
            cp.start()
        for cp in copies:
            cp.wait()
        total = buf_ref[0]
        for d in range(1, N_DEV):
            total = total + buf_ref[d]
        o_ref[...] = total

    vm = pl.BlockSpec(memory_space=pltpu.VMEM)
    return pl.pallas_call(
        body, in_specs=[vm], out_specs=vm, out_shape=jax.ShapeDtypeStruct((R, LANES), F32),
        scratch_shapes=[pltpu.VMEM((N_DEV, R, LANES), F32), pltpu.SemaphoreType.DMA((N_DEV,)),
                        pltpu.SemaphoreType.DMA((N_DEV,))],
        name="all_reduce_small")(vec)


_DIMS = {
    "nn": (((1,), (0,)), ((), ())),
    "nt": (((1,), (1,)), ((), ())),
    "tn": (((0,), (0,)), ((), ())),
}


def _matmul(name, a, b, *, mode, grid, tm, tn, a_spec, b_spec, outs, extras=(), extra_specs=(), epilogue=None,
            acc_outs=(), exchange=None):
    nk = grid[2]
    n_extra = len(extras)
    dn = _DIMS[mode]

    def body(a_ref, b_ref, *rest):
        extra_refs = rest[:n_extra]
        out_refs = rest[n_extra:-1]
        acc_ref = rest[-1]
        j = pl.program_id(1)
        k = pl.program_id(2)

        def product():
            b = b_ref[...]
            if b.ndim == 3:
                b = b.reshape(-1, b.shape[-1]) if mode == "nn" else jnp.concatenate([b[0], b[1]], axis=1)
            return lax.dot_general(a_ref[...].astype(BF16), b.astype(BF16), dn, preferred_element_type=F32)

        def finish(acc):
            vals = (acc,) if epilogue is None else epilogue(acc, *[r[...] for r in extra_refs])
            for idx, (o_ref, val) in enumerate(zip(out_refs, vals)):
                if idx in acc_outs:
                    @pl.when(j == 0)
                    def _():
                        o_ref[...] = val.astype(o_ref.dtype)

                    @pl.when(j > 0)
                    def _():
                        o_ref[...] += val.astype(o_ref.dtype)
                else:
                    o_ref[...] = val.astype(o_ref.dtype)

        if nk == 1:
            finish(product())
            return

        @pl.when(k == 0)
        def _():
            acc_ref[...] = product()

        @pl.when(k > 0)
        def _():
            acc_ref[...] += product()

        @pl.when(k == nk - 1)
        def _():
            finish(acc_ref[...])

    return _call(
        body, name=name, grid=grid, in_specs=[a_spec, b_spec, *extra_specs], out_specs=[o[2] for o in outs],
        out_shape=[jax.ShapeDtypeStruct(o[0], o[1]) for o in outs], scratch_shapes=[pltpu.VMEM((tm, tn), F32)],
        sem=("parallel", "arbitrary" if acc_outs else "parallel", "arbitrary"), args=(a, b, *extras), exchange=exchange)


def _ij(tm, tn):
    return pl.BlockSpec((tm, tn), lambda i, j, k: (i, j))


def _mm_nn(name, a, b, *, out_dtype, residual=None, out_scale=None, exchange=None, tm=1024, tn=1024, tk=2048):
    M, K = a.shape
    N = b.shape[1]
    tm, tn, tk = _tile(M, tm), _tile(N, tn), _tile(K, tk)
    extras, especs, epi = (), (), None
    if residual is not None:
        extras, especs = (residual,), (_ij(tm, tn),)
        epi = lambda acc, r: (acc + r,)
    elif out_scale is not None:
        epi = lambda acc: (acc * out_scale,)
    outs, got = _matmul(
        name, a, b, mode="nn", grid=(M // tm, N // tn, K // tk), tm=tm, tn=tn,
        a_spec=pl.BlockSpec((tm, tk), lambda i, j, k: (i, k)),
        b_spec=pl.BlockSpec((tk, tn), lambda i, j, k: (k, j)),
        outs=[((M, N), out_dtype, _ij(tm, tn))], extras=extras, extra_specs=especs, epilogue=epi, exchange=exchange)
    return outs[0] if exchange is None else (outs[0], got)


def _mm_nt(name, a, b, *, out_dtype, b_cols=None, residual=None, tm=1024, tn=1024, tk=2048):
    M, K = a.shape
    N = b.shape[0]
    c0 = 0 if b_cols is None else b_cols[0]
    tm, tn, tk = _tile(M, tm), _tile(N, tn), _tile(K, tk)
    assert c0 % tk == 0
    kb0 = c0 // tk
    extras, especs, epi = (), (), None
    if residual is not None:
        extras, especs = (residual,), (_ij(tm, tn),)
        epi = lambda acc, r: (acc + r,)
    return _matmul(
        name, a, b, mode="nt", grid=(M // tm, N // tn, K // tk), tm=tm, tn=tn,
        a_spec=pl.BlockSpec((tm, tk), lambda i, j, k: (i, k)),
        b_spec=pl.BlockSpec((tn, tk), lambda i, j, k: (j, kb0 + k)),
        outs=[((M, N), out_dtype, _ij(tm, tn))], extras=extras, extra_specs=especs, epilogue=epi)[0][0]


def _mm_tn(name, a, b, *, exchange=None, tm=1024, tn=1024, tk=2048):
    K, M = a.shape
    N = b.shape[1]
    tm, tn, tk = _tile(M, tm), _tile(N, tn), _tile(K, tk)
    outs, got = _matmul(
        name, a, b, mode="tn", grid=(M // tm, N // tn, K // tk), tm=tm, tn=tn,
        a_spec=pl.BlockSpec((tk, tm), lambda i, j, k: (k, i)),
        b_spec=pl.BlockSpec((tk, tn), lambda i, j, k: (k, j)),
        outs=[((M, N), BF16, _ij(tm, tn))], exchange=exchange)
    return outs[0], got


def _kvf_dx(dk, dv, d_f, w_kv, w_f, *, tm=1024, tn=1024):
    S, D = dk.shape
    tm, tn = _tile(S, tm), _tile(D, tn)

    def body(dk_ref, dv_ref, df_ref, wk_ref, wv_ref, wf_ref, o_ref):
        acc = lax.dot_general(dk_ref[...], wk_ref[...], _DIMS["nt"], preferred_element_type=F32)
        acc = acc + lax.dot_general(dv_ref[...], wv_ref[...], _DIMS["nt"], preferred_element_type=F32)
        o_ref[...] = acc + lax.dot_general(df_ref[...].astype(BF16), wf_ref[...], _DIMS["nt"],
                                           preferred_element_type=F32)

    row = lambda width: pl.BlockSpec((tm, width), lambda i, j: (i, 0))
    return pl.pallas_call(
        body, grid=(S // tm, D // tn),
        in_specs=[row(D), row(D), row(LANES),
                  pl.BlockSpec((tn, D), lambda i, j: (j, 0)), pl.BlockSpec((tn, D), lambda i, j: (j, 1)),
                  pl.BlockSpec((tn, LANES), lambda i, j: (j, 0))],
        out_specs=pl.BlockSpec((tm, tn), lambda i, j: (i, j)), out_shape=jax.ShapeDtypeStruct((S, D), F32),
        compiler_params=_params(("parallel", "parallel")), name="kvf_proj_dx")(dk, dv, d_f, w_kv, w_kv, w_f)


def _mlp_fwd(tag, hn, h, w_up_g, w_down_g, *, ex_up=None, ex_down=None, tm=1024, tk=2048):
    S, D = hn.shape
    fb = w_up_g.shape[2]
    F = N_DEV * fb
    tm, tku = _tile(S, tm), _tile(D, tk)

    def up_epi(acc):
        u = jnp.maximum(acc, 0.0)
        return u, u * u

    (u, uu), got_up = _matmul(
        f"mlp_up_{tag}", hn, w_up_g, mode="nn", grid=(S // tm, N_DEV, D // tku), tm=tm, tn=fb,
        a_spec=pl.BlockSpec((tm, tku), lambda i, j, k: (i, k)),
        b_spec=pl.BlockSpec((None, tku, fb), lambda i, j, k: (j, k, 0)),
        outs=[((S, F), BF16, _ij(tm, fb)), ((S, F), BF16, _ij(tm, fb))], epilogue=up_epi, exchange=ex_up)
    if w_down_g is None:
        w_down_g = got_up[0]

    tn = _tile(D, 1024)
    (h_out,), got_down = _matmul(
        f"mlp_down_{tag}", uu, w_down_g, mode="nn", grid=(S // tm, D // tn, N_DEV // 2), tm=tm, tn=tn,
        a_spec=pl.BlockSpec((tm, 2 * fb), lambda i, j, k: (i, k)),
        b_spec=pl.BlockSpec((2, fb, tn), lambda i, j, k: (k, 0, j)),
        outs=[((S, D), F32, _ij(tm, tn))], extras=(h,), extra_specs=(_ij(tm, tn),),
        epilogue=lambda acc, res: (acc + res,), exchange=ex_down)
    return u, uu, h_out, got_up, got_down


def _mlp_dpre(tag, dh, u, w_down_g, *, tm=1024, tk=2048):
    S, D = dh.shape
    fb = w_down_g.shape[1]
    tm, tkd = _tile(S, tm), _tile(D, tk)
    return _matmul(
        f"mlp_dpre_{tag}", dh, w_down_g, mode="nt", grid=(S // tm, N_DEV, D // tkd), tm=tm, tn=fb,
        a_spec=pl.BlockSpec((tm, tkd), lambda i, j, k: (i, k)),
        b_spec=pl.BlockSpec((None, fb, tkd), lambda i, j, k: (j, 0, k)),
        outs=[((S, N_DEV * fb), BF16, _ij(tm, fb))], extras=(u,), extra_specs=(_ij(tm, fb),),
        epilogue=lambda acc, uv: (acc * (2.0 * uv.astype(F32)),))[0][0]


def _mlp_dwup(tag, hn, d_pre, fb, *, exchange=None):
    S, D = hn.shape
    tmu = _tile(D, 1024)
    tks = _tile(S, 2048)
    outs, got = _matmul(
        f"mlp_dwup_{tag}", hn, d_pre, mode="tn", grid=(D // tmu, N_DEV, S // tks), tm=tmu, tn=fb,
        a_spec=pl.BlockSpec((tks, tmu), lambda i, j, k: (k, i)),
        b_spec=pl.BlockSpec((tks, fb), lambda i, j, k: (k, j)),
        outs=[((N_DEV, D, fb), BF16, pl.BlockSpec((None, tmu, fb), lambda i, j, k: (j, i, 0)))], exchange=exchange)
    return outs[0], got


def _mlp_dhn(tag, d_pre, w_up_g, *, exchange=None, tm=1024):
    S, F = d_pre.shape
    D, fb = w_up_g.shape[1], w_up_g.shape[2]
    tm, tn = _tile(S, tm), _tile(D, 1024)
    outs, got = _matmul(
        f"mlp_dhn_{tag}", d_pre, w_up_g, mode="nt", grid=(S // tm, D // tn, N_DEV // 2), tm=tm, tn=tn,
        a_spec=pl.BlockSpec((tm, 2 * fb), lambda i, j, k: (i, k)),
        b_spec=pl.BlockSpec((2, tn, fb), lambda i, j, k: (k, j, 0)),
        outs=[((S, D), F32, _ij(tm, tn))], exchange=exchange)
    return outs[0], got


def _row_spec(ts, D):
    return pl.BlockSpec((ts, D), lambda i: (i, 0))


def _vec_spec(n, D):
    return pl.BlockSpec((n, D), lambda i: (0, 0))


def _rms_fwd(name, x, gains, *, ts=512):
    S, D = x.shape
    n = gains.shape[0]
    ts = _tile(S, ts)

    def body(x_ref, g_ref, *o_refs):
        xv = x_ref[...]
        y = xv * lax.rsqrt(jnp.mean(xv * xv, axis=-1, keepdims=True) + EPS)
        for i, o_ref in enumerate(o_refs):
            o_ref[...] = (y * g_ref[i:i + 1, :]).astype(o_ref.dtype)

    return pl.pallas_call(
        body, grid=(S // ts,), in_specs=[_row_spec(ts, D), _vec_spec(n, D)],
        out_specs=[_row_spec(ts, D)] * n, out_shape=[jax.ShapeDtypeStruct((S, D), BF16)] * n,
        compiler_params=_params(("parallel",)), name=name)(x, gains)


def _rms_bwd(name, x, gains, dys, res, *, ts=256):
    S, D = x.shape
    n = gains.shape[0]
    ts = _tile(S, ts)

    def body(x_ref, g_ref, *rest):
        dy_refs = rest[:n]
        res_ref, dx_ref, dxb_ref, dg_ref = rest[n:]
        i = pl.program_id(0)
        xv = x_ref[...]
        r = lax.rsqrt(jnp.mean(xv * xv, axis=-1, keepdims=True) + EPS)
        xhat = xv * r
        dxhat = None
        dgs = []
        for k in range(n):
            dy = dy_refs[k][...].astype(F32)
            dgs.append(jnp.sum(dy * xhat, axis=0, keepdims=True))
            term = dy * g_ref[k:k + 1, :]
            dxhat = term if dxhat is None else dxhat + term
        dx = res_ref[...] + r * (dxhat - xhat * jnp.mean(dxhat * xhat, axis=-1, keepdims=True))
        dx_ref[...] = dx
        dxb_ref[...] = dx.astype(BF16)
        dg = jnp.concatenate(dgs, axis=0) if n > 1 else dgs[0]

        @pl.when(i == 0)
        def _():
            dg_ref[...] = dg

        @pl.when(i > 0)
        def _():
            dg_ref[...] += dg

    return pl.pallas_call(
        body, grid=(S // ts,),
        in_specs=[_row_spec(ts, D), _vec_spec(n, D)] + [_row_spec(ts, D)] * (n + 1),
        out_specs=[_row_spec(ts, D), _row_spec(ts, D), _vec_spec(n, D)],
        out_shape=[jax.ShapeDtypeStruct((S, D), F32), jax.ShapeDtypeStruct((S, D), BF16),
                   jax.ShapeDtypeStruct((n, D), F32)],
        compiler_params=_params(("arbitrary",)), name=name)(x, gains, *dys, res)


def _loss_head(h, gain, target, *, ts=512):
    S, D = h.shape
    ts = _tile(S, ts)

    def body(x_ref, g_ref, t_ref, dx_ref, dxb_ref, dg_ref, loss_ref):
        i = pl.program_id(0)
        xv = x_ref[...]
        g = g_ref[...]
        r = lax.rsqrt(jnp.mean(xv * xv, axis=-1, keepdims=True) + EPS)
        xhat = xv * r
        err = xhat * g - t_ref[...]
        part = 0.5 * jnp.sum(jnp.mean(err * err, axis=-1, keepdims=True), axis=0, keepdims=True)
        dy = err * (1.0 / D)
        dg = jnp.sum(dy * xhat, axis=0, keepdims=True)
        dxhat = dy * g
        dx = r * (dxhat - xhat * jnp.mean(dxhat * xhat, axis=-1, keepdims=True))
        dx_ref[...] = dx
        dxb_ref[...] = dx.astype(BF16)

        @pl.when(i == 0)
        def _():
            dg_ref[...] = dg
            loss_ref[...] = jnp.broadcast_to(part, loss_ref.shape)

        @pl.when(i > 0)
        def _():
            dg_ref[...] += dg
            loss_ref[...] += jnp.broadcast_to(part, loss_ref.shape)

    return pl.pallas_call(
        body, grid=(S // ts,),
        in_specs=[_row_spec(ts, D), _vec_spec(1, D), _row_spec(ts, D)],
        out_specs=[_row_spec(ts, D), _row_spec(ts, D), _vec_spec(1, D), pl.BlockSpec((8, LANES), lambda i: (0, 0))],
        out_shape=[jax.ShapeDtypeStruct((S, D), F32), jax.ShapeDtypeStruct((S, D), BF16),
                   jax.ShapeDtypeStruct((1, D), F32), jax.ShapeDtypeStruct((8, LANES), F32)],
        compiler_params=_params(("arbitrary",)), name="loss_head")(h, gain, target)


def _window_counts(t, w):
    return jnp.minimum(t + 1, w).astype(F32)


def _pool_fwd(x, gain, pool_w, scale, gain_next, *, ts=256):
    S, D = x.shape
    dg = D // N_GROUPS
    ts = _tile(S, ts)
    per = ts // POOL_HALO

    def body(x_ref, xh_ref, g_ref, w_ref, sc_ref, gn_ref, h_ref, diff_ref, hn_ref):
        i = pl.program_id(0)
        g = g_ref[...]
        h_parts = []

        def norm(v):
            return v * lax.rsqrt(jnp.mean(v * v, axis=-1, keepdims=True) + EPS) * g

        xc = x_ref[...]
        hn_c = norm(xc)
        hn_h = norm(xh_ref[...]) * (i > 0).astype(F32)
        ext = jnp.concatenate([hn_h, hn_c], axis=0)
        t = i * ts + lax.broadcasted_iota(jnp.int32, (ts, 1), 0)
        for gi, w in enumerate(POOL_WINDOWS):
            cols = slice(gi * dg, (gi + 1) * dg)
            s = ext[:, cols]
            step = 1
            while step < w:
                s = s + pltpu.roll(s, step, 0)
                step *= 2
            mean = s[POOL_HALO:] * (1.0 / _window_counts(t, w))
            diff = (mean - hn_c[:, cols]).astype(BF16)
            diff_ref[:, cols] = diff
            mixed = jnp.dot(diff, w_ref[gi], preferred_element_type=F32)
            h_parts.append(xc[:, cols] + mixed * sc_ref[:, cols])
        h = jnp.concatenate(h_parts, axis=1)
        h_ref[...] = h
        hn_ref[...] = (h * lax.rsqrt(jnp.mean(h * h, axis=-1, keepdims=True) + EPS) * gn_ref[...]).astype(BF16)

    return pl.pallas_call(
        body, grid=(S // ts,),
        in_specs=[_row_spec(ts, D),
                  pl.BlockSpec((POOL_HALO, D), lambda i: (jnp.maximum(i * per - 1, 0), 0)),
                  _vec_spec(1, D), pl.BlockSpec((N_GROUPS, dg, dg), lambda i: (0, 0, 0)), _vec_spec(1, D),
                  _vec_spec(1, D)],
        out_specs=[_row_spec(ts, D), _row_spec(ts, D), _row_spec(ts, D)],
        out_shape=[jax.ShapeDtypeStruct((S, D), F32), jax.ShapeDtypeStruct((S, D), BF16),
                   jax.ShapeDtypeStruct((S, D), BF16)],
        compiler_params=_params(("parallel",)), name="pool_fwd")(x, x, gain, pool_w, scale, gain_next)


def _pool_bwd(x, dh, diff, gain, pool_w, scale, *, ts=256):
    S, D = x.shape
    dg = D // N_GROUPS
    ts = _tile(S, ts)
    per = ts // POOL_HALO
    n_tiles = S // ts
    n_halo = S // POOL_HALO
    ext_rows = ts + POOL_HALO

    def body(x_ref, dh_ref, dhn_ref, diff_ref, g_ref, w_ref, sc_ref, dx_ref, dw_ref, dsc_ref, dgain_ref):
        i = pl.program_id(0)
        xc = x_ref[...]
        g = g_ref[...]
        r = lax.rsqrt(jnp.mean(xc * xc, axis=-1, keepdims=True) + EPS)
        xhat = xc * r
        dh_c = dh_ref[...]
        dh_n = dhn_ref[...] * (i < n_tiles - 1).astype(F32)
        dh_ext = jnp.concatenate([dh_c, dh_n], axis=0)
        t_ext = i * ts + lax.broadcasted_iota(jnp.int32, (ext_rows, 1), 0)
        d_hn_parts, dsc_parts = [], []
        for gi, w in enumerate(POOL_WINDOWS):
            cols = slice(gi * dg, (gi + 1) * dg)
            wg = w_ref[gi]
            dmix = (dh_ext[:, cols] * sc_ref[:, cols]).astype(BF16)
            d_diff = lax.dot_general(dmix, wg, _DIMS["nt"], preferred_element_type=F32)
            diff_c = diff_ref[:, cols]
            dwg = lax.dot_general(diff_c, dmix[:ts], _DIMS["tn"], preferred_element_type=F32)
            mixed = jnp.dot(diff_c, wg, preferred_element_type=F32)
            dsc_parts.append(jnp.sum(dh_c[:, cols] * mixed, axis=0, keepdims=True))
            e = d_diff * (1.0 / _window_counts(t_ext, w))
            step = 1
            while step < w:
                e = e + pltpu.roll(e, ext_rows - step, 0)
                step *= 2
            d_hn_parts.append(e[:ts] - d_diff[:ts])

            @pl.when(i == 0)
            def _():
                dw_ref[gi] = dwg

            @pl.when(i > 0)
            def _():
                dw_ref[gi] += dwg

        d_hn = jnp.concatenate(d_hn_parts, axis=1)
        dsc = jnp.concatenate(dsc_parts, axis=1)
        dgain = jnp.sum(d_hn * xhat, axis=0, keepdims=True)
        dxhat = d_hn * g
        dx_ref[...] = dh_c + r * (dxhat - xhat * jnp.mean(dxhat * xhat, axis=-1, keepdims=True))

        @pl.when(i == 0)
        def _():
            dsc_ref[...] = dsc
            dgain_ref[...] = dgain

        @pl.when(i > 0)
        def _():
            dsc_ref[...] += dsc
            dgain_ref[...] += dgain

    return pl.pallas_call(
        body, grid=(n_tiles,),
        in_specs=[_row_spec(ts, D), _row_spec(ts, D),
                  pl.BlockSpec((POOL_HALO, D), lambda i: (jnp.minimum((i + 1) * per, n_halo - 1), 0)),
                  _row_spec(ts, D), _vec_spec(1, D),
                  pl.BlockSpec((N_GROUPS, dg, dg), lambda i: (0, 0, 0)), _vec_spec(1, D)],
        out_specs=[_row_spec(ts, D), pl.BlockSpec((N_GROUPS, dg, dg), lambda i: (0, 0, 0)),
                   _vec_spec(1, D), _vec_spec(1, D)],
        out_shape=[jax.ShapeDtypeStruct((S, D), F32), jax.ShapeDtypeStruct((N_GROUPS, dg, dg), F32),
                   jax.ShapeDtypeStruct((1, D), F32), jax.ShapeDtypeStruct((1, D), F32)],
        compiler_params=_params(("arbitrary",)), name="pool_bwd")(x, dh, dh, diff, gain, pool_w, scale)


def _gate_fwd(f_raw, b_pad, *, ts=512):
    S = f_raw.shape[0]
    ts = _tile(S, ts)

    def body(f_ref, b_ref, c_ref, carry_ref):
        i = pl.program_id(0)

        @pl.when(i == 0)
        def _():
            carry_ref[...] = jnp.zeros_like(carry_ref)

        z = f_ref[...] + b_ref[...]
        v = jnp.minimum(z, 0.0) - jnp.log1p(jnp.exp(-jnp.abs(z)))
        row = lax.broadcasted_iota(jnp.int32, (ts, 1), 0)
        step = 1
        while step < ts:
            v = v + jnp.where(row >= step, pltpu.roll(v, step, 0), 0.0)
            step *= 2
        out = v + carry_ref[0:1, :]
        c_ref[...] = out
        carry_ref[...] = jnp.broadcast_to(out[ts - 1:ts, :], carry_ref.shape)

    return pl.pallas_call(
        body, grid=(S // ts,),
        in_specs=[pl.BlockSpec((ts, LANES), lambda i: (i, 0)), pl.BlockSpec((1, LANES), lambda i: (0, 0))],
        out_specs=pl.BlockSpec((ts, LANES), lambda i: (i, 0)),
        out_shape=jax.ShapeDtypeStruct((S, LANES), F32),
        scratch_shapes=[pltpu.VMEM((8, LANES), F32)],
        compiler_params=_params(("arbitrary",)), name="gate_fwd")(f_raw, b_pad)


def _gate_bwd(dc, f_raw, b_pad, *, ts=512):
    S = f_raw.shape[0]
    ts = _tile(S, ts)
    n = S // ts

    def body(dc_ref, f_ref, b_ref, df_ref, db_ref, carry_ref):
        i = pl.program_id(0)

        @pl.when(i == 0)
        def _():
            carry_ref[...] = jnp.zeros_like(carry_ref)

        v = dc_ref[...]
        row = lax.broadcasted_iota(jnp.int32, (ts, 1), 0)
        step = 1
        while step < ts:
            v = v + jnp.where(row < ts - step, pltpu.roll(v, ts - step, 0), 0.0)
            step *= 2
        d_logf = v + carry_ref[0:1, :]
        carry_ref[...] = jnp.broadcast_to(d_logf[0:1, :], carry_ref.shape)
        z = f_ref[...] + b_ref[...]
        df = d_logf / (1.0 + jnp.exp(z))
        df_ref[...] = df
        db = jnp.sum(df, axis=0, keepdims=True)

        @pl.when(i == 0)
        def _():
            db_ref[...] = db

        @pl.when(i > 0)
        def _():
            db_ref[...] += db

    rev = lambda i: (n - 1 - i, 0)
    return pl.pallas_call(
        body, grid=(n,),
        in_specs=[pl.BlockSpec((ts, LANES), rev), pl.BlockSpec((ts, LANES), rev),
                  pl.BlockSpec((1, LANES), lambda i: (0, 0))],
        out_specs=[pl.BlockSpec((ts, LANES), rev), pl.BlockSpec((1, LANES), lambda i: (0, 0))],
        out_shape=[jax.ShapeDtypeStruct((S, LANES), F32), jax.ShapeDtypeStruct((1, LANES), F32)],
        scratch_shapes=[pltpu.VMEM((8, LANES), F32)],
        compiler_params=_params(("arbitrary",)), name="gate_bwd")(dc, f_raw, b_pad)


def _row_layout(a_hs, t):
    n_heads, S = a_hs.shape
    return a_hs.reshape(n_heads, S // t, 1, t)


def _lane_pick(blk, h):
    lane = lax.broadcasted_iota(jnp.int32, blk.shape, 1)
    return jnp.sum(jnp.where(lane == h, blk, 0.0), axis=-1, keepdims=True)


def _lane_put(ref, h, col, first=True):
    lane = lax.broadcasted_iota(jnp.int32, ref.shape, 1)
    if not first:
        ref[...] = jnp.where(lane == h, col, ref[...])
        return

    @pl.when(h == 0)
    def _():
        ref[...] = jnp.where(lane == 0, col, 0.0)

    @pl.when(h > 0)
    def _():
        ref[...] = jnp.where(lane == h, col, ref[...])


def _causal(s, masked, fill, rows_are_queries=True):
    if not masked:
        return s
    rr = lax.broadcasted_iota(jnp.int32, s.shape, 0)
    cc = lax.broadcasted_iota(jnp.int32, s.shape, 1)
    keep = (cc <= rr) if rows_are_queries else (rr <= cc)
    return jnp.where(keep, s, fill)


def _fox_fwd(q2, kv, c_row, *, t=512, exchange=None):
    S, D = q2.shape
    H = D // HEAD_DIM
    t = _tile(S, t)
    nb = S // t
    hp = HEADS_PER_STEP
    wide = hp * HEAD_DIM

    def body(q_ref, k_ref, v_ref, cr_ref, o_ref, l2_ref):
        i = pl.program_id(0)
        g = pl.program_id(1)
        cols = [slice(a * HEAD_DIM, (a + 1) * HEAD_DIM) for a in range(hp)]
        refs_i = [cr_ref[a, i][:, 0:1] for a in range(hp)]
        qs = [q_ref[:, cols[a]] for a in range(hp)]

        def step(j, carry, masked):
            r0 = pl.multiple_of(j * t, t)
            out = []
            for a in range(hp):
                m, l, acc = carry[3 * a:3 * a + 3]
                kb = k_ref[pl.ds(r0, t), cols[a]]
                vb = v_ref[pl.ds(r0, t), cols[a]]
                ck = cr_ref[a, j] - refs_i[a]
                s = lax.dot_general(qs[a], kb, _DIMS["nt"], preferred_element_type=F32) - ck
                s = _causal(s, masked, NEG_INF)
                m_new = jnp.maximum(m, jnp.max(s, axis=-1, keepdims=True))
                alpha = jnp.exp2(m - m_new)
                p = jnp.exp2(s - m_new)
                l = alpha * l + jnp.sum(p, axis=-1, keepdims=True)
                acc = alpha * acc + jnp.dot(p.astype(BF16), vb, preferred_element_type=F32)
                out += [m_new, l, acc]
            return tuple(out)

        init = (jnp.full((t, 1), NEG_INF, F32), jnp.zeros((t, 1), F32), jnp.zeros((t, HEAD_DIM), F32)) * hp
        carry = lax.fori_loop(0, i, lambda j, c: step(j, c, False), init)
        carry = step(i, carry, True)
        for a in range(hp):
            m, l, acc = carry[3 * a:3 * a + 3]
            o_ref[:, cols[a]] = (acc / l).astype(o_ref.dtype)
            _lane_put(l2_ref, g * hp + a, m + jnp.log2(l), first=(a == 0))

    (o, l2_sh), got = _call(
        body, name="fox_fwd", grid=(nb, H // hp),
        in_specs=[pl.BlockSpec((t, wide), lambda i, g: (i, g)),
                  pl.BlockSpec((S, wide), lambda i, g: (0, g)),
                  pl.BlockSpec((S, wide), lambda i, g: (0, H // hp + g)),
                  pl.BlockSpec((hp, nb, 1, t), lambda i, g: (g, 0, 0, 0))],
        out_specs=[pl.BlockSpec((t, wide), lambda i, g: (i, g)),
                   pl.BlockSpec((t, LANES), lambda i, g: (i, 0))],
        out_shape=[jax.ShapeDtypeStruct((S, D), BF16), jax.ShapeDtypeStruct((S, LANES), F32)],
        sem=("parallel", "arbitrary"), args=(q2, kv, kv, c_row), exchange=exchange)
    return o, l2_sh, got


def _o_proj_dx(dh, wo, o, *, tm=1024, tn=1024):
    S, D = dh.shape
    tm, tn = _tile(S, tm), _tile(D, tn)
    heads = tn // HEAD_DIM

    def epilogue(acc, ov):
        j = pl.program_id(1)
        lane = lax.broadcasted_iota(jnp.int32, (tm, LANES), 1)
        prod = acc * ov.astype(F32)
        blk = jnp.zeros((tm, LANES), F32)
        for hh in range(heads):
            col = jnp.sum(prod[:, hh * HEAD_DIM:(hh + 1) * HEAD_DIM], axis=-1, keepdims=True)
            blk = jnp.where(lane == j * heads + hh, col, blk)
        return acc, blk

    (do, delta_sh), _ = _matmul(
        "o_proj_dx", dh, wo, mode="nt", grid=(S // tm, D // tn, 1), tm=tm, tn=tn,
        a_spec=pl.BlockSpec((tm, D), lambda i, j, k: (i, 0)),
        b_spec=pl.BlockSpec((tn, D), lambda i, j, k: (j, 0)),
        outs=[((S, D), BF16, _ij(tm, tn)), ((S, LANES), F32, pl.BlockSpec((tm, LANES), lambda i, j, k: (i, 0)))],
        extras=(o,), extra_specs=(_ij(tm, tn),), epilogue=epilogue, acc_outs=(1,))
    return do, delta_sh


def _fox_bwd(q2, kv, do, c_sh, c_row, l2_row, delta_row, *, t=512, exchange=None):
    S, D = q2.shape
    H = D // HEAD_DIM
    t = _tile(S, t)
    nb = S // t
    scale = HEAD_DIM ** -0.5

    def body(q_ref, do_ref, k_ref, v_ref, csh_ref, cr_ref, l2_ref, dl_ref, dq_ref, dk_ref, dv_ref, dck_ref, dcq_ref,
             dq_acc):
        h = pl.program_id(0)
        j = pl.program_id(1)
        ck = _lane_pick(csh_ref[...], h)
        kb = k_ref[...]
        vb = v_ref[...]

        @pl.when(j == 0)
        def _():
            dq_acc[...] = jnp.zeros_like(dq_acc)
            dcq_ref[...] = jnp.zeros_like(dcq_ref)

        def step(i, carry, masked):
            dk, dv, dc = carry
            r0 = pl.multiple_of(i * t, t)
            qb = q_ref[pl.ds(r0, t), :]
            dob = do_ref[pl.ds(r0, t), :]
            s = lax.dot_general(kb, qb, _DIMS["nt"], preferred_element_type=F32) - (ck - cr_ref[i][:, 0:1])
            p = _causal(jnp.exp2(s - l2_ref[i]), masked, 0.0, rows_are_queries=False)
            dv = dv + jnp.dot(p.astype(BF16), dob, preferred_element_type=F32)
            dp = lax.dot_general(vb, dob, _DIMS["nt"], preferred_element_type=F32)
            ds = p * (dp - dl_ref[i])
            dc = dc - jnp.sum(ds, axis=-1, keepdims=True)
            dcq_ref[i] += jnp.sum(ds, axis=0, keepdims=True)
            dsb = ds.astype(BF16)
            dk = dk + jnp.dot(dsb, qb, preferred_element_type=F32)
            dq_acc[pl.ds(r0, t), :] += lax.dot_general(dsb, kb, _DIMS["tn"], preferred_element_type=F32)
            return dk, dv, dc

        init = (jnp.zeros((t, HEAD_DIM), F32), jnp.zeros((t, HEAD_DIM), F32), jnp.zeros((t, 1), F32))
        carry = step(j, init, True)
        dk, dv, dc = lax.fori_loop(j + 1, nb, lambda i, c: step(i, c, False), carry)
        dk_ref[...] = (dk * LN2).astype(dk_ref.dtype)
        dv_ref[...] = dv.astype(dv_ref.dtype)
        rows = pl.ds(pl.multiple_of(j * t, t), t)
        lane = lax.broadcasted_iota(jnp.int32, (t, LANES), 1)

        @pl.when(h == 0)
        def _():
            dck_ref[rows, :] = jnp.where(lane == 0, dc, 0.0)

        @pl.when(h > 0)
        def _():
            dck_ref[rows, :] = jnp.where(lane == h, dc, dck_ref[rows, :])

        @pl.when(j == nb - 1)
        def _():
            dq_ref[...] = (dq_acc[...] * scale).astype(dq_ref.dtype)

    kspec = pl.BlockSpec((t, HEAD_DIM), lambda h, j: (j, h))
    headspec = pl.BlockSpec((S, HEAD_DIM), lambda h, j: (0, h))
    rowspec = pl.BlockSpec((None, nb, 1, t), lambda h, j: (h, 0, 0, 0))
    (dq, dk, dv, dck_sh, dcq_row), got = _call(
        body, name="fox_bwd", grid=(H, nb),
        in_specs=[headspec, headspec, kspec,
                  pl.BlockSpec((t, HEAD_DIM), lambda h, j: (j, H + h)),
                  pl.BlockSpec((t, LANES), lambda h, j: (j, 0)),
                  rowspec, rowspec, rowspec],
        out_specs=[headspec, kspec, kspec, pl.BlockSpec((S, LANES), lambda h, j: (0, 0)), rowspec],
        out_shape=[jax.ShapeDtypeStruct((S, D), BF16), jax.ShapeDtypeStruct((S, D), BF16),
                   jax.ShapeDtypeStruct((S, D), BF16), jax.ShapeDtypeStruct((S, LANES), F32),
                   jax.ShapeDtypeStruct((H, nb, 1, t), F32)],
        scratch_shapes=[pltpu.VMEM((S, HEAD_DIM), F32)],
        sem=("arbitrary", "arbitrary"), args=(q2, do, kv, kv, c_sh, c_row, l2_row, delta_row), exchange=exchange)
    return dq, dk, dv, dck_sh, dcq_row, got


def _adamw(name, parts, w, m, v, layer=None, into=None, *, tr=128):
    P, R, C = parts.shape
    tr = _tile(R, tr)

    def body(p_ref, w_ref, m_ref, v_ref, *rest):
        g_ref, d_ref, nm_ref, nv_ref = rest[-4:]
        g = p_ref[0].astype(F32)
        for k in range(1, P):
            g = g + p_ref[k].astype(F32)
        wv = w_ref[...]
        nm = ADAM_B1 * m_ref[...] + (1.0 - ADAM_B1) * g
        nv = ADAM_B2 * v_ref[...] + (1.0 - ADAM_B2) * (g * g)
        m_hat = nm / (1.0 - ADAM_B1 ** ADAM_STEP)
        v_hat = nv / (1.0 - ADAM_B2 ** ADAM_STEP)
        g_ref[...] = g
        d_ref[...] = -ADAM_LR * (m_hat / (jnp.sqrt(v_hat) + ADAM_EPS) + ADAM_WD * wv)
        nm_ref[...] = nm
        nv_ref[...] = nv

    pspec = pl.BlockSpec((P, tr, C), lambda i: (0, i, 0))
    if layer is None:
        wspec = pl.BlockSpec((tr, C), lambda i: (i, 0))
        return pl.pallas_call(
            body, grid=(R // tr,), in_specs=[pspec, wspec, wspec, wspec],
            out_specs=[wspec] * 4, out_shape=[jax.ShapeDtypeStruct((R, C), F32)] * 4,
            compiler_params=_params(("parallel",)), name=name)(parts, w, m, v)
    wspec = pl.BlockSpec((None, tr, C), lambda i: (layer, i, 0))
    prev = [] if into is None else list(into)
    return pl.pallas_call(
        body, grid=(R // tr,), in_specs=[pspec, wspec, wspec, wspec] + [_ANY] * len(prev),
        out_specs=[wspec] * 4, out_shape=[jax.ShapeDtypeStruct(w.shape, F32)] * 4,
        input_output_aliases={4 + k: k for k in range(len(prev))},
        compiler_params=_params(("parallel",)), name=name)(parts, w, m, v, *prev)


def _rows(a):
    flat = a.reshape(-1)
    pad = (-flat.shape[0]) % LANES
    if pad:
        flat = jnp.pad(flat, (0, pad))
    return flat.reshape(-1, LANES)


def kernel(x, norm_mix, norm_mlp, pool_w, pool_scale, norm_kv, w_kvf, b_f, w_q, w_o, w_up, w_down, norm_out, loss_target, m_norm_mix, m_norm_mlp, m_pool_w, m_pool_scale, m_norm_kv, m_w_kvf, m_b_f, m_w_q, m_w_o, m_w_up, m_w_down, m_norm_out, v_norm_mix, v_norm_mlp, v_pool_w, v_pool_scale, v_norm_kv, v_w_kvf, v_b_f, v_w_q, v_w_o, v_w_up, v_w_down, v_norm_out):
    _, S, D = x.shape
    H = D // HEAD_DIM
    dg = D // N_GROUPS
    n_kvf = 2 * D + H
    kvf_b = w_kvf.shape[1]
    fb = w_up.shape[2]
    ps_b = pool_scale.shape[1]
    xi, yi, ci = _position()
    my_block = 4 * xi + 2 * yi + ci
    x2 = x.reshape(S, D)
    tgt = loss_target.reshape(S, D)
    b_pad = jnp.pad(b_f, (0, LANES - H)).reshape(1, LANES)

    g_pool, g_scale, g_up0 = _all_gather_two_level([pool_w.astype(BF16), pool_scale, w_up[0].astype(BF16)])
    pw = g_pool[:, 0].transpose(1, 0, 2, 3).reshape(N_GROUPS, dg, dg)
    scale_full = g_scale.reshape(1, D)
    ex_down0 = _Exchange("gather", [w_down[0].astype(BF16)])
    ex_kvf = _Exchange("gather", [w_kvf.astype(BF16)])
    ex_q = _Exchange("gather", [w_q[0].astype(BF16)])
    ex_late = _Exchange("gather", [w_o[0].astype(BF16), w_up[1].astype(BF16), w_down[1].astype(BF16)])

    h1, diff, hn_m0 = _pool_fwd(x2, norm_mix[0:1], pw, scale_full, norm_mlp[0:1])
    u0, uu0, h2, (g_down0,), (g_kvf,) = _mlp_fwd("l0", hn_m0, h1, g_up0, None, ex_up=ex_down0, ex_down=ex_kvf)
    wkvf = g_kvf.transpose(1, 0, 2).reshape(D, n_kvf)
    w_kv = wkvf[:, :2 * D]
    w_f = jnp.pad(wkvf[:, 2 * D:], ((0, 0), (0, LANES - H)))

    gains_kv_q = jnp.stack([norm_kv, norm_mix[1]])
    hkv, hn_q = _rms_fwd("rms_kv_q", h2, gains_kv_q)
    kv, (g_q,) = _mm_nn("kv_proj", hkv, w_kv, out_dtype=BF16, exchange=ex_q)
    wq = g_q.reshape(D, D)
    f_raw = _mm_nn("f_proj", hkv, w_f, out_dtype=F32)
    c_sh = _gate_fwd(f_raw, b_pad)
    t_attn = _tile(S, 1024)
    c2_sh = c_sh * LOG2E
    c2_row = _row_layout(c2_sh[:, :H].T, t_attn)
    q2 = _mm_nn("q_proj", hn_q, wq, out_dtype=BF16, out_scale=HEAD_DIM ** -0.5 * LOG2E)
    o, l2_sh, (g_o, g_up1, g_down1) = _fox_fwd(q2, kv, c2_row, t=t_attn, exchange=ex_late)
    wo = g_o.reshape(D, D)
    h3 = _mm_nn("o_proj", o, wo, out_dtype=F32, residual=h2)
    (hn_m1,) = _rms_fwd("rms_mlp1", h3, norm_mlp[1:2])
    u1, uu1, h4, _, _ = _mlp_fwd("l1", hn_m1, h3, g_up1, g_down1)
    dh4, dh4_b, d_norm_out, loss_part = _loss_head(h4, norm_out.reshape(1, D), tgt)

    d_pre = _mlp_dpre("l1", dh4_b, u1, g_down1)
    dw_down1, _ = _mm_tn("mlp_dwdown_l1", uu1, dh4_b)
    dw_up1, _ = _mlp_dwup("l1", hn_m1, d_pre, fb)
    d_hn, _ = _mlp_dhn("l1", d_pre, g_up1)
    dh3, dh3_b, d_norm_mlp1 = _rms_bwd("rms_mlp1_bwd", h3, norm_mlp[1:2], [d_hn], dh4)

    do, delta_sh = _o_proj_dx(dh3_b, wo, o)
    dw_o, _ = _mm_tn("o_proj_dw", o, dh3_b)
    l2_row = _row_layout(l2_sh[:, :H].T, t_attn)
    delta_row = _row_layout(delta_sh[:, :H].T, t_attn)
    dq, dk, dv, dck_sh, dcq_row, (p_up1, p_down1, p_o) = _fox_bwd(
        q2, kv, do, c2_sh, c2_row, l2_row, delta_row, t=t_attn,
        exchange=_Exchange("scatter", [dw_up1, dw_down1.reshape(N_DEV, fb, D), dw_o.reshape(N_DEV, D // N_DEV, D)]))
    dw_q, _ = _mm_tn("q_proj_dw", hn_q, dq)
    d_hn_q = _mm_nt("q_proj_dx", dq, wq, out_dtype=F32)

    dcq_sh = jnp.pad(dcq_row.reshape(H, S).T, ((0, 0), (0, LANES - H)))
    d_f, d_b = _gate_bwd(dck_sh + dcq_sh, f_raw, b_pad)
    dw_k, _ = _mm_tn("k_proj_dw", hkv, dk)
    dw_v, _ = _mm_tn("v_proj_dw", hkv, dv)
    dw_f, _ = _mm_tn("f_proj_dw", hkv, d_f)
    d_hkv = _kvf_dx(dk, dv, d_f, w_kv, w_f)
    dh2, dh2_b, d_norm_kv_q = _rms_bwd("rms_kv_q_bwd", h2, gains_kv_q, [d_hkv, d_hn_q], dh3)
    dw_kvf = jnp.concatenate([dw_k, dw_v, dw_f[:, :H]], axis=1).reshape(D, N_DEV, kvf_b).transpose(1, 0, 2)

    d_pre = _mlp_dpre("l0", dh2_b, u0, g_down0)
    dw_down0, (p_kvf, p_q) = _mm_tn(
        "mlp_dwdown_l0", uu0, dh2_b, exchange=_Exchange("scatter", [dw_kvf, dw_q.reshape(N_DEV, D // N_DEV, D)]))
    dw_up0, (p_down0,) = _mlp_dwup(
        "l0", hn_m0, d_pre, fb, exchange=_Exchange("scatter", [dw_down0.reshape(N_DEV, fb, D)]))
    d_hn, (p_up0,) = _mlp_dhn("l0", d_pre, g_up0, exchange=_Exchange("scatter", [dw_up0]))
    dh1, _, d_norm_mlp0 = _rms_bwd("rms_mlp0_bwd", h1, norm_mlp[0:1], [d_hn], dh2)
    grad_x, dw_pool, d_scale, d_norm_mix0 = _pool_bwd(x2, dh1, diff, norm_mix[0:1], pw, scale_full)
    (p_pool,) = _exchange_now("scatter_pool", _Exchange("scatter", [
        dw_pool.astype(BF16).reshape(N_GROUPS, N_DEV, dg // N_DEV, dg).transpose(1, 0, 2, 3).reshape(
            N_DEV, N_GROUPS * dg // N_DEV, dg)]))

    small = jnp.concatenate([
        _rows(jnp.concatenate([d_norm_mix0, d_norm_kv_q[1:2]], axis=0)),
        _rows(jnp.concatenate([d_norm_mlp0, d_norm_mlp1], axis=0)),
        _rows(d_norm_kv_q[0:1]),
        _rows(d_norm_out),
        _rows(d_scale),
        d_b,
        jnp.pad(loss_part[0:1, 0:1], ((0, 0), (0, LANES - 1))),
    ], axis=0)
    n_small = small.shape[0]
    small = jnp.pad(small, ((0, (-n_small) % 8), (0, 0)))
    total = _all_reduce_small(small)
    rd = D // LANES
    loss = total[7 * rd + 1, 0]
    g_scale_mine = lax.dynamic_slice(total[6 * rd:7 * rd].reshape(D), (my_block * ps_b,), (ps_b,))

    def pack(nm_, nl_, kv_, out_, bf_, ps_):
        return jnp.concatenate([_rows(nm_), _rows(nl_), _rows(kv_), _rows(out_), _rows(bf_), _rows(ps_)], axis=0)

    g_small = jnp.concatenate([total[:6 * rd], total[7 * rd:7 * rd + 1], _rows(g_scale_mine)], axis=0)
    w_small = pack(norm_mix, norm_mlp, norm_kv, norm_out, b_f, pool_scale)
    m_small = pack(m_norm_mix, m_norm_mlp, m_norm_kv, m_norm_out, m_b_f, m_pool_scale)
    v_small = pack(v_norm_mix, v_norm_mlp, v_norm_kv, v_norm_out, v_b_f, v_pool_scale)
    rs = g_small.shape[0]
    padr = (-rs) % 8
    pad8 = lambda a: jnp.pad(a, ((0, padr), (0, 0)))
    small_out = _adamw("adamw_small", pad8(g_small)[None], pad8(w_small), pad8(m_small), pad8(v_small), tr=rs + padr)

    def unpack(a):
        o0 = 0
        res = []
        for shape in [(2, D), (2, D), (D,), (D,)]:
            nr = (2 * rd) if len(shape) == 2 else rd
            res.append(a[o0:o0 + nr].reshape(shape))
            o0 += nr
        res.append(a[o0, :H])
        res.append(a[o0 + 1:o0 + 1 + ps_b // LANES].reshape(1, ps_b))
        return res

    small_res = [unpack(a) for a in small_out]

    r_pool = _adamw("adamw_pool", p_pool, pool_w.reshape(-1, dg), m_pool_w.reshape(-1, dg), v_pool_w.reshape(-1, dg))
    r_kvf = _adamw("adamw_kvf", p_kvf, w_kvf, m_w_kvf, v_w_kvf)
    r_q = _adamw("adamw_q", p_q, w_q[0], m_w_q[0], v_w_q[0])
    r_o = _adamw("adamw_o", p_o, w_o[0], m_w_o[0], v_w_o[0])
    r_up = _adamw("adamw_up1", p_up1, w_up, m_w_up, v_w_up, layer=1)
    r_up = _adamw("adamw_up0", p_up0, w_up, m_w_up, v_w_up, layer=0, into=r_up)
    r_down = _adamw("adamw_down1", p_down1, w_down, m_w_down, v_w_down, layer=1)
    r_down = _adamw("adamw_down0", p_down0, w_down, m_w_down, v_w_down, layer=0, into=r_down)

    def leaves(kind):
        sm = small_res[kind]
        return [
            sm[0], sm[1],
            r_pool[kind].reshape(pool_w.shape),
            sm[5],
            sm[2],
            r_kvf[kind],
            sm[4],
            r_q[kind][None], r_o[kind][None],
            r_up[kind],
            r_down[kind],
            sm[3],
        ]

    return (loss, grad_x.reshape(x.shape), *leaves(0), *leaves(1), *leaves(2), *leaves(3))
```

```python
import jax
import jax.numpy as jnp
from jax import lax
from jax.experimental import pallas as pl
from jax.experimental.pallas import tpu as pltpu

F32 = jnp.float32
BF16 = jnp.bfloat16
MESH = pl.DeviceIdType.MESH

N_DEV = 8
EPS = 1e-6
NEG_INF = -1e30
POOL_WINDOWS = (2, 4, 8, 16)
N_GROUPS = len(POOL_WINDOWS)
POOL_HALO = 16
HEAD_DIM = 128
LANES = 128
HEADS_PER_STEP = 2
LOG2E = 1.4426950408889634
LN2 = 0.6931471805599453

ADAM_LR = 0.001
ADAM_B1 = 0.9
ADAM_B2 = 0.999
ADAM_EPS = 1e-08
ADAM_WD = 0.01
ADAM_STEP = 10

VMEM_LIMIT = 56 * 1024 * 1024

_ANY = pl.BlockSpec(memory_space=pl.ANY)


def _tile(n, want):
    t = min(n, want)
    assert n % t == 0, (n, want)
    return t


def _params(sem, vmem=VMEM_LIMIT):
    return pltpu.CompilerParams(dimension_semantics=sem, vmem_limit_bytes=vmem)


def _position():
    return lax.axis_index("x"), lax.axis_index("y"), lax.axis_index("c")


class _Exchange:
    def __init__(self, kind, arrays):
        assert kind in ("gather", "scatter")
        self.kind = kind
        self.arrays = list(arrays)
        self.n = len(self.arrays)
        shapes = [(N_DEV, *a.shape) if kind == "gather" else a.shape for a in self.arrays]
        self.out_shape = [jax.ShapeDtypeStruct(s, a.dtype) for s, a in zip(shapes, self.arrays)]
        self.scratch = [pltpu.SemaphoreType.DMA((self.n, N_DEV)), pltpu.SemaphoreType.DMA((self.n, N_DEV))]

    def _copies(self, ins, outs, send_sems, recv_sems, with_recv=True):
        x, y, c = _position()
        me = 4 * x + 2 * y + c
        gather = self.kind == "gather"
        local, sends, recvs = [], [], []
        for a in range(self.n):
            local.append(pltpu.make_async_copy(ins[a] if gather else ins[a].at[me], outs[a].at[me], send_sems.at[a, 0]))
            for k in range(1, N_DEV):
                px, py, pc = x ^ (k >> 2), y ^ ((k >> 1) & 1), c ^ (k & 1)
                peer = 4 * px + 2 * py + pc
                src = ins[a] if gather else ins[a].at[peer]
                common = dict(send_sem=send_sems.at[a, k], recv_sem=recv_sems.at[a, k], device_id=(px, py, pc),
                              device_id_type=MESH)
                sends.append(pltpu.make_async_remote_copy(src_ref=src, dst_ref=outs[a].at[me], **common))
                if with_recv:
                    recvs.append(pltpu.make_async_remote_copy(src_ref=src, dst_ref=outs[a].at[peer], **common))
        return local, sends, recvs

    def start(self, ins, outs, send_sems, recv_sems):
        local, sends, _ = self._copies(ins, outs, send_sems, recv_sems, with_recv=False)
        for cp in local + sends:
            cp.start()

    def wait(self, ins, outs, send_sems, recv_sems):
        local, sends, recvs = self._copies(ins, outs, send_sems, recv_sems)
        for send, recv in zip(sends, recvs):
            send.wait_send()
            recv.wait_recv()
        for cp in local:
            cp.wait()


class _GatherByChip:
    middle_at = 0.6

    def __init__(self, arrays):
        self.arrays = list(arrays)
        self.n = len(self.arrays)
        self.out_shape = [jax.ShapeDtypeStruct((N_DEV, *a.shape), a.dtype) for a in self.arrays]
        self.scratch = [pltpu.SemaphoreType.DMA((self.n, N_DEV)), pltpu.SemaphoreType.DMA((self.n, N_DEV))]

    def _copy(self, a, k, outs, send_sems, recv_sems, src, slot_of, to):
        x, y, c = _position()
        dev = lambda kk: (x ^ (kk >> 2), y ^ ((kk >> 1) & 1), c ^ (kk & 1))
        px, py, pc = dev(slot_of)
        slot = outs[a].at[4 * px + 2 * py + pc]
        return pltpu.make_async_remote_copy(
            src_ref=slot if src is None else src, dst_ref=slot, send_sem=send_sems.at[a, k], recv_sem=recv_sems.at[a, k],
            device_id=dev(to), device_id_type=MESH)

    def start(self, ins, outs, send_sems, recv_sems):
        x, y, c = _position()
        for a in range(self.n):
            pltpu.make_async_copy(ins[a], outs[a].at[4 * x + 2 * y + c], send_sems.at[a, 0]).start()
            for k in (1, 2, 4, 6):
                self._copy(a, k, outs, send_sems, recv_sems, ins[a], 0, k).start()

    def middle(self, ins, outs, send_sems, recv_sems):
        for a in range(self.n):
            for k in (2, 4, 6):
                self._copy(a, k, outs, send_sems, recv_sems, None, k, 0).wait_recv()
                self._copy(a, k + 1, outs, send_sems, recv_sems, None, k, 1).start()

    def wait(self, ins, outs, send_sems, recv_sems):
        x, y, c = _position()
        for a in range(self.n):
            self._copy(a, 1, outs, send_sems, recv_sems, None, 1, 0).wait_recv()
            for k in (3, 5, 7):
                self._copy(a, k, outs, send_sems, recv_sems, None, k, 0).wait_recv()
            for k in (1, 2, 4, 6):
                self._copy(a, k, outs, send_sems, recv_sems, ins[a], 0, k).wait_send()
            for k in (2, 4, 6):
                self._copy(a, k + 1, outs, send_sems, recv_sems, None, k, 1).wait_send()
            pltpu.make_async_copy(ins[a], outs[a].at[4 * x + 2 * y + c], send_sems.at[a, 0]).wait()


def _call(body, *, name, grid, in_specs, out_specs, out_shape, args, scratch_shapes=(), sem=None, exchange=None):
    if exchange is None:
        outs = pl.pallas_call(
            body, grid=grid, in_specs=in_specs, out_specs=out_specs, out_shape=out_shape,
            scratch_shapes=list(scratch_shapes), compiler_params=_params(sem), name=name)(*args)
        return list(outs), []

    n_in, n_out, n_scr, n = len(in_specs), len(out_specs), len(scratch_shapes), exchange.n

    def hosted(*refs):
        ins, refs = refs[:n_in], refs[n_in:]
        ex_in, refs = refs[:n], refs[n:]
        outs, refs = refs[:n_out], refs[n_out:]
        ex_out, refs = refs[:n], refs[n:]
        scratch, sems = refs[:n_scr], refs[n_scr:]
        first = _all_true([pl.program_id(d) == 0 for d in range(len(grid))])
        last = _all_true([pl.program_id(d) == grid[d] - 1 for d in range(len(grid))])

        @pl.when(first)
        def _():
            exchange.start(ex_in, ex_out, *sems)

        if hasattr(exchange, "middle"):
            assert len(grid) and any(g > 1 for g in grid), "a host of a three-moment exchange needs more than one grid step"
            step, total = 0, 1
            for d in range(len(grid)):
                step = step * grid[d] + pl.program_id(d)
                total *= grid[d]

            @pl.when(step == max(1, min(total - 1, int(total * exchange.middle_at))))
            def _():
                exchange.middle(ex_in, ex_out, *sems)

        body(*ins, *outs, *scratch)

        @pl.when(last)
        def _():
            exchange.wait(ex_in, ex_out, *sems)

    res = pl.pallas_call(
        hosted, grid=grid, in_specs=[*in_specs, *[_ANY] * n], out_specs=[*out_specs, *[_ANY] * n],
        out_shape=[*out_shape, *exchange.out_shape], scratch_shapes=[*scratch_shapes, *exchange.scratch],
        compiler_params=_params(("arbitrary",) * len(grid)), name=name)(*args, *exchange.arrays)
    return list(res[:n_out]), list(res[n_out:])


def _all_true(preds):
    out = preds[0]
    for p in preds[1:]:
        out = jnp.logical_and(out, p)
    return out


def _exchange_now(name, exchange):
    def body(*refs):
        n = exchange.n
        exchange.start(refs[:n], refs[n:2 * n], *refs[2 * n:])
        exchange.wait(refs[:n], refs[n:2 * n], *refs[2 * n:])

    return pl.pallas_call(
        body, in_specs=[_ANY] * exchange.n, out_specs=[_ANY] * exchange.n, out_shape=exchange.out_shape,
        scratch_shapes=exchange.scratch, name=name)(*exchange.arrays)


def _all_gather_two_level(shards):
    n = len(shards)

    def body(*refs):
        ins, outs = refs[:n], refs[n:2 * n]
        send_sems, recv_sems, local_sems = refs[2 * n:]
        x, y, c = _position()
        me, sibling = (x, y, c), (x, y, 1 - c)
        chips = [(1 - x, y), (x, 1 - y), (1 - x, 1 - y)]

        def slot(a, px, py, pc):
            return outs[a].at[4 * px + 2 * py + pc]

        def copy(a, k, block, to, src=None):
            return pltpu.make_async_remote_copy(
                src_ref=slot(a, *block) if src is None else src, dst_ref=slot(a, *block),
                send_sem=send_sems.at[a, k], recv_sem=recv_sems.at[a, k], device_id=to, device_id_type=MESH)

        mine = [pltpu.make_async_copy(ins[a], slot(a, *me), local_sems.at[a]) for a in range(n)]
        first = []
        for a in range(n):
            mine[a].start()
            first.append(copy(a, 0, me, sibling, src=ins[a]))
            first += [copy(a, 1 + j, me, (*chip, c), src=ins[a]) for j, chip in enumerate(chips)]
        for cp in first:
            cp.start()
        passed = []
        for a in range(n):
            for j, chip in enumerate(chips):
                copy(a, 1 + j, (*chip, c), me).wait_recv()
                fwd = copy(a, 4 + j, (*chip, c), sibling)
                fwd.start()
                passed.append(fwd)
        for a in range(n):
            copy(a, 0, sibling, me).wait_recv()
            for j, chip in enumerate(chips):
                copy(a, 4 + j, (*chip, 1 - c), me).wait_recv()
        for cp in first + passed:
            cp.wait_send()
        for cp in mine:
            cp.wait()

    return pl.pallas_call(
        body, in_specs=[_ANY] * n, out_specs=[_ANY] * n,
        out_shape=[jax.ShapeDtypeStruct((N_DEV, *s.shape), s.dtype) for s in shards],
        scratch_shapes=[pltpu.SemaphoreType.DMA((n, 7)), pltpu.SemaphoreType.DMA((n, 7)),
                        pltpu.SemaphoreType.DMA((n,))],
        name="all_gather_first")(*shards)


def _all_reduce_small(vec):
    R = vec.shape[0]

    def body(v_ref, o_ref, buf_ref, send_sems, recv_sems):
        x, y, c = _position()
        me = 4 * x + 2 * y + c
        buf_ref[me] = v_ref[...]
        copies = []
        for k in range(1, N_DEV):
            peer = (x ^ (k >> 2), y ^ ((k >> 1) & 1), c ^ (k & 1))
            copies.append(pltpu.make_async_remote_copy(
                src_ref=buf_ref.at[me], dst_ref=buf_ref.at[me], send_sem=send_sems.at[k], recv_sem=recv_sems.at[k],
                device_id=peer, device_id_type=MESH))
        for cp in copies:
            cp.start()
        for cp in copies:
            cp.wait()
        total = buf_ref[0]
        for d in range(1, N_DEV):
            total = total + buf_ref[d]
        o_ref[...] = total

    vm = pl.BlockSpec(memory_space=pltpu.VMEM)
    return pl.pallas_call(
        body, in_specs=[vm], out_specs=vm, out_shape=jax.ShapeDtypeStruct((R, LANES), F32),
        scratch_shapes=[pltpu.VMEM((N_DEV, R, LANES), F32), pltpu.SemaphoreType.DMA((N_DEV,)),
                        pltpu.SemaphoreType.DMA((N_DEV,))],
        name="all_reduce_small")(vec)


_DIMS = {
    "nn": (((1,), (0,)), ((), ())),
    "nt": (((1,), (1,)), ((), ())),
    "tn": (((0,), (0,)), ((), ())),
}


def _matmul(name, a, b, *, mode, grid, tm, tn, a_spec, b_spec, outs, extras=(), extra_specs=(), epilogue=None,
            acc_outs=(), exchange=None):
    nk = grid[2]
    n_extra = len(extras)
    dn = _DIMS[mode]

    def body(a_ref, b_ref, *rest):
        extra_refs = rest[:n_extra]
        out_refs = rest[n_extra:-1]
        acc_ref = rest[-1]
        j = pl.program_id(1)
        k = pl.program_id(2)

        def product():
            b = b_ref[...]
            if b.ndim == 3:
                b = b.reshape(-1, b.shape[-1]) if mode == "nn" else jnp.concatenate([b[0], b[1]], axis=1)
            return lax.dot_general(a_ref[...].astype(BF16), b.astype(BF16), dn, preferred_element_type=F32)

        def finish(acc):
            vals = (acc,) if epilogue is None else epilogue(acc, *[r[...] for r in extra_refs])
            for idx, (o_ref, val) in enumerate(zip(out_refs, vals)):
                if idx in acc_outs:
                    @pl.when(j == 0)
                    def _():
                        o_ref[...] = val.astype(o_ref.dtype)

                    @pl.when(j > 0)
                    def _():
                        o_ref[...] += val.astype(o_ref.dtype)
                else:
                    o_ref[...] = val.astype(o_ref.dtype)

        if nk == 1:
            finish(product())
            return

        @pl.when(k == 0)
        def _():
            acc_ref[...] = product()

        @pl.when(k > 0)
        def _():
            acc_ref[...] += product()

        @pl.when(k == nk - 1)
        def _():
            finish(acc_ref[...])

    return _call(
        body, name=name, grid=grid, in_specs=[a_spec, b_spec, *extra_specs], out_specs=[o[2] for o in outs],
        out_shape=[jax.ShapeDtypeStruct(o[0], o[1]) for o in outs], scratch_shapes=[pltpu.VMEM((tm, tn), F32)],
        sem=("parallel", "arbitrary" if acc_outs else "parallel", "arbitrary"), args=(a, b, *extras), exchange=exchange)


def _ij(tm, tn):
    return pl.BlockSpec((tm, tn), lambda i, j, k: (i, j))


def _mm_nn(name, a, b, *, out_dtype, residual=None, out_scale=None, exchange=None, tm=1024, tn=1024, tk=2048):
    M, K = a.shape
    N = b.shape[1]
    tm, tn, tk = _tile(M, tm), _tile(N, tn), _tile(K, tk)
    extras, especs, epi = (), (), None
    if residual is not None:
        extras, especs = (residual,), (_ij(tm, tn),)
        epi = lambda acc, r: (acc + r,)
    elif out_scale is not None:
        epi = lambda acc: (acc * out_scale,)
    outs, got = _matmul(
        name, a, b, mode="nn", grid=(M // tm, N // tn, K // tk), tm=tm, tn=tn,
        a_spec=pl.BlockSpec((tm, tk), lambda i, j, k: (i, k)),
        b_spec=pl.BlockSpec((tk, tn), lambda i, j, k: (k, j)),
        outs=[((M, N), out_dtype, _ij(tm, tn))], extras=extras, extra_specs=especs, epilogue=epi, exchange=exchange)
    return outs[0] if exchange is None else (outs[0], got)


def _mm_nt(name, a, b, *, out_dtype, b_cols=None, residual=None, tm=1024, tn=1024, tk=2048):
    M, K = a.shape
    N = b.shape[0]
    c0 = 0 if b_cols is None else b_cols[0]
    tm, tn, tk = _tile(M, tm), _tile(N, tn), _tile(K, tk)
    assert c0 % tk == 0
    kb0 = c0 // tk
    extras, especs, epi = (), (), None
    if residual is not None:
        extras, especs = (residual,), (_ij(tm, tn),)
        epi = lambda acc, r: (acc + r,)
    return _matmul(
        name, a, b, mode="nt", grid=(M // tm, N // tn, K // tk), tm=tm, tn=tn,
        a_spec=pl.BlockSpec((tm, tk), lambda i, j, k: (i, k)),
        b_spec=pl.BlockSpec((tn, tk), lambda i, j, k: (j, kb0 + k)),
        outs=[((M, N), out_dtype, _ij(tm, tn))], extras=extras, extra_specs=especs, epilogue=epi)[0][0]


def _mm_tn(name, a, b, *, exchange=None, tm=1024, tn=1024, tk=2048):
    K, M = a.shape
    N = b.shape[1]
    tm, tn, tk = _tile(M, tm), _tile(N, tn), _tile(K, tk)
    outs, got = _matmul(
        name, a, b, mode="tn", grid=(M // tm, N // tn, K // tk), tm=tm, tn=tn,
        a_spec=pl.BlockSpec((tk, tm), lambda i, j, k: (k, i)),
        b_spec=pl.BlockSpec((tk, tn), lambda i, j, k: (k, j)),
        outs=[((M, N), BF16, _ij(tm, tn))], exchange=exchange)
    return outs[0], got


def _kvf_dx(dk, dv, d_f, w_kv, w_f, *, tm=1024, tn=1024):
    S, D = dk.shape
    tm, tn = _tile(S, tm), _tile(D, tn)

    def body(dk_ref, dv_ref, df_ref, wk_ref, wv_ref, wf_ref, o_ref):
        acc = lax.dot_general(dk_ref[...], wk_ref[...], _DIMS["nt"], preferred_element_type=F32)
        acc = acc + lax.dot_general(dv_ref[...], wv_ref[...], _DIMS["nt"], preferred_element_type=F32)
        o_ref[...] = acc + lax.dot_general(df_ref[...].astype(BF16), wf_ref[...], _DIMS["nt"],
                                           preferred_element_type=F32)

    row = lambda width: pl.BlockSpec((tm, width), lambda i, j: (i, 0))
    return pl.pallas_call(
        body, grid=(S // tm, D // tn),
        in_specs=[row(D), row(D), row(LANES),
                  pl.BlockSpec((tn, D), lambda i, j: (j, 0)), pl.BlockSpec((tn, D), lambda i, j: (j, 1)),
                  pl.BlockSpec((tn, LANES), lambda i, j: (j, 0))],
        out_specs=pl.BlockSpec((tm, tn), lambda i, j: (i, j)), out_shape=jax.ShapeDtypeStruct((S, D), F32),
        compiler_params=_params(("parallel", "parallel")), name="kvf_proj_dx")(dk, dv, d_f, w_kv, w_kv, w_f)


def _mlp_fwd(tag, hn, h, w_up_g, w_down_g, *, ex_up=None, ex_down=None, tm=1024, tk=2048):
    S, D = hn.shape
    fb = w_up_g.shape[2]
    F = N_DEV * fb
    tm, tku = _tile(S, tm), _tile(D, tk)

    def up_epi(acc):
        u = jnp.maximum(acc, 0.0)
        return u, u * u

    (u, uu), got_up = _matmul(
        f"mlp_up_{tag}", hn, w_up_g, mode="nn", grid=(S // tm, N_DEV, D // tku), tm=tm, tn=fb,
        a_spec=pl.BlockSpec((tm, tku), lambda i, j, k: (i, k)),
        b_spec=pl.BlockSpec((None, tku, fb), lambda i, j, k: (j, k, 0)),
        outs=[((S, F), BF16, _ij(tm, fb)), ((S, F), BF16, _ij(tm, fb))], epilogue=up_epi, exchange=ex_up)
    if w_down_g is None:
        w_down_g = got_up[0]

    tn = _tile(D, 1024)
    (h_out,), got_down = _matmul(
        f"mlp_down_{tag}", uu, w_down_g, mode="nn", grid=(S // tm, D // tn, N_DEV // 2), tm=tm, tn=tn,
        a_spec=pl.BlockSpec((tm, 2 * fb), lambda i, j, k: (i, k)),
        b_spec=pl.BlockSpec((2, fb, tn), lambda i, j, k: (k, 0, j)),
        outs=[((S, D), F32, _ij(tm, tn))], extras=(h,), extra_specs=(_ij(tm, tn),),
        epilogue=lambda acc, res: (acc + res,), exchange=ex_down)
    return u, uu, h_out, got_up, got_down


def _mlp_dpre(tag, dh, u, w_down_g, *, tm=1024, tk=2048):
    S, D = dh.shape
    fb = w_down_g.shape[1]
    tm, tkd = _tile(S, tm), _tile(D, tk)
    return _matmul(
        f"mlp_dpre_{tag}", dh, w_down_g, mode="nt", grid=(S // tm, N_DEV, D // tkd), tm=tm, tn=fb,
        a_spec=pl.BlockSpec((tm, tkd), lambda i, j, k: (i, k)),
        b_spec=pl.BlockSpec((None, fb, tkd), lambda i, j, k: (j, 0, k)),
        outs=[((S, N_DEV * fb), BF16, _ij(tm, fb))], extras=(u,), extra_specs=(_ij(tm, fb),),
        epilogue=lambda acc, uv: (acc * (2.0 * uv.astype(F32)),))[0][0]


def _mlp_dwup(tag, hn, d_pre, fb, *, exchange=None):
    S, D = hn.shape
    tmu = _tile(D, 1024)
    tks = _tile(S, 2048)
    outs, got = _matmul(
        f"mlp_dwup_{tag}", hn, d_pre, mode="tn", grid=(D // tmu, N_DEV, S // tks), tm=tmu, tn=fb,
        a_spec=pl.BlockSpec((tks, tmu), lambda i, j, k: (k, i)),
        b_spec=pl.BlockSpec((tks, fb), lambda i, j, k: (k, j)),
        outs=[((N_DEV, D, fb), BF16, pl.BlockSpec((None, tmu, fb), lambda i, j, k: (j, i, 0)))], exchange=exchange)
    return outs[0], got


def _mlp_dhn(tag, d_pre, w_up_g, *, exchange=None, tm=1024):
    S, F = d_pre.shape
    D, fb = w_up_g.shape[1], w_up_g.shape[2]
    tm, tn = _tile(S, tm), _tile(D, 1024)
    outs, got = _matmul(
        f"mlp_dhn_{tag}", d_pre, w_up_g, mode="nt", grid=(S // tm, D // tn, N_DEV // 2), tm=tm, tn=tn,
        a_spec=pl.BlockSpec((tm, 2 * fb), lambda i, j, k: (i, k)),
        b_spec=pl.BlockSpec((2, tn, fb), lambda i, j, k: (k, j, 0)),
        outs=[((S, D), F32, _ij(tm, tn))], exchange=exchange)
    return outs[0], got


def _row_spec(ts, D):
    return pl.BlockSpec((ts, D), lambda i: (i, 0))


def _vec_spec(n, D):
    return pl.BlockSpec((n, D), lambda i: (0, 0))


def _rms_fwd(name, x, gains, *, ts=512):
    S, D = x.shape
    n = gains.shape[0]
    ts = _tile(S, ts)

    def body(x_ref, g_ref, *o_refs):
        xv = x_ref[...]
        y = xv * lax.rsqrt(jnp.mean(xv * xv, axis=-1, keepdims=True) + EPS)
        for i, o_ref in enumerate(o_refs):
            o_ref[...] = (y * g_ref[i:i + 1, :]).astype(o_ref.dtype)

    return pl.pallas_call(
        body, grid=(S // ts,), in_specs=[_row_spec(ts, D), _vec_spec(n, D)],
        out_specs=[_row_spec(ts, D)] * n, out_shape=[jax.ShapeDtypeStruct((S, D), BF16)] * n,
        compiler_params=_params(("parallel",)), name=name)(x, gains)


def _rms_bwd(name, x, gains, dys, res, *, ts=256):
    S, D = x.shape
    n = gains.shape[0]
    ts = _tile(S, ts)

    def body(x_ref, g_ref, *rest):
        dy_refs = rest[:n]
        res_ref, dx_ref, dxb_ref, dg_ref = rest[n:]
        i = pl.program_id(0)
        xv = x_ref[...]
        r = lax.rsqrt(jnp.mean(xv * xv, axis=-1, keepdims=True) + EPS)
        xhat = xv * r
        dxhat = None
        dgs = []
        for k in range(n):
            dy = dy_refs[k][...].astype(F32)
            dgs.append(jnp.sum(dy * xhat, axis=0, keepdims=True))
            term = dy * g_ref[k:k + 1, :]
            dxhat = term if dxhat is None else dxhat + term
        dx = res_ref[...] + r * (dxhat - xhat * jnp.mean(dxhat * xhat, axis=-1, keepdims=True))
        dx_ref[...] = dx
        dxb_ref[...] = dx.astype(BF16)
        dg = jnp.concatenate(dgs, axis=0) if n > 1 else dgs[0]

        @pl.when(i == 0)
        def _():
            dg_ref[...] = dg

        @pl.when(i > 0)
        def _():
            dg_ref[...] += dg

    return pl.pallas_call(
        body, grid=(S // ts,),
        in_specs=[_row_spec(ts, D), _vec_spec(n, D)] + [_row_spec(ts, D)] * (n + 1),
        out_specs=[_row_spec(ts, D), _row_spec(ts, D), _vec_spec(n, D)],
        out_shape=[jax.ShapeDtypeStruct((S, D), F32), jax.ShapeDtypeStruct((S, D), BF16),
                   jax.ShapeDtypeStruct((n, D), F32)],
        compiler_params=_params(("arbitrary",)), name=name)(x, gains, *dys, res)


def _loss_head(h, gain, target, *, ts=512):
    S, D = h.shape
    ts = _tile(S, ts)

    def body(x_ref, g_ref, t_ref, dx_ref, dxb_ref, dg_ref, loss_ref):
        i = pl.program_id(0)
        xv = x_ref[...]
        g = g_ref[...]
        r = lax.rsqrt(jnp.mean(xv * xv, axis=-1, keepdims=True) + EPS)
        xhat = xv * r
        err = xhat * g - t_ref[...]
        part = 0.5 * jnp.sum(jnp.mean(err * err, axis=-1, keepdims=True), axis=0, keepdims=True)
        dy = err * (1.0 / D)
        dg = jnp.sum(dy * xhat, axis=0, keepdims=True)
        dxhat = dy * g
        dx = r * (dxhat - xhat * jnp.mean(dxhat * xhat, axis=-1, keepdims=True))
        dx_ref[...] = dx
        dxb_ref[...] = dx.astype(BF16)

        @pl.when(i == 0)
        def _():
            dg_ref[...] = dg
            loss_ref[...] = jnp.broadcast_to(part, loss_ref.shape)

        @pl.when(i > 0)
        def _():
            dg_ref[...] += dg
            loss_ref[...] += jnp.broadcast_to(part, loss_ref.shape)

    return pl.pallas_call(
        body, grid=(S // ts,),
        in_specs=[_row_spec(ts, D), _vec_spec(1, D), _row_spec(ts, D)],
        out_specs=[_row_spec(ts, D), _row_spec(ts, D), _vec_spec(1, D), pl.BlockSpec((8, LANES), lambda i: (0, 0))],
        out_shape=[jax.ShapeDtypeStruct((S, D), F32), jax.ShapeDtypeStruct((S, D), BF16),
                   jax.ShapeDtypeStruct((1, D), F32), jax.ShapeDtypeStruct((8, LANES), F32)],
        compiler_params=_params(("arbitrary",)), name="loss_head")(h, gain, target)


def _window_counts(t, w):
    return jnp.minimum(t + 1, w).astype(F32)


def _pool_fwd(x, gain, pool_w, scale, gain_next, *, exchange=None, ts=256):
    S, D = x.shape
    dg = D // N_GROUPS
    ts = _tile(S, ts)
    per = ts // POOL_HALO

    def body(x_ref, xh_ref, g_ref, w_ref, sc_ref, gn_ref, h_ref, diff_ref, hn_ref):
        i = pl.program_id(0)
        g = g_ref[...]
        h_parts = []

        def norm(v):
            return v * lax.rsqrt(jnp.mean(v * v, axis=-1, keepdims=True) + EPS) * g

        xc = x_ref[...]
        hn_c = norm(xc)
        hn_h = norm(xh_ref[...]) * (i > 0).astype(F32)
        ext = jnp.concatenate([hn_h, hn_c], axis=0)
        t = i * ts + lax.broadcasted_iota(jnp.int32, (ts, 1), 0)
        for gi, w in enumerate(POOL_WINDOWS):
            cols = slice(gi * dg, (gi + 1) * dg)
            s = ext[:, cols]
            step = 1
            while step < w:
                s = s + pltpu.roll(s, step, 0)
                step *= 2
            mean = s[POOL_HALO:] * (1.0 / _window_counts(t, w))
            diff = (mean - hn_c[:, cols]).astype(BF16)
            diff_ref[:, cols] = diff
            mixed = jnp.dot(diff, w_ref[gi], preferred_element_type=F32)
            h_parts.append(xc[:, cols] + mixed * sc_ref[:, cols])
        h = jnp.concatenate(h_parts, axis=1)
        h_ref[...] = h
        hn_ref[...] = (h * lax.rsqrt(jnp.mean(h * h, axis=-1, keepdims=True) + EPS) * gn_ref[...]).astype(BF16)

    (h, diff, hn), got = _call(
        body, name="pool_fwd", grid=(S // ts,),
        in_specs=[_row_spec(ts, D),
                  pl.BlockSpec((POOL_HALO, D), lambda i: (jnp.maximum(i * per - 1, 0), 0)),
                  _vec_spec(1, D), pl.BlockSpec((N_GROUPS, dg, dg), lambda i: (0, 0, 0)), _vec_spec(1, D),
                  _vec_spec(1, D)],
        out_specs=[_row_spec(ts, D), _row_spec(ts, D), _row_spec(ts, D)],
        out_shape=[jax.ShapeDtypeStruct((S, D), F32), jax.ShapeDtypeStruct((S, D), BF16),
                   jax.ShapeDtypeStruct((S, D), BF16)],
        sem=("parallel",), args=(x, x, gain, pool_w, scale, gain_next), exchange=exchange)
    return h, diff, hn, got


def _pool_bwd(x, dh, diff, gain, pool_w, scale, *, ts=256):
    S, D = x.shape
    dg = D // N_GROUPS
    ts = _tile(S, ts)
    per = ts // POOL_HALO
    n_tiles = S // ts
    n_halo = S // POOL_HALO
    ext_rows = ts + POOL_HALO

    def body(x_ref, dh_ref, dhn_ref, diff_ref, g_ref, w_ref, sc_ref, dx_ref, dw_ref, dsc_ref, dgain_ref):
        i = pl.program_id(0)
        xc = x_ref[...]
        g = g_ref[...]
        r = lax.rsqrt(jnp.mean(xc * xc, axis=-1, keepdims=True) + EPS)
        xhat = xc * r
        dh_c = dh_ref[...]
        dh_n = dhn_ref[...] * (i < n_tiles - 1).astype(F32)
        dh_ext = jnp.concatenate([dh_c, dh_n], axis=0)
        t_ext = i * ts + lax.broadcasted_iota(jnp.int32, (ext_rows, 1), 0)
        d_hn_parts, dsc_parts = [], []
        for gi, w in enumerate(POOL_WINDOWS):
            cols = slice(gi * dg, (gi + 1) * dg)
            wg = w_ref[gi]
            dmix = (dh_ext[:, cols] * sc_ref[:, cols]).astype(BF16)
            d_diff = lax.dot_general(dmix, wg, _DIMS["nt"], preferred_element_type=F32)
            diff_c = diff_ref[:, cols]
            dwg = lax.dot_general(diff_c, dmix[:ts], _DIMS["tn"], preferred_element_type=F32)
            mixed = jnp.dot(diff_c, wg, preferred_element_type=F32)
            dsc_parts.append(jnp.sum(dh_c[:, cols] * mixed, axis=0, keepdims=True))
            e = d_diff * (1.0 / _window_counts(t_ext, w))
            step = 1
            while step < w:
                e = e + pltpu.roll(e, ext_rows - step, 0)
                step *= 2
            d_hn_parts.append(e[:ts] - d_diff[:ts])

            @pl.when(i == 0)
            def _():
                dw_ref[gi] = dwg

            @pl.when(i > 0)
            def _():
                dw_ref[gi] += dwg

        d_hn = jnp.concatenate(d_hn_parts, axis=1)
        dsc = jnp.concatenate(dsc_parts, axis=1)
        dgain = jnp.sum(d_hn * xhat, axis=0, keepdims=True)
        dxhat = d_hn * g
        dx_ref[...] = dh_c + r * (dxhat - xhat * jnp.mean(dxhat * xhat, axis=-1, keepdims=True))

        @pl.when(i == 0)
        def _():
            dsc_ref[...] = dsc
            dgain_ref[...] = dgain

        @pl.when(i > 0)
        def _():
            dsc_ref[...] += dsc
            dgain_ref[...] += dgain

    return pl.pallas_call(
        body, grid=(n_tiles,),
        in_specs=[_row_spec(ts, D), _row_spec(ts, D),
                  pl.BlockSpec((POOL_HALO, D), lambda i: (jnp.minimum((i + 1) * per, n_halo - 1), 0)),
                  _row_spec(ts, D), _vec_spec(1, D),
                  pl.BlockSpec((N_GROUPS, dg, dg), lambda i: (0, 0, 0)), _vec_spec(1, D)],
        out_specs=[_row_spec(ts, D), pl.BlockSpec((N_GROUPS, dg, dg), lambda i: (0, 0, 0)),
                   _vec_spec(1, D), _vec_spec(1, D)],
        out_shape=[jax.ShapeDtypeStruct((S, D), F32), jax.ShapeDtypeStruct((N_GROUPS, dg, dg), F32),
                   jax.ShapeDtypeStruct((1, D), F32), jax.ShapeDtypeStruct((1, D), F32)],
        compiler_params=_params(("arbitrary",)), name="pool_bwd")(x, dh, dh, diff, gain, pool_w, scale)


def _gate_fwd(f_raw, b_pad, *, ts=512):
    S = f_raw.shape[0]
    ts = _tile(S, ts)

    def body(f_ref, b_ref, c_ref, carry_ref):
        i = pl.program_id(0)

        @pl.when(i == 0)
        def _():
            carry_ref[...] = jnp.zeros_like(carry_ref)

        z = f_ref[...] + b_ref[...]
        v = jnp.minimum(z, 0.0) - jnp.log1p(jnp.exp(-jnp.abs(z)))
        row = lax.broadcasted_iota(jnp.int32, (ts, 1), 0)
        step = 1
        while step < ts:
            v = v + jnp.where(row >= step, pltpu.roll(v, step, 0), 0.0)
            step *= 2
        out = v + carry_ref[0:1, :]
        c_ref[...] = out
        carry_ref[...] = jnp.broadcast_to(out[ts - 1:ts, :], carry_ref.shape)

    return pl.pallas_call(
        body, grid=(S // ts,),
        in_specs=[pl.BlockSpec((ts, LANES), lambda i: (i, 0)), pl.BlockSpec((1, LANES), lambda i: (0, 0))],
        out_specs=pl.BlockSpec((ts, LANES), lambda i: (i, 0)),
        out_shape=jax.ShapeDtypeStruct((S, LANES), F32),
        scratch_shapes=[pltpu.VMEM((8, LANES), F32)],
        compiler_params=_params(("arbitrary",)), name="gate_fwd")(f_raw, b_pad)


def _gate_bwd(dc, f_raw, b_pad, *, ts=512):
    S = f_raw.shape[0]
    ts = _tile(S, ts)
    n = S // ts

    def body(dc_ref, f_ref, b_ref, df_ref, db_ref, carry_ref):
        i = pl.program_id(0)

        @pl.when(i == 0)
        def _():
            carry_ref[...] = jnp.zeros_like(carry_ref)

        v = dc_ref[...]
        row = lax.broadcasted_iota(jnp.int32, (ts, 1), 0)
        step = 1
        while step < ts:
            v = v + jnp.where(row < ts - step, pltpu.roll(v, ts - step, 0), 0.0)
            step *= 2
        d_logf = v + carry_ref[0:1, :]
        carry_ref[...] = jnp.broadcast_to(d_logf[0:1, :], carry_ref.shape)
        z = f_ref[...] + b_ref[...]
        df = d_logf / (1.0 + jnp.exp(z))
        df_ref[...] = df
        db = jnp.sum(df, axis=0, keepdims=True)

        @pl.when(i == 0)
        def _():
            db_ref[...] = db

        @pl.when(i > 0)
        def _():
            db_ref[...] += db

    rev = lambda i: (n - 1 - i, 0)
    return pl.pallas_call(
        body, grid=(n,),
        in_specs=[pl.BlockSpec((ts, LANES), rev), pl.BlockSpec((ts, LANES), rev),
                  pl.BlockSpec((1, LANES), lambda i: (0, 0))],
        out_specs=[pl.BlockSpec((ts, LANES), rev), pl.BlockSpec((1, LANES), lambda i: (0, 0))],
        out_shape=[jax.ShapeDtypeStruct((S, LANES), F32), jax.ShapeDtypeStruct((1, LANES), F32)],
        scratch_shapes=[pltpu.VMEM((8, LANES), F32)],
        compiler_params=_params(("arbitrary",)), name="gate_bwd")(dc, f_raw, b_pad)


def _row_layout(a_hs, t):
    n_heads, S = a_hs.shape
    return a_hs.reshape(n_heads, S // t, 1, t)


def _lane_pick(blk, h):
    lane = lax.broadcasted_iota(jnp.int32, blk.shape, 1)
    return jnp.sum(jnp.where(lane == h, blk, 0.0), axis=-1, keepdims=True)


def _lane_put(ref, h, col, first=True):
    lane = lax.broadcasted_iota(jnp.int32, ref.shape, 1)
    if not first:
        ref[...] = jnp.where(lane == h, col, ref[...])
        return

    @pl.when(h == 0)
    def _():
        ref[...] = jnp.where(lane == 0, col, 0.0)

    @pl.when(h > 0)
    def _():
        ref[...] = jnp.where(lane == h, col, ref[...])


def _causal(s, masked, fill, rows_are_queries=True):
    if not masked:
        return s
    rr = lax.broadcasted_iota(jnp.int32, s.shape, 0)
    cc = lax.broadcasted_iota(jnp.int32, s.shape, 1)
    keep = (cc <= rr) if rows_are_queries else (rr <= cc)
    return jnp.where(keep, s, fill)


def _fox_fwd(q2, kv, c_row, *, t=512, exchange=None):
    S, D = q2.shape
    H = D // HEAD_DIM
    t = _tile(S, t)
    nb = S // t
    hp = HEADS_PER_STEP
    wide = hp * HEAD_DIM

    def body(q_ref, k_ref, v_ref, cr_ref, o_ref, l2_ref):
        i = pl.program_id(0)
        g = pl.program_id(1)
        cols = [slice(a * HEAD_DIM, (a + 1) * HEAD_DIM) for a in range(hp)]
        refs_i = [cr_ref[a, i][:, 0:1] for a in range(hp)]
        qs = [q_ref[:, cols[a]] for a in range(hp)]

        def step(j, carry, masked):
            r0 = pl.multiple_of(j * t, t)
            out = []
            for a in range(hp):
                m, l, acc = carry[3 * a:3 * a + 3]
                kb = k_ref[pl.ds(r0, t), cols[a]]
                vb = v_ref[pl.ds(r0, t), cols[a]]
                ck = cr_ref[a, j] - refs_i[a]
                s = lax.dot_general(qs[a], kb, _DIMS["nt"], preferred_element_type=F32) - ck
                s = _causal(s, masked, NEG_INF)
                m_new = jnp.maximum(m, jnp.max(s, axis=-1, keepdims=True))
                alpha = jnp.exp2(m - m_new)
                p = jnp.exp2(s - m_new)
                l = alpha * l + jnp.sum(p, axis=-1, keepdims=True)
                acc = alpha * acc + jnp.dot(p.astype(BF16), vb, preferred_element_type=F32)
                out += [m_new, l, acc]
            return tuple(out)

        init = (jnp.full((t, 1), NEG_INF, F32), jnp.zeros((t, 1), F32), jnp.zeros((t, HEAD_DIM), F32)) * hp
        carry = lax.fori_loop(0, i, lambda j, c: step(j, c, False), init)
        carry = step(i, carry, True)
        for a in range(hp):
            m, l, acc = carry[3 * a:3 * a + 3]
            o_ref[:, cols[a]] = (acc / l).astype(o_ref.dtype)
            _lane_put(l2_ref, g * hp + a, m + jnp.log2(l), first=(a == 0))

    (o, l2_sh), got = _call(
        body, name="fox_fwd", grid=(nb, H // hp),
        in_specs=[pl.BlockSpec((t, wide), lambda i, g: (i, g)),
                  pl.BlockSpec((S, wide), lambda i, g: (0, g)),
                  pl.BlockSpec((S, wide), lambda i, g: (0, H // hp + g)),
                  pl.BlockSpec((hp, nb, 1, t), lambda i, g: (g, 0, 0, 0))],
        out_specs=[pl.BlockSpec((t, wide), lambda i, g: (i, g)),
                   pl.BlockSpec((t, LANES), lambda i, g: (i, 0))],
        out_shape=[jax.ShapeDtypeStruct((S, D), BF16), jax.ShapeDtypeStruct((S, LANES), F32)],
        sem=("parallel", "arbitrary"), args=(q2, kv, kv, c_row), exchange=exchange)
    return o, l2_sh, got


def _o_proj_dx(dh, wo, o, *, tm=1024, tn=1024):
    S, D = dh.shape
    tm, tn = _tile(S, tm), _tile(D, tn)
    heads = tn // HEAD_DIM

    def epilogue(acc, ov):
        j = pl.program_id(1)
        lane = lax.broadcasted_iota(jnp.int32, (tm, LANES), 1)
        prod = acc * ov.astype(F32)
        blk = jnp.zeros((tm, LANES), F32)
        for hh in range(heads):
            col = jnp.sum(prod[:, hh * HEAD_DIM:(hh + 1) * HEAD_DIM], axis=-1, keepdims=True)
            blk = jnp.where(lane == j * heads + hh, col, blk)
        return acc, blk

    (do, delta_sh), _ = _matmul(
        "o_proj_dx", dh, wo, mode="nt", grid=(S // tm, D // tn, 1), tm=tm, tn=tn,
        a_spec=pl.BlockSpec((tm, D), lambda i, j, k: (i, 0)),
        b_spec=pl.BlockSpec((tn, D), lambda i, j, k: (j, 0)),
        outs=[((S, D), BF16, _ij(tm, tn)), ((S, LANES), F32, pl.BlockSpec((tm, LANES), lambda i, j, k: (i, 0)))],
        extras=(o,), extra_specs=(_ij(tm, tn),), epilogue=epilogue, acc_outs=(1,))
    return do, delta_sh


def _fox_bwd(q2, kv, do, c_sh, c_row, l2_row, delta_row, *, t=512, exchange=None):
    S, D = q2.shape
    H = D // HEAD_DIM
    t = _tile(S, t)
    nb = S // t
    scale = HEAD_DIM ** -0.5

    def body(q_ref, do_ref, k_ref, v_ref, csh_ref, cr_ref, l2_ref, dl_ref, dq_ref, dk_ref, dv_ref, dck_ref, dcq_ref,
             dq_acc):
        h = pl.program_id(0)
        j = pl.program_id(1)
        ck = _lane_pick(csh_ref[...], h)
        kb = k_ref[...]
        vb = v_ref[...]

        @pl.when(j == 0)
        def _():
            dq_acc[...] = jnp.zeros_like(dq_acc)
            dcq_ref[...] = jnp.zeros_like(dcq_ref)

        def step(i, carry, masked):
            dk, dv, dc = carry
            r0 = pl.multiple_of(i * t, t)
            qb = q_ref[pl.ds(r0, t), :]
            dob = do_ref[pl.ds(r0, t), :]
            s = lax.dot_general(kb, qb, _DIMS["nt"], preferred_element_type=F32) - (ck - cr_ref[i][:, 0:1])
            p = _causal(jnp.exp2(s - l2_ref[i]), masked, 0.0, rows_are_queries=False)
            dv = dv + jnp.dot(p.astype(BF16), dob, preferred_element_type=F32)
            dp = lax.dot_general(vb, dob, _DIMS["nt"], preferred_element_type=F32)
            ds = p * (dp - dl_ref[i])
            dc = dc - jnp.sum(ds, axis=-1, keepdims=True)
            dcq_ref[i] += jnp.sum(ds, axis=0, keepdims=True)
            dsb = ds.astype(BF16)
            dk = dk + jnp.dot(dsb, qb, preferred_element_type=F32)
            dq_acc[pl.ds(r0, t), :] += lax.dot_general(dsb, kb, _DIMS["tn"], preferred_element_type=F32)
            return dk, dv, dc

        init = (jnp.zeros((t, HEAD_DIM), F32), jnp.zeros((t, HEAD_DIM), F32), jnp.zeros((t, 1), F32))
        carry = step(j, init, True)
        dk, dv, dc = lax.fori_loop(j + 1, nb, lambda i, c: step(i, c, False), carry)
        dk_ref[...] = (dk * LN2).astype(dk_ref.dtype)
        dv_ref[...] = dv.astype(dv_ref.dtype)
        rows = pl.ds(pl.multiple_of(j * t, t), t)
        lane = lax.broadcasted_iota(jnp.int32, (t, LANES), 1)

        @pl.when(h == 0)
        def _():
            dck_ref[rows, :] = jnp.where(lane == 0, dc, 0.0)

        @pl.when(h > 0)
        def _():
            dck_ref[rows, :] = jnp.where(lane == h, dc, dck_ref[rows, :])

        @pl.when(j == nb - 1)
        def _():
            dq_ref[...] = (dq_acc[...] * scale).astype(dq_ref.dtype)

    kspec = pl.BlockSpec((t, HEAD_DIM), lambda h, j: (j, h))
    headspec = pl.BlockSpec((S, HEAD_DIM), lambda h, j: (0, h))
    rowspec = pl.BlockSpec((None, nb, 1, t), lambda h, j: (h, 0, 0, 0))
    (dq, dk, dv, dck_sh, dcq_row), got = _call(
        body, name="fox_bwd", grid=(H, nb),
        in_specs=[headspec, headspec, kspec,
                  pl.BlockSpec((t, HEAD_DIM), lambda h, j: (j, H + h)),
                  pl.BlockSpec((t, LANES), lambda h, j: (j, 0)),
                  rowspec, rowspec, rowspec],
        out_specs=[headspec, kspec, kspec, pl.BlockSpec((S, LANES), lambda h, j: (0, 0)), rowspec],
        out_shape=[jax.ShapeDtypeStruct((S, D), BF16), jax.ShapeDtypeStruct((S, D), BF16),
                   jax.ShapeDtypeStruct((S, D), BF16), jax.ShapeDtypeStruct((S, LANES), F32),
                   jax.ShapeDtypeStruct((H, nb, 1, t), F32)],
        scratch_shapes=[pltpu.VMEM((S, HEAD_DIM), F32)],
        sem=("arbitrary", "arbitrary"), args=(q2, do, kv, kv, c_sh, c_row, l2_row, delta_row), exchange=exchange)
    return dq, dk, dv, dck_sh, dcq_row, got


def _adamw(name, parts, w, m, v, layer=None, into=None, *, tr=128):
    P, R, C = parts.shape
    tr = _tile(R, tr)

    def body(p_ref, w_ref, m_ref, v_ref, *rest):
        g_ref, d_ref, nm_ref, nv_ref = rest[-4:]
        g = p_ref[0].astype(F32)
        for k in range(1, P):
            g = g + p_ref[k].astype(F32)
        wv = w_ref[...]
        nm = ADAM_B1 * m_ref[...] + (1.0 - ADAM_B1) * g
        nv = ADAM_B2 * v_ref[...] + (1.0 - ADAM_B2) * (g * g)
        m_hat = nm / (1.0 - ADAM_B1 ** ADAM_STEP)
        v_hat = nv / (1.0 - ADAM_B2 ** ADAM_STEP)
        g_ref[...] = g
        d_ref[...] = -ADAM_LR * (m_hat / (jnp.sqrt(v_hat) + ADAM_EPS) + ADAM_WD * wv)
        nm_ref[...] = nm
        nv_ref[...] = nv

    pspec = pl.BlockSpec((P, tr, C), lambda i: (0, i, 0))
    if layer is None:
        wspec = pl.BlockSpec((tr, C), lambda i: (i, 0))
        return pl.pallas_call(
            body, grid=(R // tr,), in_specs=[pspec, wspec, wspec, wspec],
            out_specs=[wspec] * 4, out_shape=[jax.ShapeDtypeStruct((R, C), F32)] * 4,
            compiler_params=_params(("parallel",)), name=name)(parts, w, m, v)
    wspec = pl.BlockSpec((None, tr, C), lambda i: (layer, i, 0))
    prev = [] if into is None else list(into)
    return pl.pallas_call(
        body, grid=(R // tr,), in_specs=[pspec, wspec, wspec, wspec] + [_ANY] * len(prev),
        out_specs=[wspec] * 4, out_shape=[jax.ShapeDtypeStruct(w.shape, F32)] * 4,
        input_output_aliases={4 + k: k for k in range(len(prev))},
        compiler_params=_params(("parallel",)), name=name)(parts, w, m, v, *prev)


def _rows(a):
    flat = a.reshape(-1)
    pad = (-flat.shape[0]) % LANES
    if pad:
        flat = jnp.pad(flat, (0, pad))
    return flat.reshape(-1, LANES)


def kernel(x, norm_mix, norm_mlp, pool_w, pool_scale, norm_kv, w_kvf, b_f, w_q, w_o, w_up, w_down, norm_out, loss_target, m_norm_mix, m_norm_mlp, m_pool_w, m_pool_scale, m_norm_kv, m_w_kvf, m_b_f, m_w_q, m_w_o, m_w_up, m_w_down, m_norm_out, v_norm_mix, v_norm_mlp, v_pool_w, v_pool_scale, v_norm_kv, v_w_kvf, v_b_f, v_w_q, v_w_o, v_w_up, v_w_down, v_norm_out):
    _, S, D = x.shape
    H = D // HEAD_DIM
    dg = D // N_GROUPS
    n_kvf = 2 * D + H
    kvf_b = w_kvf.shape[1]
    fb = w_up.shape[2]
    ps_b = pool_scale.shape[1]
    xi, yi, ci = _position()
    my_block = 4 * xi + 2 * yi + ci
    x2 = x.reshape(S, D)
    tgt = loss_target.reshape(S, D)
    b_pad = jnp.pad(b_f, (0, LANES - H)).reshape(1, LANES)

    g_pool, g_scale = _all_gather_two_level([pool_w.astype(BF16), pool_scale])
    pw = g_pool[:, 0].transpose(1, 0, 2, 3).reshape(N_GROUPS, dg, dg)
    scale_full = g_scale.reshape(1, D)
    ex_up0 = _GatherByChip([w_up[0].astype(BF16)])
    ex_down0 = _GatherByChip([w_down[0].astype(BF16)])
    ex_kvf = _Exchange("gather", [w_kvf.astype(BF16)])
    ex_q = _Exchange("gather", [w_q[0].astype(BF16)])
    ex_late = _Exchange("gather", [w_o[0].astype(BF16), w_up[1].astype(BF16), w_down[1].astype(BF16)])

    h1, diff, hn_m0, (g_up0,) = _pool_fwd(x2, norm_mix[0:1], pw, scale_full, norm_mlp[0:1], exchange=ex_up0)
    u0, uu0, h2, (g_down0,), (g_kvf,) = _mlp_fwd("l0", hn_m0, h1, g_up0, None, ex_up=ex_down0, ex_down=ex_kvf)
    wkvf = g_kvf.transpose(1, 0, 2).reshape(D, n_kvf)
    w_kv = wkvf[:, :2 * D]
    w_f = jnp.pad(wkvf[:, 2 * D:], ((0, 0), (0, LANES - H)))

    gains_kv_q = jnp.stack([norm_kv, norm_mix[1]])
    hkv, hn_q = _rms_fwd("rms_kv_q", h2, gains_kv_q)
    kv, (g_q,) = _mm_nn("kv_proj", hkv, w_kv, out_dtype=BF16, exchange=ex_q)
    wq = g_q.reshape(D, D)
    f_raw = _mm_nn("f_proj", hkv, w_f, out_dtype=F32)
    c_sh = _gate_fwd(f_raw, b_pad)
    t_attn = _tile(S, 1024)
    c2_sh = c_sh * LOG2E
    c2_row = _row_layout(c2_sh[:, :H].T, t_attn)
    q2 = _mm_nn("q_proj", hn_q, wq, out_dtype=BF16, out_scale=HEAD_DIM ** -0.5 * LOG2E)
    o, l2_sh, (g_o, g_up1, g_down1) = _fox_fwd(q2, kv, c2_row, t=t_attn, exchange=ex_late)
    wo = g_o.reshape(D, D)
    h3 = _mm_nn("o_proj", o, wo, out_dtype=F32, residual=h2)
    (hn_m1,) = _rms_fwd("rms_mlp1", h3, norm_mlp[1:2])
    u1, uu1, h4, _, _ = _mlp_fwd("l1", hn_m1, h3, g_up1, g_down1)
    dh4, dh4_b, d_norm_out, loss_part = _loss_head(h4, norm_out.reshape(1, D), tgt)

    d_pre = _mlp_dpre("l1", dh4_b, u1, g_down1)
    dw_down1, _ = _mm_tn("mlp_dwdown_l1", uu1, dh4_b)
    dw_up1, _ = _mlp_dwup("l1", hn_m1, d_pre, fb)
    d_hn, _ = _mlp_dhn("l1", d_pre, g_up1)
    dh3, dh3_b, d_norm_mlp1 = _rms_bwd("rms_mlp1_bwd", h3, norm_mlp[1:2], [d_hn], dh4)

    do, delta_sh = _o_proj_dx(dh3_b, wo, o)
    dw_o, _ = _mm_tn("o_proj_dw", o, dh3_b)
    l2_row = _row_layout(l2_sh[:, :H].T, t_attn)
    delta_row = _row_layout(delta_sh[:, :H].T, t_attn)
    dq, dk, dv, dck_sh, dcq_row, (p_up1, p_down1, p_o) = _fox_bwd(
        q2, kv, do, c2_sh, c2_row, l2_row, delta_row, t=t_attn,
        exchange=_Exchange("scatter", [dw_up1, dw_down1.reshape(N_DEV, fb, D), dw_o.reshape(N_DEV, D // N_DEV, D)]))
    dw_q, _ = _mm_tn("q_proj_dw", hn_q, dq)
    d_hn_q = _mm_nt("q_proj_dx", dq, wq, out_dtype=F32)

    dcq_sh = jnp.pad(dcq_row.reshape(H, S).T, ((0, 0), (0, LANES - H)))
    d_f, d_b = _gate_bwd(dck_sh + dcq_sh, f_raw, b_pad)
    dw_k, _ = _mm_tn("k_proj_dw", hkv, dk)
    dw_v, _ = _mm_tn("v_proj_dw", hkv, dv)
    dw_f, _ = _mm_tn("f_proj_dw", hkv, d_f)
    d_hkv = _kvf_dx(dk, dv, d_f, w_kv, w_f)
    dh2, dh2_b, d_norm_kv_q = _rms_bwd("rms_kv_q_bwd", h2, gains_kv_q, [d_hkv, d_hn_q], dh3)
    dw_kvf = jnp.concatenate([dw_k, dw_v, dw_f[:, :H]], axis=1).reshape(D, N_DEV, kvf_b).transpose(1, 0, 2)

    d_pre = _mlp_dpre("l0", dh2_b, u0, g_down0)
    dw_down0, (p_kvf, p_q) = _mm_tn(
        "mlp_dwdown_l0", uu0, dh2_b, exchange=_Exchange("scatter", [dw_kvf, dw_q.reshape(N_DEV, D // N_DEV, D)]))
    dw_up0, (p_down0,) = _mlp_dwup(
        "l0", hn_m0, d_pre, fb, exchange=_Exchange("scatter", [dw_down0.reshape(N_DEV, fb, D)]))
    d_hn, (p_up0,) = _mlp_dhn("l0", d_pre, g_up0, exchange=_Exchange("scatter", [dw_up0]))
    dh1, _, d_norm_mlp0 = _rms_bwd("rms_mlp0_bwd", h1, norm_mlp[0:1], [d_hn], dh2)
    grad_x, dw_pool, d_scale, d_norm_mix0 = _pool_bwd(x2, dh1, diff, norm_mix[0:1], pw, scale_full)
    (p_pool,) = _exchange_now("scatter_pool", _Exchange("scatter", [
        dw_pool.astype(BF16).reshape(N_GROUPS, N_DEV, dg // N_DEV, dg).transpose(1, 0, 2, 3).reshape(
            N_DEV, N_GROUPS * dg // N_DEV, dg)]))

    small = jnp.concatenate([
        _rows(jnp.concatenate([d_norm_mix0, d_norm_kv_q[1:2]], axis=0)),
        _rows(jnp.concatenate([d_norm_mlp0, d_norm_mlp1], axis=0)),
        _rows(d_norm_kv_q[0:1]),
        _rows(d_norm_out),
        _rows(d_scale),
        d_b,
        jnp.pad(loss_part[0:1, 0:1], ((0, 0), (0, LANES - 1))),
    ], axis=0)
    n_small = small.shape[0]
    small = jnp.pad(small, ((0, (-n_small) % 8), (0, 0)))
    total = _all_reduce_small(small)
    rd = D // LANES
    loss = total[7 * rd + 1, 0]
    g_scale_mine = lax.dynamic_slice(total[6 * rd:7 * rd].reshape(D), (my_block * ps_b,), (ps_b,))

    def pack(nm_, nl_, kv_, out_, bf_, ps_):
        return jnp.concatenate([_rows(nm_), _rows(nl_), _rows(kv_), _rows(out_), _rows(bf_), _rows(ps_)], axis=0)

    g_small = jnp.concatenate([total[:6 * rd], total[7 * rd:7 * rd + 1], _rows(g_scale_mine)], axis=0)
    w_small = pack(norm_mix, norm_mlp, norm_kv, norm_out, b_f, pool_scale)
    m_small = pack(m_norm_mix, m_norm_mlp, m_norm_kv, m_norm_out, m_b_f, m_pool_scale)
    v_small = pack(v_norm_mix, v_norm_mlp, v_norm_kv, v_norm_out, v_b_f, v_pool_scale)
    rs = g_small.shape[0]
    padr = (-rs) % 8
    pad8 = lambda a: jnp.pad(a, ((0, padr), (0, 0)))
    small_out = _adamw("adamw_small", pad8(g_small)[None], pad8(w_small), pad8(m_small), pad8(v_small), tr=rs + padr)

    def unpack(a):
        o0 = 0
        res = []
        for shape in [(2, D), (2, D), (D,), (D,)]:
            nr = (2 * rd) if len(shape) == 2 else rd
            res.append(a[o0:o0 + nr].reshape(shape))
            o0 += nr
        res.append(a[o0, :H])
        res.append(a[o0 + 1:o0 + 1 + ps_b // LANES].reshape(1, ps_b))
        return res

    small_res = [unpack(a) for a in small_out]

    r_pool = _adamw("adamw_pool", p_pool, pool_w.reshape(-1, dg), m_pool_w.reshape(-1, dg), v_pool_w.reshape(-1, dg))
    r_kvf = _adamw("adamw_kvf", p_kvf, w_kvf, m_w_kvf, v_w_kvf)
    r_q = _adamw("adamw_q", p_q, w_q[0], m_w_q[0], v_w_q[0])
    r_o = _adamw("adamw_o", p_o, w_o[0], m_w_o[0], v_w_o[0])
    r_up = _adamw("adamw_up1", p_up1, w_up, m_w_up, v_w_up, layer=1)
    r_up = _adamw("adamw_up0", p_up0, w_up, m_w_up, v_w_up, layer=0, into=r_up)
    r_down = _adamw("adamw_down1", p_down1, w_down, m_w_down, v_w_down, layer=1)
    r_down = _adamw("adamw_down0", p_down0, w_down, m_w_down, v_w_down, layer=0, into=r_down)

    def leaves(kind):
        sm = small_res[kind]
        return [
            sm[0], sm[1],
            r_pool[kind].reshape(pool_w.shape),
            sm[5],
            sm[2],
            r_kvf[kind],
            sm[4],
            r_q[kind][None], r_o[kind][None],
            r_up[kind],
            r_down[kind],
            sm[3],
        ]

    return (loss, grad_x.reshape(x.shape), *leaves(0), *leaves(1), *leaves(2), *leaves(3))
```

```python
import jax
import jax.numpy as jnp
from jax import lax
from jax.experimental import pallas as pl
from jax.experimental.pallas import tpu as pltpu

F32 = jnp.float32
BF16 = jnp.bfloat16
MESH = pl.DeviceIdType.MESH

N_DEV = 8
EPS = 1e-6
NEG_INF = -1e30
POOL_WINDOWS = (2, 4, 8, 16)
N_GROUPS = len(POOL_WINDOWS)
POOL_HALO = 16
HEAD_DIM = 128
LANES = 128
HEADS_PER_STEP = 2
LOG2E = 1.4426950408889634
LN2 = 0.6931471805599453

ADAM_LR = 0.001
ADAM_B1 = 0.9
ADAM_B2 = 0.999
ADAM_EPS = 1e-08
ADAM_WD = 0.01
ADAM_STEP = 10

VMEM_LIMIT = 56 * 1024 * 1024

_ANY = pl.BlockSpec(memory_space=pl.ANY)


def _tile(n, want):
    t = min(n, want)
    assert n % t == 0, (n, want)
    return t


def _params(sem, vmem=VMEM_LIMIT):
    return pltpu.CompilerParams(dimension_semantics=sem, vmem_limit_bytes=vmem)


def _position():
    return lax.axis_index("x"), lax.axis_index("y"), lax.axis_index("c")


class _Exchange:
    def __init__(self, kind, arrays):
        assert kind in ("gather", "scatter")
        self.kind = kind
        self.arrays = list(arrays)
        self.n = len(self.arrays)
        shapes = [(N_DEV, *a.shape) if kind == "gather" else a.shape for a in self.arrays]
        self.out_shape = [jax.ShapeDtypeStruct(s, a.dtype) for s, a in zip(shapes, self.arrays)]
        self.scratch = [pltpu.SemaphoreType.DMA((self.n, N_DEV)), pltpu.SemaphoreType.DMA((self.n, N_DEV))]

    def _copies(self, ins, outs, send_sems, recv_sems, with_recv=True):
        x, y, c = _position()
        me = 4 * x + 2 * y + c
        gather = self.kind == "gather"
        local, sends, recvs = [], [], []
        for a in range(self.n):
            local.append(pltpu.make_async_copy(ins[a] if gather else ins[a].at[me], outs[a].at[me], send_sems.at[a, 0]))
            for k in range(1, N_DEV):
                px, py, pc = x ^ (k >> 2), y ^ ((k >> 1) & 1), c ^ (k & 1)
                peer = 4 * px + 2 * py + pc
                src = ins[a] if gather else ins[a].at[peer]
                common = dict(send_sem=send_sems.at[a, k], recv_sem=recv_sems.at[a, k], device_id=(px, py, pc),
                              device_id_type=MESH)
                sends.append(pltpu.make_async_remote_copy(src_ref=src, dst_ref=outs[a].at[me], **common))
                if with_recv:
                    recvs.append(pltpu.make_async_remote_copy(src_ref=src, dst_ref=outs[a].at[peer], **common))
        return local, sends, recvs

    def start(self, ins, outs, send_sems, recv_sems):
        local, sends, _ = self._copies(ins, outs, send_sems, recv_sems, with_recv=False)
        for cp in local + sends:
            cp.start()

    def wait(self, ins, outs, send_sems, recv_sems):
        local, sends, recvs = self._copies(ins, outs, send_sems, recv_sems)
        for send, recv in zip(sends, recvs):
            send.wait_send()
            recv.wait_recv()
        for cp in local:
            cp.wait()


class _GatherByChip:
    middle_at = 0.85

    def __init__(self, arrays):
        self.arrays = list(arrays)
        self.n = len(self.arrays)
        self.out_shape = [jax.ShapeDtypeStruct((N_DEV, *a.shape), a.dtype) for a in self.arrays]
        self.scratch = [pltpu.SemaphoreType.DMA((self.n, N_DEV)), pltpu.SemaphoreType.DMA((self.n, N_DEV))]

    def _copy(self, a, k, outs, send_sems, recv_sems, src, slot_of, to):
        x, y, c = _position()
        dev = lambda kk: (x ^ (kk >> 2), y ^ ((kk >> 1) & 1), c ^ (kk & 1))
        px, py, pc = dev(slot_of)
        slot = outs[a].at[4 * px + 2 * py + pc]
        return pltpu.make_async_remote_copy(
            src_ref=slot if src is None else src, dst_ref=slot, send_sem=send_sems.at[a, k], recv_sem=recv_sems.at[a, k],
            device_id=dev(to), device_id_type=MESH)

    def start(self, ins, outs, send_sems, recv_sems):
        x, y, c = _position()
        for a in range(self.n):
            pltpu.make_async_copy(ins[a], outs[a].at[4 * x + 2 * y + c], send_sems.at[a, 0]).start()
            for k in (1, 2, 4, 6):
                self._copy(a, k, outs, send_sems, recv_sems, ins[a], 0, k).start()

    def middle(self, ins, outs, send_sems, recv_sems):
        for a in range(self.n):
            for k in (2, 4, 6):
                self._copy(a, k, outs, send_sems, recv_sems, None, k, 0).wait_recv()
                self._copy(a, k + 1, outs, send_sems, recv_sems, None, k, 1).start()

    def wait(self, ins, outs, send_sems, recv_sems):
        x, y, c = _position()
        for a in range(self.n):
            self._copy(a, 1, outs, send_sems, recv_sems, None, 1, 0).wait_recv()
            for k in (3, 5, 7):
                self._copy(a, k, outs, send_sems, recv_sems, None, k, 0).wait_recv()
            for k in (1, 2, 4, 6):
                self._copy(a, k, outs, send_sems, recv_sems, ins[a], 0, k).wait_send()
            for k in (2, 4, 6):
                self._copy(a, k + 1, outs, send_sems, recv_sems, None, k, 1).wait_send()
            pltpu.make_async_copy(ins[a], outs[a].at[4 * x + 2 * y + c], send_sems.at[a, 0]).wait()


def _call(body, *, name, grid, in_specs, out_specs, out_shape, args, scratch_shapes=(), sem=None, exchange=None):
    if exchange is None:
        outs = pl.pallas_call(
            body, grid=grid, in_specs=in_specs, out_specs=out_specs, out_shape=out_shape,
            scratch_shapes=list(scratch_shapes), compiler_params=_params(sem), name=name)(*args)
        return list(outs), []

    n_in, n_out, n_scr, n = len(in_specs), len(out_specs), len(scratch_shapes), exchange.n

    def hosted(*refs):
        ins, refs = refs[:n_in], refs[n_in:]
        ex_in, refs = refs[:n], refs[n:]
        outs, refs = refs[:n_out], refs[n_out:]
        ex_out, refs = refs[:n], refs[n:]
        scratch, sems = refs[:n_scr], refs[n_scr:]
        first = _all_true([pl.program_id(d) == 0 for d in range(len(grid))])
        last = _all_true([pl.program_id(d) == grid[d] - 1 for d in range(len(grid))])

        @pl.when(first)
        def _():
            exchange.start(ex_in, ex_out, *sems)

        if hasattr(exchange, "middle"):
            assert len(grid) and any(g > 1 for g in grid), "a host of a three-moment exchange needs more than one grid step"
            step, total = 0, 1
            for d in range(len(grid)):
                step = step * grid[d] + pl.program_id(d)
                total *= grid[d]

            @pl.when(step == max(1, min(total - 1, int(total * exchange.middle_at))))
            def _():
                exchange.middle(ex_in, ex_out, *sems)

        body(*ins, *outs, *scratch)

        @pl.when(last)
        def _():
            exchange.wait(ex_in, ex_out, *sems)

    res = pl.pallas_call(
        hosted, grid=grid, in_specs=[*in_specs, *[_ANY] * n], out_specs=[*out_specs, *[_ANY] * n],
        out_shape=[*out_shape, *exchange.out_shape], scratch_shapes=[*scratch_shapes, *exchange.scratch],
        compiler_params=_params(("arbitrary",) * len(grid)), name=name)(*args, *exchange.arrays)
    return list(res[:n_out]), list(res[n_out:])


def _all_true(preds):
    out = preds[0]
    for p in preds[1:]:
        out = jnp.logical_and(out, p)
    return out


def _exchange_now(name, exchange):
    def body(*refs):
        n = exchange.n
        exchange.start(refs[:n], refs[n:2 * n], *refs[2 * n:])
        exchange.wait(refs[:n], refs[n:2 * n], *refs[2 * n:])

    return pl.pallas_call(
        body, in_specs=[_ANY] * exchange.n, out_specs=[_ANY] * exchange.n, out_shape=exchange.out_shape,
        scratch_shapes=exchange.scratch, name=name)(*exchange.arrays)


def _all_gather_two_level(shards):
    n = len(shards)

    def body(*refs):
        ins, outs = refs[:n], refs[n:2 * n]
        send_sems, recv_sems, local_sems = refs[2 * n:]
        x, y, c = _position()
        me, sibling = (x, y, c), (x, y, 1 - c)
        chips = [(1 - x, y), (x, 1 - y), (1 - x, 1 - y)]

        def slot(a, px, py, pc):
            return outs[a].at[4 * px + 2 * py + pc]

        def copy(a, k, block, to, src=None):
            return pltpu.make_async_remote_copy(
                src_ref=slot(a, *block) if src is None else src, dst_ref=slot(a, *block),
                send_sem=send_sems.at[a, k], recv_sem=recv_sems.at[a, k], device_id=to, device_id_type=MESH)

        mine = [pltpu.make_async_copy(ins[a], slot(a, *me), local_sems.at[a]) for a in range(n)]
        first = []
        for a in range(n):
            mine[a].start()
            first.append(copy(a, 0, me, sibling, src=ins[a]))
            first += [copy(a, 1 + j, me, (*chip, c), src=ins[a]) for j, chip in enumerate(chips)]
        for cp in first:
            cp.start()
        passed = []
        for a in range(n):
            for j, chip in enumerate(chips):
                copy(a, 1 + j, (*chip, c), me).wait_recv()
                fwd = copy(a, 4 + j, (*chip, c), sibling)
                fwd.start()
                passed.append(fwd)
        for a in range(n):
            copy(a, 0, sibling, me).wait_recv()
            for j, chip in enumerate(chips):
                copy(a, 4 + j, (*chip, 1 - c), me).wait_recv()
        for cp in first + passed:
            cp.wait_send()
        for cp in mine:
            cp.wait()

    return pl.pallas_call(
        body, in_specs=[_ANY] * n, out_specs=[_ANY] * n,
        out_shape=[jax.ShapeDtypeStruct((N_DEV, *s.shape), s.dtype) for s in shards],
        scratch_shapes=[pltpu.SemaphoreType.DMA((n, 7)), pltpu.SemaphoreType.DMA((n, 7)),
                        pltpu.SemaphoreType.DMA((n,))],
        name="all_gather_first")(*shards)


def _all_reduce_small(vec):
    R = vec.shape[0]

    def body(v_ref, o_ref, buf_ref, send_sems, recv_sems):
        x, y, c = _position()
        me = 4 * x + 2 * y + c
        buf_ref[me] = v_ref[...]
        copies = []
        for k in range(1, N_DEV):
            peer = (x ^ (k >> 2), y ^ ((k >> 1) & 1), c ^ (k & 1))
            copies.append(pltpu.make_async_remote_copy(
                src_ref=buf_ref.at[me], dst_ref=buf_ref.at[me], send_sem=send_sems.at[k], recv_sem=recv_sems.at[k],
                device_id=peer, device_id_type=MESH))
        for cp in copies:
            cp.start()
        for cp in copies:
            cp.wait()
        total = buf_ref[0]
        for d in range(1, N_DEV):
            total = total + buf_ref[d]
        o_ref[...] = total

    vm = pl.BlockSpec(memory_space=pltpu.VMEM)
    return pl.pallas_call(
        body, in_specs=[vm], out_specs=vm, out_shape=jax.ShapeDtypeStruct((R, LANES), F32),
        scratch_shapes=[pltpu.VMEM((N_DEV, R, LANES), F32), pltpu.SemaphoreType.DMA((N_DEV,)),
                        pltpu.SemaphoreType.DMA((N_DEV,))],
        name="all_reduce_small")(vec)


_DIMS = {
    "nn": (((1,), (0,)), ((), ())),
    "nt": (((1,), (1,)), ((), ())),
    "tn": (((0,), (0,)), ((), ())),
}


def _matmul(name, a, b, *, mode, grid, tm, tn, a_spec, b_spec, outs, extras=(), extra_specs=(), epilogue=None,
            acc_outs=(), exchange=None):
    nk = grid[2]
    n_extra = len(extras)
    dn = _DIMS[mode]

    def body(a_ref, b_ref, *rest):
        extra_refs = rest[:n_extra]
        out_refs = rest[n_extra:-1]
        acc_ref = rest[-1]
        j = pl.program_id(1)
        k = pl.program_id(2)

        def product():
            b = b_ref[...]
            if b.ndim == 3:
                b = b.reshape(-1, b.shape[-1]) if mode == "nn" else jnp.concatenate([b[0], b[1]], axis=1)
            return lax.dot_general(a_ref[...].astype(BF16), b.astype(BF16), dn, preferred_element_type=F32)

        def finish(acc):
            vals = (acc,) if epilogue is None else epilogue(acc, *[r[...] for r in extra_refs])
            for idx, (o_ref, val) in enumerate(zip(out_refs, vals)):
                if idx in acc_outs:
                    @pl.when(j == 0)
                    def _():
                        o_ref[...] = val.astype(o_ref.dtype)

                    @pl.when(j > 0)
                    def _():
                        o_ref[...] += val.astype(o_ref.dtype)
                else:
                    o_ref[...] = val.astype(o_ref.dtype)

        if nk == 1:
            finish(product())
            return

        @pl.when(k == 0)
        def _():
            acc_ref[...] = product()

        @pl.when(k > 0)
        def _():
            acc_ref[...] += product()

        @pl.when(k == nk - 1)
        def _():
            finish(acc_ref[...])

    return _call(
        body, name=name, grid=grid, in_specs=[a_spec, b_spec, *extra_specs], out_specs=[o[2] for o in outs],
        out_shape=[jax.ShapeDtypeStruct(o[0], o[1]) for o in outs], scratch_shapes=[pltpu.VMEM((tm, tn), F32)],
        sem=("parallel", "arbitrary" if acc_outs else "parallel", "arbitrary"), args=(a, b, *extras), exchange=exchange)


def _ij(tm, tn):
    return pl.BlockSpec((tm, tn), lambda i, j, k: (i, j))


def _mm_nn(name, a, b, *, out_dtype, residual=None, out_scale=None, exchange=None, tm=1024, tn=1024, tk=2048):
    M, K = a.shape
    N = b.shape[1]
    tm, tn, tk = _tile(M, tm), _tile(N, tn), _tile(K, tk)
    extras, especs, epi = (), (), None
    if residual is not None:
        extras, especs = (residual,), (_ij(tm, tn),)
        epi = lambda acc, r: (acc + r,)
    elif out_scale is not None:
        epi = lambda acc: (acc * out_scale,)
    outs, got = _matmul(
        name, a, b, mode="nn", grid=(M // tm, N // tn, K // tk), tm=tm, tn=tn,
        a_spec=pl.BlockSpec((tm, tk), lambda i, j, k: (i, k)),
        b_spec=pl.BlockSpec((tk, tn), lambda i, j, k: (k, j)),
        outs=[((M, N), out_dtype, _ij(tm, tn))], extras=extras, extra_specs=especs, epilogue=epi, exchange=exchange)
    return outs[0] if exchange is None else (outs[0], got)


def _mm_nt(name, a, b, *, out_dtype, b_cols=None, residual=None, tm=1024, tn=1024, tk=2048):
    M, K = a.shape
    N = b.shape[0]
    c0 = 0 if b_cols is None else b_cols[0]
    tm, tn, tk = _tile(M, tm), _tile(N, tn), _tile(K, tk)
    assert c0 % tk == 0
    kb0 = c0 // tk
    extras, especs, epi = (), (), None
    if residual is not None:
        extras, especs = (residual,), (_ij(tm, tn),)
        epi = lambda acc, r: (acc + r,)
    return _matmul(
        name, a, b, mode="nt", grid=(M // tm, N // tn, K // tk), tm=tm, tn=tn,
        a_spec=pl.BlockSpec((tm, tk), lambda i, j, k: (i, k)),
        b_spec=pl.BlockSpec((tn, tk), lambda i, j, k: (j, kb0 + k)),
        outs=[((M, N), out_dtype, _ij(tm, tn))], extras=extras, extra_specs=especs, epilogue=epi)[0][0]


def _mm_tn(name, a, b, *, exchange=None, tm=1024, tn=1024, tk=2048):
    K, M = a.shape
    N = b.shape[1]
    tm, tn, tk = _tile(M, tm), _tile(N, tn), _tile(K, tk)
    outs, got = _matmul(
        name, a, b, mode="tn", grid=(M // tm, N // tn, K // tk), tm=tm, tn=tn,
        a_spec=pl.BlockSpec((tk, tm), lambda i, j, k: (k, i)),
        b_spec=pl.BlockSpec((tk, tn), lambda i, j, k: (k, j)),
        outs=[((M, N), BF16, _ij(tm, tn))], exchange=exchange)
    return outs[0], got


def _kvf_dx(dk, dv, d_f, w_kv, w_f, *, tm=1024, tn=1024):
    S, D = dk.shape
    tm, tn = _tile(S, tm), _tile(D, tn)

    def body(dk_ref, dv_ref, df_ref, wk_ref, wv_ref, wf_ref, o_ref):
        acc = lax.dot_general(dk_ref[...], wk_ref[...], _DIMS["nt"], preferred_element_type=F32)
        acc = acc + lax.dot_general(dv_ref[...], wv_ref[...], _DIMS["nt"], preferred_element_type=F32)
        o_ref[...] = acc + lax.dot_general(df_ref[...].astype(BF16), wf_ref[...], _DIMS["nt"],
                                           preferred_element_type=F32)

    row = lambda width: pl.BlockSpec((tm, width), lambda i, j: (i, 0))
    return pl.pallas_call(
        body, grid=(S // tm, D // tn),
        in_specs=[row(D), row(D), row(LANES),
                  pl.BlockSpec((tn, D), lambda i, j: (j, 0)), pl.BlockSpec((tn, D), lambda i, j: (j, 1)),
                  pl.BlockSpec((tn, LANES), lambda i, j: (j, 0))],
        out_specs=pl.BlockSpec((tm, tn), lambda i, j: (i, j)), out_shape=jax.ShapeDtypeStruct((S, D), F32),
        compiler_params=_params(("parallel", "parallel")), name="kvf_proj_dx")(dk, dv, d_f, w_kv, w_kv, w_f)


def _mlp_fwd(tag, hn, h, w_up_g, w_down_g, *, ex_up=None, ex_down=None, tm=1024, tk=2048):
    S, D = hn.shape
    fb = w_up_g.shape[2]
    F = N_DEV * fb
    tm, tku = _tile(S, tm), _tile(D, tk)

    def up_epi(acc):
        u = jnp.maximum(acc, 0.0)
        return u, u * u

    (u, uu), got_up = _matmul(
        f"mlp_up_{tag}", hn, w_up_g, mode="nn", grid=(S // tm, N_DEV, D // tku), tm=tm, tn=fb,
        a_spec=pl.BlockSpec((tm, tku), lambda i, j, k: (i, k)),
        b_spec=pl.BlockSpec((None, tku, fb), lambda i, j, k: (j, k, 0)),
        outs=[((S, F), BF16, _ij(tm, fb)), ((S, F), BF16, _ij(tm, fb))], epilogue=up_epi, exchange=ex_up)
    if w_down_g is None:
        w_down_g = got_up[0]

    tn = _tile(D, 1024)
    (h_out,), got_down = _matmul(
        f"mlp_down_{tag}", uu, w_down_g, mode="nn", grid=(S // tm, D // tn, N_DEV // 2), tm=tm, tn=tn,
        a_spec=pl.BlockSpec((tm, 2 * fb), lambda i, j, k: (i, k)),
        b_spec=pl.BlockSpec((2, fb, tn), lambda i, j, k: (k, 0, j)),
        outs=[((S, D), F32, _ij(tm, tn))], extras=(h,), extra_specs=(_ij(tm, tn),),
        epilogue=lambda acc, res: (acc + res,), exchange=ex_down)
    return u, uu, h_out, got_up, got_down


def _mlp_dpre(tag, dh, u, w_down_g, *, tm=1024, tk=2048):
    S, D = dh.shape
    fb = w_down_g.shape[1]
    tm, tkd = _tile(S, tm), _tile(D, tk)
    return _matmul(
        f"mlp_dpre_{tag}", dh, w_down_g, mode="nt", grid=(S // tm, N_DEV, D // tkd), tm=tm, tn=fb,
        a_spec=pl.BlockSpec((tm, tkd), lambda i, j, k: (i, k)),
        b_spec=pl.BlockSpec((None, fb, tkd), lambda i, j, k: (j, 0, k)),
        outs=[((S, N_DEV * fb), BF16, _ij(tm, fb))], extras=(u,), extra_specs=(_ij(tm, fb),),
        epilogue=lambda acc, uv: (acc * (2.0 * uv.astype(F32)),))[0][0]


def _mlp_dwup(tag, hn, d_pre, fb, *, exchange=None):
    S, D = hn.shape
    tmu = _tile(D, 1024)
    tks = _tile(S, 2048)
    outs, got = _matmul(
        f"mlp_dwup_{tag}", hn, d_pre, mode="tn", grid=(D // tmu, N_DEV, S // tks), tm=tmu, tn=fb,
        a_spec=pl.BlockSpec((tks, tmu), lambda i, j, k: (k, i)),
        b_spec=pl.BlockSpec((tks, fb), lambda i, j, k: (k, j)),
        outs=[((N_DEV, D, fb), BF16, pl.BlockSpec((None, tmu, fb), lambda i, j, k: (j, i, 0)))], exchange=exchange)
    return outs[0], got


def _mlp_dhn(tag, d_pre, w_up_g, *, exchange=None, tm=1024):
    S, F = d_pre.shape
    D, fb = w_up_g.shape[1], w_up_g.shape[2]
    tm, tn = _tile(S, tm), _tile(D, 1024)
    outs, got = _matmul(
        f"mlp_dhn_{tag}", d_pre, w_up_g, mode="nt", grid=(S // tm, D // tn, N_DEV // 2), tm=tm, tn=tn,
        a_spec=pl.BlockSpec((tm, 2 * fb), lambda i, j, k: (i, k)),
        b_spec=pl.BlockSpec((2, tn, fb), lambda i, j, k: (k, j, 0)),
        outs=[((S, D), F32, _ij(tm, tn))], exchange=exchange)
    return outs[0], got


def _row_spec(ts, D):
    return pl.BlockSpec((ts, D), lambda i: (i, 0))


def _vec_spec(n, D):
    return pl.BlockSpec((n, D), lambda i: (0, 0))


def _rms_fwd(name, x, gains, *, ts=512):
    S, D = x.shape
    n = gains.shape[0]
    ts = _tile(S, ts)

    def body(x_ref, g_ref, *o_refs):
        xv = x_ref[...]
        y = xv * lax.rsqrt(jnp.mean(xv * xv, axis=-1, keepdims=True) + EPS)
        for i, o_ref in enumerate(o_refs):
            o_ref[...] = (y * g_ref[i:i + 1, :]).astype(o_ref.dtype)

    return pl.pallas_call(
        body, grid=(S // ts,), in_specs=[_row_spec(ts, D), _vec_spec(n, D)],
        out_specs=[_row_spec(ts, D)] * n, out_shape=[jax.ShapeDtypeStruct((S, D), BF16)] * n,
        compiler_params=_params(("parallel",)), name=name)(x, gains)


def _rms_bwd(name, x, gains, dys, res, *, matmul_copy=True):
    S, D = x.shape
    n = gains.shape[0]
    ts = _tile(S, 512 if n == 1 else 256)

    def body(x_ref, g_ref, *rest):
        dy_refs = rest[:n]
        res_ref, dx_ref = rest[n:n + 2]
        dxb_ref = rest[n + 2] if matmul_copy else None
        dg_ref = rest[-1]
        i = pl.program_id(0)
        xv = x_ref[...]
        r = lax.rsqrt(jnp.mean(xv * xv, axis=-1, keepdims=True) + EPS)
        xhat = xv * r
        dxhat = None
        dgs = []
        for k in range(n):
            dy = dy_refs[k][...].astype(F32)
            dgs.append(jnp.sum(dy * xhat, axis=0, keepdims=True))
            term = dy * g_ref[k:k + 1, :]
            dxhat = term if dxhat is None else dxhat + term
        dx = res_ref[...] + r * (dxhat - xhat * jnp.mean(dxhat * xhat, axis=-1, keepdims=True))
        dx_ref[...] = dx
        if matmul_copy:
            dxb_ref[...] = dx.astype(BF16)
        dg = jnp.concatenate(dgs, axis=0) if n > 1 else dgs[0]

        @pl.when(i == 0)
        def _():
            dg_ref[...] = dg

        @pl.when(i > 0)
        def _():
            dg_ref[...] += dg

    copies = [(_row_spec(ts, D), jax.ShapeDtypeStruct((S, D), BF16))] if matmul_copy else []
    outs = pl.pallas_call(
        body, grid=(S // ts,),
        in_specs=[_row_spec(ts, D), _vec_spec(n, D)] + [_row_spec(ts, D)] * (n + 1),
        out_specs=[_row_spec(ts, D), *[c[0] for c in copies], _vec_spec(n, D)],
        out_shape=[jax.ShapeDtypeStruct((S, D), F32), *[c[1] for c in copies], jax.ShapeDtypeStruct((n, D), F32)],
        compiler_params=_params(("arbitrary",)), name=name)(x, gains, *dys, res)
    return (outs[0], outs[1], outs[2]) if matmul_copy else (outs[0], None, outs[1])


def _loss_head(h, gain, target, *, ts=512):
    S, D = h.shape
    ts = _tile(S, ts)

    def body(x_ref, g_ref, t_ref, dx_ref, dxb_ref, dg_ref, loss_ref):
        i = pl.program_id(0)
        xv = x_ref[...]
        g = g_ref[...]
        r = lax.rsqrt(jnp.mean(xv * xv, axis=-1, keepdims=True) + EPS)
        xhat = xv * r
        err = xhat * g - t_ref[...]
        part = 0.5 * jnp.sum(jnp.mean(err * err, axis=-1, keepdims=True), axis=0, keepdims=True)
        dy = err * (1.0 / D)
        dg = jnp.sum(dy * xhat, axis=0, keepdims=True)
        dxhat = dy * g
        dx = r * (dxhat - xhat * jnp.mean(dxhat * xhat, axis=-1, keepdims=True))
        dx_ref[...] = dx
        dxb_ref[...] = dx.astype(BF16)

        @pl.when(i == 0)
        def _():
            dg_ref[...] = dg
            loss_ref[...] = jnp.broadcast_to(part, loss_ref.shape)

        @pl.when(i > 0)
        def _():
            dg_ref[...] += dg
            loss_ref[...] += jnp.broadcast_to(part, loss_ref.shape)

    return pl.pallas_call(
        body, grid=(S // ts,),
        in_specs=[_row_spec(ts, D), _vec_spec(1, D), _row_spec(ts, D)],
        out_specs=[_row_spec(ts, D), _row_spec(ts, D), _vec_spec(1, D), pl.BlockSpec((8, LANES), lambda i: (0, 0))],
        out_shape=[jax.ShapeDtypeStruct((S, D), F32), jax.ShapeDtypeStruct((S, D), BF16),
                   jax.ShapeDtypeStruct((1, D), F32), jax.ShapeDtypeStruct((8, LANES), F32)],
        compiler_params=_params(("arbitrary",)), name="loss_head")(h, gain, target)


def _window_counts(t, w):
    return jnp.minimum(t + 1, w).astype(F32)


def _pool_fwd(x, gain, pool_w, scale, gain_next, *, exchange=None, ts=256):
    S, D = x.shape
    dg = D // N_GROUPS
    ts = _tile(S, ts)
    per = ts // POOL_HALO

    def body(x_ref, xh_ref, g_ref, w_ref, sc_ref, gn_ref, h_ref, diff_ref, hn_ref):
        i = pl.program_id(0)
        g = g_ref[...]
        h_parts = []

        def norm(v):
            return v * lax.rsqrt(jnp.mean(v * v, axis=-1, keepdims=True) + EPS) * g

        xc = x_ref[...]
        hn_c = norm(xc)
        hn_h = norm(xh_ref[...]) * (i > 0).astype(F32)
        ext = jnp.concatenate([hn_h, hn_c], axis=0)
        t = i * ts + lax.broadcasted_iota(jnp.int32, (ts, 1), 0)
        for gi, w in enumerate(POOL_WINDOWS):
            cols = slice(gi * dg, (gi + 1) * dg)
            s = ext[:, cols]
            step = 1
            while step < w:
                s = s + pltpu.roll(s, step, 0)
                step *= 2
            mean = s[POOL_HALO:] * (1.0 / _window_counts(t, w))
            diff = (mean - hn_c[:, cols]).astype(BF16)
            diff_ref[:, cols] = diff
            mixed = jnp.dot(diff, w_ref[gi], preferred_element_type=F32)
            h_parts.append(xc[:, cols] + mixed * sc_ref[:, cols])
        h = jnp.concatenate(h_parts, axis=1)
        h_ref[...] = h
        hn_ref[...] = (h * lax.rsqrt(jnp.mean(h * h, axis=-1, keepdims=True) + EPS) * gn_ref[...]).astype(BF16)

    (h, diff, hn), got = _call(
        body, name="pool_fwd", grid=(S // ts,),
        in_specs=[_row_spec(ts, D),
                  pl.BlockSpec((POOL_HALO, D), lambda i: (jnp.maximum(i * per - 1, 0), 0)),
                  _vec_spec(1, D), pl.BlockSpec((N_GROUPS, dg, dg), lambda i: (0, 0, 0)), _vec_spec(1, D),
                  _vec_spec(1, D)],
        out_specs=[_row_spec(ts, D), _row_spec(ts, D), _row_spec(ts, D)],
        out_shape=[jax.ShapeDtypeStruct((S, D), F32), jax.ShapeDtypeStruct((S, D), BF16),
                   jax.ShapeDtypeStruct((S, D), BF16)],
        sem=("parallel",), args=(x, x, gain, pool_w, scale, gain_next), exchange=exchange)
    return h, diff, hn, got


def _pool_bwd(x, dh, diff, gain, pool_w, scale, *, ts=512):
    S, D = x.shape
    dg = D // N_GROUPS
    ts = _tile(S, ts)
    per = ts // POOL_HALO
    n_tiles = S // ts
    n_halo = S // POOL_HALO
    ext_rows = ts + POOL_HALO

    def body(x_ref, dh_ref, dhn_ref, diff_ref, g_ref, w_ref, sc_ref, dx_ref, dw_ref, dsc_ref, dgain_ref):
        i = pl.program_id(0)
        xc = x_ref[...]
        g = g_ref[...]
        r = lax.rsqrt(jnp.mean(xc * xc, axis=-1, keepdims=True) + EPS)
        xhat = xc * r
        dh_c = dh_ref[...]
        dh_n = dhn_ref[...] * (i < n_tiles - 1).astype(F32)
        dh_ext = jnp.concatenate([dh_c, dh_n], axis=0)
        t_ext = i * ts + lax.broadcasted_iota(jnp.int32, (ext_rows, 1), 0)
        d_hn_parts, dsc_parts = [], []
        for gi, w in enumerate(POOL_WINDOWS):
            cols = slice(gi * dg, (gi + 1) * dg)
            wg = w_ref[gi]
            dmix = (dh_ext[:, cols] * sc_ref[:, cols]).astype(BF16)
            d_diff = lax.dot_general(dmix, wg, _DIMS["nt"], preferred_element_type=F32)
            diff_c = diff_ref[:, cols]
            dwg = lax.dot_general(diff_c, dmix[:ts], _DIMS["tn"], preferred_element_type=F32)
            mixed = jnp.dot(diff_c, wg, preferred_element_type=F32)
            dsc_parts.append(jnp.sum(dh_c[:, cols] * mixed, axis=0, keepdims=True))
            e = d_diff * (1.0 / _window_counts(t_ext, w))
            step = 1
            while step < w:
                e = e + pltpu.roll(e, ext_rows - step, 0)
                step *= 2
            d_hn_parts.append(e[:ts] - d_diff[:ts])

            @pl.when(i == 0)
            def _():
                dw_ref[gi] = dwg

            @pl.when(i > 0)
            def _():
                dw_ref[gi] += dwg

        d_hn = jnp.concatenate(d_hn_parts, axis=1)
        dsc = jnp.concatenate(dsc_parts, axis=1)
        dgain = jnp.sum(d_hn * xhat, axis=0, keepdims=True)
        dxhat = d_hn * g
        dx_ref[...] = dh_c + r * (dxhat - xhat * jnp.mean(dxhat * xhat, axis=-1, keepdims=True))

        @pl.when(i == 0)
        def _():
            dsc_ref[...] = dsc
            dgain_ref[...] = dgain

        @pl.when(i > 0)
        def _():
            dsc_ref[...] += dsc
            dgain_ref[...] += dgain

    return pl.pallas_call(
        body, grid=(n_tiles,),
        in_specs=[_row_spec(ts, D), _row_spec(ts, D),
                  pl.BlockSpec((POOL_HALO, D), lambda i: (jnp.minimum((i + 1) * per, n_halo - 1), 0)),
                  _row_spec(ts, D), _vec_spec(1, D),
                  pl.BlockSpec((N_GROUPS, dg, dg), lambda i: (0, 0, 0)), _vec_spec(1, D)],
        out_specs=[_row_spec(ts, D), pl.BlockSpec((N_GROUPS, dg, dg), lambda i: (0, 0, 0)),
                   _vec_spec(1, D), _vec_spec(1, D)],
        out_shape=[jax.ShapeDtypeStruct((S, D), F32), jax.ShapeDtypeStruct((N_GROUPS, dg, dg), F32),
                   jax.ShapeDtypeStruct((1, D), F32), jax.ShapeDtypeStruct((1, D), F32)],
        compiler_params=_params(("arbitrary",)), name="pool_bwd")(x, dh, dh, diff, gain, pool_w, scale)


def _gate_fwd(f_raw, b_pad, *, ts=512):
    S = f_raw.shape[0]
    ts = _tile(S, ts)

    def body(f_ref, b_ref, c_ref, carry_ref):
        i = pl.program_id(0)

        @pl.when(i == 0)
        def _():
            carry_ref[...] = jnp.zeros_like(carry_ref)

        z = f_ref[...] + b_ref[...]
        v = jnp.minimum(z, 0.0) - jnp.log1p(jnp.exp(-jnp.abs(z)))
        row = lax.broadcasted_iota(jnp.int32, (ts, 1), 0)
        step = 1
        while step < ts:
            v = v + jnp.where(row >= step, pltpu.roll(v, step, 0), 0.0)
            step *= 2
        out = v + carry_ref[0:1, :]
        c_ref[...] = out
        carry_ref[...] = jnp.broadcast_to(out[ts - 1:ts, :], carry_ref.shape)

    return pl.pallas_call(
        body, grid=(S // ts,),
        in_specs=[pl.BlockSpec((ts, LANES), lambda i: (i, 0)), pl.BlockSpec((1, LANES), lambda i: (0, 0))],
        out_specs=pl.BlockSpec((ts, LANES), lambda i: (i, 0)),
        out_shape=jax.ShapeDtypeStruct((S, LANES), F32),
        scratch_shapes=[pltpu.VMEM((8, LANES), F32)],
        compiler_params=_params(("arbitrary",)), name="gate_fwd")(f_raw, b_pad)


def _gate_bwd(dc, f_raw, b_pad, *, ts=512):
    S = f_raw.shape[0]
    ts = _tile(S, ts)
    n = S // ts

    def body(dc_ref, f_ref, b_ref, df_ref, db_ref, carry_ref):
        i = pl.program_id(0)

        @pl.when(i == 0)
        def _():
            carry_ref[...] = jnp.zeros_like(carry_ref)

        v = dc_ref[...]
        row = lax.broadcasted_iota(jnp.int32, (ts, 1), 0)
        step = 1
        while step < ts:
            v = v + jnp.where(row < ts - step, pltpu.roll(v, ts - step, 0), 0.0)
            step *= 2
        d_logf = v + carry_ref[0:1, :]
        carry_ref[...] = jnp.broadcast_to(d_logf[0:1, :], carry_ref.shape)
        z = f_ref[...] + b_ref[...]
        df = d_logf / (1.0 + jnp.exp(z))
        df_ref[...] = df
        db = jnp.sum(df, axis=0, keepdims=True)

        @pl.when(i == 0)
        def _():
            db_ref[...] = db

        @pl.when(i > 0)
        def _():
            db_ref[...] += db

    rev = lambda i: (n - 1 - i, 0)
    return pl.pallas_call(
        body, grid=(n,),
        in_specs=[pl.BlockSpec((ts, LANES), rev), pl.BlockSpec((ts, LANES), rev),
                  pl.BlockSpec((1, LANES), lambda i: (0, 0))],
        out_specs=[pl.BlockSpec((ts, LANES), rev), pl.BlockSpec((1, LANES), lambda i: (0, 0))],
        out_shape=[jax.ShapeDtypeStruct((S, LANES), F32), jax.ShapeDtypeStruct((1, LANES), F32)],
        scratch_shapes=[pltpu.VMEM((8, LANES), F32)],
        compiler_params=_params(("arbitrary",)), name="gate_bwd")(dc, f_raw, b_pad)


def _row_layout(a_hs, t):
    n_heads, S = a_hs.shape
    return a_hs.reshape(n_heads, S // t, 1, t)


def _lane_pick(blk, h):
    lane = lax.broadcasted_iota(jnp.int32, blk.shape, 1)
    return jnp.sum(jnp.where(lane == h, blk, 0.0), axis=-1, keepdims=True)


def _lane_put(ref, h, col, first=True):
    lane = lax.broadcasted_iota(jnp.int32, ref.shape, 1)
    if not first:
        ref[...] = jnp.where(lane == h, col, ref[...])
        return

    @pl.when(h == 0)
    def _():
        ref[...] = jnp.where(lane == 0, col, 0.0)

    @pl.when(h > 0)
    def _():
        ref[...] = jnp.where(lane == h, col, ref[...])


def _causal(s, masked, fill, rows_are_queries=True):
    if not masked:
        return s
    rr = lax.broadcasted_iota(jnp.int32, s.shape, 0)
    cc = lax.broadcasted_iota(jnp.int32, s.shape, 1)
    keep = (cc <= rr) if rows_are_queries else (rr <= cc)
    return jnp.where(keep, s, fill)


def _fox_fwd(q2, kv, c_row, *, t=512, exchange=None):
    S, D = q2.shape
    H = D // HEAD_DIM
    t = _tile(S, t)
    nb = S // t
    hp = HEADS_PER_STEP
    wide = hp * HEAD_DIM

    def body(q_ref, k_ref, v_ref, cr_ref, o_ref, l2_ref):
        i = pl.program_id(0)
        g = pl.program_id(1)
        cols = [slice(a * HEAD_DIM, (a + 1) * HEAD_DIM) for a in range(hp)]
        refs_i = [cr_ref[a, i][:, 0:1] for a in range(hp)]
        qs = [q_ref[:, cols[a]] for a in range(hp)]

        def step(j, carry, masked):
            r0 = pl.multiple_of(j * t, t)
            out = []
            for a in range(hp):
                m, l, acc = carry[3 * a:3 * a + 3]
                kb = k_ref[pl.ds(r0, t), cols[a]]
                vb = v_ref[pl.ds(r0, t), cols[a]]
                ck = cr_ref[a, j] - refs_i[a]
                s = lax.dot_general(qs[a], kb, _DIMS["nt"], preferred_element_type=F32) - ck
                s = _causal(s, masked, NEG_INF)
                m_new = jnp.maximum(m, jnp.max(s, axis=-1, keepdims=True))
                alpha = jnp.exp2(m - m_new)
                p = jnp.exp2(s - m_new)
                l = alpha * l + jnp.sum(p, axis=-1, keepdims=True)
                acc = alpha * acc + jnp.dot(p.astype(BF16), vb, preferred_element_type=F32)
                out += [m_new, l, acc]
            return tuple(out)

        init = (jnp.full((t, 1), NEG_INF, F32), jnp.zeros((t, 1), F32), jnp.zeros((t, HEAD_DIM), F32)) * hp
        carry = lax.fori_loop(0, i, lambda j, c: step(j, c, False), init)
        carry = step(i, carry, True)
        for a in range(hp):
            m, l, acc = carry[3 * a:3 * a + 3]
            o_ref[:, cols[a]] = (acc / l).astype(o_ref.dtype)
            _lane_put(l2_ref, g * hp + a, m + jnp.log2(l), first=(a == 0))

    (o, l2_sh), got = _call(
        body, name="fox_fwd", grid=(nb, H // hp),
        in_specs=[pl.BlockSpec((t, wide), lambda i, g: (i, g)),
                  pl.BlockSpec((S, wide), lambda i, g: (0, g)),
                  pl.BlockSpec((S, wide), lambda i, g: (0, H // hp + g)),
                  pl.BlockSpec((hp, nb, 1, t), lambda i, g: (g, 0, 0, 0))],
        out_specs=[pl.BlockSpec((t, wide), lambda i, g: (i, g)),
                   pl.BlockSpec((t, LANES), lambda i, g: (i, 0))],
        out_shape=[jax.ShapeDtypeStruct((S, D), BF16), jax.ShapeDtypeStruct((S, LANES), F32)],
        sem=("parallel", "arbitrary"), args=(q2, kv, kv, c_row), exchange=exchange)
    return o, l2_sh, got


def _o_proj_dx(dh, wo, o, *, tm=1024, tn=1024):
    S, D = dh.shape
    tm, tn = _tile(S, tm), _tile(D, tn)
    heads = tn // HEAD_DIM

    def epilogue(acc, ov):
        j = pl.program_id(1)
        lane = lax.broadcasted_iota(jnp.int32, (tm, LANES), 1)
        prod = acc * ov.astype(F32)
        blk = jnp.zeros((tm, LANES), F32)
        for hh in range(heads):
            col = jnp.sum(prod[:, hh * HEAD_DIM:(hh + 1) * HEAD_DIM], axis=-1, keepdims=True)
            blk = jnp.where(lane == j * heads + hh, col, blk)
        return acc, blk

    (do, delta_sh), _ = _matmul(
        "o_proj_dx", dh, wo, mode="nt", grid=(S // tm, D // tn, 1), tm=tm, tn=tn,
        a_spec=pl.BlockSpec((tm, D), lambda i, j, k: (i, 0)),
        b_spec=pl.BlockSpec((tn, D), lambda i, j, k: (j, 0)),
        outs=[((S, D), BF16, _ij(tm, tn)), ((S, LANES), F32, pl.BlockSpec((tm, LANES), lambda i, j, k: (i, 0)))],
        extras=(o,), extra_specs=(_ij(tm, tn),), epilogue=epilogue, acc_outs=(1,))
    return do, delta_sh


def _fox_bwd(q2, kv, do, c_sh, c_row, l2_row, delta_row, *, t=512, exchange=None):
    S, D = q2.shape
    H = D // HEAD_DIM
    t = _tile(S, t)
    nb = S // t
    scale = HEAD_DIM ** -0.5

    def body(q_ref, do_ref, k_ref, v_ref, csh_ref, cr_ref, l2_ref, dl_ref, dq_ref, dk_ref, dv_ref, dck_ref, dcq_ref,
             dq_acc):
        h = pl.program_id(0)
        j = pl.program_id(1)
        ck = _lane_pick(csh_ref[...], h)
        kb = k_ref[...]
        vb = v_ref[...]

        @pl.when(j == 0)
        def _():
            dq_acc[...] = jnp.zeros_like(dq_acc)
            dcq_ref[...] = jnp.zeros_like(dcq_ref)

        def step(i, carry, masked):
            dk, dv, dc = carry
            r0 = pl.multiple_of(i * t, t)
            qb = q_ref[pl.ds(r0, t), :]
            dob = do_ref[pl.ds(r0, t), :]
            s = lax.dot_general(kb, qb, _DIMS["nt"], preferred_element_type=F32) - (ck - cr_ref[i][:, 0:1])
            p = _causal(jnp.exp2(s - l2_ref[i]), masked, 0.0, rows_are_queries=False)
            dv = dv + jnp.dot(p.astype(BF16), dob, preferred_element_type=F32)
            dp = lax.dot_general(vb, dob, _DIMS["nt"], preferred_element_type=F32)
            ds = p * (dp - dl_ref[i])
            dc = dc - jnp.sum(ds, axis=-1, keepdims=True)
            dcq_ref[i] += jnp.sum(ds, axis=0, keepdims=True)
            dsb = ds.astype(BF16)
            dk = dk + jnp.dot(dsb, qb, preferred_element_type=F32)
            dq_acc[pl.ds(r0, t), :] += lax.dot_general(dsb, kb, _DIMS["tn"], preferred_element_type=F32)
            return dk, dv, dc

        init = (jnp.zeros((t, HEAD_DIM), F32), jnp.zeros((t, HEAD_DIM), F32), jnp.zeros((t, 1), F32))
        carry = step(j, init, True)
        dk, dv, dc = lax.fori_loop(j + 1, nb, lambda i, c: step(i, c, False), carry)
        dk_ref[...] = (dk * LN2).astype(dk_ref.dtype)
        dv_ref[...] = dv.astype(dv_ref.dtype)
        rows = pl.ds(pl.multiple_of(j * t, t), t)
        lane = lax.broadcasted_iota(jnp.int32, (t, LANES), 1)

        @pl.when(h == 0)
        def _():
            dck_ref[rows, :] = jnp.where(lane == 0, dc, 0.0)

        @pl.when(h > 0)
        def _():
            dck_ref[rows, :] = jnp.where(lane == h, dc, dck_ref[rows, :])

        @pl.when(j == nb - 1)
        def _():
            dq_ref[...] = (dq_acc[...] * scale).astype(dq_ref.dtype)

    kspec = pl.BlockSpec((t, HEAD_DIM), lambda h, j: (j, h))
    headspec = pl.BlockSpec((S, HEAD_DIM), lambda h, j: (0, h))
    rowspec = pl.BlockSpec((None, nb, 1, t), lambda h, j: (h, 0, 0, 0))
    (dq, dk, dv, dck_sh, dcq_row), got = _call(
        body, name="fox_bwd", grid=(H, nb),
        in_specs=[headspec, headspec, kspec,
                  pl.BlockSpec((t, HEAD_DIM), lambda h, j: (j, H + h)),
                  pl.BlockSpec((t, LANES), lambda h, j: (j, 0)),
                  rowspec, rowspec, rowspec],
        out_specs=[headspec, kspec, kspec, pl.BlockSpec((S, LANES), lambda h, j: (0, 0)), rowspec],
        out_shape=[jax.ShapeDtypeStruct((S, D), BF16), jax.ShapeDtypeStruct((S, D), BF16),
                   jax.ShapeDtypeStruct((S, D), BF16), jax.ShapeDtypeStruct((S, LANES), F32),
                   jax.ShapeDtypeStruct((H, nb, 1, t), F32)],
        scratch_shapes=[pltpu.VMEM((S, HEAD_DIM), F32)],
        sem=("arbitrary", "arbitrary"), args=(q2, do, kv, kv, c_sh, c_row, l2_row, delta_row), exchange=exchange)
    return dq, dk, dv, dck_sh, dcq_row, got


def _adamw(name, parts, w, m, v, layer=None, into=None, *, tr=128):
    P, R, C = parts.shape
    tr = _tile(R, tr)

    def body(p_ref, w_ref, m_ref, v_ref, *rest):
        g_ref, d_ref, nm_ref, nv_ref = rest[-4:]
        g = p_ref[0].astype(F32)
        for k in range(1, P):
            g = g + p_ref[k].astype(F32)
        wv = w_ref[...]
        nm = ADAM_B1 * m_ref[...] + (1.0 - ADAM_B1) * g
        nv = ADAM_B2 * v_ref[...] + (1.0 - ADAM_B2) * (g * g)
        m_hat = nm / (1.0 - ADAM_B1 ** ADAM_STEP)
        v_hat = nv / (1.0 - ADAM_B2 ** ADAM_STEP)
        g_ref[...] = g
        d_ref[...] = -ADAM_LR * (m_hat / (jnp.sqrt(v_hat) + ADAM_EPS) + ADAM_WD * wv)
        nm_ref[...] = nm
        nv_ref[...] = nv

    pspec = pl.BlockSpec((P, tr, C), lambda i: (0, i, 0))
    if layer is None:
        wspec = pl.BlockSpec((tr, C), lambda i: (i, 0))
        return pl.pallas_call(
            body, grid=(R // tr,), in_specs=[pspec, wspec, wspec, wspec],
            out_specs=[wspec] * 4, out_shape=[jax.ShapeDtypeStruct((R, C), F32)] * 4,
            compiler_params=_params(("parallel",)), name=name)(parts, w, m, v)
    wspec = pl.BlockSpec((None, tr, C), lambda i: (layer, i, 0))
    prev = [] if into is None else list(into)
    return pl.pallas_call(
        body, grid=(R // tr,), in_specs=[pspec, wspec, wspec, wspec] + [_ANY] * len(prev),
        out_specs=[wspec] * 4, out_shape=[jax.ShapeDtypeStruct(w.shape, F32)] * 4,
        input_output_aliases={4 + k: k for k in range(len(prev))},
        compiler_params=_params(("parallel",)), name=name)(parts, w, m, v, *prev)


def _rows(a):
    flat = a.reshape(-1)
    pad = (-flat.shape[0]) % LANES
    if pad:
        flat = jnp.pad(flat, (0, pad))
    return flat.reshape(-1, LANES)


def kernel(x, norm_mix, norm_mlp, pool_w, pool_scale, norm_kv, w_kvf, b_f, w_q, w_o, w_up, w_down, norm_out, loss_target, m_norm_mix, m_norm_mlp, m_pool_w, m_pool_scale, m_norm_kv, m_w_kvf, m_b_f, m_w_q, m_w_o, m_w_up, m_w_down, m_norm_out, v_norm_mix, v_norm_mlp, v_pool_w, v_pool_scale, v_norm_kv, v_w_kvf, v_b_f, v_w_q, v_w_o, v_w_up, v_w_down, v_norm_out):
    _, S, D = x.shape
    H = D // HEAD_DIM
    dg = D // N_GROUPS
    n_kvf = 2 * D + H
    kvf_b = w_kvf.shape[1]
    fb = w_up.shape[2]
    ps_b = pool_scale.shape[1]
    xi, yi, ci = _position()
    my_block = 4 * xi + 2 * yi + ci
    x2 = x.reshape(S, D)
    tgt = loss_target.reshape(S, D)
    b_pad = jnp.pad(b_f, (0, LANES - H)).reshape(1, LANES)

    g_pool, g_scale = _all_gather_two_level([pool_w.astype(BF16), pool_scale])
    pw = g_pool[:, 0].transpose(1, 0, 2, 3).reshape(N_GROUPS, dg, dg)
    scale_full = g_scale.reshape(1, D)
    ex_up0 = _GatherByChip([w_up[0].astype(BF16)])
    ex_down0_kvf = _GatherByChip([w_down[0].astype(BF16), w_kvf.astype(BF16)])
    ex_q = _Exchange("gather", [w_q[0].astype(BF16)])
    ex_late = _Exchange("gather", [w_o[0].astype(BF16), w_up[1].astype(BF16), w_down[1].astype(BF16)])

    h1, diff, hn_m0, (g_up0,) = _pool_fwd(x2, norm_mix[0:1], pw, scale_full, norm_mlp[0:1], exchange=ex_up0)
    u0, uu0, h2, (g_down0, g_kvf), _ = _mlp_fwd("l0", hn_m0, h1, g_up0, None, ex_up=ex_down0_kvf)
    wkvf = g_kvf.transpose(1, 0, 2).reshape(D, n_kvf)
    w_kv = wkvf[:, :2 * D]
    w_f = jnp.pad(wkvf[:, 2 * D:], ((0, 0), (0, LANES - H)))

    gains_kv_q = jnp.stack([norm_kv, norm_mix[1]])
    hkv, hn_q = _rms_fwd("rms_kv_q", h2, gains_kv_q)
    kv, (g_q,) = _mm_nn("kv_proj", hkv, w_kv, out_dtype=BF16, exchange=ex_q)
    wq = g_q.reshape(D, D)
    f_raw = _mm_nn("f_proj", hkv, w_f, out_dtype=F32)
    c_sh = _gate_fwd(f_raw, b_pad)
    t_attn = _tile(S, 1024)
    c2_sh = c_sh * LOG2E
    c2_row = _row_layout(c2_sh[:, :H].T, t_attn)
    q2 = _mm_nn("q_proj", hn_q, wq, out_dtype=BF16, out_scale=HEAD_DIM ** -0.5 * LOG2E)
    o, l2_sh, (g_o, g_up1, g_down1) = _fox_fwd(q2, kv, c2_row, t=t_attn, exchange=ex_late)
    wo = g_o.reshape(D, D)
    h3 = _mm_nn("o_proj", o, wo, out_dtype=F32, residual=h2)
    (hn_m1,) = _rms_fwd("rms_mlp1", h3, norm_mlp[1:2])
    u1, uu1, h4, _, _ = _mlp_fwd("l1", hn_m1, h3, g_up1, g_down1)
    dh4, dh4_b, d_norm_out, loss_part = _loss_head(h4, norm_out.reshape(1, D), tgt)

    d_pre = _mlp_dpre("l1", dh4_b, u1, g_down1)
    dw_down1, _ = _mm_tn("mlp_dwdown_l1", uu1, dh4_b)
    dw_up1, _ = _mlp_dwup("l1", hn_m1, d_pre, fb)
    d_hn, _ = _mlp_dhn("l1", d_pre, g_up1)
    dh3, dh3_b, d_norm_mlp1 = _rms_bwd("rms_mlp1_bwd", h3, norm_mlp[1:2], [d_hn], dh4)

    do, delta_sh = _o_proj_dx(dh3_b, wo, o)
    dw_o, _ = _mm_tn("o_proj_dw", o, dh3_b)
    l2_row = _row_layout(l2_sh[:, :H].T, t_attn)
    delta_row = _row_layout(delta_sh[:, :H].T, t_attn)
    dq, dk, dv, dck_sh, dcq_row, (p_up1, p_down1, p_o) = _fox_bwd(
        q2, kv, do, c2_sh, c2_row, l2_row, delta_row, t=t_attn,
        exchange=_Exchange("scatter", [dw_up1, dw_down1.reshape(N_DEV, fb, D), dw_o.reshape(N_DEV, D // N_DEV, D)]))
    dw_q, _ = _mm_tn("q_proj_dw", hn_q, dq)
    d_hn_q = _mm_nt("q_proj_dx", dq, wq, out_dtype=F32)

    dcq_sh = jnp.pad(dcq_row.reshape(H, S).T, ((0, 0), (0, LANES - H)))
    d_f, d_b = _gate_bwd(dck_sh + dcq_sh, f_raw, b_pad)
    dw_k, _ = _mm_tn("k_proj_dw", hkv, dk)
    dw_v, _ = _mm_tn("v_proj_dw", hkv, dv)
    dw_f, _ = _mm_tn("f_proj_dw", hkv, d_f)
    d_hkv = _kvf_dx(dk, dv, d_f, w_kv, w_f)
    dh2, dh2_b, d_norm_kv_q = _rms_bwd("rms_kv_q_bwd", h2, gains_kv_q, [d_hkv, d_hn_q], dh3)
    dw_kvf = jnp.concatenate([dw_k, dw_v, dw_f[:, :H]], axis=1).reshape(D, N_DEV, kvf_b).transpose(1, 0, 2)

    d_pre = _mlp_dpre("l0", dh2_b, u0, g_down0)
    dw_down0, (p_kvf, p_q) = _mm_tn(
        "mlp_dwdown_l0", uu0, dh2_b, exchange=_Exchange("scatter", [dw_kvf, dw_q.reshape(N_DEV, D // N_DEV, D)]))
    dw_up0, (p_down0,) = _mlp_dwup(
        "l0", hn_m0, d_pre, fb, exchange=_Exchange("scatter", [dw_down0.reshape(N_DEV, fb, D)]))
    d_hn, (p_up0,) = _mlp_dhn("l0", d_pre, g_up0, exchange=_Exchange("scatter", [dw_up0]))
    dh1, _, d_norm_mlp0 = _rms_bwd("rms_mlp0_bwd", h1, norm_mlp[0:1], [d_hn], dh2, matmul_copy=False)
    grad_x, dw_pool, d_scale, d_norm_mix0 = _pool_bwd(x2, dh1, diff, norm_mix[0:1], pw, scale_full)
    (p_pool,) = _exchange_now("scatter_pool", _Exchange("scatter", [
        dw_pool.astype(BF16).reshape(N_GROUPS, N_DEV, dg // N_DEV, dg).transpose(1, 0, 2, 3).reshape(
            N_DEV, N_GROUPS * dg // N_DEV, dg)]))

    small = jnp.concatenate([
        _rows(jnp.concatenate([d_norm_mix0, d_norm_kv_q[1:2]], axis=0)),
        _rows(jnp.concatenate([d_norm_mlp0, d_norm_mlp1], axis=0)),
        _rows(d_norm_kv_q[0:1]),
        _rows(d_norm_out),
        _rows(d_scale),
        d_b,
        jnp.pad(loss_part[0:1, 0:1], ((0, 0), (0, LANES - 1))),
    ], axis=0)
    n_small = small.shape[0]
    small = jnp.pad(small, ((0, (-n_small) % 8), (0, 0)))
    total = _all_reduce_small(small)
    rd = D // LANES
    loss = total[7 * rd + 1, 0]
    g_scale_mine = lax.dynamic_slice(total[6 * rd:7 * rd].reshape(D), (my_block * ps_b,), (ps_b,))

    def pack(nm_, nl_, kv_, out_, bf_, ps_):
        return jnp.concatenate([_rows(nm_), _rows(nl_), _rows(kv_), _rows(out_), _rows(bf_), _rows(ps_)], axis=0)

    g_small = jnp.concatenate([total[:6 * rd], total[7 * rd:7 * rd + 1], _rows(g_scale_mine)], axis=0)
    w_small = pack(norm_mix, norm_mlp, norm_kv, norm_out, b_f, pool_scale)
    m_small = pack(m_norm_mix, m_norm_mlp, m_norm_kv, m_norm_out, m_b_f, m_pool_scale)
    v_small = pack(v_norm_mix, v_norm_mlp, v_norm_kv, v_norm_out, v_b_f, v_pool_scale)
    rs = g_small.shape[0]
    padr = (-rs) % 8
    pad8 = lambda a: jnp.pad(a, ((0, padr), (0, 0)))
    small_out = _adamw("adamw_small", pad8(g_small)[None], pad8(w_small), pad8(m_small), pad8(v_small), tr=rs + padr)

    def unpack(a):
        o0 = 0
        res = []
        for shape in [(2, D), (2, D), (D,), (D,)]:
            nr = (2 * rd) if len(shape) == 2 else rd
            res.append(a[o0:o0 + nr].reshape(shape))
            o0 += nr
        res.append(a[o0, :H])
        res.append(a[o0 + 1:o0 + 1 + ps_b // LANES].reshape(1, ps_b))
        return res

    small_res = [unpack(a) for a in small_out]

    r_pool = _adamw("adamw_pool", p_pool, pool_w.reshape(-1, dg), m_pool_w.reshape(-1, dg), v_pool_w.reshape(-1, dg))
    r_kvf = _adamw("adamw_kvf", p_kvf, w_kvf, m_w_kvf, v_w_kvf)
    r_q = _adamw("adamw_q", p_q, w_q[0], m_w_q[0], v_w_q[0])
    r_o = _adamw("adamw_o", p_o, w_o[0], m_w_o[0], v_w_o[0])
    r_up = _adamw("adamw_up1", p_up1, w_up, m_w_up, v_w_up, layer=1)
    r_up = _adamw("adamw_up0", p_up0, w_up, m_w_up, v_w_up, layer=0, into=r_up)
    r_down = _adamw("adamw_down1", p_down1, w_down, m_w_down, v_w_down, layer=1)
    r_down = _adamw("adamw_down0", p_down0, w_down, m_w_down, v_w_down, layer=0, into=r_down)

    def leaves(kind):
        sm = small_res[kind]
        return [
            sm[0], sm[1],
            r_pool[kind].reshape(pool_w.shape),
            sm[5],
            sm[2],
            r_kvf[kind],
            sm[4],
            r_q[kind][None], r_o[kind][None],
            r_up[kind],
            r_down[kind],
            sm[3],
        ]

    return (loss, grad_x.reshape(x.shape), *leaves(0), *leaves(1), *leaves(2), *leaves(3))
```

```python
import jax
import jax.numpy as jnp
from jax import lax
from jax.experimental import pallas as pl
from jax.experimental.pallas import tpu as pltpu

F32 = jnp.float32
BF16 = jnp.bfloat16
MESH = pl.DeviceIdType.MESH

N_DEV = 8
EPS = 1e-6
NEG_INF = -1e30
POOL_WINDOWS = (2, 4, 8, 16)
N_GROUPS = len(POOL_WINDOWS)
POOL_HALO = 16
HEAD_DIM = 128
LANES = 128
HEADS_PER_STEP = 2
LOG2E = 1.4426950408889634
LN2 = 0.6931471805599453

ADAM_LR = 0.001
ADAM_B1 = 0.9
ADAM_B2 = 0.999
ADAM_EPS = 1e-08
ADAM_WD = 0.01
ADAM_STEP = 10

VMEM_LIMIT = 56 * 1024 * 1024

_ANY = pl.BlockSpec(memory_space=pl.ANY)

NEAR = (0, 1, 2, 4, 6)
FAR = (3, 5, 7)


def _tile(n, want):
    t = min(n, want)
    assert n % t == 0, (n, want)
    return t


def _params(sem, vmem=VMEM_LIMIT):
    return pltpu.CompilerParams(dimension_semantics=sem, vmem_limit_bytes=vmem)


def _position():
    return lax.axis_index("x"), lax.axis_index("y"), lax.axis_index("c")


class _Exchange:
    def __init__(self, kind, arrays, peers=tuple(range(N_DEV))):
        assert kind in ("gather", "scatter")
        self.kind = kind
        self.peers = tuple(peers)
        self.arrays = list(arrays)
        self.n = len(self.arrays)
        shapes = [(N_DEV, *a.shape) if kind == "gather" else a.shape for a in self.arrays]
        self.out_shape = [jax.ShapeDtypeStruct(s, a.dtype) for s, a in zip(shapes, self.arrays)]
        self.scratch = [pltpu.SemaphoreType.DMA((self.n, N_DEV)), pltpu.SemaphoreType.DMA((self.n, N_DEV))]

    def _copies(self, ins, outs, send_sems, recv_sems, with_recv=True):
        x, y, c = _position()
        me = 4 * x + 2 * y + c
        gather = self.kind == "gather"
        local, sends, recvs = [], [], []
        for a in range(self.n):
            if 0 in self.peers:
                local.append(
                    pltpu.make_async_copy(ins[a] if gather else ins[a].at[me], outs[a].at[me], send_sems.at[a, 0]))
            for k in (k for k in self.peers if k):
                px, py, pc = x ^ (k >> 2), y ^ ((k >> 1) & 1), c ^ (k & 1)
                peer = 4 * px + 2 * py + pc
                src = ins[a] if gather else ins[a].at[peer]
                common = dict(send_sem=send_sems.at[a, k], recv_sem=recv_sems.at[a, k], device_id=(px, py, pc),
                              device_id_type=MESH)
                sends.append(pltpu.make_async_remote_copy(src_ref=src, dst_ref=outs[a].at[me], **common))
                if with_recv:
                    recvs.append(pltpu.make_async_remote_copy(src_ref=src, dst_ref=outs[a].at[peer], **common))
        return local, sends, recvs

    def start(self, ins, outs, send_sems, recv_sems):
        local, sends, _ = self._copies(ins, outs, send_sems, recv_sems, with_recv=False)
        for cp in local + sends:
            cp.start()

    def wait(self, ins, outs, send_sems, recv_sems):
        local, sends, recvs = self._copies(ins, outs, send_sems, recv_sems)
        for send, recv in zip(sends, recvs):
            send.wait_send()
            recv.wait_recv()
        for cp in local:
            cp.wait()


class _GatherByChip:
    middle_at = 0.85

    def __init__(self, arrays):
        self.arrays = list(arrays)
        self.n = len(self.arrays)
        self.out_shape = [jax.ShapeDtypeStruct((N_DEV, *a.shape), a.dtype) for a in self.arrays]
        self.scratch = [pltpu.SemaphoreType.DMA((self.n, N_DEV)), pltpu.SemaphoreType.DMA((self.n, N_DEV))]

    def _copy(self, a, k, outs, send_sems, recv_sems, src, slot_of, to):
        x, y, c = _position()
        dev = lambda kk: (x ^ (kk >> 2), y ^ ((kk >> 1) & 1), c ^ (kk & 1))
        px, py, pc = dev(slot_of)
        slot = outs[a].at[4 * px + 2 * py + pc]
        return pltpu.make_async_remote_copy(
            src_ref=slot if src is None else src, dst_ref=slot, send_sem=send_sems.at[a, k], recv_sem=recv_sems.at[a, k],
            device_id=dev(to), device_id_type=MESH)

    def start(self, ins, outs, send_sems, recv_sems):
        x, y, c = _position()
        for a in range(self.n):
            pltpu.make_async_copy(ins[a], outs[a].at[4 * x + 2 * y + c], send_sems.at[a, 0]).start()
            for k in (1, 2, 4, 6):
                self._copy(a, k, outs, send_sems, recv_sems, ins[a], 0, k).start()

    def middle(self, ins, outs, send_sems, recv_sems):
        for a in range(self.n):
            for k in (2, 4, 6):
                self._copy(a, k, outs, send_sems, recv_sems, None, k, 0).wait_recv()
                self._copy(a, k + 1, outs, send_sems, recv_sems, None, k, 1).start()

    def wait(self, ins, outs, send_sems, recv_sems):
        x, y, c = _position()
        for a in range(self.n):
            self._copy(a, 1, outs, send_sems, recv_sems, None, 1, 0).wait_recv()
            for k in (3, 5, 7):
                self._copy(a, k, outs, send_sems, recv_sems, None, k, 0).wait_recv()
            for k in (1, 2, 4, 6):
                self._copy(a, k, outs, send_sems, recv_sems, ins[a], 0, k).wait_send()
            for k in (2, 4, 6):
                self._copy(a, k + 1, outs, send_sems, recv_sems, None, k, 1).wait_send()
            pltpu.make_async_copy(ins[a], outs[a].at[4 * x + 2 * y + c], send_sems.at[a, 0]).wait()


def _call(body, *, name, grid, in_specs, out_specs, out_shape, args, scratch_shapes=(), sem=None, exchange=None):
    if exchange is None:
        outs = pl.pallas_call(
            body, grid=grid, in_specs=in_specs, out_specs=out_specs, out_shape=out_shape,
            scratch_shapes=list(scratch_shapes), compiler_params=_params(sem), name=name)(*args)
        return list(outs), []

    n_in, n_out, n_scr, n = len(in_specs), len(out_specs), len(scratch_shapes), exchange.n

    def hosted(*refs):
        ins, refs = refs[:n_in], refs[n_in:]
        ex_in, refs = refs[:n], refs[n:]
        outs, refs = refs[:n_out], refs[n_out:]
        ex_out, refs = refs[:n], refs[n:]
        scratch, sems = refs[:n_scr], refs[n_scr:]
        first = _all_true([pl.program_id(d) == 0 for d in range(len(grid))])
        last = _all_true([pl.program_id(d) == grid[d] - 1 for d in range(len(grid))])

        @pl.when(first)
        def _():
            exchange.start(ex_in, ex_out, *sems)

        if hasattr(exchange, "middle"):
            assert len(grid) and any(g > 1 for g in grid), "a host of a three-moment exchange needs more than one grid step"
            step, total = 0, 1
            for d in range(len(grid)):
                step = step * grid[d] + pl.program_id(d)
                total *= grid[d]

            @pl.when(step == max(1, min(total - 1, int(total * exchange.middle_at))))
            def _():
                exchange.middle(ex_in, ex_out, *sems)

        body(*ins, *outs, *scratch)

        @pl.when(last)
        def _():
            exchange.wait(ex_in, ex_out, *sems)

    res = pl.pallas_call(
        hosted, grid=grid, in_specs=[*in_specs, *[_ANY] * n], out_specs=[*out_specs, *[_ANY] * n],
        out_shape=[*out_shape, *exchange.out_shape], scratch_shapes=[*scratch_shapes, *exchange.scratch],
        compiler_params=_params(("arbitrary",) * len(grid)), name=name)(*args, *exchange.arrays)
    return list(res[:n_out]), list(res[n_out:])


def _all_true(preds):
    out = preds[0]
    for p in preds[1:]:
        out = jnp.logical_and(out, p)
    return out


def _exchange_now(name, exchange):
    def body(*refs):
        n = exchange.n
        exchange.start(refs[:n], refs[n:2 * n], *refs[2 * n:])
        exchange.wait(refs[:n], refs[n:2 * n], *refs[2 * n:])

    return pl.pallas_call(
        body, in_specs=[_ANY] * exchange.n, out_specs=[_ANY] * exchange.n, out_shape=exchange.out_shape,
        scratch_shapes=exchange.scratch, name=name)(*exchange.arrays)


def _all_gather_two_level(shards):
    n = len(shards)

    def body(*refs):
        ins, outs = refs[:n], refs[n:2 * n]
        send_sems, recv_sems, local_sems = refs[2 * n:]
        x, y, c = _position()
        me, sibling = (x, y, c), (x, y, 1 - c)
        chips = [(1 - x, y), (x, 1 - y), (1 - x, 1 - y)]

        def slot(a, px, py, pc):
            return outs[a].at[4 * px + 2 * py + pc]

        def copy(a, k, block, to, src=None):
            return pltpu.make_async_remote_copy(
                src_ref=slot(a, *block) if src is None else src, dst_ref=slot(a, *block),
                send_sem=send_sems.at[a, k], recv_sem=recv_sems.at[a, k], device_id=to, device_id_type=MESH)

        mine = [pltpu.make_async_copy(ins[a], slot(a, *me), local_sems.at[a]) for a in range(n)]
        first = []
        for a in range(n):
            mine[a].start()
            first.append(copy(a, 0, me, sibling, src=ins[a]))
            first += [copy(a, 1 + j, me, (*chip, c), src=ins[a]) for j, chip in enumerate(chips)]
        for cp in first:
            cp.start()
        passed = []
        for a in range(n):
            for j, chip in enumerate(chips):
                copy(a, 1 + j, (*chip, c), me).wait_recv()
                fwd = copy(a, 4 + j, (*chip, c), sibling)
                fwd.start()
                passed.append(fwd)
        for a in range(n):
            copy(a, 0, sibling, me).wait_recv()
            for j, chip in enumerate(chips):
                copy(a, 4 + j, (*chip, 1 - c), me).wait_recv()
        for cp in first + passed:
            cp.wait_send()
        for cp in mine:
            cp.wait()

    return pl.pallas_call(
        body, in_specs=[_ANY] * n, out_specs=[_ANY] * n,
        out_shape=[jax.ShapeDtypeStruct((N_DEV, *s.shape), s.dtype) for s in shards],
        scratch_shapes=[pltpu.SemaphoreType.DMA((n, 7)), pltpu.SemaphoreType.DMA((n, 7)),
                        pltpu.SemaphoreType.DMA((n,))],
        name="all_gather_first")(*shards)


def _all_reduce_small(vec):
    R = vec.shape[0]

    def body(v_ref, o_ref, buf_ref, send_sems, recv_sems):
        x, y, c = _position()
        me = 4 * x + 2 * y + c
        buf_ref[me] = v_ref[...]
        copies = []
        for k in range(1, N_DEV):
            peer = (x ^ (k >> 2), y ^ ((k >> 1) & 1), c ^ (k & 1))
            copies.append(pltpu.make_async_remote_copy(
                src_ref=buf_ref.at[me], dst_ref=buf_ref.at[me], send_sem=send_sems.at[k], recv_sem=recv_sems.at[k],
                device_id=peer, device_id_type=MESH))
        for cp in copies:
            cp.start()
        for cp in copies:
            cp.wait()
        total = buf_ref[0]
        for d in range(1, N_DEV):
            total = total + buf_ref[d]
        o_ref[...] = total

    vm = pl.BlockSpec(memory_space=pltpu.VMEM)
    return pl.pallas_call(
        body, in_specs=[vm], out_specs=vm, out_shape=jax.ShapeDtypeStruct((R, LANES), F32),
        scratch_shapes=[pltpu.VMEM((N_DEV, R, LANES), F32), pltpu.SemaphoreType.DMA((N_DEV,)),
                        pltpu.SemaphoreType.DMA((N_DEV,))],
        name="all_reduce_small")(vec)


_DIMS = {
    "nn": (((1,), (0,)), ((), ())),
    "nt": (((1,), (1,)), ((), ())),
    "tn": (((0,), (0,)), ((), ())),
}


def _matmul(name, a, b, *, mode, grid, tm, tn, a_spec, b_spec, outs, extras=(), extra_specs=(), epilogue=None,
            acc_outs=(), exchange=None):
    nk = grid[2]
    n_extra = len(extras)
    dn = _DIMS[mode]

    def body(a_ref, b_ref, *rest):
        extra_refs = rest[:n_extra]
        out_refs = rest[n_extra:-1]
        acc_ref = rest[-1]
        j = pl.program_id(1)
        k = pl.program_id(2)

        def product():
            b = b_ref[...]
            if b.ndim == 3:
                b = b.reshape(-1, b.shape[-1]) if mode == "nn" else jnp.concatenate([b[0], b[1]], axis=1)
            return lax.dot_general(a_ref[...].astype(BF16), b.astype(BF16), dn, preferred_element_type=F32)

        def finish(acc):
            vals = (acc,) if epilogue is None else epilogue(acc, *[r[...] for r in extra_refs])
            for idx, (o_ref, val) in enumerate(zip(out_refs, vals)):
                if idx in acc_outs:
                    @pl.when(j == 0)
                    def _():
                        o_ref[...] = val.astype(o_ref.dtype)

                    @pl.when(j > 0)
                    def _():
                        o_ref[...] += val.astype(o_ref.dtype)
                else:
                    o_ref[...] = val.astype(o_ref.dtype)

        if nk == 1:
            finish(product())
            return

        @pl.when(k == 0)
        def _():
            acc_ref[...] = product()

        @pl.when(k > 0)
        def _():
            acc_ref[...] += product()

        @pl.when(k == nk - 1)
        def _():
            finish(acc_ref[...])

    return _call(
        body, name=name, grid=grid, in_specs=[a_spec, b_spec, *extra_specs], out_specs=[o[2] for o in outs],
        out_shape=[jax.ShapeDtypeStruct(o[0], o[1]) for o in outs], scratch_shapes=[pltpu.VMEM((tm, tn), F32)],
        sem=("parallel", "arbitrary" if acc_outs else "parallel", "arbitrary"), args=(a, b, *extras), exchange=exchange)


def _ij(tm, tn):
    return pl.BlockSpec((tm, tn), lambda i, j, k: (i, j))


def _mm_nn(name, a, b, *, out_dtype, residual=None, out_scale=None, exchange=None, tm=1024, tn=1024, tk=2048):
    M, K = a.shape
    N = b.shape[1]
    tm, tn, tk = _tile(M, tm), _tile(N, tn), _tile(K, tk)
    extras, especs, epi = (), (), None
    if residual is not None:
        extras, especs = (residual,), (_ij(tm, tn),)
        epi = lambda acc, r: (acc + r,)
    elif out_scale is not None:
        epi = lambda acc: (acc * out_scale,)
    outs, got = _matmul(
        name, a, b, mode="nn", grid=(M // tm, N // tn, K // tk), tm=tm, tn=tn,
        a_spec=pl.BlockSpec((tm, tk), lambda i, j, k: (i, k)),
        b_spec=pl.BlockSpec((tk, tn), lambda i, j, k: (k, j)),
        outs=[((M, N), out_dtype, _ij(tm, tn))], extras=extras, extra_specs=especs, epilogue=epi, exchange=exchange)
    return outs[0] if exchange is None else (outs[0], got)


def _mm_nt(name, a, b, *, out_dtype, b_cols=None, residual=None, tm=1024, tn=1024, tk=2048):
    M, K = a.shape
    N = b.shape[0]
    c0 = 0 if b_cols is None else b_cols[0]
    tm, tn, tk = _tile(M, tm), _tile(N, tn), _tile(K, tk)
    assert c0 % tk == 0
    kb0 = c0 // tk
    extras, especs, epi = (), (), None
    if residual is not None:
        extras, especs = (residual,), (_ij(tm, tn),)
        epi = lambda acc, r: (acc + r,)
    return _matmul(
        name, a, b, mode="nt", grid=(M // tm, N // tn, K // tk), tm=tm, tn=tn,
        a_spec=pl.BlockSpec((tm, tk), lambda i, j, k: (i, k)),
        b_spec=pl.BlockSpec((tn, tk), lambda i, j, k: (j, kb0 + k)),
        outs=[((M, N), out_dtype, _ij(tm, tn))], extras=extras, extra_specs=especs, epilogue=epi)[0][0]


def _mm_tn(name, a, b, *, exchange=None, tm=1024, tn=1024, tk=2048):
    K, M = a.shape
    N = b.shape[1]
    tm, tn, tk = _tile(M, tm), _tile(N, tn), _tile(K, tk)
    outs, got = _matmul(
        name, a, b, mode="tn", grid=(M // tm, N // tn, K // tk), tm=tm, tn=tn,
        a_spec=pl.BlockSpec((tk, tm), lambda i, j, k: (k, i)),
        b_spec=pl.BlockSpec((tk, tn), lambda i, j, k: (k, j)),
        outs=[((M, N), BF16, _ij(tm, tn))], exchange=exchange)
    return outs[0], got


def _kvf_dx(dk, dv, d_f, w_kv, w_f, *, tm=1024, tn=1024):
    S, D = dk.shape
    tm, tn = _tile(S, tm), _tile(D, tn)

    def body(dk_ref, dv_ref, df_ref, wk_ref, wv_ref, wf_ref, o_ref):
        acc = lax.dot_general(dk_ref[...], wk_ref[...], _DIMS["nt"], preferred_element_type=F32)
        acc = acc + lax.dot_general(dv_ref[...], wv_ref[...], _DIMS["nt"], preferred_element_type=F32)
        o_ref[...] = acc + lax.dot_general(df_ref[...].astype(BF16), wf_ref[...], _DIMS["nt"],
                                           preferred_element_type=F32)

    row = lambda width: pl.BlockSpec((tm, width), lambda i, j: (i, 0))
    return pl.pallas_call(
        body, grid=(S // tm, D // tn),
        in_specs=[row(D), row(D), row(LANES),
                  pl.BlockSpec((tn, D), lambda i, j: (j, 0)), pl.BlockSpec((tn, D), lambda i, j: (j, 1)),
                  pl.BlockSpec((tn, LANES), lambda i, j: (j, 0))],
        out_specs=pl.BlockSpec((tm, tn), lambda i, j: (i, j)), out_shape=jax.ShapeDtypeStruct((S, D), F32),
        compiler_params=_params(("parallel", "parallel")), name="kvf_proj_dx")(dk, dv, d_f, w_kv, w_kv, w_f)


def _mlp_fwd(tag, hn, h, w_up_g, w_down_g, *, ex_up=None, ex_down=None, tm=1024, tk=2048):
    S, D = hn.shape
    fb = w_up_g.shape[2]
    F = N_DEV * fb
    tm, tku = _tile(S, tm), _tile(D, tk)

    def up_epi(acc):
        u = jnp.maximum(acc, 0.0)
        return u, u * u

    (u, uu), got_up = _matmul(
        f"mlp_up_{tag}", hn, w_up_g, mode="nn", grid=(S // tm, N_DEV, D // tku), tm=tm, tn=fb,
        a_spec=pl.BlockSpec((tm, tku), lambda i, j, k: (i, k)),
        b_spec=pl.BlockSpec((None, tku, fb), lambda i, j, k: (j, k, 0)),
        outs=[((S, F), BF16, _ij(tm, fb)), ((S, F), BF16, _ij(tm, fb))], epilogue=up_epi, exchange=ex_up)
    if w_down_g is None:
        w_down_g = got_up[0]

    tn = _tile(D, 1024)
    (h_out,), got_down = _matmul(
        f"mlp_down_{tag}", uu, w_down_g, mode="nn", grid=(S // tm, D // tn, N_DEV // 2), tm=tm, tn=tn,
        a_spec=pl.BlockSpec((tm, 2 * fb), lambda i, j, k: (i, k)),
        b_spec=pl.BlockSpec((2, fb, tn), lambda i, j, k: (k, 0, j)),
        outs=[((S, D), F32, _ij(tm, tn))], extras=(h,), extra_specs=(_ij(tm, tn),),
        epilogue=lambda acc, res: (acc + res,), exchange=ex_down)
    return u, uu, h_out, got_up, got_down


def _mlp_dpre(tag, dh, u, w_down_g, *, exchange=None, tm=1024, tk=2048):
    S, D = dh.shape
    fb = w_down_g.shape[1]
    tm, tkd = _tile(S, tm), _tile(D, tk)
    outs, got = _matmul(
        f"mlp_dpre_{tag}", dh, w_down_g, mode="nt", grid=(S // tm, N_DEV, D // tkd), tm=tm, tn=fb,
        a_spec=pl.BlockSpec((tm, tkd), lambda i, j, k: (i, k)),
        b_spec=pl.BlockSpec((None, fb, tkd), lambda i, j, k: (j, 0, k)),
        outs=[((S, N_DEV * fb), BF16, _ij(tm, fb))], extras=(u,), extra_specs=(_ij(tm, fb),),
        epilogue=lambda acc, uv: (acc * (2.0 * uv.astype(F32)),), exchange=exchange)
    return outs[0], got


def _mlp_dwup(tag, hn, d_pre, fb, *, exchange=None):
    S, D = hn.shape
    tmu = _tile(D, 1024)
    tks = _tile(S, 2048)
    outs, got = _matmul(
        f"mlp_dwup_{tag}", hn, d_pre, mode="tn", grid=(D // tmu, N_DEV, S // tks), tm=tmu, tn=fb,
        a_spec=pl.BlockSpec((tks, tmu), lambda i, j, k: (k, i)),
        b_spec=pl.BlockSpec((tks, fb), lambda i, j, k: (k, j)),
        outs=[((N_DEV, D, fb), BF16, pl.BlockSpec((None, tmu, fb), lambda i, j, k: (j, i, 0)))], exchange=exchange)
    return outs[0], got


def _mlp_dhn(tag, d_pre, w_up_g, *, exchange=None, tm=1024):
    S, F = d_pre.shape
    D, fb = w_up_g.shape[1], w_up_g.shape[2]
    tm, tn = _tile(S, tm), _tile(D, 1024)
    outs, got = _matmul(
        f"mlp_dhn_{tag}", d_pre, w_up_g, mode="nt", grid=(S // tm, D // tn, N_DEV // 2), tm=tm, tn=tn,
        a_spec=pl.BlockSpec((tm, 2 * fb), lambda i, j, k: (i, k)),
        b_spec=pl.BlockSpec((2, tn, fb), lambda i, j, k: (k, j, 0)),
        outs=[((S, D), F32, _ij(tm, tn))], exchange=exchange)
    return outs[0], got


def _row_spec(ts, D):
    return pl.BlockSpec((ts, D), lambda i: (i, 0))


def _vec_spec(n, D):
    return pl.BlockSpec((n, D), lambda i: (0, 0))


def _rms_fwd(name, x, gains, *, ts=512):
    S, D = x.shape
    n = gains.shape[0]
    ts = _tile(S, ts)

    def body(x_ref, g_ref, *o_refs):
        xv = x_ref[...]
        y = xv * lax.rsqrt(jnp.mean(xv * xv, axis=-1, keepdims=True) + EPS)
        for i, o_ref in enumerate(o_refs):
            o_ref[...] = (y * g_ref[i:i + 1, :]).astype(o_ref.dtype)

    return pl.pallas_call(
        body, grid=(S // ts,), in_specs=[_row_spec(ts, D), _vec_spec(n, D)],
        out_specs=[_row_spec(ts, D)] * n, out_shape=[jax.ShapeDtypeStruct((S, D), BF16)] * n,
        compiler_params=_params(("parallel",)), name=name)(x, gains)


def _rms_bwd(name, x, gains, dys, res, *, matmul_copy=True):
    S, D = x.shape
    n = gains.shape[0]
    ts = _tile(S, 512 if n == 1 else 256)

    def body(x_ref, g_ref, *rest):
        dy_refs = rest[:n]
        res_ref, dx_ref = rest[n:n + 2]
        dxb_ref = rest[n + 2] if matmul_copy else None
        dg_ref = rest[-1]
        i = pl.program_id(0)
        xv = x_ref[...]
        r = lax.rsqrt(jnp.mean(xv * xv, axis=-1, keepdims=True) + EPS)
        xhat = xv * r
        dxhat = None
        dgs = []
        for k in range(n):
            dy = dy_refs[k][...].astype(F32)
            dgs.append(jnp.sum(dy * xhat, axis=0, keepdims=True))
            term = dy * g_ref[k:k + 1, :]
            dxhat = term if dxhat is None else dxhat + term
        dx = res_ref[...] + r * (dxhat - xhat * jnp.mean(dxhat * xhat, axis=-1, keepdims=True))
        dx_ref[...] = dx
        if matmul_copy:
            dxb_ref[...] = dx.astype(BF16)
        dg = jnp.concatenate(dgs, axis=0) if n > 1 else dgs[0]

        @pl.when(i == 0)
        def _():
            dg_ref[...] = dg

        @pl.when(i > 0)
        def _():
            dg_ref[...] += dg

    copies = [(_row_spec(ts, D), jax.ShapeDtypeStruct((S, D), BF16))] if matmul_copy else []
    outs = pl.pallas_call(
        body, grid=(S // ts,),
        in_specs=[_row_spec(ts, D), _vec_spec(n, D)] + [_row_spec(ts, D)] * (n + 1),
        out_specs=[_row_spec(ts, D), *[c[0] for c in copies], _vec_spec(n, D)],
        out_shape=[jax.ShapeDtypeStruct((S, D), F32), *[c[1] for c in copies], jax.ShapeDtypeStruct((n, D), F32)],
        compiler_params=_params(("arbitrary",)), name=name)(x, gains, *dys, res)
    return (outs[0], outs[1], outs[2]) if matmul_copy else (outs[0], None, outs[1])


def _loss_head(h, gain, target, *, ts=512):
    S, D = h.shape
    ts = _tile(S, ts)

    def body(x_ref, g_ref, t_ref, dx_ref, dxb_ref, dg_ref, loss_ref):
        i = pl.program_id(0)
        xv = x_ref[...]
        g = g_ref[...]
        r = lax.rsqrt(jnp.mean(xv * xv, axis=-1, keepdims=True) + EPS)
        xhat = xv * r
        err = xhat * g - t_ref[...]
        part = 0.5 * jnp.sum(jnp.mean(err * err, axis=-1, keepdims=True), axis=0, keepdims=True)
        dy = err * (1.0 / D)
        dg = jnp.sum(dy * xhat, axis=0, keepdims=True)
        dxhat = dy * g
        dx = r * (dxhat - xhat * jnp.mean(dxhat * xhat, axis=-1, keepdims=True))
        dx_ref[...] = dx
        dxb_ref[...] = dx.astype(BF16)

        @pl.when(i == 0)
        def _():
            dg_ref[...] = dg
            loss_ref[...] = jnp.broadcast_to(part, loss_ref.shape)

        @pl.when(i > 0)
        def _():
            dg_ref[...] += dg
            loss_ref[...] += jnp.broadcast_to(part, loss_ref.shape)

    return pl.pallas_call(
        body, grid=(S // ts,),
        in_specs=[_row_spec(ts, D), _vec_spec(1, D), _row_spec(ts, D)],
        out_specs=[_row_spec(ts, D), _row_spec(ts, D), _vec_spec(1, D), pl.BlockSpec((8, LANES), lambda i: (0, 0))],
        out_shape=[jax.ShapeDtypeStruct((S, D), F32), jax.ShapeDtypeStruct((S, D), BF16),
                   jax.ShapeDtypeStruct((1, D), F32), jax.ShapeDtypeStruct((8, LANES), F32)],
        compiler_params=_params(("arbitrary",)), name="loss_head")(h, gain, target)


def _window_counts(t, w):
    return jnp.minimum(t + 1, w).astype(F32)


def _pool_fwd(x, gain, pool_w, scale, gain_next, *, exchange=None, ts=256):
    S, D = x.shape
    dg = D // N_GROUPS
    ts = _tile(S, ts)
    per = ts // POOL_HALO

    def body(x_ref, xh_ref, g_ref, w_ref, sc_ref, gn_ref, h_ref, diff_ref, hn_ref):
        i = pl.program_id(0)
        g = g_ref[...]
        h_parts = []

        def norm(v):
            return v * lax.rsqrt(jnp.mean(v * v, axis=-1, keepdims=True) + EPS) * g

        xc = x_ref[...]
        hn_c = norm(xc)
        hn_h = norm(xh_ref[...]) * (i > 0).astype(F32)
        ext = jnp.concatenate([hn_h, hn_c], axis=0)
        t = i * ts + lax.broadcasted_iota(jnp.int32, (ts, 1), 0)
        for gi, w in enumerate(POOL_WINDOWS):
            cols = slice(gi * dg, (gi + 1) * dg)
            s = ext[:, cols]
            step = 1
            while step < w:
                s = s + pltpu.roll(s, step, 0)
                step *= 2
            mean = s[POOL_HALO:] * (1.0 / _window_counts(t, w))
            diff = (mean - hn_c[:, cols]).astype(BF16)
            diff_ref[:, cols] = diff
            mixed = jnp.dot(diff, w_ref[gi], preferred_element_type=F32)
            h_parts.append(xc[:, cols] + mixed * sc_ref[:, cols])
        h = jnp.concatenate(h_parts, axis=1)
        h_ref[...] = h
        hn_ref[...] = (h * lax.rsqrt(jnp.mean(h * h, axis=-1, keepdims=True) + EPS) * gn_ref[...]).astype(BF16)

    (h, diff, hn), got = _call(
        body, name="pool_fwd", grid=(S // ts,),
        in_specs=[_row_spec(ts, D),
                  pl.BlockSpec((POOL_HALO, D), lambda i: (jnp.maximum(i * per - 1, 0), 0)),
                  _vec_spec(1, D), pl.BlockSpec((N_GROUPS, dg, dg), lambda i: (0, 0, 0)), _vec_spec(1, D),
                  _vec_spec(1, D)],
        out_specs=[_row_spec(ts, D), _row_spec(ts, D), _row_spec(ts, D)],
        out_shape=[jax.ShapeDtypeStruct((S, D), F32), jax.ShapeDtypeStruct((S, D), BF16),
                   jax.ShapeDtypeStruct((S, D), BF16)],
        sem=("parallel",), args=(x, x, gain, pool_w, scale, gain_next), exchange=exchange)
    return h, diff, hn, got


def _pool_bwd(x, dh, diff, gain, pool_w, scale, *, exchange=None, ts=512):
    S, D = x.shape
    dg = D // N_GROUPS
    ts = _tile(S, ts)
    per = ts // POOL_HALO
    n_tiles = S // ts
    n_halo = S // POOL_HALO
    ext_rows = ts + POOL_HALO

    def body(x_ref, dh_ref, dhn_ref, diff_ref, g_ref, w_ref, sc_ref, dx_ref, dw_ref, dsc_ref, dgain_ref):
        i = pl.program_id(0)
        xc = x_ref[...]
        g = g_ref[...]
        r = lax.rsqrt(jnp.mean(xc * xc, axis=-1, keepdims=True) + EPS)
        xhat = xc * r
        dh_c = dh_ref[...]
        dh_n = dhn_ref[...] * (i < n_tiles - 1).astype(F32)
        dh_ext = jnp.concatenate([dh_c, dh_n], axis=0)
        t_ext = i * ts + lax.broadcasted_iota(jnp.int32, (ext_rows, 1), 0)
        d_hn_parts, dsc_parts = [], []
        for gi, w in enumerate(POOL_WINDOWS):
            cols = slice(gi * dg, (gi + 1) * dg)
            wg = w_ref[gi]
            dmix = (dh_ext[:, cols] * sc_ref[:, cols]).astype(BF16)
            d_diff = lax.dot_general(dmix, wg, _DIMS["nt"], preferred_element_type=F32)
            diff_c = diff_ref[:, cols]
            dwg = lax.dot_general(diff_c, dmix[:ts], _DIMS["tn"], preferred_element_type=F32)
            mixed = jnp.dot(diff_c, wg, preferred_element_type=F32)
            dsc_parts.append(jnp.sum(dh_c[:, cols] * mixed, axis=0, keepdims=True))
            e = d_diff * (1.0 / _window_counts(t_ext, w))
            step = 1
            while step < w:
                e = e + pltpu.roll(e, ext_rows - step, 0)
                step *= 2
            d_hn_parts.append(e[:ts] - d_diff[:ts])

            @pl.when(i == 0)
            def _():
                dw_ref[gi] = dwg

            @pl.when(i > 0)
            def _():
                dw_ref[gi] += dwg

        d_hn = jnp.concatenate(d_hn_parts, axis=1)
        dsc = jnp.concatenate(dsc_parts, axis=1)
        dgain = jnp.sum(d_hn * xhat, axis=0, keepdims=True)
        dxhat = d_hn * g
        dx_ref[...] = dh_c + r * (dxhat - xhat * jnp.mean(dxhat * xhat, axis=-1, keepdims=True))

        @pl.when(i == 0)
        def _():
            dsc_ref[...] = dsc
            dgain_ref[...] = dgain

        @pl.when(i > 0)
        def _():
            dsc_ref[...] += dsc
            dgain_ref[...] += dgain

    outs, got = _call(
        body, name="pool_bwd", grid=(n_tiles,),
        in_specs=[_row_spec(ts, D), _row_spec(ts, D),
                  pl.BlockSpec((POOL_HALO, D), lambda i: (jnp.minimum((i + 1) * per, n_halo - 1), 0)),
                  _row_spec(ts, D), _vec_spec(1, D),
                  pl.BlockSpec((N_GROUPS, dg, dg), lambda i: (0, 0, 0)), _vec_spec(1, D)],
        out_specs=[_row_spec(ts, D), pl.BlockSpec((N_GROUPS, dg, dg), lambda i: (0, 0, 0)),
                   _vec_spec(1, D), _vec_spec(1, D)],
        out_shape=[jax.ShapeDtypeStruct((S, D), F32), jax.ShapeDtypeStruct((N_GROUPS, dg, dg), F32),
                   jax.ShapeDtypeStruct((1, D), F32), jax.ShapeDtypeStruct((1, D), F32)],
        sem=("arbitrary",), args=(x, dh, dh, diff, gain, pool_w, scale), exchange=exchange)
    return (*outs, got)


def _gate_fwd(f_raw, b_pad, *, ts=512):
    S = f_raw.shape[0]
    ts = _tile(S, ts)

    def body(f_ref, b_ref, c_ref, carry_ref):
        i = pl.program_id(0)

        @pl.when(i == 0)
        def _():
            carry_ref[...] = jnp.zeros_like(carry_ref)

        z = f_ref[...] + b_ref[...]
        v = jnp.minimum(z, 0.0) - jnp.log1p(jnp.exp(-jnp.abs(z)))
        row = lax.broadcasted_iota(jnp.int32, (ts, 1), 0)
        step = 1
        while step < ts:
            v = v + jnp.where(row >= step, pltpu.roll(v, step, 0), 0.0)
            step *= 2
        out = v + carry_ref[0:1, :]
        c_ref[...] = out
        carry_ref[...] = jnp.broadcast_to(out[ts - 1:ts, :], carry_ref.shape)

    return pl.pallas_call(
        body, grid=(S // ts,),
        in_specs=[pl.BlockSpec((ts, LANES), lambda i: (i, 0)), pl.BlockSpec((1, LANES), lambda i: (0, 0))],
        out_specs=pl.BlockSpec((ts, LANES), lambda i: (i, 0)),
        out_shape=jax.ShapeDtypeStruct((S, LANES), F32),
        scratch_shapes=[pltpu.VMEM((8, LANES), F32)],
        compiler_params=_params(("arbitrary",)), name="gate_fwd")(f_raw, b_pad)


def _gate_bwd(dc, f_raw, b_pad, *, ts=512):
    S = f_raw.shape[0]
    ts = _tile(S, ts)
    n = S // ts

    def body(dc_ref, f_ref, b_ref, df_ref, db_ref, carry_ref):
        i = pl.program_id(0)

        @pl.when(i == 0)
        def _():
            carry_ref[...] = jnp.zeros_like(carry_ref)

        v = dc_ref[...]
        row = lax.broadcasted_iota(jnp.int32, (ts, 1), 0)
        step = 1
        while step < ts:
            v = v + jnp.where(row < ts - step, pltpu.roll(v, ts - step, 0), 0.0)
            step *= 2
        d_logf = v + carry_ref[0:1, :]
        carry_ref[...] = jnp.broadcast_to(d_logf[0:1, :], carry_ref.shape)
        z = f_ref[...] + b_ref[...]
        df = d_logf / (1.0 + jnp.exp(z))
        df_ref[...] = df
        db = jnp.sum(df, axis=0, keepdims=True)

        @pl.when(i == 0)
        def _():
            db_ref[...] = db

        @pl.when(i > 0)
        def _():
            db_ref[...] += db

    rev = lambda i: (n - 1 - i, 0)
    return pl.pallas_call(
        body, grid=(n,),
        in_specs=[pl.BlockSpec((ts, LANES), rev), pl.BlockSpec((ts, LANES), rev),
                  pl.BlockSpec((1, LANES), lambda i: (0, 0))],
        out_specs=[pl.BlockSpec((ts, LANES), rev), pl.BlockSpec((1, LANES), lambda i: (0, 0))],
        out_shape=[jax.ShapeDtypeStruct((S, LANES), F32), jax.ShapeDtypeStruct((1, LANES), F32)],
        scratch_shapes=[pltpu.VMEM((8, LANES), F32)],
        compiler_params=_params(("arbitrary",)), name="gate_bwd")(dc, f_raw, b_pad)


def _row_layout(a_hs, t):
    n_heads, S = a_hs.shape
    return a_hs.reshape(n_heads, S // t, 1, t)


def _lane_pick(blk, h):
    lane = lax.broadcasted_iota(jnp.int32, blk.shape, 1)
    return jnp.sum(jnp.where(lane == h, blk, 0.0), axis=-1, keepdims=True)


def _lane_put(ref, h, col, first=True):
    lane = lax.broadcasted_iota(jnp.int32, ref.shape, 1)
    if not first:
        ref[...] = jnp.where(lane == h, col, ref[...])
        return

    @pl.when(h == 0)
    def _():
        ref[...] = jnp.where(lane == 0, col, 0.0)

    @pl.when(h > 0)
    def _():
        ref[...] = jnp.where(lane == h, col, ref[...])


def _causal(s, masked, fill, rows_are_queries=True):
    if not masked:
        return s
    rr = lax.broadcasted_iota(jnp.int32, s.shape, 0)
    cc = lax.broadcasted_iota(jnp.int32, s.shape, 1)
    keep = (cc <= rr) if rows_are_queries else (rr <= cc)
    return jnp.where(keep, s, fill)


def _fox_fwd(q2, kv, c_row, *, t=512, exchange=None):
    S, D = q2.shape
    H = D // HEAD_DIM
    t = _tile(S, t)
    nb = S // t
    hp = HEADS_PER_STEP
    wide = hp * HEAD_DIM

    def body(q_ref, k_ref, v_ref, cr_ref, o_ref, l2_ref):
        i = pl.program_id(0)
        g = pl.program_id(1)
        cols = [slice(a * HEAD_DIM, (a + 1) * HEAD_DIM) for a in range(hp)]
        refs_i = [cr_ref[a, i][:, 0:1] for a in range(hp)]
        qs = [q_ref[:, cols[a]] for a in range(hp)]

        def step(j, carry, masked):
            r0 = pl.multiple_of(j * t, t)
            out = []
            for a in range(hp):
                m, l, acc = carry[3 * a:3 * a + 3]
                kb = k_ref[pl.ds(r0, t), cols[a]]
                vb = v_ref[pl.ds(r0, t), cols[a]]
                ck = cr_ref[a, j] - refs_i[a]
                s = lax.dot_general(qs[a], kb, _DIMS["nt"], preferred_element_type=F32) - ck
                s = _causal(s, masked, NEG_INF)
                m_new = jnp.maximum(m, jnp.max(s, axis=-1, keepdims=True))
                alpha = jnp.exp2(m - m_new)
                p = jnp.exp2(s - m_new)
                l = alpha * l + jnp.sum(p, axis=-1, keepdims=True)
                acc = alpha * acc + jnp.dot(p.astype(BF16), vb, preferred_element_type=F32)
                out += [m_new, l, acc]
            return tuple(out)

        init = (jnp.full((t, 1), NEG_INF, F32), jnp.zeros((t, 1), F32), jnp.zeros((t, HEAD_DIM), F32)) * hp
        carry = lax.fori_loop(0, i, lambda j, c: step(j, c, False), init)
        carry = step(i, carry, True)
        for a in range(hp):
            m, l, acc = carry[3 * a:3 * a + 3]
            o_ref[:, cols[a]] = (acc / l).astype(o_ref.dtype)
            _lane_put(l2_ref, g * hp + a, m + jnp.log2(l), first=(a == 0))

    (o, l2_sh), got = _call(
        body, name="fox_fwd", grid=(nb, H // hp),
        in_specs=[pl.BlockSpec((t, wide), lambda i, g: (i, g)),
                  pl.BlockSpec((S, wide), lambda i, g: (0, g)),
                  pl.BlockSpec((S, wide), lambda i, g: (0, H // hp + g)),
                  pl.BlockSpec((hp, nb, 1, t), lambda i, g: (g, 0, 0, 0))],
        out_specs=[pl.BlockSpec((t, wide), lambda i, g: (i, g)),
                   pl.BlockSpec((t, LANES), lambda i, g: (i, 0))],
        out_shape=[jax.ShapeDtypeStruct((S, D), BF16), jax.ShapeDtypeStruct((S, LANES), F32)],
        sem=("parallel", "arbitrary"), args=(q2, kv, kv, c_row), exchange=exchange)
    return o, l2_sh, got


def _o_proj_dx(dh, wo, o, *, tm=1024, tn=1024):
    S, D = dh.shape
    tm, tn = _tile(S, tm), _tile(D, tn)
    heads = tn // HEAD_DIM

    def epilogue(acc, ov):
        j = pl.program_id(1)
        lane = lax.broadcasted_iota(jnp.int32, (tm, LANES), 1)
        prod = acc * ov.astype(F32)
        blk = jnp.zeros((tm, LANES), F32)
        for hh in range(heads):
            col = jnp.sum(prod[:, hh * HEAD_DIM:(hh + 1) * HEAD_DIM], axis=-1, keepdims=True)
            blk = jnp.where(lane == j * heads + hh, col, blk)
        return acc, blk

    (do, delta_sh), _ = _matmul(
        "o_proj_dx", dh, wo, mode="nt", grid=(S // tm, D // tn, 1), tm=tm, tn=tn,
        a_spec=pl.BlockSpec((tm, D), lambda i, j, k: (i, 0)),
        b_spec=pl.BlockSpec((tn, D), lambda i, j, k: (j, 0)),
        outs=[((S, D), BF16, _ij(tm, tn)), ((S, LANES), F32, pl.BlockSpec((tm, LANES), lambda i, j, k: (i, 0)))],
        extras=(o,), extra_specs=(_ij(tm, tn),), epilogue=epilogue, acc_outs=(1,))
    return do, delta_sh


def _fox_bwd(q2, kv, do, c_sh, c_row, l2_row, delta_row, *, t=512, exchange=None):
    S, D = q2.shape
    H = D // HEAD_DIM
    t = _tile(S, t)
    nb = S // t
    scale = HEAD_DIM ** -0.5

    def body(q_ref, do_ref, k_ref, v_ref, csh_ref, cr_ref, l2_ref, dl_ref, dq_ref, dk_ref, dv_ref, dck_ref, dcq_ref,
             dq_acc):
        h = pl.program_id(0)
        j = pl.program_id(1)
        ck = _lane_pick(csh_ref[...], h)
        kb = k_ref[...]
        vb = v_ref[...]

        @pl.when(j == 0)
        def _():
            dq_acc[...] = jnp.zeros_like(dq_acc)
            dcq_ref[...] = jnp.zeros_like(dcq_ref)

        def step(i, carry, masked):
            dk, dv, dc = carry
            r0 = pl.multiple_of(i * t, t)
            qb = q_ref[pl.ds(r0, t), :]
            dob = do_ref[pl.ds(r0, t), :]
            s = lax.dot_general(kb, qb, _DIMS["nt"], preferred_element_type=F32) - (ck - cr_ref[i][:, 0:1])
            p = _causal(jnp.exp2(s - l2_ref[i]), masked, 0.0, rows_are_queries=False)
            dv = dv + jnp.dot(p.astype(BF16), dob, preferred_element_type=F32)
            dp = lax.dot_general(vb, dob, _DIMS["nt"], preferred_element_type=F32)
            ds = p * (dp - dl_ref[i])
            dc = dc - jnp.sum(ds, axis=-1, keepdims=True)
            dcq_ref[i] += jnp.sum(ds, axis=0, keepdims=True)
            dsb = ds.astype(BF16)
            dk = dk + jnp.dot(dsb, qb, preferred_element_type=F32)
            dq_acc[pl.ds(r0, t), :] += lax.dot_general(dsb, kb, _DIMS["tn"], preferred_element_type=F32)
            return dk, dv, dc

        init = (jnp.zeros((t, HEAD_DIM), F32), jnp.zeros((t, HEAD_DIM), F32), jnp.zeros((t, 1), F32))
        carry = step(j, init, True)
        dk, dv, dc = lax.fori_loop(j + 1, nb, lambda i, c: step(i, c, False), carry)
        dk_ref[...] = (dk * LN2).astype(dk_ref.dtype)
        dv_ref[...] = dv.astype(dv_ref.dtype)
        rows = pl.ds(pl.multiple_of(j * t, t), t)
        lane = lax.broadcasted_iota(jnp.int32, (t, LANES), 1)

        @pl.when(h == 0)
        def _():
            dck_ref[rows, :] = jnp.where(lane == 0, dc, 0.0)

        @pl.when(h > 0)
        def _():
            dck_ref[rows, :] = jnp.where(lane == h, dc, dck_ref[rows, :])

        @pl.when(j == nb - 1)
        def _():
            dq_ref[...] = (dq_acc[...] * scale).astype(dq_ref.dtype)

    kspec = pl.BlockSpec((t, HEAD_DIM), lambda h, j: (j, h))
    headspec = pl.BlockSpec((S, HEAD_DIM), lambda h, j: (0, h))
    rowspec = pl.BlockSpec((None, nb, 1, t), lambda h, j: (h, 0, 0, 0))
    (dq, dk, dv, dck_sh, dcq_row), got = _call(
        body, name="fox_bwd", grid=(H, nb),
        in_specs=[headspec, headspec, kspec,
                  pl.BlockSpec((t, HEAD_DIM), lambda h, j: (j, H + h)),
                  pl.BlockSpec((t, LANES), lambda h, j: (j, 0)),
                  rowspec, rowspec, rowspec],
        out_specs=[headspec, kspec, kspec, pl.BlockSpec((S, LANES), lambda h, j: (0, 0)), rowspec],
        out_shape=[jax.ShapeDtypeStruct((S, D), BF16), jax.ShapeDtypeStruct((S, D), BF16),
                   jax.ShapeDtypeStruct((S, D), BF16), jax.ShapeDtypeStruct((S, LANES), F32),
                   jax.ShapeDtypeStruct((H, nb, 1, t), F32)],
        scratch_shapes=[pltpu.VMEM((S, HEAD_DIM), F32)],
        sem=("arbitrary", "arbitrary"), args=(q2, do, kv, kv, c_sh, c_row, l2_row, delta_row), exchange=exchange)
    return dq, dk, dv, dck_sh, dcq_row, got


def _adamw_update(g, w_ref, m_ref, v_ref, g_ref, d_ref, nm_ref, nv_ref):
    nm = ADAM_B1 * m_ref[...] + (1.0 - ADAM_B1) * g
    nv = ADAM_B2 * v_ref[...] + (1.0 - ADAM_B2) * (g * g)
    m_hat = nm / (1.0 - ADAM_B1 ** ADAM_STEP)
    v_hat = nv / (1.0 - ADAM_B2 ** ADAM_STEP)
    g_ref[...] = g
    d_ref[...] = -ADAM_LR * (m_hat / (jnp.sqrt(v_hat) + ADAM_EPS) + ADAM_WD * w_ref[...])
    nm_ref[...] = nm
    nv_ref[...] = nv


def _adamw_split(name, parts_a, parts_b, peers_a, me, w, m, v, layer, *, tr=128):
    _, R, C = parts_a.shape
    tr = _tile(R, tr)

    def body(me_ref, *refs):
        p_refs = refs[:N_DEV]
        w_ref, m_ref, v_ref = refs[N_DEV:N_DEV + 3]
        g = p_refs[0][...].astype(F32)
        for k in range(1, N_DEV):
            g = g + p_refs[k][...].astype(F32)
        _adamw_update(g, w_ref, m_ref, v_ref, *refs[-4:])

    def pspec(k):
        return pl.BlockSpec((None, tr, C), lambda i, me_ref: (jnp.bitwise_xor(me_ref[0], k), i, 0))

    wspec = pl.BlockSpec((None, tr, C), lambda i, me_ref: (layer, i, 0))
    return pl.pallas_call(
        body,
        grid_spec=pltpu.PrefetchScalarGridSpec(
            num_scalar_prefetch=1, grid=(R // tr,),
            in_specs=[pspec(k) for k in range(N_DEV)] + [wspec] * 3, out_specs=[wspec] * 4),
        out_shape=[jax.ShapeDtypeStruct(w.shape, F32)] * 4,
        compiler_params=_params(("parallel",)), name=name)(
            me, *[parts_a if k in peers_a else parts_b for k in range(N_DEV)], w, m, v)


def _adamw(name, parts, w, m, v, layer=None, into=None, *, tr=128):
    P, R, C = parts.shape
    tr = _tile(R, tr)

    def body(p_ref, w_ref, m_ref, v_ref, *rest):
        g = p_ref[0].astype(F32)
        for k in range(1, P):
            g = g + p_ref[k].astype(F32)
        _adamw_update(g, w_ref, m_ref, v_ref, *rest[-4:])

    pspec = pl.BlockSpec((P, tr, C), lambda i: (0, i, 0))
    if layer is None:
        wspec = pl.BlockSpec((tr, C), lambda i: (i, 0))
        return pl.pallas_call(
            body, grid=(R // tr,), in_specs=[pspec, wspec, wspec, wspec],
            out_specs=[wspec] * 4, out_shape=[jax.ShapeDtypeStruct((R, C), F32)] * 4,
            compiler_params=_params(("parallel",)), name=name)(parts, w, m, v)
    wspec = pl.BlockSpec((None, tr, C), lambda i: (layer, i, 0))
    prev = [] if into is None else list(into)
    return pl.pallas_call(
        body, grid=(R // tr,), in_specs=[pspec, wspec, wspec, wspec] + [_ANY] * len(prev),
        out_specs=[wspec] * 4, out_shape=[jax.ShapeDtypeStruct(w.shape, F32)] * 4,
        input_output_aliases={4 + k: k for k in range(len(prev))},
        compiler_params=_params(("parallel",)), name=name)(parts, w, m, v, *prev)


def _rows(a):
    flat = a.reshape(-1)
    pad = (-flat.shape[0]) % LANES
    if pad:
        flat = jnp.pad(flat, (0, pad))
    return flat.reshape(-1, LANES)


def kernel(x, norm_mix, norm_mlp, pool_w, pool_scale, norm_kv, w_kvf, b_f, w_q, w_o, w_up, w_down, norm_out, loss_target, m_norm_mix, m_norm_mlp, m_pool_w, m_pool_scale, m_norm_kv, m_w_kvf, m_b_f, m_w_q, m_w_o, m_w_up, m_w_down, m_norm_out, v_norm_mix, v_norm_mlp, v_pool_w, v_pool_scale, v_norm_kv, v_w_kvf, v_b_f, v_w_q, v_w_o, v_w_up, v_w_down, v_norm_out):
    _, S, D = x.shape
    H = D // HEAD_DIM
    dg = D // N_GROUPS
    n_kvf = 2 * D + H
    kvf_b = w_kvf.shape[1]
    fb = w_up.shape[2]
    ps_b = pool_scale.shape[1]
    xi, yi, ci = _position()
    my_block = 4 * xi + 2 * yi + ci
    x2 = x.reshape(S, D)
    tgt = loss_target.reshape(S, D)
    b_pad = jnp.pad(b_f, (0, LANES - H)).reshape(1, LANES)

    g_pool, g_scale = _all_gather_two_level([pool_w.astype(BF16), pool_scale])
    pw = g_pool[:, 0].transpose(1, 0, 2, 3).reshape(N_GROUPS, dg, dg)
    scale_full = g_scale.reshape(1, D)
    ex_up0 = _GatherByChip([w_up[0].astype(BF16)])
    ex_down0 = _GatherByChip([w_down[0].astype(BF16)])
    ex_kvf = _Exchange("gather", [w_kvf.astype(BF16)])
    ex_q = _Exchange("gather", [w_q[0].astype(BF16)])
    ex_late = _Exchange("gather", [w_o[0].astype(BF16), w_up[1].astype(BF16), w_down[1].astype(BF16)])

    h1, diff, hn_m0, (g_up0,) = _pool_fwd(x2, norm_mix[0:1], pw, scale_full, norm_mlp[0:1], exchange=ex_up0)
    u0, uu0, h2, (g_down0,), (g_kvf,) = _mlp_fwd("l0", hn_m0, h1, g_up0, None, ex_up=ex_down0, ex_down=ex_kvf)
    wkvf = g_kvf.transpose(1, 0, 2).reshape(D, n_kvf)
    w_kv = wkvf[:, :2 * D]
    w_f = jnp.pad(wkvf[:, 2 * D:], ((0, 0), (0, LANES - H)))

    gains_kv_q = jnp.stack([norm_kv, norm_mix[1]])
    hkv, hn_q = _rms_fwd("rms_kv_q", h2, gains_kv_q)
    kv, (g_q,) = _mm_nn("kv_proj", hkv, w_kv, out_dtype=BF16, exchange=ex_q)
    wq = g_q.reshape(D, D)
    f_raw = _mm_nn("f_proj", hkv, w_f, out_dtype=F32)
    c_sh = _gate_fwd(f_raw, b_pad)
    t_attn = _tile(S, 1024)
    c2_sh = c_sh * LOG2E
    c2_row = _row_layout(c2_sh[:, :H].T, t_attn)
    q2 = _mm_nn("q_proj", hn_q, wq, out_dtype=BF16, out_scale=HEAD_DIM ** -0.5 * LOG2E)
    o, l2_sh, (g_o, g_up1, g_down1) = _fox_fwd(q2, kv, c2_row, t=t_attn, exchange=ex_late)
    wo = g_o.reshape(D, D)
    h3 = _mm_nn("o_proj", o, wo, out_dtype=F32, residual=h2)
    (hn_m1,) = _rms_fwd("rms_mlp1", h3, norm_mlp[1:2])
    u1, uu1, h4, _, _ = _mlp_fwd("l1", hn_m1, h3, g_up1, g_down1)
    dh4, dh4_b, d_norm_out, loss_part = _loss_head(h4, norm_out.reshape(1, D), tgt)

    d_pre, _ = _mlp_dpre("l1", dh4_b, u1, g_down1)
    dw_down1, _ = _mm_tn("mlp_dwdown_l1", uu1, dh4_b)
    dw_up1, _ = _mlp_dwup("l1", hn_m1, d_pre, fb)
    d_hn, _ = _mlp_dhn("l1", d_pre, g_up1)
    dh3, dh3_b, d_norm_mlp1 = _rms_bwd("rms_mlp1_bwd", h3, norm_mlp[1:2], [d_hn], dh4)

    do, delta_sh = _o_proj_dx(dh3_b, wo, o)
    dw_o, _ = _mm_tn("o_proj_dw", o, dh3_b)
    l2_row = _row_layout(l2_sh[:, :H].T, t_attn)
    delta_row = _row_layout(delta_sh[:, :H].T, t_attn)
    dq, dk, dv, dck_sh, dcq_row, (p_up1, p_down1, p_o) = _fox_bwd(
        q2, kv, do, c2_sh, c2_row, l2_row, delta_row, t=t_attn,
        exchange=_Exchange("scatter", [dw_up1, dw_down1.reshape(N_DEV, fb, D), dw_o.reshape(N_DEV, D // N_DEV, D)]))
    dw_q, _ = _mm_tn("q_proj_dw", hn_q, dq)
    d_hn_q = _mm_nt("q_proj_dx", dq, wq, out_dtype=F32)

    dcq_sh = jnp.pad(dcq_row.reshape(H, S).T, ((0, 0), (0, LANES - H)))
    d_f, d_b = _gate_bwd(dck_sh + dcq_sh, f_raw, b_pad)
    dw_k, _ = _mm_tn("k_proj_dw", hkv, dk)
    dw_v, _ = _mm_tn("v_proj_dw", hkv, dv)
    dw_f, _ = _mm_tn("f_proj_dw", hkv, d_f)
    d_hkv = _kvf_dx(dk, dv, d_f, w_kv, w_f)
    dh2, dh2_b, d_norm_kv_q = _rms_bwd("rms_kv_q_bwd", h2, gains_kv_q, [d_hkv, d_hn_q], dh3)
    dw_kvf = jnp.concatenate([dw_k, dw_v, dw_f[:, :H]], axis=1).reshape(D, N_DEV, kvf_b).transpose(1, 0, 2)

    dw_down0, (p_kvf, p_q) = _mm_tn(
        "mlp_dwdown_l0", uu0, dh2_b, exchange=_Exchange("scatter", [dw_kvf, dw_q.reshape(N_DEV, D // N_DEV, D)]))
    dw_down0 = dw_down0.reshape(N_DEV, fb, D)
    d_pre, (pa_down0,) = _mlp_dpre("l0", dh2_b, u0, g_down0, exchange=_Exchange("scatter", [dw_down0], NEAR))
    dw_up0, (pb_down0,) = _mlp_dwup("l0", hn_m0, d_pre, fb, exchange=_Exchange("scatter", [dw_down0], FAR))
    d_hn, (pa_up0,) = _mlp_dhn("l0", d_pre, g_up0, exchange=_Exchange("scatter", [dw_up0], NEAR))
    dh1, _, d_norm_mlp0 = _rms_bwd("rms_mlp0_bwd", h1, norm_mlp[0:1], [d_hn], dh2, matmul_copy=False)
    grad_x, dw_pool, d_scale, d_norm_mix0, (pb_up0,) = _pool_bwd(
        x2, dh1, diff, norm_mix[0:1], pw, scale_full, exchange=_Exchange("scatter", [dw_up0], FAR))
    (p_pool,) = _exchange_now("scatter_pool", _Exchange("scatter", [
        dw_pool.astype(BF16).reshape(N_GROUPS, N_DEV, dg // N_DEV, dg).transpose(1, 0, 2, 3).reshape(
            N_DEV, N_GROUPS * dg // N_DEV, dg)]))

    small = jnp.concatenate([
        _rows(jnp.concatenate([d_norm_mix0, d_norm_kv_q[1:2]], axis=0)),
        _rows(jnp.concatenate([d_norm_mlp0, d_norm_mlp1], axis=0)),
        _rows(d_norm_kv_q[0:1]),
        _rows(d_norm_out),
        _rows(d_scale),
        d_b,
        jnp.pad(loss_part[0:1, 0:1], ((0, 0), (0, LANES - 1))),
    ], axis=0)
    n_small = small.shape[0]
    small = jnp.pad(small, ((0, (-n_small) % 8), (0, 0)))
    total = _all_reduce_small(small)
    rd = D // LANES
    loss = total[7 * rd + 1, 0]
    g_scale_mine = lax.dynamic_slice(total[6 * rd:7 * rd].reshape(D), (my_block * ps_b,), (ps_b,))

    def pack(nm_, nl_, kv_, out_, bf_, ps_):
        return jnp.concatenate([_rows(nm_), _rows(nl_), _rows(kv_), _rows(out_), _rows(bf_), _rows(ps_)], axis=0)

    g_small = jnp.concatenate([total[:6 * rd], total[7 * rd:7 * rd + 1], _rows(g_scale_mine)], axis=0)
    w_small = pack(norm_mix, norm_mlp, norm_kv, norm_out, b_f, pool_scale)
    m_small = pack(m_norm_mix, m_norm_mlp, m_norm_kv, m_norm_out, m_b_f, m_pool_scale)
    v_small = pack(v_norm_mix, v_norm_mlp, v_norm_kv, v_norm_out, v_b_f, v_pool_scale)
    rs = g_small.shape[0]
    padr = (-rs) % 8
    pad8 = lambda a: jnp.pad(a, ((0, padr), (0, 0)))
    small_out = _adamw("adamw_small", pad8(g_small)[None], pad8(w_small), pad8(m_small), pad8(v_small), tr=rs + padr)

    def unpack(a):
        o0 = 0
        res = []
        for shape in [(2, D), (2, D), (D,), (D,)]:
            nr = (2 * rd) if len(shape) == 2 else rd
            res.append(a[o0:o0 + nr].reshape(shape))
            o0 += nr
        res.append(a[o0, :H])
        res.append(a[o0 + 1:o0 + 1 + ps_b // LANES].reshape(1, ps_b))
        return res

    small_res = [unpack(a) for a in small_out]

    r_pool = _adamw("adamw_pool", p_pool, pool_w.reshape(-1, dg), m_pool_w.reshape(-1, dg), v_pool_w.reshape(-1, dg))
    r_kvf = _adamw("adamw_kvf", p_kvf, w_kvf, m_w_kvf, v_w_kvf)
    r_q = _adamw("adamw_q", p_q, w_q[0], m_w_q[0], v_w_q[0])
    r_o = _adamw("adamw_o", p_o, w_o[0], m_w_o[0], v_w_o[0])
    me = my_block.astype(jnp.int32).reshape(1)
    r_up = _adamw_split("adamw_up0", pa_up0, pb_up0, NEAR, me, w_up, m_w_up, v_w_up, 0)
    r_up = _adamw("adamw_up1", p_up1, w_up, m_w_up, v_w_up, layer=1, into=r_up)
    r_down = _adamw_split("adamw_down0", pa_down0, pb_down0, NEAR, me, w_down, m_w_down, v_w_down, 0)
    r_down = _adamw("adamw_down1", p_down1, w_down, m_w_down, v_w_down, layer=1, into=r_down)

    def leaves(kind):
        sm = small_res[kind]
        return [
            sm[0], sm[1],
            r_pool[kind].reshape(pool_w.shape),
            sm[5],
            sm[2],
            r_kvf[kind],
            sm[4],
            r_q[kind][None], r_o[kind][None],
            r_up[kind],
            r_down[kind],
            sm[3],
        ]

    return (loss, grad_x.reshape(x.shape), *leaves(0), *leaves(1), *leaves(2), *leaves(3))
```

```python
import jax
import jax.numpy as jnp
from jax import lax
from jax.experimental import pallas as pl
from jax.experimental.pallas import tpu as pltpu

F32 = jnp.float32
BF16 = jnp.bfloat16
MESH = pl.DeviceIdType.MESH

N_DEV = 8
EPS = 1e-6
NEG_INF = -1e30
POOL_WINDOWS = (2, 4, 8, 16)
N_GROUPS = len(POOL_WINDOWS)
POOL_HALO = 16
HEAD_DIM = 128
LANES = 128
HEADS_PER_STEP = 2
LOG2E = 1.4426950408889634
LN2 = 0.6931471805599453

ADAM_LR = 0.001
ADAM_B1 = 0.9
ADAM_B2 = 0.999
ADAM_EPS = 1e-08
ADAM_WD = 0.01
ADAM_STEP = 10

VMEM_LIMIT = 56 * 1024 * 1024

_ANY = pl.BlockSpec(memory_space=pl.ANY)

NEAR = (0, 1, 2, 4, 6)
FAR = (3, 5, 7)


def _tile(n, want):
    t = min(n, want)
    assert n % t == 0, (n, want)
    return t


def _params(sem, vmem=VMEM_LIMIT):
    return pltpu.CompilerParams(dimension_semantics=sem, vmem_limit_bytes=vmem)


def _position():
    return lax.axis_index("x"), lax.axis_index("y"), lax.axis_index("c")


class _Exchange:
    def __init__(self, kind, arrays, peers=tuple(range(N_DEV))):
        assert kind in ("gather", "scatter")
        self.kind = kind
        self.peers = tuple(peers)
        self.arrays = list(arrays)
        self.n = len(self.arrays)
        shapes = [(N_DEV, *a.shape) if kind == "gather" else a.shape for a in self.arrays]
        self.out_shape = [jax.ShapeDtypeStruct(s, a.dtype) for s, a in zip(shapes, self.arrays)]
        self.scratch = [pltpu.SemaphoreType.DMA((self.n, N_DEV)), pltpu.SemaphoreType.DMA((self.n, N_DEV))]

    def _copies(self, ins, outs, send_sems, recv_sems, with_recv=True):
        x, y, c = _position()
        me = 4 * x + 2 * y + c
        gather = self.kind == "gather"
        local, sends, recvs = [], [], []
        for a in range(self.n):
            if 0 in self.peers:
                local.append(
                    pltpu.make_async_copy(ins[a] if gather else ins[a].at[me], outs[a].at[me], send_sems.at[a, 0]))
            for k in (k for k in self.peers if k):
                px, py, pc = x ^ (k >> 2), y ^ ((k >> 1) & 1), c ^ (k & 1)
                peer = 4 * px + 2 * py + pc
                src = ins[a] if gather else ins[a].at[peer]
                common = dict(send_sem=send_sems.at[a, k], recv_sem=recv_sems.at[a, k], device_id=(px, py, pc),
                              device_id_type=MESH)
                sends.append(pltpu.make_async_remote_copy(src_ref=src, dst_ref=outs[a].at[me], **common))
                if with_recv:
                    recvs.append(pltpu.make_async_remote_copy(src_ref=src, dst_ref=outs[a].at[peer], **common))
        return local, sends, recvs

    def start(self, ins, outs, send_sems, recv_sems):
        local, sends, _ = self._copies(ins, outs, send_sems, recv_sems, with_recv=False)
        for cp in local + sends:
            cp.start()

    def wait(self, ins, outs, send_sems, recv_sems):
        local, sends, recvs = self._copies(ins, outs, send_sems, recv_sems)
        for send, recv in zip(sends, recvs):
            send.wait_send()
            recv.wait_recv()
        for cp in local:
            cp.wait()


class _GatherByChip:
    middle_at = 0.85

    def __init__(self, arrays):
        self.arrays = list(arrays)
        self.n = len(self.arrays)
        self.out_shape = [jax.ShapeDtypeStruct((N_DEV, *a.shape), a.dtype) for a in self.arrays]
        self.scratch = [pltpu.SemaphoreType.DMA((self.n, N_DEV)), pltpu.SemaphoreType.DMA((self.n, N_DEV))]

    def _copy(self, a, k, outs, send_sems, recv_sems, src, slot_of, to):
        x, y, c = _position()
        dev = lambda kk: (x ^ (kk >> 2), y ^ ((kk >> 1) & 1), c ^ (kk & 1))
        px, py, pc = dev(slot_of)
        slot = outs[a].at[4 * px + 2 * py + pc]
        return pltpu.make_async_remote_copy(
            src_ref=slot if src is None else src, dst_ref=slot, send_sem=send_sems.at[a, k], recv_sem=recv_sems.at[a, k],
            device_id=dev(to), device_id_type=MESH)

    def start(self, ins, outs, send_sems, recv_sems):
        x, y, c = _position()
        for a in range(self.n):
            pltpu.make_async_copy(ins[a], outs[a].at[4 * x + 2 * y + c], send_sems.at[a, 0]).start()
            for k in (1, 2, 4, 6):
                self._copy(a, k, outs, send_sems, recv_sems, ins[a], 0, k).start()

    def middle(self, ins, outs, send_sems, recv_sems):
        for a in range(self.n):
            for k in (2, 4, 6):
                self._copy(a, k, outs, send_sems, recv_sems, None, k, 0).wait_recv()
                self._copy(a, k + 1, outs, send_sems, recv_sems, None, k, 1).start()

    def wait(self, ins, outs, send_sems, recv_sems):
        x, y, c = _position()
        for a in range(self.n):
            self._copy(a, 1, outs, send_sems, recv_sems, None, 1, 0).wait_recv()
            for k in (3, 5, 7):
                self._copy(a, k, outs, send_sems, recv_sems, None, k, 0).wait_recv()
            for k in (1, 2, 4, 6):
                self._copy(a, k, outs, send_sems, recv_sems, ins[a], 0, k).wait_send()
            for k in (2, 4, 6):
                self._copy(a, k + 1, outs, send_sems, recv_sems, None, k, 1).wait_send()
            pltpu.make_async_copy(ins[a], outs[a].at[4 * x + 2 * y + c], send_sems.at[a, 0]).wait()


def _call(body, *, name, grid, in_specs, out_specs, out_shape, args, scratch_shapes=(), sem=None, exchange=None):
    if exchange is None:
        outs = pl.pallas_call(
            body, grid=grid, in_specs=in_specs, out_specs=out_specs, out_shape=out_shape,
            scratch_shapes=list(scratch_shapes), compiler_params=_params(sem), name=name)(*args)
        return list(outs), []

    n_in, n_out, n_scr, n = len(in_specs), len(out_specs), len(scratch_shapes), exchange.n

    def hosted(*refs):
        ins, refs = refs[:n_in], refs[n_in:]
        ex_in, refs = refs[:n], refs[n:]
        outs, refs = refs[:n_out], refs[n_out:]
        ex_out, refs = refs[:n], refs[n:]
        scratch, sems = refs[:n_scr], refs[n_scr:]
        first = _all_true([pl.program_id(d) == 0 for d in range(len(grid))])
        last = _all_true([pl.program_id(d) == grid[d] - 1 for d in range(len(grid))])

        @pl.when(first)
        def _():
            exchange.start(ex_in, ex_out, *sems)

        if hasattr(exchange, "middle"):
            assert len(grid) and any(g > 1 for g in grid), "a host of a three-moment exchange needs more than one grid step"
            step, total = 0, 1
            for d in range(len(grid)):
                step = step * grid[d] + pl.program_id(d)
                total *= grid[d]

            @pl.when(step == max(1, min(total - 1, int(total * exchange.middle_at))))
            def _():
                exchange.middle(ex_in, ex_out, *sems)

        body(*ins, *outs, *scratch)

        @pl.when(last)
        def _():
            exchange.wait(ex_in, ex_out, *sems)

    res = pl.pallas_call(
        hosted, grid=grid, in_specs=[*in_specs, *[_ANY] * n], out_specs=[*out_specs, *[_ANY] * n],
        out_shape=[*out_shape, *exchange.out_shape], scratch_shapes=[*scratch_shapes, *exchange.scratch],
        compiler_params=_params(("arbitrary",) * len(grid)), name=name)(*args, *exchange.arrays)
    return list(res[:n_out]), list(res[n_out:])


def _all_true(preds):
    out = preds[0]
    for p in preds[1:]:
        out = jnp.logical_and(out, p)
    return out


def _exchange_now(name, exchange):
    def body(*refs):
        n = exchange.n
        exchange.start(refs[:n], refs[n:2 * n], *refs[2 * n:])
        exchange.wait(refs[:n], refs[n:2 * n], *refs[2 * n:])

    return pl.pallas_call(
        body, in_specs=[_ANY] * exchange.n, out_specs=[_ANY] * exchange.n, out_shape=exchange.out_shape,
        scratch_shapes=exchange.scratch, name=name)(*exchange.arrays)


def _all_gather_two_level(shards):
    n = len(shards)

    def body(*refs):
        ins, outs = refs[:n], refs[n:2 * n]
        send_sems, recv_sems, local_sems = refs[2 * n:]
        x, y, c = _position()
        me, sibling = (x, y, c), (x, y, 1 - c)
        chips = [(1 - x, y), (x, 1 - y), (1 - x, 1 - y)]

        def slot(a, px, py, pc):
            return outs[a].at[4 * px + 2 * py + pc]

        def copy(a, k, block, to, src=None):
            return pltpu.make_async_remote_copy(
                src_ref=slot(a, *block) if src is None else src, dst_ref=slot(a, *block),
                send_sem=send_sems.at[a, k], recv_sem=recv_sems.at[a, k], device_id=to, device_id_type=MESH)

        mine = [pltpu.make_async_copy(ins[a], slot(a, *me), local_sems.at[a]) for a in range(n)]
        first = []
        for a in range(n):
            mine[a].start()
            first.append(copy(a, 0, me, sibling, src=ins[a]))
            first += [copy(a, 1 + j, me, (*chip, c), src=ins[a]) for j, chip in enumerate(chips)]
        for cp in first:
            cp.start()
        passed = []
        for a in range(n):
            for j, chip in enumerate(chips):
                copy(a, 1 + j, (*chip, c), me).wait_recv()
                fwd = copy(a, 4 + j, (*chip, c), sibling)
                fwd.start()
                passed.append(fwd)
        for a in range(n):
            copy(a, 0, sibling, me).wait_recv()
            for j, chip in enumerate(chips):
                copy(a, 4 + j, (*chip, 1 - c), me).wait_recv()
        for cp in first + passed:
            cp.wait_send()
        for cp in mine:
            cp.wait()

    return pl.pallas_call(
        body, in_specs=[_ANY] * n, out_specs=[_ANY] * n,
        out_shape=[jax.ShapeDtypeStruct((N_DEV, *s.shape), s.dtype) for s in shards],
        scratch_shapes=[pltpu.SemaphoreType.DMA((n, 7)), pltpu.SemaphoreType.DMA((n, 7)),
                        pltpu.SemaphoreType.DMA((n,))],
        name="all_gather_first")(*shards)


def _all_reduce_small(vec, exchange):
    R = vec.shape[0]
    n = exchange.n

    def body(v_ref, *refs):
        ex_in, refs = refs[:n], refs[n:]
        o_ref, refs = refs[0], refs[1:]
        ex_out, refs = refs[:n], refs[n:]
        buf_ref, send_sems, recv_sems = refs[:3]
        ex_sems = refs[3:]
        exchange.start(ex_in, ex_out, *ex_sems)
        x, y, c = _position()
        me = 4 * x + 2 * y + c
        buf_ref[me] = v_ref[...]
        copies = []
        for k in range(1, N_DEV):
            peer = (x ^ (k >> 2), y ^ ((k >> 1) & 1), c ^ (k & 1))
            copies.append(pltpu.make_async_remote_copy(
                src_ref=buf_ref.at[me], dst_ref=buf_ref.at[me], send_sem=send_sems.at[k], recv_sem=recv_sems.at[k],
                device_id=peer, device_id_type=MESH))
        for cp in copies:
            cp.start()
        for cp in copies:
            cp.wait()
        total = buf_ref[0]
        for d in range(1, N_DEV):
            total = total + buf_ref[d]
        o_ref[...] = total
        exchange.wait(ex_in, ex_out, *ex_sems)

    vm = pl.BlockSpec(memory_space=pltpu.VMEM)
    res = pl.pallas_call(
        body, in_specs=[vm, *[_ANY] * n], out_specs=[vm, *[_ANY] * n],
        out_shape=[jax.ShapeDtypeStruct((R, LANES), F32), *exchange.out_shape],
        scratch_shapes=[pltpu.VMEM((N_DEV, R, LANES), F32), pltpu.SemaphoreType.DMA((N_DEV,)),
                        pltpu.SemaphoreType.DMA((N_DEV,)), *exchange.scratch],
        name="all_reduce_small")(vec, *exchange.arrays)
    return res[0], list(res[1:])


_DIMS = {
    "nn": (((1,), (0,)), ((), ())),
    "nt": (((1,), (1,)), ((), ())),
    "tn": (((0,), (0,)), ((), ())),
}


def _matmul(name, a, b, *, mode, grid, tm, tn, a_spec, b_spec, outs, extras=(), extra_specs=(), epilogue=None,
            acc_outs=(), exchange=None):
    nk = grid[2]
    n_extra = len(extras)
    dn = _DIMS[mode]

    def body(a_ref, b_ref, *rest):
        extra_refs = rest[:n_extra]
        out_refs = rest[n_extra:-1]
        acc_ref = rest[-1]
        j = pl.program_id(1)
        k = pl.program_id(2)

        def product():
            b = b_ref[...]
            if b.ndim == 3:
                b = b.reshape(-1, b.shape[-1]) if mode == "nn" else jnp.concatenate([b[0], b[1]], axis=1)
            return lax.dot_general(a_ref[...].astype(BF16), b.astype(BF16), dn, preferred_element_type=F32)

        def finish(acc):
            vals = (acc,) if epilogue is None else epilogue(acc, *[r[...] for r in extra_refs])
            for idx, (o_ref, val) in enumerate(zip(out_refs, vals)):
                if idx in acc_outs:
                    @pl.when(j == 0)
                    def _():
                        o_ref[...] = val.astype(o_ref.dtype)

                    @pl.when(j > 0)
                    def _():
                        o_ref[...] += val.astype(o_ref.dtype)
                else:
                    o_ref[...] = val.astype(o_ref.dtype)

        if nk == 1:
            finish(product())
            return

        @pl.when(k == 0)
        def _():
            acc_ref[...] = product()

        @pl.when(k > 0)
        def _():
            acc_ref[...] += product()

        @pl.when(k == nk - 1)
        def _():
            finish(acc_ref[...])

    return _call(
        body, name=name, grid=grid, in_specs=[a_spec, b_spec, *extra_specs], out_specs=[o[2] for o in outs],
        out_shape=[jax.ShapeDtypeStruct(o[0], o[1]) for o in outs], scratch_shapes=[pltpu.VMEM((tm, tn), F32)],
        sem=("parallel", "arbitrary" if acc_outs else "parallel", "arbitrary"), args=(a, b, *extras), exchange=exchange)


def _ij(tm, tn):
    return pl.BlockSpec((tm, tn), lambda i, j, k: (i, j))


def _mm_nn(name, a, b, *, out_dtype, residual=None, out_scale=None, exchange=None, tm=1024, tn=1024, tk=2048):
    M, K = a.shape
    N = b.shape[1]
    tm, tn, tk = _tile(M, tm), _tile(N, tn), _tile(K, tk)
    extras, especs, epi = (), (), None
    if residual is not None:
        extras, especs = (residual,), (_ij(tm, tn),)
        epi = lambda acc, r: (acc + r,)
    elif out_scale is not None:
        epi = lambda acc: (acc * out_scale,)
    outs, got = _matmul(
        name, a, b, mode="nn", grid=(M // tm, N // tn, K // tk), tm=tm, tn=tn,
        a_spec=pl.BlockSpec((tm, tk), lambda i, j, k: (i, k)),
        b_spec=pl.BlockSpec((tk, tn), lambda i, j, k: (k, j)),
        outs=[((M, N), out_dtype, _ij(tm, tn))], extras=extras, extra_specs=especs, epilogue=epi, exchange=exchange)
    return outs[0] if exchange is None else (outs[0], got)


def _mm_nt(name, a, b, *, out_dtype, b_cols=None, residual=None, tm=1024, tn=1024, tk=2048):
    M, K = a.shape
    N = b.shape[0]
    c0 = 0 if b_cols is None else b_cols[0]
    tm, tn, tk = _tile(M, tm), _tile(N, tn), _tile(K, tk)
    assert c0 % tk == 0
    kb0 = c0 // tk
    extras, especs, epi = (), (), None
    if residual is not None:
        extras, especs = (residual,), (_ij(tm, tn),)
        epi = lambda acc, r: (acc + r,)
    return _matmul(
        name, a, b, mode="nt", grid=(M // tm, N // tn, K // tk), tm=tm, tn=tn,
        a_spec=pl.BlockSpec((tm, tk), lambda i, j, k: (i, k)),
        b_spec=pl.BlockSpec((tn, tk), lambda i, j, k: (j, kb0 + k)),
        outs=[((M, N), out_dtype, _ij(tm, tn))], extras=extras, extra_specs=especs, epilogue=epi)[0][0]


def _mm_tn(name, a, b, *, exchange=None, tm=1024, tn=1024, tk=2048):
    K, M = a.shape
    N = b.shape[1]
    tm, tn, tk = _tile(M, tm), _tile(N, tn), _tile(K, tk)
    outs, got = _matmul(
        name, a, b, mode="tn", grid=(M // tm, N // tn, K // tk), tm=tm, tn=tn,
        a_spec=pl.BlockSpec((tk, tm), lambda i, j, k: (k, i)),
        b_spec=pl.BlockSpec((tk, tn), lambda i, j, k: (k, j)),
        outs=[((M, N), BF16, _ij(tm, tn))], exchange=exchange)
    return outs[0], got


def _kvf_dx(dk, dv, d_f, w_kv, w_f, *, tm=1024, tn=1024):
    S, D = dk.shape
    tm, tn = _tile(S, tm), _tile(D, tn)

    def body(dk_ref, dv_ref, df_ref, wk_ref, wv_ref, wf_ref, o_ref):
        acc = lax.dot_general(dk_ref[...], wk_ref[...], _DIMS["nt"], preferred_element_type=F32)
        acc = acc + lax.dot_general(dv_ref[...], wv_ref[...], _DIMS["nt"], preferred_element_type=F32)
        o_ref[...] = acc + lax.dot_general(df_ref[...].astype(BF16), wf_ref[...], _DIMS["nt"],
                                           preferred_element_type=F32)

    row = lambda width: pl.BlockSpec((tm, width), lambda i, j: (i, 0))
    return pl.pallas_call(
        body, grid=(S // tm, D // tn),
        in_specs=[row(D), row(D), row(LANES),
                  pl.BlockSpec((tn, D), lambda i, j: (j, 0)), pl.BlockSpec((tn, D), lambda i, j: (j, 1)),
                  pl.BlockSpec((tn, LANES), lambda i, j: (j, 0))],
        out_specs=pl.BlockSpec((tm, tn), lambda i, j: (i, j)), out_shape=jax.ShapeDtypeStruct((S, D), F32),
        compiler_params=_params(("parallel", "parallel")), name="kvf_proj_dx")(dk, dv, d_f, w_kv, w_kv, w_f)


def _mlp_fwd(tag, hn, h, w_up_g, w_down_g, *, ex_up=None, ex_down=None, tm=1024, tk=2048):
    S, D = hn.shape
    fb = w_up_g.shape[2]
    F = N_DEV * fb
    tm, tku = _tile(S, tm), _tile(D, tk)

    def up_epi(acc):
        u = jnp.maximum(acc, 0.0)
        return u, u * u

    (u, uu), got_up = _matmul(
        f"mlp_up_{tag}", hn, w_up_g, mode="nn", grid=(S // tm, N_DEV, D // tku), tm=tm, tn=fb,
        a_spec=pl.BlockSpec((tm, tku), lambda i, j, k: (i, k)),
        b_spec=pl.BlockSpec((None, tku, fb), lambda i, j, k: (j, k, 0)),
        outs=[((S, F), BF16, _ij(tm, fb)), ((S, F), BF16, _ij(tm, fb))], epilogue=up_epi, exchange=ex_up)
    if w_down_g is None:
        w_down_g = got_up[0]

    tn = _tile(D, 1024)
    (h_out,), got_down = _matmul(
        f"mlp_down_{tag}", uu, w_down_g, mode="nn", grid=(S // tm, D // tn, N_DEV // 2), tm=tm, tn=tn,
        a_spec=pl.BlockSpec((tm, 2 * fb), lambda i, j, k: (i, k)),
        b_spec=pl.BlockSpec((2, fb, tn), lambda i, j, k: (k, 0, j)),
        outs=[((S, D), F32, _ij(tm, tn))], extras=(h,), extra_specs=(_ij(tm, tn),),
        epilogue=lambda acc, res: (acc + res,), exchange=ex_down)
    return u, uu, h_out, got_up, got_down


def _mlp_dpre(tag, dh, u, w_down_g, *, exchange=None, tm=1024, tk=2048):
    S, D = dh.shape
    fb = w_down_g.shape[1]
    tm, tkd = _tile(S, tm), _tile(D, tk)
    outs, got = _matmul(
        f"mlp_dpre_{tag}", dh, w_down_g, mode="nt", grid=(S // tm, N_DEV, D // tkd), tm=tm, tn=fb,
        a_spec=pl.BlockSpec((tm, tkd), lambda i, j, k: (i, k)),
        b_spec=pl.BlockSpec((None, fb, tkd), lambda i, j, k: (j, 0, k)),
        outs=[((S, N_DEV * fb), BF16, _ij(tm, fb))], extras=(u,), extra_specs=(_ij(tm, fb),),
        epilogue=lambda acc, uv: (acc * (2.0 * uv.astype(F32)),), exchange=exchange)
    return outs[0], got


def _mlp_dwup(tag, hn, d_pre, fb, *, exchange=None):
    S, D = hn.shape
    tmu = _tile(D, 1024)
    tks = _tile(S, 2048)
    outs, got = _matmul(
        f"mlp_dwup_{tag}", hn, d_pre, mode="tn", grid=(D // tmu, N_DEV, S // tks), tm=tmu, tn=fb,
        a_spec=pl.BlockSpec((tks, tmu), lambda i, j, k: (k, i)),
        b_spec=pl.BlockSpec((tks, fb), lambda i, j, k: (k, j)),
        outs=[((N_DEV, D, fb), BF16, pl.BlockSpec((None, tmu, fb), lambda i, j, k: (j, i, 0)))], exchange=exchange)
    return outs[0], got


def _mlp_dhn(tag, d_pre, w_up_g, *, exchange=None, tm=1024):
    S, F = d_pre.shape
    D, fb = w_up_g.shape[1], w_up_g.shape[2]
    tm, tn = _tile(S, tm), _tile(D, 1024)
    outs, got = _matmul(
        f"mlp_dhn_{tag}", d_pre, w_up_g, mode="nt", grid=(S // tm, D // tn, N_DEV // 2), tm=tm, tn=tn,
        a_spec=pl.BlockSpec((tm, 2 * fb), lambda i, j, k: (i, k)),
        b_spec=pl.BlockSpec((2, tn, fb), lambda i, j, k: (k, j, 0)),
        outs=[((S, D), F32, _ij(tm, tn))], exchange=exchange)
    return outs[0], got


def _row_spec(ts, D):
    return pl.BlockSpec((ts, D), lambda i: (i, 0))


def _vec_spec(n, D):
    return pl.BlockSpec((n, D), lambda i: (0, 0))


def _rms_fwd(name, x, gains, *, ts=512):
    S, D = x.shape
    n = gains.shape[0]
    ts = _tile(S, ts)

    def body(x_ref, g_ref, *o_refs):
        xv = x_ref[...]
        y = xv * lax.rsqrt(jnp.mean(xv * xv, axis=-1, keepdims=True) + EPS)
        for i, o_ref in enumerate(o_refs):
            o_ref[...] = (y * g_ref[i:i + 1, :]).astype(o_ref.dtype)

    return pl.pallas_call(
        body, grid=(S // ts,), in_specs=[_row_spec(ts, D), _vec_spec(n, D)],
        out_specs=[_row_spec(ts, D)] * n, out_shape=[jax.ShapeDtypeStruct((S, D), BF16)] * n,
        compiler_params=_params(("parallel",)), name=name)(x, gains)


def _rms_bwd(name, x, gains, dys, res, *, matmul_copy=True):
    S, D = x.shape
    n = gains.shape[0]
    ts = _tile(S, 512 if n == 1 else 256)

    def body(x_ref, g_ref, *rest):
        dy_refs = rest[:n]
        res_ref, dx_ref = rest[n:n + 2]
        dxb_ref = rest[n + 2] if matmul_copy else None
        dg_ref = rest[-1]
        i = pl.program_id(0)
        xv = x_ref[...]
        r = lax.rsqrt(jnp.mean(xv * xv, axis=-1, keepdims=True) + EPS)
        xhat = xv * r
        dxhat = None
        dgs = []
        for k in range(n):
            dy = dy_refs[k][...].astype(F32)
            dgs.append(jnp.sum(dy * xhat, axis=0, keepdims=True))
            term = dy * g_ref[k:k + 1, :]
            dxhat = term if dxhat is None else dxhat + term
        dx = res_ref[...] + r * (dxhat - xhat * jnp.mean(dxhat * xhat, axis=-1, keepdims=True))
        dx_ref[...] = dx
        if matmul_copy:
            dxb_ref[...] = dx.astype(BF16)
        dg = jnp.concatenate(dgs, axis=0) if n > 1 else dgs[0]

        @pl.when(i == 0)
        def _():
            dg_ref[...] = dg

        @pl.when(i > 0)
        def _():
            dg_ref[...] += dg

    copies = [(_row_spec(ts, D), jax.ShapeDtypeStruct((S, D), BF16))] if matmul_copy else []
    outs = pl.pallas_call(
        body, grid=(S // ts,),
        in_specs=[_row_spec(ts, D), _vec_spec(n, D)] + [_row_spec(ts, D)] * (n + 1),
        out_specs=[_row_spec(ts, D), *[c[0] for c in copies], _vec_spec(n, D)],
        out_shape=[jax.ShapeDtypeStruct((S, D), F32), *[c[1] for c in copies], jax.ShapeDtypeStruct((n, D), F32)],
        compiler_params=_params(("arbitrary",)), name=name)(x, gains, *dys, res)
    return (outs[0], outs[1], outs[2]) if matmul_copy else (outs[0], None, outs[1])


def _loss_head(h, gain, target, *, ts=512):
    S, D = h.shape
    ts = _tile(S, ts)

    def body(x_ref, g_ref, t_ref, dx_ref, dxb_ref, dg_ref, loss_ref):
        i = pl.program_id(0)
        xv = x_ref[...]
        g = g_ref[...]
        r = lax.rsqrt(jnp.mean(xv * xv, axis=-1, keepdims=True) + EPS)
        xhat = xv * r
        err = xhat * g - t_ref[...]
        part = 0.5 * jnp.sum(jnp.mean(err * err, axis=-1, keepdims=True), axis=0, keepdims=True)
        dy = err * (1.0 / D)
        dg = jnp.sum(dy * xhat, axis=0, keepdims=True)
        dxhat = dy * g
        dx = r * (dxhat - xhat * jnp.mean(dxhat * xhat, axis=-1, keepdims=True))
        dx_ref[...] = dx
        dxb_ref[...] = dx.astype(BF16)

        @pl.when(i == 0)
        def _():
            dg_ref[...] = dg
            loss_ref[...] = jnp.broadcast_to(part, loss_ref.shape)

        @pl.when(i > 0)
        def _():
            dg_ref[...] += dg
            loss_ref[...] += jnp.broadcast_to(part, loss_ref.shape)

    return pl.pallas_call(
        body, grid=(S // ts,),
        in_specs=[_row_spec(ts, D), _vec_spec(1, D), _row_spec(ts, D)],
        out_specs=[_row_spec(ts, D), _row_spec(ts, D), _vec_spec(1, D), pl.BlockSpec((8, LANES), lambda i: (0, 0))],
        out_shape=[jax.ShapeDtypeStruct((S, D), F32), jax.ShapeDtypeStruct((S, D), BF16),
                   jax.ShapeDtypeStruct((1, D), F32), jax.ShapeDtypeStruct((8, LANES), F32)],
        compiler_params=_params(("arbitrary",)), name="loss_head")(h, gain, target)


def _window_counts(t, w):
    return jnp.minimum(t + 1, w).astype(F32)


def _pool_fwd(x, gain, pool_w, scale, gain_next, *, exchange=None, ts=256):
    S, D = x.shape
    dg = D // N_GROUPS
    ts = _tile(S, ts)
    per = ts // POOL_HALO

    def body(x_ref, xh_ref, g_ref, w_ref, sc_ref, gn_ref, h_ref, diff_ref, hn_ref):
        i = pl.program_id(0)
        g = g_ref[...]
        h_parts = []

        def norm(v):
            return v * lax.rsqrt(jnp.mean(v * v, axis=-1, keepdims=True) + EPS) * g

        xc = x_ref[...]
        hn_c = norm(xc)
        hn_h = norm(xh_ref[...]) * (i > 0).astype(F32)
        ext = jnp.concatenate([hn_h, hn_c], axis=0)
        t = i * ts + lax.broadcasted_iota(jnp.int32, (ts, 1), 0)
        for gi, w in enumerate(POOL_WINDOWS):
            cols = slice(gi * dg, (gi + 1) * dg)
            s = ext[:, cols]
            step = 1
            while step < w:
                s = s + pltpu.roll(s, step, 0)
                step *= 2
            mean = s[POOL_HALO:] * (1.0 / _window_counts(t, w))
            diff = (mean - hn_c[:, cols]).astype(BF16)
            diff_ref[:, cols] = diff
            mixed = jnp.dot(diff, w_ref[gi], preferred_element_type=F32)
            h_parts.append(xc[:, cols] + mixed * sc_ref[:, cols])
        h = jnp.concatenate(h_parts, axis=1)
        h_ref[...] = h
        hn_ref[...] = (h * lax.rsqrt(jnp.mean(h * h, axis=-1, keepdims=True) + EPS) * gn_ref[...]).astype(BF16)

    (h, diff, hn), got = _call(
        body, name="pool_fwd", grid=(S // ts,),
        in_specs=[_row_spec(ts, D),
                  pl.BlockSpec((POOL_HALO, D), lambda i: (jnp.maximum(i * per - 1, 0), 0)),
                  _vec_spec(1, D), pl.BlockSpec((N_GROUPS, dg, dg), lambda i: (0, 0, 0)), _vec_spec(1, D),
                  _vec_spec(1, D)],
        out_specs=[_row_spec(ts, D), _row_spec(ts, D), _row_spec(ts, D)],
        out_shape=[jax.ShapeDtypeStruct((S, D), F32), jax.ShapeDtypeStruct((S, D), BF16),
                   jax.ShapeDtypeStruct((S, D), BF16)],
        sem=("parallel",), args=(x, x, gain, pool_w, scale, gain_next), exchange=exchange)
    return h, diff, hn, got


def _pool_bwd(x, dh, diff, gain, pool_w, scale, *, exchange=None, ts=512):
    S, D = x.shape
    dg = D // N_GROUPS
    ts = _tile(S, ts)
    per = ts // POOL_HALO
    n_tiles = S // ts
    n_halo = S // POOL_HALO
    ext_rows = ts + POOL_HALO

    def body(x_ref, dh_ref, dhn_ref, diff_ref, g_ref, w_ref, sc_ref, dx_ref, dw_ref, dsc_ref, dgain_ref):
        i = pl.program_id(0)
        xc = x_ref[...]
        g = g_ref[...]
        r = lax.rsqrt(jnp.mean(xc * xc, axis=-1, keepdims=True) + EPS)
        xhat = xc * r
        dh_c = dh_ref[...]
        dh_n = dhn_ref[...] * (i < n_tiles - 1).astype(F32)
        dh_ext = jnp.concatenate([dh_c, dh_n], axis=0)
        t_ext = i * ts + lax.broadcasted_iota(jnp.int32, (ext_rows, 1), 0)
        d_hn_parts, dsc_parts = [], []
        for gi, w in enumerate(POOL_WINDOWS):
            cols = slice(gi * dg, (gi + 1) * dg)
            wg = w_ref[gi]
            dmix = (dh_ext[:, cols] * sc_ref[:, cols]).astype(BF16)
            d_diff = lax.dot_general(dmix, wg, _DIMS["nt"], preferred_element_type=F32)
            diff_c = diff_ref[:, cols]
            dwg = lax.dot_general(diff_c, dmix[:ts], _DIMS["tn"], preferred_element_type=F32)
            mixed = jnp.dot(diff_c, wg, preferred_element_type=F32)
            dsc_parts.append(jnp.sum(dh_c[:, cols] * mixed, axis=0, keepdims=True))
            e = d_diff * (1.0 / _window_counts(t_ext, w))
            step = 1
            while step < w:
                e = e + pltpu.roll(e, ext_rows - step, 0)
                step *= 2
            d_hn_parts.append(e[:ts] - d_diff[:ts])

            @pl.when(i == 0)
            def _():
                dw_ref[gi] = dwg

            @pl.when(i > 0)
            def _():
                dw_ref[gi] += dwg

        d_hn = jnp.concatenate(d_hn_parts, axis=1)
        dsc = jnp.concatenate(dsc_parts, axis=1)
        dgain = jnp.sum(d_hn * xhat, axis=0, keepdims=True)
        dxhat = d_hn * g
        dx_ref[...] = dh_c + r * (dxhat - xhat * jnp.mean(dxhat * xhat, axis=-1, keepdims=True))

        @pl.when(i == 0)
        def _():
            dsc_ref[...] = dsc
            dgain_ref[...] = dgain

        @pl.when(i > 0)
        def _():
            dsc_ref[...] += dsc
            dgain_ref[...] += dgain

    outs, got = _call(
        body, name="pool_bwd", grid=(n_tiles,),
        in_specs=[_row_spec(ts, D), _row_spec(ts, D),
                  pl.BlockSpec((POOL_HALO, D), lambda i: (jnp.minimum((i + 1) * per, n_halo - 1), 0)),
                  _row_spec(ts, D), _vec_spec(1, D),
                  pl.BlockSpec((N_GROUPS, dg, dg), lambda i: (0, 0, 0)), _vec_spec(1, D)],
        out_specs=[_row_spec(ts, D), pl.BlockSpec((N_GROUPS, dg, dg), lambda i: (0, 0, 0)),
                   _vec_spec(1, D), _vec_spec(1, D)],
        out_shape=[jax.ShapeDtypeStruct((S, D), F32), jax.ShapeDtypeStruct((N_GROUPS, dg, dg), F32),
                   jax.ShapeDtypeStruct((1, D), F32), jax.ShapeDtypeStruct((1, D), F32)],
        sem=("arbitrary",), args=(x, dh, dh, diff, gain, pool_w, scale), exchange=exchange)
    return (*outs, got)


def _gate_fwd(f_raw, b_pad, *, ts=512):
    S = f_raw.shape[0]
    ts = _tile(S, ts)

    def body(f_ref, b_ref, c_ref, carry_ref):
        i = pl.program_id(0)

        @pl.when(i == 0)
        def _():
            carry_ref[...] = jnp.zeros_like(carry_ref)

        z = f_ref[...] + b_ref[...]
        v = jnp.minimum(z, 0.0) - jnp.log1p(jnp.exp(-jnp.abs(z)))
        row = lax.broadcasted_iota(jnp.int32, (ts, 1), 0)
        step = 1
        while step < ts:
            v = v + jnp.where(row >= step, pltpu.roll(v, step, 0), 0.0)
            step *= 2
        out = v + carry_ref[0:1, :]
        c_ref[...] = out
        carry_ref[...] = jnp.broadcast_to(out[ts - 1:ts, :], carry_ref.shape)

    return pl.pallas_call(
        body, grid=(S // ts,),
        in_specs=[pl.BlockSpec((ts, LANES), lambda i: (i, 0)), pl.BlockSpec((1, LANES), lambda i: (0, 0))],
        out_specs=pl.BlockSpec((ts, LANES), lambda i: (i, 0)),
        out_shape=jax.ShapeDtypeStruct((S, LANES), F32),
        scratch_shapes=[pltpu.VMEM((8, LANES), F32)],
        compiler_params=_params(("arbitrary",)), name="gate_fwd")(f_raw, b_pad)


def _gate_bwd(dc, f_raw, b_pad, *, ts=512):
    S = f_raw.shape[0]
    ts = _tile(S, ts)
    n = S // ts

    def body(dc_ref, f_ref, b_ref, df_ref, db_ref, carry_ref):
        i = pl.program_id(0)

        @pl.when(i == 0)
        def _():
            carry_ref[...] = jnp.zeros_like(carry_ref)

        v = dc_ref[...]
        row = lax.broadcasted_iota(jnp.int32, (ts, 1), 0)
        step = 1
        while step < ts:
            v = v + jnp.where(row < ts - step, pltpu.roll(v, ts - step, 0), 0.0)
            step *= 2
        d_logf = v + carry_ref[0:1, :]
        carry_ref[...] = jnp.broadcast_to(d_logf[0:1, :], carry_ref.shape)
        z = f_ref[...] + b_ref[...]
        df = d_logf / (1.0 + jnp.exp(z))
        df_ref[...] = df
        db = jnp.sum(df, axis=0, keepdims=True)

        @pl.when(i == 0)
        def _():
            db_ref[...] = db

        @pl.when(i > 0)
        def _():
            db_ref[...] += db

    rev = lambda i: (n - 1 - i, 0)
    return pl.pallas_call(
        body, grid=(n,),
        in_specs=[pl.BlockSpec((ts, LANES), rev), pl.BlockSpec((ts, LANES), rev),
                  pl.BlockSpec((1, LANES), lambda i: (0, 0))],
        out_specs=[pl.BlockSpec((ts, LANES), rev), pl.BlockSpec((1, LANES), lambda i: (0, 0))],
        out_shape=[jax.ShapeDtypeStruct((S, LANES), F32), jax.ShapeDtypeStruct((1, LANES), F32)],
        scratch_shapes=[pltpu.VMEM((8, LANES), F32)],
        compiler_params=_params(("arbitrary",)), name="gate_bwd")(dc, f_raw, b_pad)


def _row_layout(a_hs, t):
    n_heads, S = a_hs.shape
    return a_hs.reshape(n_heads, S // t, 1, t)


def _lane_pick(blk, h):
    lane = lax.broadcasted_iota(jnp.int32, blk.shape, 1)
    return jnp.sum(jnp.where(lane == h, blk, 0.0), axis=-1, keepdims=True)


def _lane_put(ref, h, col, first=True):
    lane = lax.broadcasted_iota(jnp.int32, ref.shape, 1)
    if not first:
        ref[...] = jnp.where(lane == h, col, ref[...])
        return

    @pl.when(h == 0)
    def _():
        ref[...] = jnp.where(lane == 0, col, 0.0)

    @pl.when(h > 0)
    def _():
        ref[...] = jnp.where(lane == h, col, ref[...])


def _causal(s, masked, fill, rows_are_queries=True):
    if not masked:
        return s
    rr = lax.broadcasted_iota(jnp.int32, s.shape, 0)
    cc = lax.broadcasted_iota(jnp.int32, s.shape, 1)
    keep = (cc <= rr) if rows_are_queries else (rr <= cc)
    return jnp.where(keep, s, fill)


def _fox_fwd(q2, kv, c_row, *, t=512, exchange=None):
    S, D = q2.shape
    H = D // HEAD_DIM
    t = _tile(S, t)
    nb = S // t
    hp = HEADS_PER_STEP
    wide = hp * HEAD_DIM

    def body(q_ref, k_ref, v_ref, cr_ref, o_ref, l2_ref):
        i = pl.program_id(0)
        g = pl.program_id(1)
        cols = [slice(a * HEAD_DIM, (a + 1) * HEAD_DIM) for a in range(hp)]
        refs_i = [cr_ref[a, i][:, 0:1] for a in range(hp)]
        qs = [q_ref[:, cols[a]] for a in range(hp)]

        def step(j, carry, masked):
            r0 = pl.multiple_of(j * t, t)
            out = []
            for a in range(hp):
                m, l, acc = carry[3 * a:3 * a + 3]
                kb = k_ref[pl.ds(r0, t), cols[a]]
                vb = v_ref[pl.ds(r0, t), cols[a]]
                ck = cr_ref[a, j] - refs_i[a]
                s = lax.dot_general(qs[a], kb, _DIMS["nt"], preferred_element_type=F32) - ck
                s = _causal(s, masked, NEG_INF)
                m_new = jnp.maximum(m, jnp.max(s, axis=-1, keepdims=True))
                alpha = jnp.exp2(m - m_new)
                p = jnp.exp2(s - m_new)
                l = alpha * l + jnp.sum(p, axis=-1, keepdims=True)
                acc = alpha * acc + jnp.dot(p.astype(BF16), vb, preferred_element_type=F32)
                out += [m_new, l, acc]
            return tuple(out)

        init = (jnp.full((t, 1), NEG_INF, F32), jnp.zeros((t, 1), F32), jnp.zeros((t, HEAD_DIM), F32)) * hp
        carry = lax.fori_loop(0, i, lambda j, c: step(j, c, False), init)
        carry = step(i, carry, True)
        for a in range(hp):
            m, l, acc = carry[3 * a:3 * a + 3]
            o_ref[:, cols[a]] = (acc / l).astype(o_ref.dtype)
            _lane_put(l2_ref, g * hp + a, m + jnp.log2(l), first=(a == 0))

    (o, l2_sh), got = _call(
        body, name="fox_fwd", grid=(nb, H // hp),
        in_specs=[pl.BlockSpec((t, wide), lambda i, g: (i, g)),
                  pl.BlockSpec((S, wide), lambda i, g: (0, g)),
                  pl.BlockSpec((S, wide), lambda i, g: (0, H // hp + g)),
                  pl.BlockSpec((hp, nb, 1, t), lambda i, g: (g, 0, 0, 0))],
        out_specs=[pl.BlockSpec((t, wide), lambda i, g: (i, g)),
                   pl.BlockSpec((t, LANES), lambda i, g: (i, 0))],
        out_shape=[jax.ShapeDtypeStruct((S, D), BF16), jax.ShapeDtypeStruct((S, LANES), F32)],
        sem=("parallel", "arbitrary"), args=(q2, kv, kv, c_row), exchange=exchange)
    return o, l2_sh, got


def _o_proj_dx(dh, wo, o, *, tm=1024, tn=1024):
    S, D = dh.shape
    tm, tn = _tile(S, tm), _tile(D, tn)
    heads = tn // HEAD_DIM

    def epilogue(acc, ov):
        j = pl.program_id(1)
        lane = lax.broadcasted_iota(jnp.int32, (tm, LANES), 1)
        prod = acc * ov.astype(F32)
        blk = jnp.zeros((tm, LANES), F32)
        for hh in range(heads):
            col = jnp.sum(prod[:, hh * HEAD_DIM:(hh + 1) * HEAD_DIM], axis=-1, keepdims=True)
            blk = jnp.where(lane == j * heads + hh, col, blk)
        return acc, blk

    (do, delta_sh), _ = _matmul(
        "o_proj_dx", dh, wo, mode="nt", grid=(S // tm, D // tn, 1), tm=tm, tn=tn,
        a_spec=pl.BlockSpec((tm, D), lambda i, j, k: (i, 0)),
        b_spec=pl.BlockSpec((tn, D), lambda i, j, k: (j, 0)),
        outs=[((S, D), BF16, _ij(tm, tn)), ((S, LANES), F32, pl.BlockSpec((tm, LANES), lambda i, j, k: (i, 0)))],
        extras=(o,), extra_specs=(_ij(tm, tn),), epilogue=epilogue, acc_outs=(1,))
    return do, delta_sh


def _fox_bwd(q2, kv, do, c_sh, c_row, l2_row, delta_row, *, t=512, exchange=None):
    S, D = q2.shape
    H = D // HEAD_DIM
    t = _tile(S, t)
    nb = S // t
    scale = HEAD_DIM ** -0.5

    def body(q_ref, do_ref, k_ref, v_ref, csh_ref, cr_ref, l2_ref, dl_ref, dq_ref, dk_ref, dv_ref, dck_ref, dcq_ref,
             dq_acc):
        h = pl.program_id(0)
        j = pl.program_id(1)
        ck = _lane_pick(csh_ref[...], h)
        kb = k_ref[...]
        vb = v_ref[...]

        @pl.when(j == 0)
        def _():
            dq_acc[...] = jnp.zeros_like(dq_acc)
            dcq_ref[...] = jnp.zeros_like(dcq_ref)

        def step(i, carry, masked):
            dk, dv, dc = carry
            r0 = pl.multiple_of(i * t, t)
            qb = q_ref[pl.ds(r0, t), :]
            dob = do_ref[pl.ds(r0, t), :]
            s = lax.dot_general(kb, qb, _DIMS["nt"], preferred_element_type=F32) - (ck - cr_ref[i][:, 0:1])
            p = _causal(jnp.exp2(s - l2_ref[i]), masked, 0.0, rows_are_queries=False)
            dv = dv + jnp.dot(p.astype(BF16), dob, preferred_element_type=F32)
            dp = lax.dot_general(vb, dob, _DIMS["nt"], preferred_element_type=F32)
            ds = p * (dp - dl_ref[i])
            dc = dc - jnp.sum(ds, axis=-1, keepdims=True)
            dcq_ref[i] += jnp.sum(ds, axis=0, keepdims=True)
            dsb = ds.astype(BF16)
            dk = dk + jnp.dot(dsb, qb, preferred_element_type=F32)
            dq_acc[pl.ds(r0, t), :] += lax.dot_general(dsb, kb, _DIMS["tn"], preferred_element_type=F32)
            return dk, dv, dc

        init = (jnp.zeros((t, HEAD_DIM), F32), jnp.zeros((t, HEAD_DIM), F32), jnp.zeros((t, 1), F32))
        carry = step(j, init, True)
        dk, dv, dc = lax.fori_loop(j + 1, nb, lambda i, c: step(i, c, False), carry)
        dk_ref[...] = (dk * LN2).astype(dk_ref.dtype)
        dv_ref[...] = dv.astype(dv_ref.dtype)
        rows = pl.ds(pl.multiple_of(j * t, t), t)
        lane = lax.broadcasted_iota(jnp.int32, (t, LANES), 1)

        @pl.when(h == 0)
        def _():
            dck_ref[rows, :] = jnp.where(lane == 0, dc, 0.0)

        @pl.when(h > 0)
        def _():
            dck_ref[rows, :] = jnp.where(lane == h, dc, dck_ref[rows, :])

        @pl.when(j == nb - 1)
        def _():
            dq_ref[...] = (dq_acc[...] * scale).astype(dq_ref.dtype)

    kspec = pl.BlockSpec((t, HEAD_DIM), lambda h, j: (j, h))
    headspec = pl.BlockSpec((S, HEAD_DIM), lambda h, j: (0, h))
    rowspec = pl.BlockSpec((None, nb, 1, t), lambda h, j: (h, 0, 0, 0))
    (dq, dk, dv, dck_sh, dcq_row), got = _call(
        body, name="fox_bwd", grid=(H, nb),
        in_specs=[headspec, headspec, kspec,
                  pl.BlockSpec((t, HEAD_DIM), lambda h, j: (j, H + h)),
                  pl.BlockSpec((t, LANES), lambda h, j: (j, 0)),
                  rowspec, rowspec, rowspec],
        out_specs=[headspec, kspec, kspec, pl.BlockSpec((S, LANES), lambda h, j: (0, 0)), rowspec],
        out_shape=[jax.ShapeDtypeStruct((S, D), BF16), jax.ShapeDtypeStruct((S, D), BF16),
                   jax.ShapeDtypeStruct((S, D), BF16), jax.ShapeDtypeStruct((S, LANES), F32),
                   jax.ShapeDtypeStruct((H, nb, 1, t), F32)],
        scratch_shapes=[pltpu.VMEM((S, HEAD_DIM), F32)],
        sem=("arbitrary", "arbitrary"), args=(q2, do, kv, kv, c_sh, c_row, l2_row, delta_row), exchange=exchange)
    return dq, dk, dv, dck_sh, dcq_row, got


def _adamw_update(g, w_ref, m_ref, v_ref, g_ref, d_ref, nm_ref, nv_ref):
    nm = ADAM_B1 * m_ref[...] + (1.0 - ADAM_B1) * g
    nv = ADAM_B2 * v_ref[...] + (1.0 - ADAM_B2) * (g * g)
    m_hat = nm / (1.0 - ADAM_B1 ** ADAM_STEP)
    v_hat = nv / (1.0 - ADAM_B2 ** ADAM_STEP)
    g_ref[...] = g
    d_ref[...] = -ADAM_LR * (m_hat / (jnp.sqrt(v_hat) + ADAM_EPS) + ADAM_WD * w_ref[...])
    nm_ref[...] = nm
    nv_ref[...] = nv


def _adamw_split(name, parts_a, parts_b, peers_a, me, w, m, v, layer, *, tr=128):
    _, R, C = parts_a.shape
    tr = _tile(R, tr)

    def body(me_ref, *refs):
        p_refs = refs[:N_DEV]
        w_ref, m_ref, v_ref = refs[N_DEV:N_DEV + 3]
        g = p_refs[0][...].astype(F32)
        for k in range(1, N_DEV):
            g = g + p_refs[k][...].astype(F32)
        _adamw_update(g, w_ref, m_ref, v_ref, *refs[-4:])

    def pspec(k):
        return pl.BlockSpec((None, tr, C), lambda i, me_ref: (jnp.bitwise_xor(me_ref[0], k), i, 0))

    wspec = pl.BlockSpec((None, tr, C), lambda i, me_ref: (layer, i, 0))
    return pl.pallas_call(
        body,
        grid_spec=pltpu.PrefetchScalarGridSpec(
            num_scalar_prefetch=1, grid=(R // tr,),
            in_specs=[pspec(k) for k in range(N_DEV)] + [wspec] * 3, out_specs=[wspec] * 4),
        out_shape=[jax.ShapeDtypeStruct(w.shape, F32)] * 4,
        compiler_params=_params(("parallel",)), name=name)(
            me, *[parts_a if k in peers_a else parts_b for k in range(N_DEV)], w, m, v)


def _adamw(name, parts, w, m, v, layer=None, into=None, *, tr=128):
    P, R, C = parts.shape
    tr = _tile(R, tr)

    def body(p_ref, w_ref, m_ref, v_ref, *rest):
        g = p_ref[0].astype(F32)
        for k in range(1, P):
            g = g + p_ref[k].astype(F32)
        _adamw_update(g, w_ref, m_ref, v_ref, *rest[-4:])

    pspec = pl.BlockSpec((P, tr, C), lambda i: (0, i, 0))
    if layer is None:
        wspec = pl.BlockSpec((tr, C), lambda i: (i, 0))
        return pl.pallas_call(
            body, grid=(R // tr,), in_specs=[pspec, wspec, wspec, wspec],
            out_specs=[wspec] * 4, out_shape=[jax.ShapeDtypeStruct((R, C), F32)] * 4,
            compiler_params=_params(("parallel",)), name=name)(parts, w, m, v)
    wspec = pl.BlockSpec((None, tr, C), lambda i: (layer, i, 0))
    prev = [] if into is None else list(into)
    return pl.pallas_call(
        body, grid=(R // tr,), in_specs=[pspec, wspec, wspec, wspec] + [_ANY] * len(prev),
        out_specs=[wspec] * 4, out_shape=[jax.ShapeDtypeStruct(w.shape, F32)] * 4,
        input_output_aliases={4 + k: k for k in range(len(prev))},
        compiler_params=_params(("parallel",)), name=name)(parts, w, m, v, *prev)


def _rows(a):
    flat = a.reshape(-1)
    pad = (-flat.shape[0]) % LANES
    if pad:
        flat = jnp.pad(flat, (0, pad))
    return flat.reshape(-1, LANES)


def kernel(x, norm_mix, norm_mlp, pool_w, pool_scale, norm_kv, w_kvf, b_f, w_q, w_o, w_up, w_down, norm_out, loss_target, m_norm_mix, m_norm_mlp, m_pool_w, m_pool_scale, m_norm_kv, m_w_kvf, m_b_f, m_w_q, m_w_o, m_w_up, m_w_down, m_norm_out, v_norm_mix, v_norm_mlp, v_pool_w, v_pool_scale, v_norm_kv, v_w_kvf, v_b_f, v_w_q, v_w_o, v_w_up, v_w_down, v_norm_out):
    _, S, D = x.shape
    H = D // HEAD_DIM
    dg = D // N_GROUPS
    n_kvf = 2 * D + H
    kvf_b = w_kvf.shape[1]
    fb = w_up.shape[2]
    ps_b = pool_scale.shape[1]
    xi, yi, ci = _position()
    my_block = 4 * xi + 2 * yi + ci
    x2 = x.reshape(S, D)
    tgt = loss_target.reshape(S, D)
    b_pad = jnp.pad(b_f, (0, LANES - H)).reshape(1, LANES)

    g_pool, g_scale = _all_gather_two_level([pool_w.astype(BF16), pool_scale])
    pw = g_pool[:, 0].transpose(1, 0, 2, 3).reshape(N_GROUPS, dg, dg)
    scale_full = g_scale.reshape(1, D)
    ex_up0 = _GatherByChip([w_up[0].astype(BF16)])
    ex_down0 = _GatherByChip([w_down[0].astype(BF16)])
    ex_kvf = _Exchange("gather", [w_kvf.astype(BF16)])
    ex_q = _Exchange("gather", [w_q[0].astype(BF16)])
    ex_late = _Exchange("gather", [w_o[0].astype(BF16), w_up[1].astype(BF16), w_down[1].astype(BF16)])

    h1, diff, hn_m0, (g_up0,) = _pool_fwd(x2, norm_mix[0:1], pw, scale_full, norm_mlp[0:1], exchange=ex_up0)
    u0, uu0, h2, (g_down0,), (g_kvf,) = _mlp_fwd("l0", hn_m0, h1, g_up0, None, ex_up=ex_down0, ex_down=ex_kvf)
    wkvf = g_kvf.transpose(1, 0, 2).reshape(D, n_kvf)
    w_kv = wkvf[:, :2 * D]
    w_f = jnp.pad(wkvf[:, 2 * D:], ((0, 0), (0, LANES - H)))

    gains_kv_q = jnp.stack([norm_kv, norm_mix[1]])
    hkv, hn_q = _rms_fwd("rms_kv_q", h2, gains_kv_q)
    kv, (g_q,) = _mm_nn("kv_proj", hkv, w_kv, out_dtype=BF16, exchange=ex_q)
    wq = g_q.reshape(D, D)
    f_raw = _mm_nn("f_proj", hkv, w_f, out_dtype=F32)
    c_sh = _gate_fwd(f_raw, b_pad)
    t_attn = _tile(S, 1024)
    c2_sh = c_sh * LOG2E
    c2_row = _row_layout(c2_sh[:, :H].T, t_attn)
    q2 = _mm_nn("q_proj", hn_q, wq, out_dtype=BF16, out_scale=HEAD_DIM ** -0.5 * LOG2E)
    o, l2_sh, (g_o, g_up1, g_down1) = _fox_fwd(q2, kv, c2_row, t=t_attn, exchange=ex_late)
    wo = g_o.reshape(D, D)
    h3 = _mm_nn("o_proj", o, wo, out_dtype=F32, residual=h2)
    (hn_m1,) = _rms_fwd("rms_mlp1", h3, norm_mlp[1:2])
    u1, uu1, h4, _, _ = _mlp_fwd("l1", hn_m1, h3, g_up1, g_down1)
    dh4, dh4_b, d_norm_out, loss_part = _loss_head(h4, norm_out.reshape(1, D), tgt)

    d_pre, _ = _mlp_dpre("l1", dh4_b, u1, g_down1)
    dw_down1, _ = _mm_tn("mlp_dwdown_l1", uu1, dh4_b)
    dw_up1, _ = _mlp_dwup("l1", hn_m1, d_pre, fb)
    d_hn, _ = _mlp_dhn("l1", d_pre, g_up1)
    dh3, dh3_b, d_norm_mlp1 = _rms_bwd("rms_mlp1_bwd", h3, norm_mlp[1:2], [d_hn], dh4)

    do, delta_sh = _o_proj_dx(dh3_b, wo, o)
    dw_o, _ = _mm_tn("o_proj_dw", o, dh3_b)
    l2_row = _row_layout(l2_sh[:, :H].T, t_attn)
    delta_row = _row_layout(delta_sh[:, :H].T, t_attn)
    dq, dk, dv, dck_sh, dcq_row, (p_up1, p_down1, p_o) = _fox_bwd(
        q2, kv, do, c2_sh, c2_row, l2_row, delta_row, t=t_attn,
        exchange=_Exchange("scatter", [dw_up1, dw_down1.reshape(N_DEV, fb, D), dw_o.reshape(N_DEV, D // N_DEV, D)]))
    dw_q, _ = _mm_tn("q_proj_dw", hn_q, dq)
    d_hn_q = _mm_nt("q_proj_dx", dq, wq, out_dtype=F32)

    dcq_sh = jnp.pad(dcq_row.reshape(H, S).T, ((0, 0), (0, LANES - H)))
    d_f, d_b = _gate_bwd(dck_sh + dcq_sh, f_raw, b_pad)
    dw_k, _ = _mm_tn("k_proj_dw", hkv, dk)
    dw_v, _ = _mm_tn("v_proj_dw", hkv, dv)
    dw_f, _ = _mm_tn("f_proj_dw", hkv, d_f)
    d_hkv = _kvf_dx(dk, dv, d_f, w_kv, w_f)
    dh2, dh2_b, d_norm_kv_q = _rms_bwd("rms_kv_q_bwd", h2, gains_kv_q, [d_hkv, d_hn_q], dh3)
    dw_kvf = jnp.concatenate([dw_k, dw_v, dw_f[:, :H]], axis=1).reshape(D, N_DEV, kvf_b).transpose(1, 0, 2)

    dw_down0, (p_kvf, p_q) = _mm_tn(
        "mlp_dwdown_l0", uu0, dh2_b, exchange=_Exchange("scatter", [dw_kvf, dw_q.reshape(N_DEV, D // N_DEV, D)]))
    dw_down0 = dw_down0.reshape(N_DEV, fb, D)
    d_pre, (pa_down0,) = _mlp_dpre("l0", dh2_b, u0, g_down0, exchange=_Exchange("scatter", [dw_down0], NEAR))
    dw_up0, (pb_down0,) = _mlp_dwup("l0", hn_m0, d_pre, fb, exchange=_Exchange("scatter", [dw_down0], FAR))
    d_hn, (pa_up0,) = _mlp_dhn("l0", d_pre, g_up0, exchange=_Exchange("scatter", [dw_up0], NEAR))
    dh1, _, d_norm_mlp0 = _rms_bwd("rms_mlp0_bwd", h1, norm_mlp[0:1], [d_hn], dh2, matmul_copy=False)
    grad_x, dw_pool, d_scale, d_norm_mix0, (pb_up0,) = _pool_bwd(
        x2, dh1, diff, norm_mix[0:1], pw, scale_full, exchange=_Exchange("scatter", [dw_up0], FAR))
    ex_pool = _Exchange("scatter", [
        dw_pool.astype(BF16).reshape(N_GROUPS, N_DEV, dg // N_DEV, dg).transpose(1, 0, 2, 3).reshape(
            N_DEV, N_GROUPS * dg // N_DEV, dg)])

    small = jnp.concatenate([
        _rows(jnp.concatenate([d_norm_mix0, d_norm_kv_q[1:2]], axis=0)),
        _rows(jnp.concatenate([d_norm_mlp0, d_norm_mlp1], axis=0)),
        _rows(d_norm_kv_q[0:1]),
        _rows(d_norm_out),
        _rows(d_scale),
        d_b,
        jnp.pad(loss_part[0:1, 0:1], ((0, 0), (0, LANES - 1))),
    ], axis=0)
    n_small = small.shape[0]
    small = jnp.pad(small, ((0, (-n_small) % 8), (0, 0)))
    total, (p_pool,) = _all_reduce_small(small, ex_pool)
    rd = D // LANES
    loss = total[7 * rd + 1, 0]
    g_scale_mine = lax.dynamic_slice(total[6 * rd:7 * rd].reshape(D), (my_block * ps_b,), (ps_b,))

    def pack(nm_, nl_, kv_, out_, bf_, ps_):
        return jnp.concatenate([_rows(nm_), _rows(nl_), _rows(kv_), _rows(out_), _rows(bf_), _rows(ps_)], axis=0)

    g_small = jnp.concatenate([total[:6 * rd], total[7 * rd:7 * rd + 1], _rows(g_scale_mine)], axis=0)
    w_small = pack(norm_mix, norm_mlp, norm_kv, norm_out, b_f, pool_scale)
    m_small = pack(m_norm_mix, m_norm_mlp, m_norm_kv, m_norm_out, m_b_f, m_pool_scale)
    v_small = pack(v_norm_mix, v_norm_mlp, v_norm_kv, v_norm_out, v_b_f, v_pool_scale)
    rs = g_small.shape[0]
    padr = (-rs) % 8
    pad8 = lambda a: jnp.pad(a, ((0, padr), (0, 0)))
    small_out = _adamw("adamw_small", pad8(g_small)[None], pad8(w_small), pad8(m_small), pad8(v_small), tr=rs + padr)

    def unpack(a):
        o0 = 0
        res = []
        for shape in [(2, D), (2, D), (D,), (D,)]:
            nr = (2 * rd) if len(shape) == 2 else rd
            res.append(a[o0:o0 + nr].reshape(shape))
            o0 += nr
        res.append(a[o0, :H])
        res.append(a[o0 + 1:o0 + 1 + ps_b // LANES].reshape(1, ps_b))
        return res

    small_res = [unpack(a) for a in small_out]

    r_pool = _adamw("adamw_pool", p_pool, pool_w.reshape(-1, dg), m_pool_w.reshape(-1, dg), v_pool_w.reshape(-1, dg))
    r_kvf = _adamw("adamw_kvf", p_kvf, w_kvf, m_w_kvf, v_w_kvf)
    r_q = _adamw("adamw_q", p_q, w_q[0], m_w_q[0], v_w_q[0])
    r_o = _adamw("adamw_o", p_o, w_o[0], m_w_o[0], v_w_o[0])
    me = my_block.astype(jnp.int32).reshape(1)
    r_up = _adamw_split("adamw_up0", pa_up0, pb_up0, NEAR, me, w_up, m_w_up, v_w_up, 0)
    r_up = _adamw("adamw_up1", p_up1, w_up, m_w_up, v_w_up, layer=1, into=r_up)
    r_down = _adamw_split("adamw_down0", pa_down0, pb_down0, NEAR, me, w_down, m_w_down, v_w_down, 0)
    r_down = _adamw("adamw_down1", p_down1, w_down, m_w_down, v_w_down, layer=1, into=r_down)

    def leaves(kind):
        sm = small_res[kind]
        return [
            sm[0], sm[1],
            r_pool[kind].reshape(pool_w.shape),
            sm[5],
            sm[2],
            r_kvf[kind],
            sm[4],
            r_q[kind][None], r_o[kind][None],
            r_up[kind],
            r_down[kind],
            sm[3],
        ]

    return (loss, grad_x.reshape(x.shape), *leaves(0), *leaves(1), *leaves(2), *leaves(3))
```
